```python
import math
import jax
import jax.numpy as jnp
from jax import lax
import numpy as np

D_MODEL = 1024
BATCH = 8
SEQ = 4096
DEPTH = 4

GRID_W = 64
CTX_LEN = 256
NORM_EPS = 1e-6
ROPE_BASE = 10000.0
DEEPNORM_ALPHA = (2.0 * DEPTH) ** 0.25
DEEPNORM_BETA = (8.0 * DEPTH) ** -0.25

MLA_HEADS = 8
MLA_Q_LORA = 256
MLA_KV_LORA = 128
MLA_NOPE = 64
MLA_ROPE = 32
MLA_V = 64
MLA_SCALE = (MLA_NOPE + MLA_ROPE) ** -0.5

DN_HEADS = 8
DN_HEAD_DIM = 64
DN_WIDTH = DN_HEADS * DN_HEAD_DIM
DN_CONV = 5
DN_CHUNK = 64

GQA_HEADS = 8
GQA_KV_HEADS = 2
GQA_GROUP = GQA_HEADS // GQA_KV_HEADS
GQA_HEAD_DIM = 64
GQA_SCALE = GQA_HEAD_DIM ** -0.5
WINDOW = 128
Q_BLOCK = 128

N_EXPERTS = 64
TOP_K = 8
N_GROUPS = 8
TOPK_GROUPS = 4
EXPERT_DIM = 256
SHARED_DIM = 256
ROUTED_SCALE = 2.5
EXPERT_BLOCK = 128

IN_SIZES = (MLA_Q_LORA, MLA_KV_LORA, MLA_ROPE,
            3 * DN_WIDTH, 2 * DN_HEADS, 2 * DN_HEADS, DN_WIDTH,
            GQA_HEADS * GQA_HEAD_DIM, GQA_KV_HEADS * GQA_HEAD_DIM, GQA_KV_HEADS * GQA_HEAD_DIM,
            3 * D_MODEL)
D_IN = sum(IN_SIZES)

kernel_name = 'hybrid_mla_deltanet_swa_moe_trunk'


def _split(z, sizes):
    return jnp.split(z, np.cumsum(sizes)[:-1].tolist(), axis=-1)


def _layernorm(x, g, b):
    xf = x.astype(jnp.float32)
    mu = jnp.mean(xf, -1, keepdims=True)
    var = jnp.mean(jnp.square(xf - mu), -1, keepdims=True)
    y = (xf - mu) * lax.rsqrt(var + NORM_EPS) * g.astype(jnp.float32) + b.astype(jnp.float32)
    return y.astype(x.dtype)


def _rmsnorm(x, g):
    xf = x.astype(jnp.float32)
    y = xf * lax.rsqrt(jnp.mean(xf * xf, -1, keepdims=True) + NORM_EPS)
    return (y * g.astype(jnp.float32)).astype(x.dtype)


def _l2norm(x):
    xf = x.astype(jnp.float32)
    return xf * lax.rsqrt(jnp.sum(xf * xf, -1, keepdims=True) + 1e-6)


def _axial_angles(n_rows, dim):
    row = jnp.repeat(jnp.arange(n_rows, dtype=jnp.float32), GRID_W)
    col = jnp.tile(jnp.arange(GRID_W, dtype=jnp.float32), n_rows)
    n = dim // 4
    inv = ROPE_BASE ** (-jnp.arange(n, dtype=jnp.float32) / n)
    return jnp.concatenate([row[:, None] * inv, col[:, None] * inv], axis=-1)


def _rope(x, ang):
    shape = (ang.shape[0],) + (1,) * (x.ndim - 3) + (ang.shape[1],)
    cos = jnp.cos(ang).reshape(shape).astype(x.dtype)
    sin = jnp.sin(ang).reshape(shape).astype(x.dtype)
    x1, x2 = jnp.split(x, 2, axis=-1)
    return jnp.concatenate([x1 * cos - x2 * sin, x2 * cos + x1 * sin], axis=-1)


def _blocks(t):
    B, L = t.shape[:2]
    return jnp.moveaxis(t.reshape((B, L // Q_BLOCK, Q_BLOCK) + t.shape[2:]), 1, 0)


def _unblocks(t):
    t = jnp.moveaxis(t, 0, 1)
    return t.reshape((t.shape[0], t.shape[1] * t.shape[2]) + t.shape[3:])


def _mla_project(cq, ckv, kr, ang, q_norm, kv_norm, w_uq, w_ukv):
    B, L, _ = cq.shape
    q = (_rmsnorm(cq, q_norm) @ w_uq).reshape(B, L, MLA_HEADS, MLA_NOPE + MLA_ROPE)
    q_nope, q_rope = q[..., :MLA_NOPE], q[..., MLA_NOPE:]
    kv = (_rmsnorm(ckv, kv_norm) @ w_ukv).reshape(B, L, MLA_HEADS, MLA_NOPE + MLA_V)
    k_nope, v = kv[..., :MLA_NOPE], kv[..., MLA_NOPE:]
    if ang is not None:
        q_rope = _rope(q_rope, ang)
        kr = _rope(kr, ang)
    return q_nope, q_rope, k_nope, kr, v


def _mla_attend(q_nope, q_rope, k_nope, k_rope, v):
    s = (jnp.einsum('bqhd,bkhd->bhqk', q_nope, k_nope)
         + jnp.einsum('bqhr,bkr->bhqk', q_rope, k_rope))
    p = jax.nn.softmax(s.astype(jnp.float32) * MLA_SCALE, axis=-1).astype(v.dtype)
    return jnp.einsum('bhqk,bkhd->bqhd', p, v)


def _short_conv(x, w):
    ch = x.shape[-1]
    y = lax.conv_general_dilated(x, w[:, None, :].astype(x.dtype), window_strides=(1,),
                                 padding=[(DN_CONV // 2, DN_CONV // 2)],
                                 dimension_numbers=('NWC', 'WIO', 'NWC'), feature_group_count=ch)
    return jax.nn.silu(y)


def _dn_prepare(qkv, a, b, conv_w, a_log, dt_bias):
    B, L, _ = qkv.shape
    q, k, v = jnp.split(_short_conv(qkv, conv_w), 3, axis=-1)
    shp = (B, L, DN_HEADS, DN_HEAD_DIM)
    q = _l2norm(q.reshape(shp))
    k = _l2norm(k.reshape(shp))
    v = v.reshape(shp).astype(jnp.float32)
    a = a.astype(jnp.float32).reshape(B, L, 2, DN_HEADS)
    b = b.astype(jnp.float32).reshape(B, L, 2, DN_HEADS)
    g = -jnp.exp(a_log.astype(jnp.float32)) * jax.nn.softplus(a + dt_bias.astype(jnp.float32))
    return q, k, v, g, jax.nn.sigmoid(b)


def _to_chunks(t):
    B, L, H = t.shape[:3]
    t = t.reshape((B, L // DN_CHUNK, DN_CHUNK, H) + t.shape[3:])
    return jnp.moveaxis(t, (1, 3), (0, 2))


def _chunk_gated_delta(q, k, v, g, beta, s0):
    B, L, H, dk = q.shape
    dv = v.shape[-1]
    q, k, v, g, beta = (_to_chunks(t) for t in (q, k, v, g, beta))
    q = q * dk ** -0.5
    gc = jnp.cumsum(g, axis=-1)
    idx = jnp.arange(DN_CHUNK)
    incl = idx[:, None] >= idx[None, :]
    strict = idx[:, None] > idx[None, :]
    decay = jnp.exp(jnp.where(incl, gc[..., :, None] - gc[..., None, :], -jnp.inf))
    kb = k * beta[..., None]
    a_mat = jnp.where(strict, jnp.einsum('nbhid,nbhjd->nbhij', kb, k) * decay, 0.0)
    rhs = jnp.concatenate([v * beta[..., None], kb * jnp.exp(gc)[..., None]], axis=-1)
    sol = lax.linalg.triangular_solve(a_mat + jnp.eye(DN_CHUNK, dtype=a_mat.dtype), rhs,
                                      left_side=True, lower=True, unit_diagonal=True)
    u, w = sol[..., :dv], sol[..., dv:]
    qk = jnp.einsum('nbhid,nbhjd->nbhij', q, k) * decay
    qg = q * jnp.exp(gc)[..., None]
    kg = k * jnp.exp(gc[..., -1:] - gc)[..., None]
    g_last = jnp.exp(gc[..., -1])

    def step(S, xs):
        qg_i, kg_i, u_i, w_i, qk_i, gl_i = xs
        v_new = u_i - jnp.einsum('bhcd,bhde->bhce', w_i, S)
        o = jnp.einsum('bhcd,bhde->bhce', qg_i, S) + jnp.einsum('bhij,bhje->bhie', qk_i, v_new)
        S = S * gl_i[..., None, None] + jnp.einsum('bhcd,bhce->bhde', kg_i, v_new)
        return S, o

    s_fin, o = lax.scan(step, s0, (qg, kg, u, w, qk, g_last))
    o = jnp.moveaxis(o, (0, 2), (1, 3)).reshape(B, L, H, dv)
    return o, s_fin


def _dn_bidirectional(dc, dl):
    qc, kc, vc, gc, bc = dc
    ql, kl, vl, gl, bl = dl
    B, _, H, dk = qc.shape
    s0 = jnp.zeros((B, H, dk, vc.shape[-1]), jnp.float32)
    outs_c, outs_l = [], []
    for d in range(2):
        f = (lambda t: t) if d == 0 else (lambda t: jnp.flip(t, axis=1))
        oc, sc = _chunk_gated_delta(f(qc), f(kc), f(vc), f(gc[:, :, d]), f(bc[:, :, d]), s0)
        ol, _ = _chunk_gated_delta(f(ql), f(kl), f(vl), f(gl[:, :, d]), f(bl[:, :, d]), sc)
        outs_c.append(f(oc))
        outs_l.append(f(ol))
    return outs_c[0] + outs_c[1], outs_l[0] + outs_l[1]


def _dn_out(o, z, norm_w):
    B, L = o.shape[:2]
    zf = z.astype(jnp.float32).reshape(B, L, DN_HEADS, DN_HEAD_DIM)
    return (_rmsnorm(o, norm_w) * jax.nn.silu(zf)).reshape(B, L, DN_WIDTH).astype(z.dtype)


def _gqa_heads(q, k, v):
    B, L, _ = q.shape
    q = q.reshape(B, L, GQA_KV_HEADS, GQA_GROUP, GQA_HEAD_DIM)
    k = k.reshape(B, L, GQA_KV_HEADS, GQA_HEAD_DIM)
    v = v.reshape(B, L, GQA_KV_HEADS, GQA_HEAD_DIM)
    return q, k, v


def _gqa_ctx(q, k, v, sink):
    s = jnp.einsum('bqhgd,bkhd->bhgqk', q, k).astype(jnp.float32) * GQA_SCALE
    sk = jnp.broadcast_to(sink.astype(jnp.float32).reshape(GQA_KV_HEADS, GQA_GROUP)[None, :, :, None, None],
                          s.shape[:-1] + (1,))
    p = jax.nn.softmax(jnp.concatenate([s, sk], axis=-1), axis=-1)[..., :-1].astype(v.dtype)
    return jnp.einsum('bhgqk,bkhd->bqhgd', p, v)


def _gqa_latent(q, k, v, k_ctx, v_ctx, sink):
    B, L = q.shape[:2]
    C = k_ctx.shape[1]
    span = Q_BLOCK + 2 * WINDOW
    kp = jnp.pad(k, ((0, 0), (WINDOW, WINDOW), (0, 0), (0, 0)))
    vp = jnp.pad(v, ((0, 0), (WINDOW, WINDOW), (0, 0), (0, 0)))
    offs_q = jnp.arange(Q_BLOCK)
    offs_k = jnp.arange(span) - WINDOW
    sink_f = sink.astype(jnp.float32).reshape(GQA_KV_HEADS, GQA_GROUP)[None, :, :, None, None]

    def block(args):
        qb, start = args
        kb = lax.dynamic_slice_in_dim(kp, start, span, axis=1)
        vb = lax.dynamic_slice_in_dim(vp, start, span, axis=1)
        q_pos = start + offs_q
        k_pos = start + offs_k
        ok = ((jnp.abs(k_pos[None, :] - q_pos[:, None]) <= WINDOW)
              & (k_pos[None, :] >= 0) & (k_pos[None, :] < L))
        s_loc = jnp.einsum('bqhgd,bkhd->bhgqk', qb, kb).astype(jnp.float32) * GQA_SCALE
        s_loc = jnp.where(ok, s_loc, -jnp.inf)
        s_ctx = jnp.einsum('bqhgd,bkhd->bhgqk', qb, k_ctx).astype(jnp.float32) * GQA_SCALE
        sk = jnp.broadcast_to(sink_f, s_loc.shape[:-1] + (1,))
        p = jax.nn.softmax(jnp.concatenate([s_loc, s_ctx, sk], axis=-1), axis=-1).astype(v.dtype)
        return (jnp.einsum('bhgqk,bkhd->bqhgd', p[..., :span], vb)
                + jnp.einsum('bhgqk,bkhd->bqhgd', p[..., span:span + C], v_ctx))

    starts = jnp.arange(L // Q_BLOCK) * Q_BLOCK
    return _unblocks(lax.map(block, (_blocks(q), starts)))


def _merge(o_mla, o_dn, o_gqa, gates, w_o_mla, w_o_dn, w_o_gqa, w_out):
    g = jax.nn.sigmoid(gates.astype(jnp.float32)).astype(gates.dtype)
    g1, g2, g3 = jnp.split(g, 3, axis=-1)
    m = g1 * (o_mla @ w_o_mla) + g2 * (o_dn @ w_o_dn) + g3 * (o_gqa @ w_o_gqa)
    return m @ w_out


def _mixer(u_ctx, u_lat, ang_mla, ang_gqa, w_in, mla_q_norm, mla_kv_norm, w_uq, w_ukv,
           dn_conv, dn_a_log, dn_dt_bias, dn_norm, gqa_sink, w_o_mla, w_o_dn, w_o_gqa, w_out, need_ctx):
    B, C, _ = u_ctx.shape
    L = u_lat.shape[1]
    z = jnp.concatenate([u_ctx, u_lat], axis=1) @ w_in
    cq, ckv, kr, dqkv, da, db, dz, gq, gk, gv, gates = _split(z, IN_SIZES)
    cs = lambda t: t[:, :C]
    ls = lambda t: t[:, C:]
    mc = _mla_project(cs(cq), cs(ckv), cs(kr), None, mla_q_norm, mla_kv_norm, w_uq, w_ukv)
    ml = _mla_project(ls(cq), ls(ckv), ls(kr), ang_mla, mla_q_norm, mla_kv_norm, w_uq, w_ukv)
    kn_all = jnp.concatenate([mc[2], ml[2]], axis=1)
    kr_all = jnp.concatenate([mc[3], ml[3]], axis=1)
    v_all = jnp.concatenate([mc[4], ml[4]], axis=1)
    o_mla = _unblocks(lax.map(lambda qs: _mla_attend(qs[0], qs[1], kn_all, kr_all, v_all),
                              (_blocks(ml[0]), _blocks(ml[1]))))
    dc = _dn_prepare(cs(dqkv), cs(da), cs(db), dn_conv, dn_a_log, dn_dt_bias)
    dl = _dn_prepare(ls(dqkv), ls(da), ls(db), dn_conv, dn_a_log, dn_dt_bias)
    o_dn_c, o_dn_l = _dn_bidirectional(dc, dl)
    qc, kc, vc = _gqa_heads(cs(gq), cs(gk), cs(gv))
    ql, kl, vl = _gqa_heads(ls(gq), ls(gk), ls(gv))
    o_gqa = _gqa_latent(_rope(ql, ang_gqa), _rope(kl, ang_gqa), vl, kc, vc, gqa_sink)
    y_lat = _merge(o_mla.reshape(B, L, -1), _dn_out(o_dn_l, ls(dz), dn_norm), o_gqa.reshape(B, L, -1),
                   ls(gates), w_o_mla, w_o_dn, w_o_gqa, w_out)
    if not need_ctx:
        return None, y_lat
    y_ctx = _merge(_mla_attend(*mc).reshape(B, C, -1), _dn_out(o_dn_c, cs(dz), dn_norm),
                   _gqa_ctx(qc, kc, vc, gqa_sink).reshape(B, C, -1),
                   cs(gates), w_o_mla, w_o_dn, w_o_gqa, w_out)
    return y_ctx, y_lat


def _route(u, w_router, router_bias):
    scores = jax.nn.sigmoid((u @ w_router).astype(jnp.float32))
    sel = scores + router_bias.astype(jnp.float32)
    grp = sel.reshape(sel.shape[:-1] + (N_GROUPS, N_EXPERTS // N_GROUPS))
    grp_score = jnp.sum(lax.top_k(grp, 2)[0], axis=-1)
    _, gidx = lax.top_k(grp_score, TOPK_GROUPS)
    gmask = jnp.any(gidx[..., None] == jnp.arange(N_GROUPS), axis=-2)
    sel = jnp.where(jnp.repeat(gmask, N_EXPERTS // N_GROUPS, axis=-1), sel, -jnp.inf)
    _, eidx = lax.top_k(sel, TOP_K)
    w = jnp.take_along_axis(scores, eidx, axis=-1)
    return eidx, w / jnp.sum(w, axis=-1, keepdims=True) * ROUTED_SCALE


def _routed_experts(u, eidx, ew, w_gate, w_up, w_down):
    T = u.shape[0]
    A = T * TOP_K
    n_blocks = -(-(A + N_EXPERTS * (EXPERT_BLOCK - 1)) // EXPERT_BLOCK)
    rows = n_blocks * EXPERT_BLOCK
    flat_e = eidx.reshape(-1)
    order = jnp.argsort(flat_e)
    e_sorted = flat_e[order]
    counts = jnp.bincount(flat_e, length=N_EXPERTS)
    padded = (counts + EXPERT_BLOCK - 1) // EXPERT_BLOCK * EXPERT_BLOCK
    start_sorted = jnp.cumsum(counts) - counts
    pad_end = jnp.cumsum(padded)
    start_pad = pad_end - padded
    dest = start_pad[e_sorted] + jnp.arange(A) - start_sorted[e_sorted]
    row_tok = jnp.zeros((rows,), jnp.int32).at[dest].set((order // TOP_K).astype(jnp.int32))
    row_w = jnp.zeros((rows,), ew.dtype).at[dest].set(ew.reshape(-1)[order])
    block_e = jnp.minimum(jnp.searchsorted(pad_end, jnp.arange(n_blocks) * EXPERT_BLOCK, side='right'),
                          N_EXPERTS - 1)
    xb = u[row_tok].reshape(n_blocks, EXPERT_BLOCK, -1)

    def expert_block(args):
        xr, e = args
        hdn = jax.nn.silu(xr @ w_gate[e]) * (xr @ w_up[e])
        return hdn @ w_down[e]

    y = lax.map(expert_block, (xb, block_e)).reshape(rows, -1)
    return jnp.zeros_like(u).at[row_tok].add((y * row_w[:, None]).astype(u.dtype))


def _moe(u, w_router, router_bias, w_gate, w_up, w_down, w_sh_gate, w_sh_up, w_sh_down):
    eidx, ew = _route(u, w_router, router_bias)
    routed = lax.map(lambda a: _routed_experts(a[0], a[1], a[2], w_gate, w_up, w_down), (u, eidx, ew))
    shared = (jax.nn.silu(u @ w_sh_gate) * (u @ w_sh_up)) @ w_sh_down
    return shared + routed


def setup_inputs(seed: int = 0) -> dict:
    key = jax.random.key(seed)
    keys = iter(jax.random.split(key, 40))
    f32 = jnp.float32

    def normal(shape, std=1.0):
        return jax.random.normal(next(keys), shape, f32) * std

    D, NL = D_MODEL, DEPTH
    dt = jnp.exp(jax.random.uniform(next(keys), (NL, 2, DN_HEADS), f32, math.log(1e-3), math.log(1e-1)))
    a_log = jnp.log(jax.random.uniform(next(keys), (NL, 2, DN_HEADS), f32, 1.0, 16.0))
    return {
        'x': normal((BATCH, SEQ, D)),
        'c': normal((BATCH, D)),
        'ctx': normal((BATCH, CTX_LEN, D)),
        'c_ctx': normal((D,)),
        'w_ada': normal((NL, D, 6 * D), 0.5 * D ** -0.5),
        'b_ada': normal((NL, 6 * D), 0.02),
        'w_in': normal((NL, D, D_IN), D ** -0.5),
        'mla_q_norm': 1.0 + normal((NL, MLA_Q_LORA), 0.02),
        'mla_kv_norm': 1.0 + normal((NL, MLA_KV_LORA), 0.02),
        'w_uq': normal((NL, MLA_Q_LORA, MLA_HEADS * (MLA_NOPE + MLA_ROPE)), MLA_Q_LORA ** -0.5),
        'w_ukv': normal((NL, MLA_KV_LORA, MLA_HEADS * (MLA_NOPE + MLA_V)), MLA_KV_LORA ** -0.5),
        'dn_conv': normal((NL, DN_CONV, 3 * DN_WIDTH), DN_CONV ** -0.5),
        'dn_a_log': a_log,
        'dn_dt_bias': dt + jnp.log(-jnp.expm1(-dt)),
        'dn_norm': 1.0 + normal((NL, DN_HEAD_DIM), 0.02),
        'gqa_sink': normal((NL, GQA_HEADS), 0.5),
        'w_o_mla': normal((NL, MLA_HEADS * MLA_V, D), (MLA_HEADS * MLA_V) ** -0.5),
        'w_o_dn': normal((NL, DN_WIDTH, D), DN_WIDTH ** -0.5),
        'w_o_gqa': normal((NL, GQA_HEADS * GQA_HEAD_DIM, D), (GQA_HEADS * GQA_HEAD_DIM) ** -0.5),
        'w_out': normal((NL, D, D), DEEPNORM_BETA * D ** -0.5),
        'ln1_g': 1.0 + normal((NL, D), 0.02),
        'ln1_b': normal((NL, D), 0.02),
        'w_router': normal((NL, D, N_EXPERTS), D ** -0.5),
        'router_bias': normal((NL, N_EXPERTS), 0.01),
        'w_exp_gate': normal((NL, N_EXPERTS, D, EXPERT_DIM), D ** -0.5),
        'w_exp_up': normal((NL, N_EXPERTS, D, EXPERT_DIM), D ** -0.5),
        'w_exp_down': normal((NL, N_EXPERTS, EXPERT_DIM, D), DEEPNORM_BETA * EXPERT_DIM ** -0.5),
        'w_sh_gate': normal((NL, D, SHARED_DIM), D ** -0.5),
        'w_sh_up': normal((NL, D, SHARED_DIM), D ** -0.5),
        'w_sh_down': normal((NL, SHARED_DIM, D), DEEPNORM_BETA * SHARED_DIM ** -0.5),
        'ln2_g': 1.0 + normal((NL, D), 0.02),
        'ln2_b': normal((NL, D), 0.02),
    }


def reference(x, c, ctx, c_ctx, w_ada, b_ada, w_in, mla_q_norm, mla_kv_norm, w_uq, w_ukv,
              dn_conv, dn_a_log, dn_dt_bias, dn_norm, gqa_sink, w_o_mla, w_o_dn, w_o_gqa, w_out,
              ln1_g, ln1_b, w_router, router_bias, w_exp_gate, w_exp_up, w_exp_down,
              w_sh_gate, w_sh_up, w_sh_down, ln2_g, ln2_b):
    n_lat = x.shape[1]
    n_ctx = ctx.shape[1]
    n_rows = n_lat // GRID_W
    ang_mla = _axial_angles(n_rows, MLA_ROPE)
    ang_gqa = _axial_angles(n_rows, GQA_HEAD_DIM)
    h = ctx
    for l in range(DEPTH):
        need_ctx = l < DEPTH - 1
        mod = jax.nn.silu(c) @ w_ada[l] + b_ada[l]
        mod_c = jax.nn.silu(c_ctx) @ w_ada[l] + b_ada[l]
        sh_a, sc_a, g_a, sh_f, sc_f, g_f = jnp.split(mod[:, None, :], 6, axis=-1)
        csh_a, csc_a, cg_a, csh_f, csc_f, cg_f = jnp.split(mod_c, 6, axis=-1)
        y_ctx, y_lat = _mixer(h * (1.0 + csc_a) + csh_a, x * (1.0 + sc_a) + sh_a, ang_mla, ang_gqa,
                              w_in[l], mla_q_norm[l], mla_kv_norm[l], w_uq[l], w_ukv[l],
                              dn_conv[l], dn_a_log[l], dn_dt_bias[l], dn_norm[l], gqa_sink[l],
                              w_o_mla[l], w_o_dn[l], w_o_gqa[l], w_out[l], need_ctx)
        x = _layernorm(DEEPNORM_ALPHA * x + g_a * y_lat, ln1_g[l], ln1_b[l])
        moe_w = (w_router[l], router_bias[l], w_exp_gate[l], w_exp_up[l], w_exp_down[l],
                 w_sh_gate[l], w_sh_up[l], w_sh_down[l])
        if need_ctx:
            h = _layernorm(DEEPNORM_ALPHA * h + cg_a * y_ctx, ln1_g[l], ln1_b[l])
            f = _moe(jnp.concatenate([h * (1.0 + csc_f) + csh_f, x * (1.0 + sc_f) + sh_f], axis=1), *moe_w)
            h = _layernorm(DEEPNORM_ALPHA * h + cg_f * f[:, :n_ctx], ln2_g[l], ln2_b[l])
            x = _layernorm(DEEPNORM_ALPHA * x + g_f * f[:, n_ctx:], ln2_g[l], ln2_b[l])
        else:
            f = _moe(x * (1.0 + sc_f) + sh_f, *moe_w)
            x = _layernorm(DEEPNORM_ALPHA * x + g_f * f, ln2_g[l], ln2_b[l])
    return x
```

```python
import functools

import numpy as np
import jax
import jax.numpy as jnp
from jax import lax
from jax.experimental import pallas as pl
from jax.experimental.pallas import tpu as pltpu

F32 = jnp.float32
MXU_DTYPE = jnp.bfloat16

D_MODEL = 1024
DEPTH = 4
GRID_W = 64
NORM_EPS = 1e-6
ROPE_BASE = 10000.0
DEEPNORM_ALPHA = (2.0 * DEPTH) ** 0.25

MLA_HEADS = 8
MLA_Q_LORA = 256
MLA_KV_LORA = 128
MLA_NOPE = 64
MLA_ROPE = 32
MLA_V = 64
MLA_SCALE = (MLA_NOPE + MLA_ROPE) ** -0.5

DN_HEADS = 8
DN_HEAD_DIM = 64
DN_WIDTH = DN_HEADS * DN_HEAD_DIM
DN_CONV = 5
DN_CHUNK = 64

GQA_HEADS = 8
GQA_KV_HEADS = 2
GQA_HEAD_DIM = 64
GQA_SCALE = GQA_HEAD_DIM ** -0.5
WINDOW = 128

N_EXPERTS = 64
TOP_K = 8
N_GROUPS = 8
TOPK_GROUPS = 4
EXPERT_DIM = 256
SHARED_DIM = 256
ROUTED_SCALE = 2.5

IN_SIZES = (MLA_Q_LORA, MLA_KV_LORA, MLA_ROPE,
            3 * DN_WIDTH, 2 * DN_HEADS, 2 * DN_HEADS, DN_WIDTH,
            GQA_HEADS * GQA_HEAD_DIM, GQA_KV_HEADS * GQA_HEAD_DIM, GQA_KV_HEADS * GQA_HEAD_DIM,
            3 * D_MODEL)

LANES = 128
TM = 256
MOD_ROWS = 16
CTX_MOD_ROW = 8

OFF_A = 0
OFF_DQKV = 640
OFF_AB = OFF_DQKV + 3 * DN_WIDTH
OFF_DZ = OFF_AB + LANES
OFF_GQ = OFF_DZ + DN_WIDTH
OFF_GK = OFF_GQ + 1024
OFF_GATES = OFF_GK + 768
NZ = OFF_GATES + 3 * D_MODEL

VMEM_LIMIT = 56 * 1024 * 1024


def _mm(a, b):
    return jnp.dot(a.astype(MXU_DTYPE), b.astype(MXU_DTYPE), preferred_element_type=F32)


def _mm_nt(a, b):
    return lax.dot_general(a.astype(MXU_DTYPE), b.astype(MXU_DTYPE), (((1,), (1,)), ((), ())),
                           preferred_element_type=F32)


def _mm_tn(a, b):
    return lax.dot_general(a.astype(MXU_DTYPE), b.astype(MXU_DTYPE), (((0,), (0,)), ((), ())),
                           preferred_element_type=F32)


def _bmm(a, b):
    return jnp.einsum('cik,ckj->cij', a.astype(MXU_DTYPE), b.astype(MXU_DTYPE), preferred_element_type=F32)


def _bmm_nt(a, b):
    return jnp.einsum('cik,cjk->cij', a.astype(MXU_DTYPE), b.astype(MXU_DTYPE), preferred_element_type=F32)


def _split3(x):
    hi = x.astype(jnp.bfloat16).astype(F32)
    r = x - hi
    mid = r.astype(jnp.bfloat16).astype(F32)
    lo = (r - mid).astype(jnp.bfloat16).astype(F32)
    return hi, mid, lo


def _exact_mm(x, m01):
    hi, mid, lo = _split3(x)
    return _mm(hi, m01) + _mm(mid, m01) + _mm(lo, m01)


def _exact_mm_left(m01, x):
    hi, mid, lo = _split3(x)
    return _mm(m01, hi) + _mm(m01, mid) + _mm(m01, lo)


def _silu(x):
    return x * jax.nn.sigmoid(x)


def _layernorm(v, g, b):
    mu = jnp.mean(v, -1, keepdims=True)
    d = v - mu
    var = jnp.mean(d * d, -1, keepdims=True)
    return d * lax.rsqrt(var + NORM_EPS) * g + b


def _mod_index(tiles_per_b, ctx_tiles, k):
    def index(i):
        row = jnp.where((i % tiles_per_b) < ctx_tiles, CTX_MOD_ROW, i // tiles_per_b)
        return (row * 6 + k, 0, 0)
    return index


def _params(*sem):
    return pltpu.CompilerParams(dimension_semantics=sem, vmem_limit_bytes=VMEM_LIMIT)


def _ada_kernel(c_ref, w_ref, b_ref, o_ref):
    o_ref[0] = _mm(_silu(c_ref[...]), w_ref[0]) + b_ref[0]


def _ada_call(cc, w_ada, b_ada):
    nl, d, n6 = w_ada.shape
    tn = 1536
    return pl.pallas_call(
        _ada_kernel,
        grid=(nl, n6 // tn),
        in_specs=[pl.BlockSpec((MOD_ROWS, d), lambda l, j: (0, 0)),
                  pl.BlockSpec((1, d, tn), lambda l, j: (l, 0, j)),
                  pl.BlockSpec((1, 1, tn), lambda l, j: (l, 0, j))],
        out_specs=pl.BlockSpec((1, MOD_ROWS, tn), lambda l, j: (l, 0, j)),
        out_shape=jax.ShapeDtypeStruct((nl, MOD_ROWS, n6), F32),
        compiler_params=_params("parallel", "parallel"),
        name="ada_mod",
    )(cc, w_ada, b_ada.reshape(nl, 1, n6))


def _inproj_kernel(x_ref, sh_ref, sc_ref, w_ref, wuq_ref, wukv_ref, qn_ref, kvn_ref,
                   cm_ref, sm_ref, cg_ref, sg_ref,
                   q_out, k_out, v_out, dqkv_out, ab_out, dz_out, gq_out, gkv_out, gates_out):
    u = (x_ref[...] * (1.0 + sc_ref[0]) + sh_ref[0]).astype(MXU_DTYPE)

    def z(off, width):
        return jnp.dot(u, w_ref[:, off:off + width], preferred_element_type=F32)

    def rms(v, g):
        return v * lax.rsqrt(jnp.mean(v * v, -1, keepdims=True) + NORM_EPS) * g

    cm, sm, cg, sg = cm_ref[...], sm_ref[...], cg_ref[...], sg_ref[...]

    za = z(OFF_A, 640)
    qq = _mm(rms(za[:, 0:256], qn_ref[...]), wuq_ref[...])
    kvv = _mm(rms(za[:, 256:384], kvn_ref[...]), wukv_ref[...])
    k_rope = za[:, 384:512] * cm + za[:, 512:640] * sm
    for h in range(MLA_HEADS):
        sl = slice(h * LANES, (h + 1) * LANES)
        qa = qq[:, h * LANES:(h + 1) * LANES]
        qb = qq[:, 1024 + h * LANES:1024 + (h + 1) * LANES]
        q_out[:, sl] = ((qa * cm + qb * sm) * MLA_SCALE).astype(q_out.dtype)
        k_out[:, sl] = (kvv[:, sl] + k_rope).astype(k_out.dtype)
    v_out[...] = kvv[:, 1024:1536].astype(v_out.dtype)

    for t in range(3):
        dqkv_out[:, t * 512:(t + 1) * 512] = z(OFF_DQKV + t * 512, 512).astype(dqkv_out.dtype)
    ab_out[...] = z(OFF_AB, LANES)
    dz_out[...] = z(OFF_DZ, DN_WIDTH).astype(dz_out.dtype)

    zq = z(OFF_GQ, 1024)
    for p in range(4):
        sl = slice(p * LANES, (p + 1) * LANES)
        gq_out[:, sl] = ((zq[:, sl] * cg + zq[:, 512 + p * LANES:512 + (p + 1) * LANES] * sg)
                         * GQA_SCALE).astype(gq_out.dtype)
    zk = z(OFF_GK, 768)
    for j in range(2):
        sl = slice(j * LANES, (j + 1) * LANES)
        gkv_out[:, sl] = (zk[:, sl] * cg + zk[:, 256 + j * LANES:256 + (j + 1) * LANES] * sg).astype(gkv_out.dtype)
    gkv_out[:, 256:512] = zk[:, 512:768].astype(gkv_out.dtype)

    for t in range(3):
        gates_out[:, t * 1024:(t + 1) * 1024] = jax.nn.sigmoid(z(OFF_GATES + t * 1024, 1024)).astype(gates_out.dtype)


def _inproj_call(xs, mod_l, w_cat, wuq_cat, wukv_cat, qn, kvn, tabs, B, S, C):
    T, D = xs.shape
    tpb, ctiles = S // TM, C // TM
    act = MXU_DTYPE
    row = lambda i: (i, 0)
    const = lambda i: (0, 0)
    tab = lambda i: (i % tpb, 0)
    widths = (1024, 1024, 512, 1536, LANES, 512, 512, 512, 3072)
    dtypes = (act, act, act, act, F32, act, act, act, act)
    return pl.pallas_call(
        _inproj_kernel,
        grid=(T // TM,),
        in_specs=[pl.BlockSpec((TM, D), row),
                  pl.BlockSpec((1, 1, D), _mod_index(tpb, ctiles, 0)),
                  pl.BlockSpec((1, 1, D), _mod_index(tpb, ctiles, 1)),
                  pl.BlockSpec((D, NZ), const),
                  pl.BlockSpec((MLA_Q_LORA, 2048), const),
                  pl.BlockSpec((MLA_KV_LORA, 1536), const),
                  pl.BlockSpec((1, MLA_Q_LORA), const),
                  pl.BlockSpec((1, MLA_KV_LORA), const)]
                 + [pl.BlockSpec((TM, LANES), tab)] * 4,
        out_specs=[pl.BlockSpec((TM, w), row) for w in widths],
        out_shape=[jax.ShapeDtypeStruct((T, w), dt) for w, dt in zip(widths, dtypes)],
        compiler_params=_params("parallel"),
        name="in_proj",
    )(xs, mod_l, mod_l, w_cat, wuq_cat, wukv_cat, qn, kvn, *tabs)


def _mla_kernel(q_ref, k_ref, v_ref, o_ref, *, n_ctx, n_all):
    i = pl.program_id(2)
    tq = q_ref.shape[1]
    left = lax.broadcasted_iota(jnp.int32, (tq, LANES), 1) < MLA_V

    def attend(nk):
        outs = []
        for hh in range(2):
            q = q_ref[0, :, hh * LANES:(hh + 1) * LANES]
            k = k_ref[0, 0:nk, hh * LANES:(hh + 1) * LANES]
            s = _mm_nt(q, k)
            p = jnp.exp(s - jnp.max(s, -1, keepdims=True))
            l = jnp.sum(p, -1, keepdims=True)
            outs.append(_mm(p, v_ref[0, 0:nk, :]) / l)
        o_ref[0] = jnp.where(left, outs[0], outs[1]).astype(o_ref.dtype)

    ctx_tiles = n_ctx // tq

    @pl.when(i < ctx_tiles)
    def _():
        attend(n_ctx)

    @pl.when(i >= ctx_tiles)
    def _():
        attend(n_all)


def _mla_call(q, k, v, B, S, C):
    tq = 256
    q3, k3, v3 = q.reshape(B, S, 1024), k.reshape(B, S, 1024), v.reshape(B, S, 512)
    out = pl.pallas_call(
        functools.partial(_mla_kernel, n_ctx=C, n_all=S),
        grid=(B, MLA_HEADS // 2, S // tq),
        in_specs=[pl.BlockSpec((1, tq, 2 * LANES), lambda b, j, i: (b, i, j)),
                  pl.BlockSpec((1, S, 2 * LANES), lambda b, j, i: (b, 0, j)),
                  pl.BlockSpec((1, S, LANES), lambda b, j, i: (b, 0, j))],
        out_specs=pl.BlockSpec((1, tq, LANES), lambda b, j, i: (b, i, j)),
        out_shape=jax.ShapeDtypeStruct((B, S, 512), MXU_DTYPE),
        compiler_params=_params("parallel", "parallel", "arbitrary"),
        name="mla_attn",
    )(q3, k3, v3)
    return out.reshape(B * S, 512)


def _gqa_kernel(sink_ref, q_ref, kv_ref, o_ref, *, n_ctx, n_all):
    i = pl.program_id(1)
    qb_rows = q_ref.shape[1]
    span = qb_rows + 2 * WINDOW
    ctx_blocks = n_ctx // qb_rows
    lane = lax.broadcasted_iota(jnp.int32, (qb_rows, LANES), 1)
    left = lane < GQA_HEAD_DIM

    def run(latent):
        if latent:
            qb = i - ctx_blocks
            ws = pl.multiple_of(jnp.minimum((qb + 1) * qb_rows, n_all - span), qb_rows)
            q_pos = qb * qb_rows + lax.broadcasted_iota(jnp.int32, (qb_rows, span), 0)
            k_pos = ws - n_ctx + lax.broadcasted_iota(jnp.int32, (qb_rows, span), 1)
            ok = (jnp.abs(k_pos - q_pos) <= WINDOW) & (k_pos >= 0)
        for p in range(4):
            j = p // 2
            qp = q_ref[0, :, p * LANES:(p + 1) * LANES]
            kc = kv_ref[0, 0:n_ctx, j * LANES:(j + 1) * LANES]
            vc = kv_ref[0, 0:n_ctx, 256 + j * LANES:256 + (j + 1) * LANES]
            if latent:
                kl = kv_ref[0, pl.ds(ws, span), j * LANES:(j + 1) * LANES]
                vl = kv_ref[0, pl.ds(ws, span), 256 + j * LANES:256 + (j + 1) * LANES]
            outs = []
            for r in range(2):
                qm = jnp.where(left if r == 0 else ~left, qp, jnp.zeros_like(qp))
                sink = sink_ref[2 * p + r]
                s_ctx = _mm_nt(qm, kc)
                m = jnp.maximum(jnp.max(s_ctx, -1, keepdims=True), sink)
                if latent:
                    s_loc = jnp.where(ok, _mm_nt(qm, kl), -jnp.inf)
                    m = jnp.maximum(m, jnp.max(s_loc, -1, keepdims=True))
                p_ctx = jnp.exp(s_ctx - m)
                l = jnp.sum(p_ctx, -1, keepdims=True) + jnp.exp(sink - m)
                o = _mm(p_ctx, vc)
                if latent:
                    p_loc = jnp.exp(s_loc - m)
                    l = l + jnp.sum(p_loc, -1, keepdims=True)
                    o = o + _mm(p_loc, vl)
                outs.append(o / l)
            o_ref[0, :, p * LANES:(p + 1) * LANES] = jnp.where(left, outs[0], outs[1]).astype(o_ref.dtype)

    @pl.when(i < ctx_blocks)
    def _():
        run(False)

    @pl.when(i >= ctx_blocks)
    def _():
        run(True)


def _gqa_call(gq, gkv, sink, B, S, C):
    qb_rows = 128
    out = pl.pallas_call(
        functools.partial(_gqa_kernel, n_ctx=C, n_all=S),
        grid=(B, S // qb_rows),
        in_specs=[pl.BlockSpec(memory_space=pltpu.SMEM),
                  pl.BlockSpec((1, qb_rows, 512), lambda b, i: (b, i, 0)),
                  pl.BlockSpec((1, S, 512), lambda b, i: (b, 0, 0))],
        out_specs=pl.BlockSpec((1, qb_rows, 512), lambda b, i: (b, i, 0)),
        out_shape=jax.ShapeDtypeStruct((B, S, 512), MXU_DTYPE),
        compiler_params=_params("parallel", "arbitrary"),
        name="gqa_attn",
    )(sink, gq.reshape(B, S, 512), gkv.reshape(B, S, 512))
    return out.reshape(B * S, 512)


DN_TR = 256
DN_HALO = 16
DN_CPT = DN_TR // DN_CHUNK


def _stack(x, left):
    z = jnp.zeros_like(x)
    return jnp.concatenate([jnp.where(left, x, z), jnp.where(left, z, x)], axis=1)


def _dnlocal_kernel(main_ref, prev_ref, next_ref, ab_ref, conv_ref, gp_ref, bd_ref, trif_ref, trib_ref,
                    eg_ref, eb_ref,
                    u_out, w_out, qg_out, kg_out, qk_out, gl_out, pad_ref, *, ctx_tiles, n_tiles):
    i = pl.program_id(1)
    tr = DN_TR
    first = (i == 0) | (i == ctx_tiles)
    last = (i == ctx_tiles - 1) | (i == n_tiles - 1)
    xp = prev_ref[0].astype(F32)
    xn = next_ref[0].astype(F32)
    pad_ref[0:DN_HALO, :] = jnp.where(first, jnp.zeros_like(xp), xp)
    pad_ref[DN_HALO:DN_HALO + tr, :] = main_ref[0].astype(F32)
    pad_ref[DN_HALO + tr:, :] = jnp.where(last, jnp.zeros_like(xn), xn)
    y = jnp.zeros((tr, 3 * DN_WIDTH), F32)
    for t in range(DN_CONV):
        y = y + conv_ref[t:t + 1, :] * pad_ref[pl.ds(DN_HALO - DN_CONV // 2 + t, tr), :]
    y = _silu(y)
    q, k, v = y[:, 0:512], y[:, 512:1024], y[:, 1024:1536]
    bd = bd_ref[...]
    q = q * lax.rsqrt(_exact_mm(q * q, bd) + 1e-6) * (DN_HEAD_DIM ** -0.5)
    k = k * lax.rsqrt(_exact_mm(k * k, bd) + 1e-6)

    ab = ab_ref[0]
    g = -jnp.exp(gp_ref[0:1, :]) * jax.nn.softplus(ab + gp_ref[1:2, :])
    beta = jax.nn.sigmoid(ab)
    lane = lax.broadcasted_iota(jnp.int32, (tr, LANES), 1)
    gc = jnp.where(lane < DN_HEADS, _exact_mm_left(trif_ref[...], g), _exact_mm_left(trib_ref[...], g))
    gcx_all = _exact_mm(gc, eg_ref[...])
    bx_all = _exact_mm(beta, eb_ref[...])

    c = DN_CPT
    lane3 = lax.broadcasted_iota(jnp.int32, (1, 1, LANES), 2)
    left = (lane3 % LANES) < DN_HEAD_DIM
    tpos = lane3 % DN_HEAD_DIM
    lane6 = lax.broadcasted_iota(jnp.int32, (1, 1, 2 * LANES), 2)
    left6 = (lane6 % LANES) < DN_HEAD_DIM
    ri = lax.broadcasted_iota(jnp.int32, (1, DN_CHUNK, LANES), 1)
    cj = lax.broadcasted_iota(jnp.int32, (1, DN_CHUNK, LANES), 2) % DN_HEAD_DIM
    one = jnp.ones((), F32)
    zero = jnp.zeros((), F32)

    for d in range(2):
        incl = (ri >= cj) if d == 0 else (ri <= cj)
        strict = (ri > cj) if d == 0 else (ri < cj)
        for j in range(4):
            off = d * 512 + j * LANES
            gcx = gcx_all[:, off:off + LANES].reshape(c, DN_CHUNK, LANES)
            bx = bx_all[:, off:off + LANES].reshape(c, DN_CHUNK, LANES)
            qp = q[:, j * LANES:(j + 1) * LANES].reshape(c, DN_CHUNK, LANES)
            kp = k[:, j * LANES:(j + 1) * LANES].reshape(c, DN_CHUNK, LANES)
            vp = v[:, j * LANES:(j + 1) * LANES].reshape(c, DN_CHUNK, LANES)
            gl = gcx[:, DN_CHUNK - 1:DN_CHUNK, :] if d == 0 else gcx[:, 0:1, :]
            eg = jnp.exp(gcx)
            kb = kp * bx
            kst = _stack(kp, left)
            kk = _bmm_nt(kb, kst)
            qk = _bmm_nt(qp, kst)
            hi, mid, lo = _split3(gcx)
            a6 = jnp.where(tpos == 0, hi, jnp.where(tpos == 1, mid, jnp.where(tpos == 2, lo,
                           jnp.where(tpos < 6, one, zero))))
            b6 = jnp.where(tpos < 3, one, jnp.where(tpos == 3, -hi, jnp.where(tpos == 4, -mid,
                           jnp.where(tpos == 5, -lo, zero))))
            diff = _bmm_nt(a6, _stack(b6, left))
            dm = jnp.exp(jnp.where(incl, diff, -jnp.inf))
            x = -jnp.where(strict, kk * dm, zero)
            r = x
            for _ in range(5):
                x = _bmm(x, _stack(x, left))
                r = r + x + _bmm(r, _stack(x, left))
            rhs = jnp.concatenate([vp * bx, kb * eg], axis=-1)
            sol = rhs + _bmm(r, _stack(rhs, left6))
            sl = slice(j * LANES, (j + 1) * LANES)
            u_out[0, d, :, sl] = sol[:, :, 0:LANES].reshape(tr, LANES)
            w_out[0, d, :, sl] = sol[:, :, LANES:].reshape(tr, LANES).astype(w_out.dtype)
            qg_out[0, d, :, sl] = (qp * eg).reshape(tr, LANES).astype(qg_out.dtype)
            kg_out[0, d, :, sl] = (kp * jnp.exp(gl - gcx)).reshape(tr, LANES).astype(kg_out.dtype)
            qk_out[0, d, :, sl] = (qk * dm).reshape(tr, LANES).astype(qk_out.dtype)
            gl_out[0, d, :, :, sl] = jnp.exp(gl)


def _dn_constants():
    idx = np.arange(DN_TR)
    same = (idx[:, None] // DN_CHUNK) == (idx[None, :] // DN_CHUNK)
    trif = (same & (idx[None, :] <= idx[:, None])).astype(np.float32)
    trib = (same & (idx[None, :] >= idx[:, None])).astype(np.float32)
    h = np.arange(512)
    bd = ((h[:, None] // DN_HEAD_DIM) == (h[None, :] // DN_HEAD_DIM)).astype(np.float32)
    col = np.arange(LANES)[:, None]
    out = np.arange(1024)[None, :]
    unit = (out // 512) * DN_HEADS + (out % 512) // DN_HEAD_DIM
    eg = (col == unit).astype(np.float32)
    eb = (col == unit + 2 * DN_HEADS).astype(np.float32)
    return tuple(jnp.asarray(a, MXU_DTYPE) for a in (bd, trif, trib, eg, eb))


def _dnlocal_call(dqkv, ab, conv_w, gp, consts, B, S, C):
    tr = DN_TR
    n_tiles = S // tr
    hb = tr // DN_HALO
    n_hblk = S // DN_HALO
    bd, trif, trib, eg, eb = consts
    const2 = lambda b, i: (0, 0)
    big = lambda b, i: (b, 0, i, 0)
    act = MXU_DTYPE
    shp = (B, 2, S, 512)
    return pl.pallas_call(
        functools.partial(_dnlocal_kernel, ctx_tiles=C // tr, n_tiles=n_tiles),
        grid=(B, n_tiles),
        in_specs=[pl.BlockSpec((1, tr, 1536), lambda b, i: (b, i, 0)),
                  pl.BlockSpec((1, DN_HALO, 1536), lambda b, i: (b, jnp.maximum(i * hb - 1, 0), 0)),
                  pl.BlockSpec((1, DN_HALO, 1536), lambda b, i: (b, jnp.minimum((i + 1) * hb, n_hblk - 1), 0)),
                  pl.BlockSpec((1, tr, LANES), lambda b, i: (b, i, 0)),
                  pl.BlockSpec((8, 1536), const2),
                  pl.BlockSpec((8, LANES), const2),
                  pl.BlockSpec((512, 512), const2),
                  pl.BlockSpec((tr, tr), const2),
                  pl.BlockSpec((tr, tr), const2),
                  pl.BlockSpec((LANES, 1024), const2),
                  pl.BlockSpec((LANES, 1024), const2)],
        out_specs=[pl.BlockSpec((1, 2, tr, 512), big)] * 5
                  + [pl.BlockSpec((1, 2, DN_CPT, 1, 512), lambda b, i: (b, 0, i, 0, 0))],
        out_shape=[jax.ShapeDtypeStruct(shp, F32)] + [jax.ShapeDtypeStruct(shp, act)] * 4
                  + [jax.ShapeDtypeStruct((B, 2, S // DN_CHUNK, 1, 512), F32)],
        scratch_shapes=[pltpu.VMEM((tr + 2 * DN_HALO, 1536), F32)],
        compiler_params=_params("parallel", "parallel"),
        name="dn_local",
    )(dqkv.reshape(B, S, 1536), dqkv.reshape(B, S, 1536), dqkv.reshape(B, S, 1536), ab.reshape(B, S, LANES),
      conv_w, gp, bd, trif, trib, eg, eb)


def _dnscan_kernel(uf, wf, qgf, kgf, qkf, glf, ub, wb, qgb, kgb, qkb, glb, of_out, ob_out, s_ref):
    n = pl.program_id(1)

    @pl.when(n == 0)
    def _():
        s_ref[...] = jnp.zeros_like(s_ref)

    lane = lax.broadcasted_iota(jnp.int32, (1, LANES), 1)
    left = lane < DN_HEAD_DIM
    row = lax.broadcasted_iota(jnp.int32, (LANES, LANES), 0)
    col = lax.broadcasted_iota(jnp.int32, (LANES, LANES), 1)
    same_head = (row < DN_HEAD_DIM) == (col < DN_HEAD_DIM)
    dirs = ((uf, wf, qgf, kgf, qkf, glf, of_out), (ub, wb, qgb, kgb, qkb, glb, ob_out))
    for d, (u_r, w_r, qg_r, kg_r, qk_r, gl_r, o_r) in enumerate(dirs):
        for j in range(4):
            sl = slice(j * LANES, (j + 1) * LANES)
            st = s_ref[d * 4 + j]
            vn = u_r[0, 0, :, sl] - _mm(w_r[0, 0, :, sl], st)
            z = jnp.zeros_like(vn)
            vst = jnp.concatenate([jnp.where(left, vn, z), jnp.where(left, z, vn)], axis=0)
            o_r[0, :, sl] = _mm(qg_r[0, 0, :, sl], st) + _mm(qk_r[0, 0, :, sl], vst)
            upd = _mm_tn(kg_r[0, 0, :, sl], vn)
            s_ref[d * 4 + j] = st * gl_r[0, 0, 0, :, sl] + jnp.where(same_head, upd, jnp.zeros_like(upd))


def _dnscan_call(local, B, S, C):
    u, w, qg, kg, qk, gl = local
    nch, nc = S // DN_CHUNK, C // DN_CHUNK

    def bidx(n):
        return jnp.where(n < nc, nc - 1 - n, nch - 1 + nc - n)

    fspec = pl.BlockSpec((1, 1, DN_CHUNK, 512), lambda b, n: (b, 0, n, 0))
    bspec = pl.BlockSpec((1, 1, DN_CHUNK, 512), lambda b, n: (b, 1, bidx(n), 0))
    fgl = pl.BlockSpec((1, 1, 1, 1, 512), lambda b, n: (b, 0, n, 0, 0))
    bgl = pl.BlockSpec((1, 1, 1, 1, 512), lambda b, n: (b, 1, bidx(n), 0, 0))
    return pl.pallas_call(
        _dnscan_kernel,
        grid=(B, nch),
        in_specs=[fspec] * 5 + [fgl] + [bspec] * 5 + [bgl],
        out_specs=[pl.BlockSpec((1, DN_CHUNK, 512), lambda b, n: (b, n, 0)),
                   pl.BlockSpec((1, DN_CHUNK, 512), lambda b, n: (b, bidx(n), 0))],
        out_shape=[jax.ShapeDtypeStruct((B, S, 512), F32)] * 2,
        scratch_shapes=[pltpu.VMEM((8, LANES, LANES), F32)],
        compiler_params=_params("parallel", "arbitrary"),
        name="dn_scan",
    )(u, w, qg, kg, qk, gl, u, w, qg, kg, qk, gl)


def _merge_kernel(x_ref, ga_ref, shf_ref, scf_ref, omla_ref, of_ref, ob_ref, dz_ref, ogqa_ref, gates_ref,
                  wo1_ref, wo2_ref, wo3_ref, wout_ref, dnn_ref, bd_ref, lng_ref, lnb_ref, wr_ref, rb_ref,
                  x1_out, uf_out, wtok_out):
    o = of_ref[...] + ob_ref[...]
    ms = _exact_mm(o * o, bd_ref[...]) * (1.0 / DN_HEAD_DIM)
    dn = o * lax.rsqrt(ms + NORM_EPS) * dnn_ref[...] * _silu(dz_ref[...].astype(F32))
    g1 = gates_ref[:, 0:1024].astype(F32)
    g2 = gates_ref[:, 1024:2048].astype(F32)
    g3 = gates_ref[:, 2048:3072].astype(F32)
    m = (g1 * _mm(omla_ref[...], wo1_ref[...]) + g2 * _mm(dn, wo2_ref[...])
         + g3 * _mm(ogqa_ref[...], wo3_ref[...]))
    y = _mm(m, wout_ref[...])
    x1 = _layernorm(DEEPNORM_ALPHA * x_ref[...] + ga_ref[0] * y, lng_ref[...], lnb_ref[...])
    x1_out[...] = x1
    uf = (x1 * (1.0 + scf_ref[0]) + shf_ref[0]).astype(uf_out.dtype)
    uf_out[...] = uf

    tm = uf.shape[0]
    scores = jax.nn.sigmoid(_mm_nt(wr_ref[...], uf))[0:N_EXPERTS]
    sel = scores + rb_ref[0:N_EXPERTS, :]
    gsz = N_EXPERTS // N_GROUPS
    neg = jnp.full((), -jnp.inf, F32)
    sel3 = sel.reshape(N_GROUPS, gsz, tm)
    mem = lax.broadcasted_iota(jnp.int32, (N_GROUPS, gsz, tm), 1)
    m1 = jnp.max(sel3, 1, keepdims=True)
    i1 = jnp.min(jnp.where(sel3 == m1, mem, gsz), 1, keepdims=True)
    m2 = jnp.max(jnp.where(mem == i1, neg, sel3), 1, keepdims=True)
    gs = (m1 + m2).reshape(N_GROUPS, tm)
    gi = lax.broadcasted_iota(jnp.int32, (N_GROUPS, tm), 0)
    rank = jnp.zeros((N_GROUPS, tm), jnp.int32)
    for gp in range(N_GROUPS):
        other = gs[gp:gp + 1, :]
        beats = (other > gs) | ((other == gs) & (gp < gi))
        rank = rank + beats.astype(jnp.int32)
    gsel = (rank < TOPK_GROUPS).reshape(N_GROUPS, 1, tm)
    cur = jnp.where(gsel, sel3, neg).reshape(N_EXPERTS, tm)
    ei = lax.broadcasted_iota(jnp.int32, (N_EXPERTS, tm), 0)
    chosen = jnp.zeros((N_EXPERTS, tm), jnp.bool_)
    for _ in range(TOP_K):
        mx = jnp.max(cur, 0, keepdims=True)
        ix = jnp.min(jnp.where(cur == mx, ei, N_EXPERTS), 0, keepdims=True)
        pick = ei == ix
        chosen = chosen | pick
        cur = jnp.where(pick, neg, cur)
    w = jnp.where(chosen, scores, jnp.zeros_like(scores))
    w = w / jnp.sum(w, 0, keepdims=True) * ROUTED_SCALE
    wtok_out[...] = jnp.concatenate([w, jnp.zeros((LANES - N_EXPERTS, tm), F32)], axis=0).T


def _merge_call(xs, mod_l, omla, of, ob, dz, ogqa, gates, wo1, wo2, wo3, wout, dnn, bd, lng, lnb, wr, rb, B, S, C):
    T, D = xs.shape
    tpb, ctiles = S // TM, C // TM
    row = lambda i: (i, 0)
    const = lambda i: (0, 0)
    modspec = lambda k: pl.BlockSpec((1, 1, D), _mod_index(tpb, ctiles, k))
    return pl.pallas_call(
        _merge_kernel,
        grid=(T // TM,),
        in_specs=[pl.BlockSpec((TM, D), row), modspec(2), modspec(3), modspec(4),
                  pl.BlockSpec((TM, 512), row), pl.BlockSpec((TM, 512), row), pl.BlockSpec((TM, 512), row),
                  pl.BlockSpec((TM, 512), row), pl.BlockSpec((TM, 512), row), pl.BlockSpec((TM, 3072), row),
                  pl.BlockSpec((512, D), const), pl.BlockSpec((512, D), const), pl.BlockSpec((512, D), const),
                  pl.BlockSpec((D, D), const), pl.BlockSpec((1, 512), const), pl.BlockSpec((512, 512), const),
                  pl.BlockSpec((1, D), const), pl.BlockSpec((1, D), const),
                  pl.BlockSpec((LANES, D), const), pl.BlockSpec((LANES, 1), const)],
        out_specs=[pl.BlockSpec((TM, D), row), pl.BlockSpec((TM, D), row), pl.BlockSpec((TM, LANES), row)],
        out_shape=[jax.ShapeDtypeStruct((T, D), F32), jax.ShapeDtypeStruct((T, D), MXU_DTYPE),
                   jax.ShapeDtypeStruct((T, LANES), F32)],
        compiler_params=_params("parallel"),
        name="merge_norm_route",
    )(xs, mod_l, mod_l, mod_l, omla, of.reshape(T, 512), ob.reshape(T, 512), dz, ogqa, gates,
      wo1, wo2, wo3, wout, dnn, bd, lng, lnb, wr, rb)


MOE_TM = 1024


def _moe_kernel(u_ref, wtok_ref, wg_ref, wu_ref, wd_ref, sg_ref, su_ref, sd_ref, f_out):
    e = pl.program_id(1)
    x = u_ref[...]

    @pl.when(e == 0)
    def _():
        hs = _silu(_mm(x, sg_ref[...])) * _mm(x, su_ref[...])
        f_out[...] = _mm(hs, sd_ref[...])

    lane = lax.broadcasted_iota(jnp.int32, wtok_ref.shape, 1)
    wt = wtok_ref[...]
    col = jnp.sum(jnp.where(lane == e, wt, jnp.zeros_like(wt)), axis=1, keepdims=True)
    h = _silu(_mm(x, wg_ref[0])) * _mm(x, wu_ref[0]) * col
    f_out[...] += _mm(h, wd_ref[0])


def _moe_call(uf, wtok, wg, wu, wd, sg, su, sd):
    T, D = uf.shape
    tm = MOE_TM if T % MOE_TM == 0 else TM
    row = lambda i, e: (i, 0)
    const = lambda i, e: (0, 0)
    return pl.pallas_call(
        _moe_kernel,
        grid=(T // tm, N_EXPERTS),
        in_specs=[pl.BlockSpec((tm, D), row), pl.BlockSpec((tm, LANES), row),
                  pl.BlockSpec((1, D, EXPERT_DIM), lambda i, e: (e, 0, 0)),
                  pl.BlockSpec((1, D, EXPERT_DIM), lambda i, e: (e, 0, 0)),
                  pl.BlockSpec((1, EXPERT_DIM, D), lambda i, e: (e, 0, 0)),
                  pl.BlockSpec((D, SHARED_DIM), const), pl.BlockSpec((D, SHARED_DIM), const),
                  pl.BlockSpec((SHARED_DIM, D), const)],
        out_specs=pl.BlockSpec((tm, D), row),
        out_shape=jax.ShapeDtypeStruct((T, D), F32),
        compiler_params=_params("parallel", "arbitrary"),
        name="moe_dense",
    )(uf, wtok, wg, wu, wd, sg, su, sd)


def _ln2_kernel(x_ref, f_ref, gf_ref, g_ref, b_ref, o_ref):
    o_ref[...] = _layernorm(DEEPNORM_ALPHA * x_ref[...] + gf_ref[0] * f_ref[...], g_ref[...], b_ref[...])


def _ln2_call(x1, f, mod_l, g, b, B, S, C):
    T, D = x1.shape
    tpb, ctiles = S // TM, C // TM
    row = lambda i: (i, 0)
    const = lambda i: (0, 0)
    return pl.pallas_call(
        _ln2_kernel,
        grid=(T // TM,),
        in_specs=[pl.BlockSpec((TM, D), row), pl.BlockSpec((TM, D), row),
                  pl.BlockSpec((1, 1, D), _mod_index(tpb, ctiles, 5)),
                  pl.BlockSpec((1, D), const), pl.BlockSpec((1, D), const)],
        out_specs=pl.BlockSpec((TM, D), row),
        out_shape=jax.ShapeDtypeStruct((T, D), F32),
        compiler_params=_params("parallel"),
        name="post_norm2",
    )(x1, f, mod_l, g, b)


def _rot_cols(w, half):
    return jnp.concatenate([-w[:, half:], w[:, :half]], axis=1)


def _prep_w_in(w):
    d = w.shape[0]
    offs = np.cumsum((0,) + IN_SIZES)
    cq, ckv, kr, dqkv, da, db, dz, gq, gk, gv, gates = (w[:, offs[t]:offs[t + 1]] for t in range(len(IN_SIZES)))
    z = lambda n: jnp.zeros((d, n), w.dtype)
    krg = jnp.concatenate([z(64), kr, z(32)], 1)
    krr = jnp.concatenate([z(64), _rot_cols(kr, MLA_ROPE // 2), z(32)], 1)
    ab = jnp.concatenate([da, db, z(LANES - 4 * DN_HEADS)], 1)
    hd = GQA_HEAD_DIM
    gq_rot = jnp.concatenate([_rot_cols(gq[:, h * hd:(h + 1) * hd], hd // 2) for h in range(GQA_HEADS)], 1)
    dup = lambda t: jnp.concatenate([t[:, 0:hd], t[:, 0:hd], t[:, hd:2 * hd], t[:, hd:2 * hd]], 1)
    gk_rot = jnp.concatenate([_rot_cols(gk[:, h * hd:(h + 1) * hd], hd // 2) for h in range(GQA_KV_HEADS)], 1)
    cat = jnp.concatenate([cq, ckv, krg, krr, dqkv, ab, dz, gq, gq_rot, dup(gk), dup(gk_rot), dup(gv), gates], 1)
    assert cat.shape[1] == NZ
    return cat.astype(MXU_DTYPE)


def _prep_w_uq(w):
    d = w.shape[0]
    hw = MLA_NOPE + MLA_ROPE
    a, b = [], []
    for h in range(MLA_HEADS):
        wh = w[:, h * hw:(h + 1) * hw]
        a += [wh, jnp.zeros((d, LANES - hw), w.dtype)]
        b += [jnp.zeros((d, MLA_NOPE), w.dtype), _rot_cols(wh[:, MLA_NOPE:], MLA_ROPE // 2),
              jnp.zeros((d, LANES - hw), w.dtype)]
    return jnp.concatenate(a + b, 1).astype(MXU_DTYPE)


def _prep_w_ukv(w):
    d = w.shape[0]
    hw = MLA_NOPE + MLA_V
    kpart, vpart = [], []
    for h in range(MLA_HEADS):
        wh = w[:, h * hw:(h + 1) * hw]
        kpart += [wh[:, :MLA_NOPE], jnp.zeros((d, LANES - MLA_NOPE), w.dtype)]
        vpart += [wh[:, MLA_NOPE:]]
    return jnp.concatenate(kpart + vpart, 1).astype(MXU_DTYPE)


def _rope_tables(n_rows, C):
    row = jnp.repeat(jnp.arange(n_rows, dtype=F32), GRID_W)
    col = jnp.tile(jnp.arange(GRID_W, dtype=F32), n_rows)

    def angles(dim):
        n = dim // 4
        inv = ROPE_BASE ** (-jnp.arange(n, dtype=F32) / n)
        return jnp.concatenate([row[:, None] * inv, col[:, None] * inv], axis=-1)

    def with_ctx(cos, sin):
        return (jnp.concatenate([jnp.ones((C, LANES), F32), cos], 0),
                jnp.concatenate([jnp.zeros((C, LANES), F32), sin], 0))

    L = n_rows * GRID_W
    am = angles(MLA_ROPE)
    one, zero = jnp.ones((L, MLA_NOPE), F32), jnp.zeros((L, MLA_NOPE), F32)
    cm = jnp.concatenate([one, jnp.cos(am), jnp.cos(am), one[:, :32]], 1)
    sm = jnp.concatenate([zero, jnp.sin(am), jnp.sin(am), zero[:, :32]], 1)
    ag = angles(GQA_HEAD_DIM)
    cg = jnp.tile(jnp.cos(ag), (1, 4))
    sg = jnp.tile(jnp.sin(ag), (1, 4))
    return with_ctx(cm, sm) + with_ctx(cg, sg)


def kernel(x, c, ctx, c_ctx, w_ada, b_ada, w_in, mla_q_norm, mla_kv_norm, w_uq, w_ukv, dn_conv, dn_a_log, dn_dt_bias, dn_norm, gqa_sink, w_o_mla, w_o_dn, w_o_gqa, w_out, ln1_g, ln1_b, w_router, router_bias, w_exp_gate, w_exp_up, w_exp_down, w_sh_gate, w_sh_up, w_sh_down, ln2_g, ln2_b):
    B, L, D = x.shape
    C = ctx.shape[1]
    S = C + L
    nl = w_in.shape[0]
    assert D == D_MODEL and nl == DEPTH and B <= CTX_MOD_ROW
    assert C % TM == 0 and L % TM == 0 and L % GRID_W == 0 and L >= 3 * WINDOW
    cast = lambda t: t.astype(MXU_DTYPE)

    cc = jnp.zeros((MOD_ROWS, D), F32).at[0:B].set(c).at[CTX_MOD_ROW].set(c_ctx)
    mods = _ada_call(cc, w_ada, b_ada)
    tabs = _rope_tables(L // GRID_W, C)
    dn_consts = _dn_constants()
    xs = jnp.concatenate([ctx, x], axis=1).reshape(B * S, D)

    for l in range(nl):
        mod_l = mods[l].reshape(MOD_ROWS * 6, 1, D)
        q, k, v, dqkv, ab, dz, gq, gkv, gates = _inproj_call(
            xs, mod_l, _prep_w_in(w_in[l]), _prep_w_uq(w_uq[l]), _prep_w_ukv(w_ukv[l]),
            mla_q_norm[l].reshape(1, -1), mla_kv_norm[l].reshape(1, -1), tabs, B, S, C)
        omla = _mla_call(q, k, v, B, S, C)
        conv8 = jnp.zeros((8, 3 * DN_WIDTH), F32).at[0:DN_CONV].set(dn_conv[l])
        gp = (jnp.zeros((8, LANES), F32).at[0, 0:2 * DN_HEADS].set(dn_a_log[l].reshape(-1))
              .at[1, 0:2 * DN_HEADS].set(dn_dt_bias[l].reshape(-1)))
        local = _dnlocal_call(dqkv, ab, conv8, gp, dn_consts, B, S, C)
        of, ob = _dnscan_call(local, B, S, C)
        ogqa = _gqa_call(gq, gkv, gqa_sink[l], B, S, C)
        wr = jnp.zeros((LANES, D), F32).at[0:N_EXPERTS].set(w_router[l].T)
        rb = jnp.zeros((LANES, 1), F32).at[0:N_EXPERTS, 0].set(router_bias[l])
        x1, uf, wtok = _merge_call(
            xs, mod_l, omla, of, ob, dz, ogqa, gates,
            cast(w_o_mla[l]), cast(w_o_dn[l]), cast(w_o_gqa[l]), cast(w_out[l]),
            jnp.tile(dn_norm[l], DN_HEADS).reshape(1, DN_WIDTH), dn_consts[0],
            ln1_g[l].reshape(1, D), ln1_b[l].reshape(1, D), cast(wr), rb, B, S, C)
        f = _moe_call(uf, wtok, cast(w_exp_gate[l]), cast(w_exp_up[l]), cast(w_exp_down[l]),
                      cast(w_sh_gate[l]), cast(w_sh_up[l]), cast(w_sh_down[l]))
        xs = _ln2_call(x1, f, mod_l, ln2_g[l].reshape(1, D), ln2_b[l].reshape(1, D), B, S, C)
    return xs.reshape(B, S, D)[:, C:, :]
```

```python
import functools

import numpy as np
import jax
import jax.numpy as jnp
from jax import lax
from jax.experimental import pallas as pl
from jax.experimental.pallas import tpu as pltpu
from jax.experimental.pallas import tpu_sc as plsc

F32 = jnp.float32
MXU_DTYPE = jnp.bfloat16

D_MODEL = 1024
DEPTH = 4
GRID_W = 64
NORM_EPS = 1e-6
ROPE_BASE = 10000.0
DEEPNORM_ALPHA = (2.0 * DEPTH) ** 0.25

MLA_HEADS = 8
MLA_Q_LORA = 256
MLA_KV_LORA = 128
MLA_NOPE = 64
MLA_ROPE = 32
MLA_V = 64
MLA_SCALE = (MLA_NOPE + MLA_ROPE) ** -0.5

DN_HEADS = 8
DN_HEAD_DIM = 64
DN_WIDTH = DN_HEADS * DN_HEAD_DIM
DN_CONV = 5
DN_CHUNK = 64

GQA_HEADS = 8
GQA_KV_HEADS = 2
GQA_HEAD_DIM = 64
GQA_SCALE = GQA_HEAD_DIM ** -0.5
WINDOW = 128

N_EXPERTS = 64
TOP_K = 8
N_GROUPS = 8
TOPK_GROUPS = 4
EXPERT_DIM = 256
SHARED_DIM = 256
ROUTED_SCALE = 2.5

IN_SIZES = (MLA_Q_LORA, MLA_KV_LORA, MLA_ROPE,
            3 * DN_WIDTH, 2 * DN_HEADS, 2 * DN_HEADS, DN_WIDTH,
            GQA_HEADS * GQA_HEAD_DIM, GQA_KV_HEADS * GQA_HEAD_DIM, GQA_KV_HEADS * GQA_HEAD_DIM,
            3 * D_MODEL)

LANES = 128
TM = 256
MOD_ROWS = 16
CTX_MOD_ROW = 8

OFF_A = 0
OFF_DQKV = 640
OFF_AB = OFF_DQKV + 3 * DN_WIDTH
OFF_DZ = OFF_AB + LANES
OFF_GQ = OFF_DZ + DN_WIDTH
OFF_GK = OFF_GQ + 1024
OFF_GATES = OFF_GK + 768
NZ = OFF_GATES + 3 * D_MODEL

VMEM_LIMIT = 56 * 1024 * 1024


def _mm(a, b):
    return jnp.dot(a.astype(MXU_DTYPE), b.astype(MXU_DTYPE), preferred_element_type=F32)


def _mm_nt(a, b):
    return lax.dot_general(a.astype(MXU_DTYPE), b.astype(MXU_DTYPE), (((1,), (1,)), ((), ())),
                           preferred_element_type=F32)


def _mm_tn(a, b):
    return lax.dot_general(a.astype(MXU_DTYPE), b.astype(MXU_DTYPE), (((0,), (0,)), ((), ())),
                           preferred_element_type=F32)


def _bmm(a, b):
    return jnp.einsum('cik,ckj->cij', a.astype(MXU_DTYPE), b.astype(MXU_DTYPE), preferred_element_type=F32)


def _bmm_nt(a, b):
    return jnp.einsum('cik,cjk->cij', a.astype(MXU_DTYPE), b.astype(MXU_DTYPE), preferred_element_type=F32)


def _split3(x):
    hi = x.astype(jnp.bfloat16).astype(F32)
    r = x - hi
    mid = r.astype(jnp.bfloat16).astype(F32)
    lo = (r - mid).astype(jnp.bfloat16).astype(F32)
    return hi, mid, lo


def _exact_mm(x, m01):
    hi, mid, lo = _split3(x)
    return _mm(hi, m01) + _mm(mid, m01) + _mm(lo, m01)


def _exact_mm_left(m01, x):
    hi, mid, lo = _split3(x)
    return _mm(m01, hi) + _mm(m01, mid) + _mm(m01, lo)


def _silu(x):
    return x * jax.nn.sigmoid(x)


def _layernorm(v, g, b):
    mu = jnp.mean(v, -1, keepdims=True)
    d = v - mu
    var = jnp.mean(d * d, -1, keepdims=True)
    return d * lax.rsqrt(var + NORM_EPS) * g + b


def _mod_index(tiles_per_b, ctx_tiles, k):
    def index(i):
        row = jnp.where((i % tiles_per_b) < ctx_tiles, CTX_MOD_ROW, i // tiles_per_b)
        return (row * 6 + k, 0, 0)
    return index


def _params(*sem):
    return pltpu.CompilerParams(dimension_semantics=sem, vmem_limit_bytes=VMEM_LIMIT)


def _ada_kernel(c_ref, w_ref, b_ref, o_ref):
    o_ref[0] = _mm(_silu(c_ref[...]), w_ref[0]) + b_ref[0]


def _ada_call(cc, w_ada, b_ada):
    nl, d, n6 = w_ada.shape
    tn = 1536
    return pl.pallas_call(
        _ada_kernel,
        grid=(nl, n6 // tn),
        in_specs=[pl.BlockSpec((MOD_ROWS, d), lambda l, j: (0, 0)),
                  pl.BlockSpec((1, d, tn), lambda l, j: (l, 0, j)),
                  pl.BlockSpec((1, 1, tn), lambda l, j: (l, 0, j))],
        out_specs=pl.BlockSpec((1, MOD_ROWS, tn), lambda l, j: (l, 0, j)),
        out_shape=jax.ShapeDtypeStruct((nl, MOD_ROWS, n6), F32),
        compiler_params=_params("parallel", "parallel"),
        name="ada_mod",
    )(cc, w_ada, b_ada.reshape(nl, 1, n6))


def _inproj_kernel(x_ref, sh_ref, sc_ref, w_ref, wuq_ref, wukv_ref, qn_ref, kvn_ref,
                   cm_ref, sm_ref, cg_ref, sg_ref,
                   q_out, k_out, v_out, dqkv_out, ab_out, dz_out, gq_out, gkv_out, gates_out):
    u = (x_ref[...] * (1.0 + sc_ref[0]) + sh_ref[0]).astype(MXU_DTYPE)

    def z(off, width):
        return jnp.dot(u, w_ref[:, off:off + width], preferred_element_type=F32)

    def rms(v, g):
        return v * lax.rsqrt(jnp.mean(v * v, -1, keepdims=True) + NORM_EPS) * g

    cm, sm, cg, sg = cm_ref[...], sm_ref[...], cg_ref[...], sg_ref[...]

    za = z(OFF_A, 640)
    qq = _mm(rms(za[:, 0:256], qn_ref[...]), wuq_ref[...])
    kvv = _mm(rms(za[:, 256:384], kvn_ref[...]), wukv_ref[...])
    k_rope = za[:, 384:512] * cm + za[:, 512:640] * sm
    for h in range(MLA_HEADS):
        sl = slice(h * LANES, (h + 1) * LANES)
        qa = qq[:, h * LANES:(h + 1) * LANES]
        qb = qq[:, 1024 + h * LANES:1024 + (h + 1) * LANES]
        q_out[:, sl] = ((qa * cm + qb * sm) * MLA_SCALE).astype(q_out.dtype)
        k_out[:, sl] = (kvv[:, sl] + k_rope).astype(k_out.dtype)
    v_out[...] = kvv[:, 1024:1536].astype(v_out.dtype)

    for t in range(3):
        dqkv_out[:, t * 512:(t + 1) * 512] = z(OFF_DQKV + t * 512, 512).astype(dqkv_out.dtype)
    ab_out[...] = z(OFF_AB, LANES)
    dz_out[...] = z(OFF_DZ, DN_WIDTH).astype(dz_out.dtype)

    zq = z(OFF_GQ, 1024)
    for p in range(4):
        sl = slice(p * LANES, (p + 1) * LANES)
        gq_out[:, sl] = ((zq[:, sl] * cg + zq[:, 512 + p * LANES:512 + (p + 1) * LANES] * sg)
                         * GQA_SCALE).astype(gq_out.dtype)
    zk = z(OFF_GK, 768)
    for j in range(2):
        sl = slice(j * LANES, (j + 1) * LANES)
        gkv_out[:, sl] = (zk[:, sl] * cg + zk[:, 256 + j * LANES:256 + (j + 1) * LANES] * sg).astype(gkv_out.dtype)
    gkv_out[:, 256:512] = zk[:, 512:768].astype(gkv_out.dtype)

    for t in range(3):
        gates_out[:, t * 1024:(t + 1) * 1024] = jax.nn.sigmoid(z(OFF_GATES + t * 1024, 1024)).astype(gates_out.dtype)


def _inproj_call(xs, mod_l, w_cat, wuq_cat, wukv_cat, qn, kvn, tabs, B, S, C):
    T, D = xs.shape
    tpb, ctiles = S // TM, C // TM
    act = MXU_DTYPE
    row = lambda i: (i, 0)
    const = lambda i: (0, 0)
    tab = lambda i: (i % tpb, 0)
    widths = (1024, 1024, 512, 1536, LANES, 512, 512, 512, 3072)
    dtypes = (act, act, act, act, F32, act, act, act, act)
    return pl.pallas_call(
        _inproj_kernel,
        grid=(T // TM,),
        in_specs=[pl.BlockSpec((TM, D), row),
                  pl.BlockSpec((1, 1, D), _mod_index(tpb, ctiles, 0)),
                  pl.BlockSpec((1, 1, D), _mod_index(tpb, ctiles, 1)),
                  pl.BlockSpec((D, NZ), const),
                  pl.BlockSpec((MLA_Q_LORA, 2048), const),
                  pl.BlockSpec((MLA_KV_LORA, 1536), const),
                  pl.BlockSpec((1, MLA_Q_LORA), const),
                  pl.BlockSpec((1, MLA_KV_LORA), const)]
                 + [pl.BlockSpec((TM, LANES), tab)] * 4,
        out_specs=[pl.BlockSpec((TM, w), row) for w in widths],
        out_shape=[jax.ShapeDtypeStruct((T, w), dt) for w, dt in zip(widths, dtypes)],
        compiler_params=_params("parallel"),
        name="in_proj",
    )(xs, mod_l, mod_l, w_cat, wuq_cat, wukv_cat, qn, kvn, *tabs)


def _mla_kernel(q_ref, k_ref, v_ref, o_ref, *, n_ctx, n_all):
    i = pl.program_id(2)
    tq = q_ref.shape[1]
    left = lax.broadcasted_iota(jnp.int32, (tq, LANES), 1) < MLA_V

    def attend(nk):
        outs = []
        for hh in range(2):
            q = q_ref[0, :, hh * LANES:(hh + 1) * LANES]
            k = k_ref[0, 0:nk, hh * LANES:(hh + 1) * LANES]
            s = _mm_nt(q, k)
            p = jnp.exp(s - jnp.max(s, -1, keepdims=True))
            l = jnp.sum(p, -1, keepdims=True)
            outs.append(_mm(p, v_ref[0, 0:nk, :]) / l)
        o_ref[0] = jnp.where(left, outs[0], outs[1]).astype(o_ref.dtype)

    ctx_tiles = n_ctx // tq

    @pl.when(i < ctx_tiles)
    def _():
        attend(n_ctx)

    @pl.when(i >= ctx_tiles)
    def _():
        attend(n_all)


def _mla_call(q, k, v, B, S, C):
    tq = 256
    q3, k3, v3 = q.reshape(B, S, 1024), k.reshape(B, S, 1024), v.reshape(B, S, 512)
    out = pl.pallas_call(
        functools.partial(_mla_kernel, n_ctx=C, n_all=S),
        grid=(B, MLA_HEADS // 2, S // tq),
        in_specs=[pl.BlockSpec((1, tq, 2 * LANES), lambda b, j, i: (b, i, j)),
                  pl.BlockSpec((1, S, 2 * LANES), lambda b, j, i: (b, 0, j)),
                  pl.BlockSpec((1, S, LANES), lambda b, j, i: (b, 0, j))],
        out_specs=pl.BlockSpec((1, tq, LANES), lambda b, j, i: (b, i, j)),
        out_shape=jax.ShapeDtypeStruct((B, S, 512), MXU_DTYPE),
        compiler_params=_params("parallel", "parallel", "arbitrary"),
        name="mla_attn",
    )(q3, k3, v3)
    return out.reshape(B * S, 512)


def _gqa_kernel(sink_ref, q_ref, kv_ref, o_ref, *, n_ctx, n_all):
    i = pl.program_id(1)
    qb_rows = q_ref.shape[1]
    span = qb_rows + 2 * WINDOW
    ctx_blocks = n_ctx // qb_rows
    lane = lax.broadcasted_iota(jnp.int32, (qb_rows, LANES), 1)
    left = lane < GQA_HEAD_DIM

    def run(latent):
        if latent:
            qb = i - ctx_blocks
            ws = pl.multiple_of(jnp.minimum((qb + 1) * qb_rows, n_all - span), qb_rows)
            q_pos = qb * qb_rows + lax.broadcasted_iota(jnp.int32, (qb_rows, span), 0)
            k_pos = ws - n_ctx + lax.broadcasted_iota(jnp.int32, (qb_rows, span), 1)
            ok = (jnp.abs(k_pos - q_pos) <= WINDOW) & (k_pos >= 0)
        for p in range(4):
            j = p // 2
            qp = q_ref[0, :, p * LANES:(p + 1) * LANES]
            kc = kv_ref[0, 0:n_ctx, j * LANES:(j + 1) * LANES]
            vc = kv_ref[0, 0:n_ctx, 256 + j * LANES:256 + (j + 1) * LANES]
            if latent:
                kl = kv_ref[0, pl.ds(ws, span), j * LANES:(j + 1) * LANES]
                vl = kv_ref[0, pl.ds(ws, span), 256 + j * LANES:256 + (j + 1) * LANES]
            outs = []
            for r in range(2):
                qm = jnp.where(left if r == 0 else ~left, qp, jnp.zeros_like(qp))
                sink = sink_ref[2 * p + r]
                s_ctx = _mm_nt(qm, kc)
                m = jnp.maximum(jnp.max(s_ctx, -1, keepdims=True), sink)
                if latent:
                    s_loc = jnp.where(ok, _mm_nt(qm, kl), -jnp.inf)
                    m = jnp.maximum(m, jnp.max(s_loc, -1, keepdims=True))
                p_ctx = jnp.exp(s_ctx - m)
                l = jnp.sum(p_ctx, -1, keepdims=True) + jnp.exp(sink - m)
                o = _mm(p_ctx, vc)
                if latent:
                    p_loc = jnp.exp(s_loc - m)
                    l = l + jnp.sum(p_loc, -1, keepdims=True)
                    o = o + _mm(p_loc, vl)
                outs.append(o / l)
            o_ref[0, :, p * LANES:(p + 1) * LANES] = jnp.where(left, outs[0], outs[1]).astype(o_ref.dtype)

    @pl.when(i < ctx_blocks)
    def _():
        run(False)

    @pl.when(i >= ctx_blocks)
    def _():
        run(True)


def _gqa_call(gq, gkv, sink, B, S, C):
    qb_rows = 128
    out = pl.pallas_call(
        functools.partial(_gqa_kernel, n_ctx=C, n_all=S),
        grid=(B, S // qb_rows),
        in_specs=[pl.BlockSpec(memory_space=pltpu.SMEM),
                  pl.BlockSpec((1, qb_rows, 512), lambda b, i: (b, i, 0)),
                  pl.BlockSpec((1, S, 512), lambda b, i: (b, 0, 0))],
        out_specs=pl.BlockSpec((1, qb_rows, 512), lambda b, i: (b, i, 0)),
        out_shape=jax.ShapeDtypeStruct((B, S, 512), MXU_DTYPE),
        compiler_params=_params("parallel", "arbitrary"),
        name="gqa_attn",
    )(sink, gq.reshape(B, S, 512), gkv.reshape(B, S, 512))
    return out.reshape(B * S, 512)


DN_TR = 256
DN_HALO = 16
DN_CPT = DN_TR // DN_CHUNK


def _stack(x, left):
    z = jnp.zeros_like(x)
    return jnp.concatenate([jnp.where(left, x, z), jnp.where(left, z, x)], axis=1)


def _dnlocal_kernel(main_ref, prev_ref, next_ref, ab_ref, conv_ref, gp_ref, bd_ref, trif_ref, trib_ref,
                    eg_ref, eb_ref,
                    u_out, w_out, qg_out, kg_out, qk_out, gl_out, pad_ref, *, ctx_tiles, n_tiles):
    i = pl.program_id(1)
    tr = DN_TR
    first = (i == 0) | (i == ctx_tiles)
    last = (i == ctx_tiles - 1) | (i == n_tiles - 1)
    xp = prev_ref[0].astype(F32)
    xn = next_ref[0].astype(F32)
    pad_ref[0:DN_HALO, :] = jnp.where(first, jnp.zeros_like(xp), xp)
    pad_ref[DN_HALO:DN_HALO + tr, :] = main_ref[0].astype(F32)
    pad_ref[DN_HALO + tr:, :] = jnp.where(last, jnp.zeros_like(xn), xn)
    y = jnp.zeros((tr, 3 * DN_WIDTH), F32)
    for t in range(DN_CONV):
        y = y + conv_ref[t:t + 1, :] * pad_ref[pl.ds(DN_HALO - DN_CONV // 2 + t, tr), :]
    y = _silu(y)
    q, k, v = y[:, 0:512], y[:, 512:1024], y[:, 1024:1536]
    bd = bd_ref[...]
    q = q * lax.rsqrt(_exact_mm(q * q, bd) + 1e-6) * (DN_HEAD_DIM ** -0.5)
    k = k * lax.rsqrt(_exact_mm(k * k, bd) + 1e-6)

    ab = ab_ref[0]
    g = -jnp.exp(gp_ref[0:1, :]) * jax.nn.softplus(ab + gp_ref[1:2, :])
    beta = jax.nn.sigmoid(ab)
    lane = lax.broadcasted_iota(jnp.int32, (tr, LANES), 1)
    gc = jnp.where(lane < DN_HEADS, _exact_mm_left(trif_ref[...], g), _exact_mm_left(trib_ref[...], g))
    gcx_all = _exact_mm(gc, eg_ref[...])
    bx_all = _exact_mm(beta, eb_ref[...])

    c = DN_CPT
    lane3 = lax.broadcasted_iota(jnp.int32, (1, 1, LANES), 2)
    left = (lane3 % LANES) < DN_HEAD_DIM
    tpos = lane3 % DN_HEAD_DIM
    lane6 = lax.broadcasted_iota(jnp.int32, (1, 1, 2 * LANES), 2)
    left6 = (lane6 % LANES) < DN_HEAD_DIM
    ri = lax.broadcasted_iota(jnp.int32, (1, DN_CHUNK, LANES), 1)
    cj = lax.broadcasted_iota(jnp.int32, (1, DN_CHUNK, LANES), 2) % DN_HEAD_DIM
    one = jnp.ones((), F32)
    zero = jnp.zeros((), F32)

    for d in range(2):
        incl = (ri >= cj) if d == 0 else (ri <= cj)
        strict = (ri > cj) if d == 0 else (ri < cj)
        for j in range(4):
            off = d * 512 + j * LANES
            gcx = gcx_all[:, off:off + LANES].reshape(c, DN_CHUNK, LANES)
            bx = bx_all[:, off:off + LANES].reshape(c, DN_CHUNK, LANES)
            qp = q[:, j * LANES:(j + 1) * LANES].reshape(c, DN_CHUNK, LANES)
            kp = k[:, j * LANES:(j + 1) * LANES].reshape(c, DN_CHUNK, LANES)
            vp = v[:, j * LANES:(j + 1) * LANES].reshape(c, DN_CHUNK, LANES)
            gl = gcx[:, DN_CHUNK - 1:DN_CHUNK, :] if d == 0 else gcx[:, 0:1, :]
            eg = jnp.exp(gcx)
            kb = kp * bx
            kst = _stack(kp, left)
            kk = _bmm_nt(kb, kst)
            qk = _bmm_nt(qp, kst)
            hi, mid, lo = _split3(gcx)
            a6 = jnp.where(tpos == 0, hi, jnp.where(tpos == 1, mid, jnp.where(tpos == 2, lo,
                           jnp.where(tpos < 6, one, zero))))
            b6 = jnp.where(tpos < 3, one, jnp.where(tpos == 3, -hi, jnp.where(tpos == 4, -mid,
                           jnp.where(tpos == 5, -lo, zero))))
            diff = _bmm_nt(a6, _stack(b6, left))
            dm = jnp.exp(jnp.where(incl, diff, -jnp.inf))
            x = -jnp.where(strict, kk * dm, zero)
            r = x
            for _ in range(5):
                x = _bmm(x, _stack(x, left))
                r = r + x + _bmm(r, _stack(x, left))
            rhs = jnp.concatenate([vp * bx, kb * eg], axis=-1)
            sol = rhs + _bmm(r, _stack(rhs, left6))
            sl = slice(j * LANES, (j + 1) * LANES)
            u_out[0, d, :, sl] = sol[:, :, 0:LANES].reshape(tr, LANES)
            w_out[0, d, :, sl] = sol[:, :, LANES:].reshape(tr, LANES).astype(w_out.dtype)
            qg_out[0, d, :, sl] = (qp * eg).reshape(tr, LANES).astype(qg_out.dtype)
            kg_out[0, d, :, sl] = (kp * jnp.exp(gl - gcx)).reshape(tr, LANES).astype(kg_out.dtype)
            qk_out[0, d, :, sl] = (qk * dm).reshape(tr, LANES).astype(qk_out.dtype)
            gl_out[0, d, :, :, sl] = jnp.exp(gl)


def _dn_constants():
    idx = np.arange(DN_TR)
    same = (idx[:, None] // DN_CHUNK) == (idx[None, :] // DN_CHUNK)
    trif = (same & (idx[None, :] <= idx[:, None])).astype(np.float32)
    trib = (same & (idx[None, :] >= idx[:, None])).astype(np.float32)
    h = np.arange(512)
    bd = ((h[:, None] // DN_HEAD_DIM) == (h[None, :] // DN_HEAD_DIM)).astype(np.float32)
    col = np.arange(LANES)[:, None]
    out = np.arange(1024)[None, :]
    unit = (out // 512) * DN_HEADS + (out % 512) // DN_HEAD_DIM
    eg = (col == unit).astype(np.float32)
    eb = (col == unit + 2 * DN_HEADS).astype(np.float32)
    return tuple(jnp.asarray(a, MXU_DTYPE) for a in (bd, trif, trib, eg, eb))


def _dnlocal_call(dqkv, ab, conv_w, gp, consts, B, S, C):
    tr = DN_TR
    n_tiles = S // tr
    hb = tr // DN_HALO
    n_hblk = S // DN_HALO
    bd, trif, trib, eg, eb = consts
    const2 = lambda b, i: (0, 0)
    big = lambda b, i: (b, 0, i, 0)
    act = MXU_DTYPE
    shp = (B, 2, S, 512)
    return pl.pallas_call(
        functools.partial(_dnlocal_kernel, ctx_tiles=C // tr, n_tiles=n_tiles),
        grid=(B, n_tiles),
        in_specs=[pl.BlockSpec((1, tr, 1536), lambda b, i: (b, i, 0)),
                  pl.BlockSpec((1, DN_HALO, 1536), lambda b, i: (b, jnp.maximum(i * hb - 1, 0), 0)),
                  pl.BlockSpec((1, DN_HALO, 1536), lambda b, i: (b, jnp.minimum((i + 1) * hb, n_hblk - 1), 0)),
                  pl.BlockSpec((1, tr, LANES), lambda b, i: (b, i, 0)),
                  pl.BlockSpec((8, 1536), const2),
                  pl.BlockSpec((8, LANES), const2),
                  pl.BlockSpec((512, 512), const2),
                  pl.BlockSpec((tr, tr), const2),
                  pl.BlockSpec((tr, tr), const2),
                  pl.BlockSpec((LANES, 1024), const2),
                  pl.BlockSpec((LANES, 1024), const2)],
        out_specs=[pl.BlockSpec((1, 2, tr, 512), big)] * 5
                  + [pl.BlockSpec((1, 2, DN_CPT, 1, 512), lambda b, i: (b, 0, i, 0, 0))],
        out_shape=[jax.ShapeDtypeStruct(shp, F32)] + [jax.ShapeDtypeStruct(shp, act)] * 4
                  + [jax.ShapeDtypeStruct((B, 2, S // DN_CHUNK, 1, 512), F32)],
        scratch_shapes=[pltpu.VMEM((tr + 2 * DN_HALO, 1536), F32)],
        compiler_params=_params("parallel", "parallel"),
        name="dn_local",
    )(dqkv.reshape(B, S, 1536), dqkv.reshape(B, S, 1536), dqkv.reshape(B, S, 1536), ab.reshape(B, S, LANES),
      conv_w, gp, bd, trif, trib, eg, eb)


def _dnscan_kernel(uf, wf, qgf, kgf, qkf, glf, ub, wb, qgb, kgb, qkb, glb, of_out, ob_out, s_ref):
    n = pl.program_id(1)

    @pl.when(n == 0)
    def _():
        s_ref[...] = jnp.zeros_like(s_ref)

    lane = lax.broadcasted_iota(jnp.int32, (1, LANES), 1)
    left = lane < DN_HEAD_DIM
    row = lax.broadcasted_iota(jnp.int32, (LANES, LANES), 0)
    col = lax.broadcasted_iota(jnp.int32, (LANES, LANES), 1)
    same_head = (row < DN_HEAD_DIM) == (col < DN_HEAD_DIM)
    dirs = ((uf, wf, qgf, kgf, qkf, glf, of_out), (ub, wb, qgb, kgb, qkb, glb, ob_out))
    for d, (u_r, w_r, qg_r, kg_r, qk_r, gl_r, o_r) in enumerate(dirs):
        for j in range(4):
            sl = slice(j * LANES, (j + 1) * LANES)
            st = s_ref[d * 4 + j]
            vn = u_r[0, 0, :, sl] - _mm(w_r[0, 0, :, sl], st)
            z = jnp.zeros_like(vn)
            vst = jnp.concatenate([jnp.where(left, vn, z), jnp.where(left, z, vn)], axis=0)
            o_r[0, :, sl] = _mm(qg_r[0, 0, :, sl], st) + _mm(qk_r[0, 0, :, sl], vst)
            upd = _mm_tn(kg_r[0, 0, :, sl], vn)
            s_ref[d * 4 + j] = st * gl_r[0, 0, 0, :, sl] + jnp.where(same_head, upd, jnp.zeros_like(upd))


def _dnscan_call(local, B, S, C):
    u, w, qg, kg, qk, gl = local
    nch, nc = S // DN_CHUNK, C // DN_CHUNK

    def bidx(n):
        return jnp.where(n < nc, nc - 1 - n, nch - 1 + nc - n)

    fspec = pl.BlockSpec((1, 1, DN_CHUNK, 512), lambda b, n: (b, 0, n, 0))
    bspec = pl.BlockSpec((1, 1, DN_CHUNK, 512), lambda b, n: (b, 1, bidx(n), 0))
    fgl = pl.BlockSpec((1, 1, 1, 1, 512), lambda b, n: (b, 0, n, 0, 0))
    bgl = pl.BlockSpec((1, 1, 1, 1, 512), lambda b, n: (b, 1, bidx(n), 0, 0))
    return pl.pallas_call(
        _dnscan_kernel,
        grid=(B, nch),
        in_specs=[fspec] * 5 + [fgl] + [bspec] * 5 + [bgl],
        out_specs=[pl.BlockSpec((1, DN_CHUNK, 512), lambda b, n: (b, n, 0)),
                   pl.BlockSpec((1, DN_CHUNK, 512), lambda b, n: (b, bidx(n), 0))],
        out_shape=[jax.ShapeDtypeStruct((B, S, 512), F32)] * 2,
        scratch_shapes=[pltpu.VMEM((8, LANES, LANES), F32)],
        compiler_params=_params("parallel", "arbitrary"),
        name="dn_scan",
    )(u, w, qg, kg, qk, gl, u, w, qg, kg, qk, gl)


def _pack_pairs(v):
    w = v.shape[1] // 2
    bits = lax.bitcast_convert_type(v.astype(jnp.bfloat16).astype(F32), jnp.int32)
    return lax.shift_right_logical(bits[:, :w], 16) | bits[:, w:]


def _unpack_pairs(p):
    lo = lax.bitcast_convert_type(lax.shift_left(p, 16), F32)
    hi = lax.bitcast_convert_type(p & jnp.int32(-65536), F32)
    return lo, hi


def _merge_kernel(x_ref, ga_ref, shf_ref, scf_ref, omla_ref, of_ref, ob_ref, dz_ref, ogqa_ref, gates_ref,
                  wo1_ref, wo2_ref, wo3_ref, wout_ref, dnn_ref, bd_ref, lng_ref, lnb_ref, wr_ref, rb_ref,
                  x1_out, ufp_out, eidx_out, rank_out, ew_out, cnt_out, cnt_ref):
    @pl.when(pl.program_id(0) == 0)
    def _():
        cnt_ref[...] = jnp.zeros_like(cnt_ref)

    o = of_ref[...] + ob_ref[...]
    ms = _exact_mm(o * o, bd_ref[...]) * (1.0 / DN_HEAD_DIM)
    dn = o * lax.rsqrt(ms + NORM_EPS) * dnn_ref[...] * _silu(dz_ref[...].astype(F32))
    g1 = gates_ref[:, 0:1024].astype(F32)
    g2 = gates_ref[:, 1024:2048].astype(F32)
    g3 = gates_ref[:, 2048:3072].astype(F32)
    m = (g1 * _mm(omla_ref[...], wo1_ref[...]) + g2 * _mm(dn, wo2_ref[...])
         + g3 * _mm(ogqa_ref[...], wo3_ref[...]))
    y = _mm(m, wout_ref[...])
    x1 = _layernorm(DEEPNORM_ALPHA * x_ref[...] + ga_ref[0] * y, lng_ref[...], lnb_ref[...])
    x1_out[...] = x1
    uf = x1 * (1.0 + scf_ref[0]) + shf_ref[0]
    ufp_out[...] = _pack_pairs(uf)

    tm = uf.shape[0]
    scores = jax.nn.sigmoid(_mm_nt(wr_ref[...], uf))[0:N_EXPERTS]
    sel = scores + rb_ref[0:N_EXPERTS, :]
    gsz = N_EXPERTS // N_GROUPS
    neg = jnp.full((), -jnp.inf, F32)
    sel3 = sel.reshape(N_GROUPS, gsz, tm)
    mem = lax.broadcasted_iota(jnp.int32, (N_GROUPS, gsz, tm), 1)
    m1 = jnp.max(sel3, 1, keepdims=True)
    i1 = jnp.min(jnp.where(sel3 == m1, mem, gsz), 1, keepdims=True)
    m2 = jnp.max(jnp.where(mem == i1, neg, sel3), 1, keepdims=True)
    gs = (m1 + m2).reshape(N_GROUPS, tm)
    gi = lax.broadcasted_iota(jnp.int32, (N_GROUPS, tm), 0)
    grank = jnp.zeros((N_GROUPS, tm), jnp.int32)
    for gp in range(N_GROUPS):
        other = gs[gp:gp + 1, :]
        beats = (other > gs) | ((other == gs) & (gp < gi))
        grank = grank + beats.astype(jnp.int32)
    gsel = (grank < TOPK_GROUPS).reshape(N_GROUPS, 1, tm)
    cur = jnp.where(gsel, sel3, neg).reshape(N_EXPERTS, tm)
    ei = lax.broadcasted_iota(jnp.int32, (N_EXPERTS, tm), 0)
    zero = jnp.zeros((N_EXPERTS, tm), F32)
    one = jnp.ones((N_EXPERTS, tm), F32)
    chosen = zero
    picks = []
    for _ in range(TOP_K):
        mx = jnp.max(cur, 0, keepdims=True)
        ix = jnp.min(jnp.where(cur == mx, ei, N_EXPERTS), 0, keepdims=True)
        pick = ei == ix
        picks.append((ix, pick))
        chosen = chosen + jnp.where(pick, one, zero)
        cur = jnp.where(pick, neg, cur)

    r_i = lax.broadcasted_iota(jnp.int32, (tm, tm), 0)
    c_i = lax.broadcasted_iota(jnp.int32, (tm, tm), 1)
    before = jnp.where(r_i < c_i, 1.0, 0.0)
    pos = cnt_ref[:, 0:1] + _mm(chosen, before)
    cnt_new = cnt_ref[...] + jnp.sum(chosen, 1, keepdims=True)
    cnt_ref[...] = cnt_new
    cnt_out[...] = cnt_new

    w_rows = [jnp.sum(jnp.where(pick, scores, zero), 0, keepdims=True) for _, pick in picks]
    wsum = w_rows[0]
    for w_k in w_rows[1:]:
        wsum = wsum + w_k
    eidx_out[...] = jnp.concatenate([ix for ix, _ in picks], axis=0)
    rank_out[...] = jnp.concatenate(
        [jnp.sum(jnp.where(pick, pos, zero), 0, keepdims=True) for _, pick in picks], axis=0).astype(jnp.int32)
    w8 = jnp.concatenate([w_k / wsum * ROUTED_SCALE for w_k in w_rows], axis=0)
    ew_out[...] = jnp.concatenate([w8, jnp.zeros((LANES - TOP_K, tm), F32)], axis=0).T


def _merge_call(xs, mod_l, omla, of, ob, dz, ogqa, gates, wo1, wo2, wo3, wout, dnn, bd, lng, lnb, wr, rb, B, S, C):
    T, D = xs.shape
    tpb, ctiles = S // TM, C // TM
    row = lambda i: (i, 0)
    const = lambda i: (0, 0)
    modspec = lambda k: pl.BlockSpec((1, 1, D), _mod_index(tpb, ctiles, k))
    return pl.pallas_call(
        _merge_kernel,
        grid=(T // TM,),
        in_specs=[pl.BlockSpec((TM, D), row), modspec(2), modspec(3), modspec(4),
                  pl.BlockSpec((TM, 512), row), pl.BlockSpec((TM, 512), row), pl.BlockSpec((TM, 512), row),
                  pl.BlockSpec((TM, 512), row), pl.BlockSpec((TM, 512), row), pl.BlockSpec((TM, 3072), row),
                  pl.BlockSpec((512, D), const), pl.BlockSpec((512, D), const), pl.BlockSpec((512, D), const),
                  pl.BlockSpec((D, D), const), pl.BlockSpec((1, 512), const), pl.BlockSpec((512, 512), const),
                  pl.BlockSpec((1, D), const), pl.BlockSpec((1, D), const),
                  pl.BlockSpec((LANES, D), const), pl.BlockSpec((LANES, 1), const)],
        out_specs=[pl.BlockSpec((TM, D), row), pl.BlockSpec((TM, D // 2), row),
                   pl.BlockSpec((TOP_K, TM), lambda i: (0, i)), pl.BlockSpec((TOP_K, TM), lambda i: (0, i)),
                   pl.BlockSpec((TM, LANES), row), pl.BlockSpec((N_EXPERTS, LANES), const)],
        out_shape=[jax.ShapeDtypeStruct((T, D), F32), jax.ShapeDtypeStruct((T, D // 2), jnp.int32),
                   jax.ShapeDtypeStruct((TOP_K, T), jnp.int32), jax.ShapeDtypeStruct((TOP_K, T), jnp.int32),
                   jax.ShapeDtypeStruct((T, LANES), F32), jax.ShapeDtypeStruct((N_EXPERTS, LANES), F32)],
        scratch_shapes=[pltpu.VMEM((N_EXPERTS, LANES), F32)],
        compiler_params=_params("arbitrary"),
        name="merge_norm_route",
    )(xs, mod_l, mod_l, mod_l, omla, of.reshape(T, 512), ob.reshape(T, 512), dz, ogqa, gates,
      wo1, wo2, wo3, wout, dnn, bd, lng, lnb, wr, rb)


EXPERT_BLOCK = 128
SC_WINDOW = 128
SC_SPLIT = 2


def _sc_mesh():
    return plsc.VectorSubcoreMesh(core_axis_name="c", subcore_axis_name="s")


def _sc_gather_rows(y, idx):
    n = idx.shape[1]
    W = y.shape[1]

    @pl.kernel(out_type=jax.ShapeDtypeStruct((n, W), y.dtype), mesh=_sc_mesh(), scratch_types=[])
    def gather(y_hbm, i_hbm, o_hbm):
        def body(i_vmem, o_vmem):
            pltpu.sync_copy(y_hbm.at[i_vmem.at[0]], o_vmem)

        pltpu.emit_pipeline(
            body,
            grid=(n // SC_WINDOW,),
            in_specs=[pl.BlockSpec((1, SC_WINDOW), lambda i: (0, i))],
            out_specs=[pl.BlockSpec((SC_WINDOW, W), lambda i: (i, 0))],
            core_axis_name=("c", "s"),
            dimension_semantics=(pltpu.PARALLEL,),
        )(i_hbm, o_hbm)

    return gather(y, idx)


def _experts_kernel(be_ref, nv_ref, xb_ref, wg_ref, wu_ref, wd_ref, y_out):
    b = pl.program_id(0)
    nv = nv_ref[b]

    @pl.when(nv > 0)
    def _():
        rows = lax.broadcasted_iota(jnp.int32, xb_ref.shape, 0)
        xp = jnp.where(rows < nv, xb_ref[...], jnp.zeros_like(xb_ref[...]))
        lo, hi = _unpack_pairs(xp)
        half = lo.shape[1]
        g = _mm(lo, wg_ref[0, 0:half, :]) + _mm(hi, wg_ref[0, half:, :])
        u = _mm(lo, wu_ref[0, 0:half, :]) + _mm(hi, wu_ref[0, half:, :])
        y_out[...] = _pack_pairs(_mm(_silu(g) * u, wd_ref[0]))


def _experts_call(xb, block_e, nvalid, wg, wu, wd):
    R, W = xb.shape
    D = 2 * W
    grid_spec = pltpu.PrefetchScalarGridSpec(
        num_scalar_prefetch=2,
        grid=(R // EXPERT_BLOCK,),
        in_specs=[pl.BlockSpec((EXPERT_BLOCK, W), lambda b, be, nv: (b, 0)),
                  pl.BlockSpec((1, D, EXPERT_DIM), lambda b, be, nv: (be[b], 0, 0)),
                  pl.BlockSpec((1, D, EXPERT_DIM), lambda b, be, nv: (be[b], 0, 0)),
                  pl.BlockSpec((1, EXPERT_DIM, D), lambda b, be, nv: (be[b], 0, 0))],
        out_specs=pl.BlockSpec((EXPERT_BLOCK, W), lambda b, be, nv: (b, 0)),
    )
    return pl.pallas_call(
        _experts_kernel,
        grid_spec=grid_spec,
        out_shape=jax.ShapeDtypeStruct((R, W), jnp.int32),
        compiler_params=_params("arbitrary"),
        name="moe_experts",
    )(block_e, nvalid, xb, wg, wu, wd)


def _combine_kernel(x_ref, ufp_ref, yg_ref, ew_ref, gf_ref, sg_ref, su_ref, sd_ref, g_ref, b_ref, o_ref):
    lo, hi = _unpack_pairs(ufp_ref[...])
    half = lo.shape[1]
    hs = (_silu(_mm(lo, sg_ref[0:half, :]) + _mm(hi, sg_ref[half:, :]))
          * (_mm(lo, su_ref[0:half, :]) + _mm(hi, su_ref[half:, :])))
    f = _mm(hs, sd_ref[...])
    ew = ew_ref[...]
    lane = lax.broadcasted_iota(jnp.int32, ew.shape, 1)
    f_lo = jnp.zeros((x_ref.shape[0], half), F32)
    f_hi = jnp.zeros((x_ref.shape[0], half), F32)
    for k in range(TOP_K):
        wk = jnp.sum(jnp.where(lane == k, ew, jnp.zeros_like(ew)), axis=1, keepdims=True)
        ylo, yhi = _unpack_pairs(yg_ref[:, k * half:(k + 1) * half])
        f_lo = f_lo + wk * ylo
        f_hi = f_hi + wk * yhi
    f = f + jnp.concatenate([f_lo, f_hi], axis=1)
    o_ref[...] = _layernorm(DEEPNORM_ALPHA * x_ref[...] + gf_ref[0] * f, g_ref[...], b_ref[...])


def _combine_call(x1, ufp, yg, ew, mod_l, sg, su, sd, g, b, B, S, C):
    T, D = x1.shape
    tpb, ctiles = S // TM, C // TM
    row = lambda i: (i, 0)
    const = lambda i: (0, 0)
    return pl.pallas_call(
        _combine_kernel,
        grid=(T // TM,),
        in_specs=[pl.BlockSpec((TM, D), row), pl.BlockSpec((TM, D // 2), row),
                  pl.BlockSpec((TM, TOP_K * D // 2), row), pl.BlockSpec((TM, LANES), row),
                  pl.BlockSpec((1, 1, D), _mod_index(tpb, ctiles, 5)),
                  pl.BlockSpec((D, SHARED_DIM), const), pl.BlockSpec((D, SHARED_DIM), const),
                  pl.BlockSpec((SHARED_DIM, D), const),
                  pl.BlockSpec((1, D), const), pl.BlockSpec((1, D), const)],
        out_specs=pl.BlockSpec((TM, D), row),
        out_shape=jax.ShapeDtypeStruct((T, D), F32),
        compiler_params=_params("parallel"),
        name="moe_combine_norm",
    )(x1, ufp, yg, ew, mod_l, sg, su, sd, g, b)


def _moe_routed(ufp, eidx_t, rank_t, counts, wg, wu, wd):
    T = ufp.shape[0]
    n_blocks = -(-(T * TOP_K + N_EXPERTS * (EXPERT_BLOCK - 1)) // EXPERT_BLOCK)
    n_rows = n_blocks * EXPERT_BLOCK
    padded = (counts + EXPERT_BLOCK - 1) // EXPERT_BLOCK * EXPERT_BLOCK
    pad_end = jnp.cumsum(padded)
    start_pad = pad_end - padded
    dest_t = start_pad[eidx_t] + rank_t
    blk = jnp.arange(n_blocks, dtype=jnp.int32) * EXPERT_BLOCK
    block_e = jnp.minimum(jnp.sum((blk[:, None] >= pad_end[None, :]).astype(jnp.int32), axis=1), N_EXPERTS - 1)
    nvalid = jnp.clip(counts[block_e] - (blk - start_pad[block_e]), 0, EXPERT_BLOCK).astype(jnp.int32)
    W = ufp.shape[1]
    piece = jnp.arange(SC_SPLIT, dtype=jnp.int32)
    tok = jnp.broadcast_to(jnp.arange(T, dtype=jnp.int32), (TOP_K, T))
    row_tok = (jnp.arange(n_rows, dtype=jnp.int32) % T).at[dest_t.reshape(-1)].set(
        tok.reshape(-1), unique_indices=True)
    src_p = (row_tok[:, None] * SC_SPLIT + piece).reshape(1, n_rows * SC_SPLIT)
    xb = _sc_gather_rows(ufp.reshape(T * SC_SPLIT, W // SC_SPLIT), src_p).reshape(n_rows, W)
    yb = _experts_call(xb, block_e.astype(jnp.int32), nvalid, wg, wu, wd)
    idx_p = (dest_t.T[:, :, None] * SC_SPLIT + piece).reshape(1, T * TOP_K * SC_SPLIT)
    yg = _sc_gather_rows(yb.reshape(n_rows * SC_SPLIT, W // SC_SPLIT), idx_p)
    return yg.reshape(T, TOP_K * W)


def _rot_cols(w, half):
    return jnp.concatenate([-w[:, half:], w[:, :half]], axis=1)


def _prep_w_in(w):
    d = w.shape[0]
    offs = np.cumsum((0,) + IN_SIZES)
    cq, ckv, kr, dqkv, da, db, dz, gq, gk, gv, gates = (w[:, offs[t]:offs[t + 1]] for t in range(len(IN_SIZES)))
    z = lambda n: jnp.zeros((d, n), w.dtype)
    krg = jnp.concatenate([z(64), kr, z(32)], 1)
    krr = jnp.concatenate([z(64), _rot_cols(kr, MLA_ROPE // 2), z(32)], 1)
    ab = jnp.concatenate([da, db, z(LANES - 4 * DN_HEADS)], 1)
    hd = GQA_HEAD_DIM
    gq_rot = jnp.concatenate([_rot_cols(gq[:, h * hd:(h + 1) * hd], hd // 2) for h in range(GQA_HEADS)], 1)
    dup = lambda t: jnp.concatenate([t[:, 0:hd], t[:, 0:hd], t[:, hd:2 * hd], t[:, hd:2 * hd]], 1)
    gk_rot = jnp.concatenate([_rot_cols(gk[:, h * hd:(h + 1) * hd], hd // 2) for h in range(GQA_KV_HEADS)], 1)
    cat = jnp.concatenate([cq, ckv, krg, krr, dqkv, ab, dz, gq, gq_rot, dup(gk), dup(gk_rot), dup(gv), gates], 1)
    assert cat.shape[1] == NZ
    return cat.astype(MXU_DTYPE)


def _prep_w_uq(w):
    d = w.shape[0]
    hw = MLA_NOPE + MLA_ROPE
    a, b = [], []
    for h in range(MLA_HEADS):
        wh = w[:, h * hw:(h + 1) * hw]
        a += [wh, jnp.zeros((d, LANES - hw), w.dtype)]
        b += [jnp.zeros((d, MLA_NOPE), w.dtype), _rot_cols(wh[:, MLA_NOPE:], MLA_ROPE // 2),
              jnp.zeros((d, LANES - hw), w.dtype)]
    return jnp.concatenate(a + b, 1).astype(MXU_DTYPE)


def _prep_w_ukv(w):
    d = w.shape[0]
    hw = MLA_NOPE + MLA_V
    kpart, vpart = [], []
    for h in range(MLA_HEADS):
        wh = w[:, h * hw:(h + 1) * hw]
        kpart += [wh[:, :MLA_NOPE], jnp.zeros((d, LANES - MLA_NOPE), w.dtype)]
        vpart += [wh[:, MLA_NOPE:]]
    return jnp.concatenate(kpart + vpart, 1).astype(MXU_DTYPE)


def _rope_tables(n_rows, C):
    row = jnp.repeat(jnp.arange(n_rows, dtype=F32), GRID_W)
    col = jnp.tile(jnp.arange(GRID_W, dtype=F32), n_rows)

    def angles(dim):
        n = dim // 4
        inv = ROPE_BASE ** (-jnp.arange(n, dtype=F32) / n)
        return jnp.concatenate([row[:, None] * inv, col[:, None] * inv], axis=-1)

    def with_ctx(cos, sin):
        return (jnp.concatenate([jnp.ones((C, LANES), F32), cos], 0),
                jnp.concatenate([jnp.zeros((C, LANES), F32), sin], 0))

    L = n_rows * GRID_W
    am = angles(MLA_ROPE)
    one, zero = jnp.ones((L, MLA_NOPE), F32), jnp.zeros((L, MLA_NOPE), F32)
    cm = jnp.concatenate([one, jnp.cos(am), jnp.cos(am), one[:, :32]], 1)
    sm = jnp.concatenate([zero, jnp.sin(am), jnp.sin(am), zero[:, :32]], 1)
    ag = angles(GQA_HEAD_DIM)
    cg = jnp.tile(jnp.cos(ag), (1, 4))
    sg = jnp.tile(jnp.sin(ag), (1, 4))
    return with_ctx(cm, sm) + with_ctx(cg, sg)


def kernel(x, c, ctx, c_ctx, w_ada, b_ada, w_in, mla_q_norm, mla_kv_norm, w_uq, w_ukv, dn_conv, dn_a_log, dn_dt_bias, dn_norm, gqa_sink, w_o_mla, w_o_dn, w_o_gqa, w_out, ln1_g, ln1_b, w_router, router_bias, w_exp_gate, w_exp_up, w_exp_down, w_sh_gate, w_sh_up, w_sh_down, ln2_g, ln2_b):
    B, L, D = x.shape
    C = ctx.shape[1]
    S = C + L
    nl = w_in.shape[0]
    assert D == D_MODEL and nl == DEPTH and B <= CTX_MOD_ROW
    assert C % TM == 0 and L % TM == 0 and L % GRID_W == 0 and L >= 3 * WINDOW
    cast = lambda t: t.astype(MXU_DTYPE)

    cc = jnp.zeros((MOD_ROWS, D), F32).at[0:B].set(c).at[CTX_MOD_ROW].set(c_ctx)
    mods = _ada_call(cc, w_ada, b_ada)
    tabs = _rope_tables(L // GRID_W, C)
    dn_consts = _dn_constants()
    xs = jnp.concatenate([ctx, x], axis=1).reshape(B * S, D)

    for l in range(nl):
        mod_l = mods[l].reshape(MOD_ROWS * 6, 1, D)
        q, k, v, dqkv, ab, dz, gq, gkv, gates = _inproj_call(
            xs, mod_l, _prep_w_in(w_in[l]), _prep_w_uq(w_uq[l]), _prep_w_ukv(w_ukv[l]),
            mla_q_norm[l].reshape(1, -1), mla_kv_norm[l].reshape(1, -1), tabs, B, S, C)
        omla = _mla_call(q, k, v, B, S, C)
        conv8 = jnp.zeros((8, 3 * DN_WIDTH), F32).at[0:DN_CONV].set(dn_conv[l])
        gp = (jnp.zeros((8, LANES), F32).at[0, 0:2 * DN_HEADS].set(dn_a_log[l].reshape(-1))
              .at[1, 0:2 * DN_HEADS].set(dn_dt_bias[l].reshape(-1)))
        local = _dnlocal_call(dqkv, ab, conv8, gp, dn_consts, B, S, C)
        of, ob = _dnscan_call(local, B, S, C)
        ogqa = _gqa_call(gq, gkv, gqa_sink[l], B, S, C)
        wr = jnp.zeros((LANES, D), F32).at[0:N_EXPERTS].set(w_router[l].T)
        rb = jnp.zeros((LANES, 1), F32).at[0:N_EXPERTS, 0].set(router_bias[l])
        x1, ufp, eidx_t, rank_t, ew, cnt = _merge_call(
            xs, mod_l, omla, of, ob, dz, ogqa, gates,
            cast(w_o_mla[l]), cast(w_o_dn[l]), cast(w_o_gqa[l]), cast(w_out[l]),
            jnp.tile(dn_norm[l], DN_HEADS).reshape(1, DN_WIDTH), dn_consts[0],
            ln1_g[l].reshape(1, D), ln1_b[l].reshape(1, D), cast(wr), rb, B, S, C)
        yg = _moe_routed(ufp, eidx_t, rank_t, cnt[:, 0].astype(jnp.int32),
                         cast(w_exp_gate[l]), cast(w_exp_up[l]), cast(w_exp_down[l]))
        xs = _combine_call(x1, ufp, yg, ew, mod_l, cast(w_sh_gate[l]), cast(w_sh_up[l]), cast(w_sh_down[l]),
                           ln2_g[l].reshape(1, D), ln2_b[l].reshape(1, D), B, S, C)
    return xs.reshape(B, S, D)[:, C:, :]
```

```python
import functools

import numpy as np
import jax
import jax.numpy as jnp
from jax import lax
from jax.experimental import pallas as pl
from jax.experimental.pallas import tpu as pltpu
from jax.experimental.pallas import tpu_sc as plsc

F32 = jnp.float32
MXU_DTYPE = jnp.bfloat16

D_MODEL = 1024
DEPTH = 4
GRID_W = 64
NORM_EPS = 1e-6
ROPE_BASE = 10000.0
DEEPNORM_ALPHA = (2.0 * DEPTH) ** 0.25

MLA_HEADS = 8
MLA_Q_LORA = 256
MLA_KV_LORA = 128
MLA_NOPE = 64
MLA_ROPE = 32
MLA_V = 64
MLA_SCALE = (MLA_NOPE + MLA_ROPE) ** -0.5

DN_HEADS = 8
DN_HEAD_DIM = 64
DN_WIDTH = DN_HEADS * DN_HEAD_DIM
DN_CONV = 5
DN_CHUNK = 64

GQA_HEADS = 8
GQA_KV_HEADS = 2
GQA_HEAD_DIM = 64
GQA_SCALE = GQA_HEAD_DIM ** -0.5
WINDOW = 128

N_EXPERTS = 64
TOP_K = 8
N_GROUPS = 8
TOPK_GROUPS = 4
EXPERT_DIM = 256
SHARED_DIM = 256
ROUTED_SCALE = 2.5

IN_SIZES = (MLA_Q_LORA, MLA_KV_LORA, MLA_ROPE,
            3 * DN_WIDTH, 2 * DN_HEADS, 2 * DN_HEADS, DN_WIDTH,
            GQA_HEADS * GQA_HEAD_DIM, GQA_KV_HEADS * GQA_HEAD_DIM, GQA_KV_HEADS * GQA_HEAD_DIM,
            3 * D_MODEL)

LANES = 128
TM = 256
MOD_ROWS = 16
CTX_MOD_ROW = 8

OFF_A = 0
OFF_DQKV = 640
OFF_AB = OFF_DQKV + 3 * DN_WIDTH
OFF_DZ = OFF_AB + LANES
OFF_GQ = OFF_DZ + DN_WIDTH
OFF_GK = OFF_GQ + 1024
OFF_GATES = OFF_GK + 768
NZ = OFF_GATES + 3 * D_MODEL

VMEM_LIMIT = 56 * 1024 * 1024


def _mm(a, b):
    return jnp.dot(a.astype(MXU_DTYPE), b.astype(MXU_DTYPE), preferred_element_type=F32)


def _mm_nt(a, b):
    return lax.dot_general(a.astype(MXU_DTYPE), b.astype(MXU_DTYPE), (((1,), (1,)), ((), ())),
                           preferred_element_type=F32)


def _mm_tn(a, b):
    return lax.dot_general(a.astype(MXU_DTYPE), b.astype(MXU_DTYPE), (((0,), (0,)), ((), ())),
                           preferred_element_type=F32)


def _bmm(a, b):
    return jnp.einsum('cik,ckj->cij', a.astype(MXU_DTYPE), b.astype(MXU_DTYPE), preferred_element_type=F32)


def _bmm_nt(a, b):
    return jnp.einsum('cik,cjk->cij', a.astype(MXU_DTYPE), b.astype(MXU_DTYPE), preferred_element_type=F32)


def _split3(x):
    hi = x.astype(jnp.bfloat16).astype(F32)
    r = x - hi
    mid = r.astype(jnp.bfloat16).astype(F32)
    lo = (r - mid).astype(jnp.bfloat16).astype(F32)
    return hi, mid, lo


def _exact_mm(x, m01):
    hi, mid, lo = _split3(x)
    return _mm(hi, m01) + _mm(mid, m01) + _mm(lo, m01)


def _exact_mm_left(m01, x):
    hi, mid, lo = _split3(x)
    return _mm(m01, hi) + _mm(m01, mid) + _mm(m01, lo)


def _silu(x):
    return x * jax.nn.sigmoid(x)


def _layernorm(v, g, b):
    mu = jnp.mean(v, -1, keepdims=True)
    d = v - mu
    var = jnp.mean(d * d, -1, keepdims=True)
    return d * lax.rsqrt(var + NORM_EPS) * g + b


def _mod_index(tiles_per_b, ctx_tiles, k):
    def index(i):
        row = jnp.where((i % tiles_per_b) < ctx_tiles, CTX_MOD_ROW, i // tiles_per_b)
        return (row * 6 + k, 0, 0)
    return index


def _params(*sem):
    return pltpu.CompilerParams(dimension_semantics=sem, vmem_limit_bytes=VMEM_LIMIT)


def _ada_kernel(c_ref, w_ref, b_ref, o_ref):
    o_ref[0] = _mm(_silu(c_ref[...]), w_ref[0]) + b_ref[0]


def _ada_call(cc, w_ada, b_ada):
    nl, d, n6 = w_ada.shape
    tn = 1536
    return pl.pallas_call(
        _ada_kernel,
        grid=(nl, n6 // tn),
        in_specs=[pl.BlockSpec((MOD_ROWS, d), lambda l, j: (0, 0)),
                  pl.BlockSpec((1, d, tn), lambda l, j: (l, 0, j)),
                  pl.BlockSpec((1, 1, tn), lambda l, j: (l, 0, j))],
        out_specs=pl.BlockSpec((1, MOD_ROWS, tn), lambda l, j: (l, 0, j)),
        out_shape=jax.ShapeDtypeStruct((nl, MOD_ROWS, n6), F32),
        compiler_params=_params("parallel", "parallel"),
        name="ada_mod",
    )(cc, w_ada, b_ada.reshape(nl, 1, n6))


def _inproj_kernel(x_ref, sh_ref, sc_ref, w_ref, wuq_ref, wukv_ref, qn_ref, kvn_ref,
                   cm_ref, sm_ref, cg_ref, sg_ref,
                   q_out, k_out, v_out, dqkv_out, ab_out, dz_out, gq_out, gkv_out, gates_out):
    u = (x_ref[...] * (1.0 + sc_ref[0]) + sh_ref[0]).astype(MXU_DTYPE)

    def z(off, width):
        return jnp.dot(u, w_ref[:, off:off + width], preferred_element_type=F32)

    def rms(v, g):
        return v * lax.rsqrt(jnp.mean(v * v, -1, keepdims=True) + NORM_EPS) * g

    cm, sm, cg, sg = cm_ref[...], sm_ref[...], cg_ref[...], sg_ref[...]

    za = z(OFF_A, 640)
    qq = _mm(rms(za[:, 0:256], qn_ref[...]), wuq_ref[...])
    kvv = _mm(rms(za[:, 256:384], kvn_ref[...]), wukv_ref[...])
    k_rope = za[:, 384:512] * cm + za[:, 512:640] * sm
    for h in range(MLA_HEADS):
        sl = slice(h * LANES, (h + 1) * LANES)
        qa = qq[:, h * LANES:(h + 1) * LANES]
        qb = qq[:, 1024 + h * LANES:1024 + (h + 1) * LANES]
        q_out[:, sl] = ((qa * cm + qb * sm) * MLA_SCALE).astype(q_out.dtype)
        k_out[:, sl] = (kvv[:, sl] + k_rope).astype(k_out.dtype)
    v_out[...] = kvv[:, 1024:1536].astype(v_out.dtype)

    for t in range(3):
        dqkv_out[:, t * 512:(t + 1) * 512] = z(OFF_DQKV + t * 512, 512).astype(dqkv_out.dtype)
    ab_out[...] = z(OFF_AB, LANES)
    dz_out[...] = z(OFF_DZ, DN_WIDTH).astype(dz_out.dtype)

    zq = z(OFF_GQ, 1024)
    for p in range(4):
        sl = slice(p * LANES, (p + 1) * LANES)
        gq_out[:, sl] = ((zq[:, sl] * cg + zq[:, 512 + p * LANES:512 + (p + 1) * LANES] * sg)
                         * GQA_SCALE).astype(gq_out.dtype)
    zk = z(OFF_GK, 768)
    for j in range(2):
        sl = slice(j * LANES, (j + 1) * LANES)
        gkv_out[:, sl] = (zk[:, sl] * cg + zk[:, 256 + j * LANES:256 + (j + 1) * LANES] * sg).astype(gkv_out.dtype)
    gkv_out[:, 256:512] = zk[:, 512:768].astype(gkv_out.dtype)

    for t in range(3):
        gates_out[:, t * 1024:(t + 1) * 1024] = jax.nn.sigmoid(z(OFF_GATES + t * 1024, 1024)).astype(gates_out.dtype)


def _inproj_call(xs, mod_l, w_cat, wuq_cat, wukv_cat, qn, kvn, tabs, B, S, C):
    T, D = xs.shape
    tpb, ctiles = S // TM, C // TM
    act = MXU_DTYPE
    row = lambda i: (i, 0)
    const = lambda i: (0, 0)
    tab = lambda i: (i % tpb, 0)
    widths = (1024, 1024, 512, 1536, LANES, 512, 512, 512, 3072)
    dtypes = (act, act, act, act, F32, act, act, act, act)
    return pl.pallas_call(
        _inproj_kernel,
        grid=(T // TM,),
        in_specs=[pl.BlockSpec((TM, D), row),
                  pl.BlockSpec((1, 1, D), _mod_index(tpb, ctiles, 0)),
                  pl.BlockSpec((1, 1, D), _mod_index(tpb, ctiles, 1)),
                  pl.BlockSpec((D, NZ), const),
                  pl.BlockSpec((MLA_Q_LORA, 2048), const),
                  pl.BlockSpec((MLA_KV_LORA, 1536), const),
                  pl.BlockSpec((1, MLA_Q_LORA), const),
                  pl.BlockSpec((1, MLA_KV_LORA), const)]
                 + [pl.BlockSpec((TM, LANES), tab)] * 4,
        out_specs=[pl.BlockSpec((TM, w), row) for w in widths],
        out_shape=[jax.ShapeDtypeStruct((T, w), dt) for w, dt in zip(widths, dtypes)],
        compiler_params=_params("parallel"),
        name="in_proj",
    )(xs, mod_l, mod_l, w_cat, wuq_cat, wukv_cat, qn, kvn, *tabs)


def _mla_kernel(q_ref, k_ref, v_ref, o_ref, *, n_ctx, n_all):
    i = pl.program_id(2)
    tq = q_ref.shape[1]
    left = lax.broadcasted_iota(jnp.int32, (tq, LANES), 1) < MLA_V

    def attend(nk):
        outs = []
        for hh in range(2):
            q = q_ref[0, :, hh * LANES:(hh + 1) * LANES]
            k = k_ref[0, 0:nk, hh * LANES:(hh + 1) * LANES]
            s = _mm_nt(q, k)
            p = jnp.exp(s - jnp.max(s, -1, keepdims=True))
            l = jnp.sum(p, -1, keepdims=True)
            outs.append(_mm(p, v_ref[0, 0:nk, :]) / l)
        o_ref[0] = jnp.where(left, outs[0], outs[1]).astype(o_ref.dtype)

    ctx_tiles = n_ctx // tq

    @pl.when(i < ctx_tiles)
    def _():
        attend(n_ctx)

    @pl.when(i >= ctx_tiles)
    def _():
        attend(n_all)


def _mla_call(q, k, v, B, S, C):
    tq = 256
    q3, k3, v3 = q.reshape(B, S, 1024), k.reshape(B, S, 1024), v.reshape(B, S, 512)
    out = pl.pallas_call(
        functools.partial(_mla_kernel, n_ctx=C, n_all=S),
        grid=(B, MLA_HEADS // 2, S // tq),
        in_specs=[pl.BlockSpec((1, tq, 2 * LANES), lambda b, j, i: (b, i, j)),
                  pl.BlockSpec((1, S, 2 * LANES), lambda b, j, i: (b, 0, j)),
                  pl.BlockSpec((1, S, LANES), lambda b, j, i: (b, 0, j))],
        out_specs=pl.BlockSpec((1, tq, LANES), lambda b, j, i: (b, i, j)),
        out_shape=jax.ShapeDtypeStruct((B, S, 512), MXU_DTYPE),
        compiler_params=_params("parallel", "parallel", "arbitrary"),
        name="mla_attn",
    )(q3, k3, v3)
    return out.reshape(B * S, 512)


def _gqa_kernel(sink_ref, q_ref, kv_ref, o_ref, *, n_ctx, n_all):
    i = pl.program_id(1)
    qb_rows = q_ref.shape[1]
    span = qb_rows + 2 * WINDOW
    ctx_blocks = n_ctx // qb_rows
    lane = lax.broadcasted_iota(jnp.int32, (qb_rows, LANES), 1)
    left = lane < GQA_HEAD_DIM

    def run(latent):
        if latent:
            qb = i - ctx_blocks
            ws = pl.multiple_of(jnp.minimum((qb + 1) * qb_rows, n_all - span), qb_rows)
            q_pos = qb * qb_rows + lax.broadcasted_iota(jnp.int32, (qb_rows, span), 0)
            k_pos = ws - n_ctx + lax.broadcasted_iota(jnp.int32, (qb_rows, span), 1)
            ok = (jnp.abs(k_pos - q_pos) <= WINDOW) & (k_pos >= 0)
        for p in range(4):
            j = p // 2
            qp = q_ref[0, :, p * LANES:(p + 1) * LANES]
            kc = kv_ref[0, 0:n_ctx, j * LANES:(j + 1) * LANES]
            vc = kv_ref[0, 0:n_ctx, 256 + j * LANES:256 + (j + 1) * LANES]
            if latent:
                kl = kv_ref[0, pl.ds(ws, span), j * LANES:(j + 1) * LANES]
                vl = kv_ref[0, pl.ds(ws, span), 256 + j * LANES:256 + (j + 1) * LANES]
            outs = []
            for r in range(2):
                qm = jnp.where(left if r == 0 else ~left, qp, jnp.zeros_like(qp))
                sink = sink_ref[2 * p + r]
                s_ctx = _mm_nt(qm, kc)
                m = jnp.maximum(jnp.max(s_ctx, -1, keepdims=True), sink)
                if latent:
                    s_loc = jnp.where(ok, _mm_nt(qm, kl), -jnp.inf)
                    m = jnp.maximum(m, jnp.max(s_loc, -1, keepdims=True))
                p_ctx = jnp.exp(s_ctx - m)
                l = jnp.sum(p_ctx, -1, keepdims=True) + jnp.exp(sink - m)
                o = _mm(p_ctx, vc)
                if latent:
                    p_loc = jnp.exp(s_loc - m)
                    l = l + jnp.sum(p_loc, -1, keepdims=True)
                    o = o + _mm(p_loc, vl)
                outs.append(o / l)
            o_ref[0, :, p * LANES:(p + 1) * LANES] = jnp.where(left, outs[0], outs[1]).astype(o_ref.dtype)

    @pl.when(i < ctx_blocks)
    def _():
        run(False)

    @pl.when(i >= ctx_blocks)
    def _():
        run(True)


def _gqa_call(gq, gkv, sink, B, S, C):
    qb_rows = 128
    out = pl.pallas_call(
        functools.partial(_gqa_kernel, n_ctx=C, n_all=S),
        grid=(B, S // qb_rows),
        in_specs=[pl.BlockSpec(memory_space=pltpu.SMEM),
                  pl.BlockSpec((1, qb_rows, 512), lambda b, i: (b, i, 0)),
                  pl.BlockSpec((1, S, 512), lambda b, i: (b, 0, 0))],
        out_specs=pl.BlockSpec((1, qb_rows, 512), lambda b, i: (b, i, 0)),
        out_shape=jax.ShapeDtypeStruct((B, S, 512), MXU_DTYPE),
        compiler_params=_params("parallel", "arbitrary"),
        name="gqa_attn",
    )(sink, gq.reshape(B, S, 512), gkv.reshape(B, S, 512))
    return out.reshape(B * S, 512)


DN_TR = 256
DN_HALO = 16
DN_CPT = DN_TR // DN_CHUNK


def _stack(x, left):
    z = jnp.zeros_like(x)
    return jnp.concatenate([jnp.where(left, x, z), jnp.where(left, z, x)], axis=1)


def _dnlocal_kernel(main_ref, prev_ref, next_ref, ab_ref, conv_ref, gp_ref, bd_ref, trif_ref, trib_ref,
                    eg_ref, eb_ref,
                    u_out, w_out, qg_out, kg_out, qk_out, gl_out, pad_ref, *, ctx_tiles, n_tiles):
    i = pl.program_id(1)
    tr = DN_TR
    first = (i == 0) | (i == ctx_tiles)
    last = (i == ctx_tiles - 1) | (i == n_tiles - 1)
    xp = prev_ref[0].astype(F32)
    xn = next_ref[0].astype(F32)
    pad_ref[0:DN_HALO, :] = jnp.where(first, jnp.zeros_like(xp), xp)
    pad_ref[DN_HALO:DN_HALO + tr, :] = main_ref[0].astype(F32)
    pad_ref[DN_HALO + tr:, :] = jnp.where(last, jnp.zeros_like(xn), xn)
    y = jnp.zeros((tr, 3 * DN_WIDTH), F32)
    for t in range(DN_CONV):
        y = y + conv_ref[t:t + 1, :] * pad_ref[pl.ds(DN_HALO - DN_CONV // 2 + t, tr), :]
    y = _silu(y)
    q, k, v = y[:, 0:512], y[:, 512:1024], y[:, 1024:1536]
    bd = bd_ref[...]
    q = q * lax.rsqrt(_exact_mm(q * q, bd) + 1e-6) * (DN_HEAD_DIM ** -0.5)
    k = k * lax.rsqrt(_exact_mm(k * k, bd) + 1e-6)

    ab = ab_ref[0]
    g = -jnp.exp(gp_ref[0:1, :]) * jax.nn.softplus(ab + gp_ref[1:2, :])
    beta = jax.nn.sigmoid(ab)
    lane = lax.broadcasted_iota(jnp.int32, (tr, LANES), 1)
    gc = jnp.where(lane < DN_HEADS, _exact_mm_left(trif_ref[...], g), _exact_mm_left(trib_ref[...], g))
    gcx_all = _exact_mm(gc, eg_ref[...])
    bx_all = _exact_mm(beta, eb_ref[...])

    c = DN_CPT
    lane3 = lax.broadcasted_iota(jnp.int32, (1, 1, LANES), 2)
    left = (lane3 % LANES) < DN_HEAD_DIM
    tpos = lane3 % DN_HEAD_DIM
    lane6 = lax.broadcasted_iota(jnp.int32, (1, 1, 2 * LANES), 2)
    left6 = (lane6 % LANES) < DN_HEAD_DIM
    ri = lax.broadcasted_iota(jnp.int32, (1, DN_CHUNK, LANES), 1)
    cj = lax.broadcasted_iota(jnp.int32, (1, DN_CHUNK, LANES), 2) % DN_HEAD_DIM
    one = jnp.ones((), F32)
    zero = jnp.zeros((), F32)

    for d in range(2):
        incl = (ri >= cj) if d == 0 else (ri <= cj)
        strict = (ri > cj) if d == 0 else (ri < cj)
        for j in range(4):
            off = d * 512 + j * LANES
            gcx = gcx_all[:, off:off + LANES].reshape(c, DN_CHUNK, LANES)
            bx = bx_all[:, off:off + LANES].reshape(c, DN_CHUNK, LANES)
            qp = q[:, j * LANES:(j + 1) * LANES].reshape(c, DN_CHUNK, LANES)
            kp = k[:, j * LANES:(j + 1) * LANES].reshape(c, DN_CHUNK, LANES)
            vp = v[:, j * LANES:(j + 1) * LANES].reshape(c, DN_CHUNK, LANES)
            gl = gcx[:, DN_CHUNK - 1:DN_CHUNK, :] if d == 0 else gcx[:, 0:1, :]
            eg = jnp.exp(gcx)
            kb = kp * bx
            kst = _stack(kp, left)
            kk = _bmm_nt(kb, kst)
            qk = _bmm_nt(qp, kst)
            hi, mid, lo = _split3(gcx)
            a6 = jnp.where(tpos == 0, hi, jnp.where(tpos == 1, mid, jnp.where(tpos == 2, lo,
                           jnp.where(tpos < 6, one, zero))))
            b6 = jnp.where(tpos < 3, one, jnp.where(tpos == 3, -hi, jnp.where(tpos == 4, -mid,
                           jnp.where(tpos == 5, -lo, zero))))
            diff = _bmm_nt(a6, _stack(b6, left))
            dm = jnp.exp(jnp.where(incl, diff, -jnp.inf))
            x = -jnp.where(strict, kk * dm, zero)
            r = x
            for _ in range(5):
                x = _bmm(x, _stack(x, left))
                r = r + x + _bmm(r, _stack(x, left))
            rhs = jnp.concatenate([vp * bx, kb * eg], axis=-1)
            sol = rhs + _bmm(r, _stack(rhs, left6))
            sl = slice(j * LANES, (j + 1) * LANES)
            u_out[0, d, :, sl] = sol[:, :, 0:LANES].reshape(tr, LANES)
            w_out[0, d, :, sl] = sol[:, :, LANES:].reshape(tr, LANES).astype(w_out.dtype)
            qg_out[0, d, :, sl] = (qp * eg).reshape(tr, LANES).astype(qg_out.dtype)
            kg_out[0, d, :, sl] = (kp * jnp.exp(gl - gcx)).reshape(tr, LANES).astype(kg_out.dtype)
            qk_out[0, d, :, sl] = (qk * dm).reshape(tr, LANES).astype(qk_out.dtype)
            gl_out[0, d, :, :, sl] = jnp.exp(gl)


def _dn_constants():
    idx = np.arange(DN_TR)
    same = (idx[:, None] // DN_CHUNK) == (idx[None, :] // DN_CHUNK)
    trif = (same & (idx[None, :] <= idx[:, None])).astype(np.float32)
    trib = (same & (idx[None, :] >= idx[:, None])).astype(np.float32)
    h = np.arange(512)
    bd = ((h[:, None] // DN_HEAD_DIM) == (h[None, :] // DN_HEAD_DIM)).astype(np.float32)
    col = np.arange(LANES)[:, None]
    out = np.arange(1024)[None, :]
    unit = (out // 512) * DN_HEADS + (out % 512) // DN_HEAD_DIM
    eg = (col == unit).astype(np.float32)
    eb = (col == unit + 2 * DN_HEADS).astype(np.float32)
    return tuple(jnp.asarray(a, MXU_DTYPE) for a in (bd, trif, trib, eg, eb))


def _dnlocal_call(dqkv, ab, conv_w, gp, consts, B, S, C):
    tr = DN_TR
    n_tiles = S // tr
    hb = tr // DN_HALO
    n_hblk = S // DN_HALO
    bd, trif, trib, eg, eb = consts
    const2 = lambda b, i: (0, 0)
    big = lambda b, i: (b, 0, i, 0)
    act = MXU_DTYPE
    shp = (B, 2, S, 512)
    return pl.pallas_call(
        functools.partial(_dnlocal_kernel, ctx_tiles=C // tr, n_tiles=n_tiles),
        grid=(B, n_tiles),
        in_specs=[pl.BlockSpec((1, tr, 1536), lambda b, i: (b, i, 0)),
                  pl.BlockSpec((1, DN_HALO, 1536), lambda b, i: (b, jnp.maximum(i * hb - 1, 0), 0)),
                  pl.BlockSpec((1, DN_HALO, 1536), lambda b, i: (b, jnp.minimum((i + 1) * hb, n_hblk - 1), 0)),
                  pl.BlockSpec((1, tr, LANES), lambda b, i: (b, i, 0)),
                  pl.BlockSpec((8, 1536), const2),
                  pl.BlockSpec((8, LANES), const2),
                  pl.BlockSpec((512, 512), const2),
                  pl.BlockSpec((tr, tr), const2),
                  pl.BlockSpec((tr, tr), const2),
                  pl.BlockSpec((LANES, 1024), const2),
                  pl.BlockSpec((LANES, 1024), const2)],
        out_specs=[pl.BlockSpec((1, 2, tr, 512), big)] * 5
                  + [pl.BlockSpec((1, 2, DN_CPT, 1, 512), lambda b, i: (b, 0, i, 0, 0))],
        out_shape=[jax.ShapeDtypeStruct(shp, F32)] + [jax.ShapeDtypeStruct(shp, act)] * 4
                  + [jax.ShapeDtypeStruct((B, 2, S // DN_CHUNK, 1, 512), F32)],
        scratch_shapes=[pltpu.VMEM((tr + 2 * DN_HALO, 1536), F32)],
        compiler_params=_params("parallel", "parallel"),
        name="dn_local",
    )(dqkv.reshape(B, S, 1536), dqkv.reshape(B, S, 1536), dqkv.reshape(B, S, 1536), ab.reshape(B, S, LANES),
      conv_w, gp, bd, trif, trib, eg, eb)


def _dnscan_kernel(uf, wf, qgf, kgf, qkf, glf, ub, wb, qgb, kgb, qkb, glb, of_out, ob_out, s_ref):
    n = pl.program_id(1)

    @pl.when(n == 0)
    def _():
        s_ref[...] = jnp.zeros_like(s_ref)

    lane = lax.broadcasted_iota(jnp.int32, (1, LANES), 1)
    left = lane < DN_HEAD_DIM
    row = lax.broadcasted_iota(jnp.int32, (LANES, LANES), 0)
    col = lax.broadcasted_iota(jnp.int32, (LANES, LANES), 1)
    same_head = (row < DN_HEAD_DIM) == (col < DN_HEAD_DIM)
    dirs = ((uf, wf, qgf, kgf, qkf, glf, of_out), (ub, wb, qgb, kgb, qkb, glb, ob_out))
    for d, (u_r, w_r, qg_r, kg_r, qk_r, gl_r, o_r) in enumerate(dirs):
        for j in range(4):
            sl = slice(j * LANES, (j + 1) * LANES)
            st = s_ref[d * 4 + j]
            vn = u_r[0, 0, :, sl] - _mm(w_r[0, 0, :, sl], st)
            z = jnp.zeros_like(vn)
            vst = jnp.concatenate([jnp.where(left, vn, z), jnp.where(left, z, vn)], axis=0)
            o_r[0, :, sl] = _mm(qg_r[0, 0, :, sl], st) + _mm(qk_r[0, 0, :, sl], vst)
            upd = _mm_tn(kg_r[0, 0, :, sl], vn)
            s_ref[d * 4 + j] = st * gl_r[0, 0, 0, :, sl] + jnp.where(same_head, upd, jnp.zeros_like(upd))


def _dnscan_call(local, B, S, C):
    u, w, qg, kg, qk, gl = local
    nch, nc = S // DN_CHUNK, C // DN_CHUNK

    def bidx(n):
        return jnp.where(n < nc, nc - 1 - n, nch - 1 + nc - n)

    fspec = pl.BlockSpec((1, 1, DN_CHUNK, 512), lambda b, n: (b, 0, n, 0))
    bspec = pl.BlockSpec((1, 1, DN_CHUNK, 512), lambda b, n: (b, 1, bidx(n), 0))
    fgl = pl.BlockSpec((1, 1, 1, 1, 512), lambda b, n: (b, 0, n, 0, 0))
    bgl = pl.BlockSpec((1, 1, 1, 1, 512), lambda b, n: (b, 1, bidx(n), 0, 0))
    return pl.pallas_call(
        _dnscan_kernel,
        grid=(B, nch),
        in_specs=[fspec] * 5 + [fgl] + [bspec] * 5 + [bgl],
        out_specs=[pl.BlockSpec((1, DN_CHUNK, 512), lambda b, n: (b, n, 0)),
                   pl.BlockSpec((1, DN_CHUNK, 512), lambda b, n: (b, bidx(n), 0))],
        out_shape=[jax.ShapeDtypeStruct((B, S, 512), F32)] * 2,
        scratch_shapes=[pltpu.VMEM((8, LANES, LANES), F32)],
        compiler_params=_params("parallel", "arbitrary"),
        name="dn_scan",
    )(u, w, qg, kg, qk, gl, u, w, qg, kg, qk, gl)


def _pack_pairs(v):
    w = v.shape[1] // 2
    bits = lax.bitcast_convert_type(v.astype(jnp.bfloat16).astype(F32), jnp.int32)
    return lax.shift_right_logical(bits[:, :w], 16) | bits[:, w:]


def _unpack_pairs(p):
    lo = lax.bitcast_convert_type(lax.shift_left(p, 16), F32)
    hi = lax.bitcast_convert_type(p & jnp.int32(-65536), F32)
    return lo, hi


def _merge_kernel(x_ref, ga_ref, shf_ref, scf_ref, omla_ref, of_ref, ob_ref, dz_ref, ogqa_ref, gates_ref,
                  wo1_ref, wo2_ref, wo3_ref, wout_ref, dnn_ref, bd_ref, lng_ref, lnb_ref, wr_ref, rb_ref,
                  x1_out, ufp_out, eidx_out, rank_out, ew_out, cnt_out, cnt_ref):
    @pl.when(pl.program_id(0) == 0)
    def _():
        cnt_ref[...] = jnp.zeros_like(cnt_ref)

    o = of_ref[...] + ob_ref[...]
    ms = _exact_mm(o * o, bd_ref[...]) * (1.0 / DN_HEAD_DIM)
    dn = o * lax.rsqrt(ms + NORM_EPS) * dnn_ref[...] * _silu(dz_ref[...].astype(F32))
    g1 = gates_ref[:, 0:1024].astype(F32)
    g2 = gates_ref[:, 1024:2048].astype(F32)
    g3 = gates_ref[:, 2048:3072].astype(F32)
    m = (g1 * _mm(omla_ref[...], wo1_ref[...]) + g2 * _mm(dn, wo2_ref[...])
         + g3 * _mm(ogqa_ref[...], wo3_ref[...]))
    y = _mm(m, wout_ref[...])
    x1 = _layernorm(DEEPNORM_ALPHA * x_ref[...] + ga_ref[0] * y, lng_ref[...], lnb_ref[...])
    x1_out[...] = x1
    uf = x1 * (1.0 + scf_ref[0]) + shf_ref[0]
    _split_pieces(_pack_pairs(uf), ufp_out)

    tm = uf.shape[0]
    scores = jax.nn.sigmoid(_mm_nt(wr_ref[...], uf))[0:N_EXPERTS]
    sel = scores + rb_ref[0:N_EXPERTS, :]
    gsz = N_EXPERTS // N_GROUPS
    neg = jnp.full((), -jnp.inf, F32)
    sel3 = sel.reshape(N_GROUPS, gsz, tm)
    mem = lax.broadcasted_iota(jnp.int32, (N_GROUPS, gsz, tm), 1)
    m1 = jnp.max(sel3, 1, keepdims=True)
    i1 = jnp.min(jnp.where(sel3 == m1, mem, gsz), 1, keepdims=True)
    m2 = jnp.max(jnp.where(mem == i1, neg, sel3), 1, keepdims=True)
    gs = (m1 + m2).reshape(N_GROUPS, tm)
    gi = lax.broadcasted_iota(jnp.int32, (N_GROUPS, tm), 0)
    grank = jnp.zeros((N_GROUPS, tm), jnp.int32)
    for gp in range(N_GROUPS):
        other = gs[gp:gp + 1, :]
        beats = (other > gs) | ((other == gs) & (gp < gi))
        grank = grank + beats.astype(jnp.int32)
    gsel = (grank < TOPK_GROUPS).reshape(N_GROUPS, 1, tm)
    cur = jnp.where(gsel, sel3, neg).reshape(N_EXPERTS, tm)
    ei = lax.broadcasted_iota(jnp.int32, (N_EXPERTS, tm), 0)
    zero = jnp.zeros((N_EXPERTS, tm), F32)
    one = jnp.ones((N_EXPERTS, tm), F32)
    chosen = zero
    picks = []
    for _ in range(TOP_K):
        mx = jnp.max(cur, 0, keepdims=True)
        ix = jnp.min(jnp.where(cur == mx, ei, N_EXPERTS), 0, keepdims=True)
        pick = ei == ix
        picks.append((ix, pick))
        chosen = chosen + jnp.where(pick, one, zero)
        cur = jnp.where(pick, neg, cur)

    r_i = lax.broadcasted_iota(jnp.int32, (tm, tm), 0)
    c_i = lax.broadcasted_iota(jnp.int32, (tm, tm), 1)
    before = jnp.where(r_i < c_i, 1.0, 0.0)
    pos = cnt_ref[:, 0:1] + _mm(chosen, before)
    cnt_new = cnt_ref[...] + jnp.sum(chosen, 1, keepdims=True)
    cnt_ref[...] = cnt_new
    cnt_out[...] = cnt_new

    w_rows = [jnp.sum(jnp.where(pick, scores, zero), 0, keepdims=True) for _, pick in picks]
    wsum = w_rows[0]
    for w_k in w_rows[1:]:
        wsum = wsum + w_k
    eidx_out[...] = jnp.concatenate([ix for ix, _ in picks], axis=0)
    rank_out[...] = jnp.concatenate(
        [jnp.sum(jnp.where(pick, pos, zero), 0, keepdims=True) for _, pick in picks], axis=0).astype(jnp.int32)
    w8 = jnp.concatenate([w_k / wsum * ROUTED_SCALE for w_k in w_rows], axis=0)
    ew_out[...] = jnp.concatenate([w8, jnp.zeros((LANES - TOP_K, tm), F32)], axis=0).T


def _merge_call(xs, mod_l, omla, of, ob, dz, ogqa, gates, wo1, wo2, wo3, wout, dnn, bd, lng, lnb, wr, rb, B, S, C):
    T, D = xs.shape
    tpb, ctiles = S // TM, C // TM
    row = lambda i: (i, 0)
    const = lambda i: (0, 0)
    modspec = lambda k: pl.BlockSpec((1, 1, D), _mod_index(tpb, ctiles, k))
    return pl.pallas_call(
        _merge_kernel,
        grid=(T // TM,),
        in_specs=[pl.BlockSpec((TM, D), row), modspec(2), modspec(3), modspec(4),
                  pl.BlockSpec((TM, 512), row), pl.BlockSpec((TM, 512), row), pl.BlockSpec((TM, 512), row),
                  pl.BlockSpec((TM, 512), row), pl.BlockSpec((TM, 512), row), pl.BlockSpec((TM, 3072), row),
                  pl.BlockSpec((512, D), const), pl.BlockSpec((512, D), const), pl.BlockSpec((512, D), const),
                  pl.BlockSpec((D, D), const), pl.BlockSpec((1, 512), const), pl.BlockSpec((512, 512), const),
                  pl.BlockSpec((1, D), const), pl.BlockSpec((1, D), const),
                  pl.BlockSpec((LANES, D), const), pl.BlockSpec((LANES, 1), const)],
        out_specs=[pl.BlockSpec((TM, D), row), pl.BlockSpec((N_PIECES, TM, PIECE), lambda i: (0, i, 0)),
                   pl.BlockSpec((TOP_K, TM), lambda i: (0, i)), pl.BlockSpec((TOP_K, TM), lambda i: (0, i)),
                   pl.BlockSpec((TM, LANES), row), pl.BlockSpec((N_EXPERTS, LANES), const)],
        out_shape=[jax.ShapeDtypeStruct((T, D), F32), jax.ShapeDtypeStruct((N_PIECES, T, PIECE), jnp.int32),
                   jax.ShapeDtypeStruct((TOP_K, T), jnp.int32), jax.ShapeDtypeStruct((TOP_K, T), jnp.int32),
                   jax.ShapeDtypeStruct((T, LANES), F32), jax.ShapeDtypeStruct((N_EXPERTS, LANES), F32)],
        scratch_shapes=[pltpu.VMEM((N_EXPERTS, LANES), F32)],
        compiler_params=_params("arbitrary"),
        name="merge_norm_route",
    )(xs, mod_l, mod_l, mod_l, omla, of.reshape(T, 512), ob.reshape(T, 512), dz, ogqa, gates,
      wo1, wo2, wo3, wout, dnn, bd, lng, lnb, wr, rb)


EXPERT_BLOCK = 512
SC_WINDOW = 128
N_PIECES = 2
PIECE = D_MODEL // 2 // N_PIECES


def _split_pieces(packed, out_ref):
    for h in range(N_PIECES):
        out_ref[h] = packed[:, h * PIECE:(h + 1) * PIECE]


def _mm_pieces(pieces, w):
    acc = None
    for h, (lo, hi) in enumerate(pieces):
        t = (_mm(lo, w[h * PIECE:(h + 1) * PIECE, :])
             + _mm(hi, w[D_MODEL // 2 + h * PIECE:D_MODEL // 2 + (h + 1) * PIECE, :]))
        acc = t if acc is None else acc + t
    return acc


def _sc_mesh():
    return plsc.VectorSubcoreMesh(core_axis_name="c", subcore_axis_name="s")


def _sc_gather_rows(y, idx):
    n = idx.shape[1]
    W = y.shape[1]

    @pl.kernel(out_type=jax.ShapeDtypeStruct((n, W), y.dtype), mesh=_sc_mesh(), scratch_types=[])
    def gather(y_hbm, i_hbm, o_hbm):
        def body(i_vmem, o_vmem):
            pltpu.sync_copy(y_hbm.at[i_vmem.at[0]], o_vmem)

        pltpu.emit_pipeline(
            body,
            grid=(n // SC_WINDOW,),
            in_specs=[pl.BlockSpec((1, SC_WINDOW), lambda i: (0, i))],
            out_specs=[pl.BlockSpec((SC_WINDOW, W), lambda i: (i, 0))],
            core_axis_name=("c", "s"),
            dimension_semantics=(pltpu.PARALLEL,),
        )(i_hbm, o_hbm)

    return gather(y, idx)


def _experts_kernel(be_ref, nv_ref, xb_ref, wg_ref, wu_ref, wd_ref, y_out):
    b = pl.program_id(0)
    nv = nv_ref[b]

    @pl.when(nv > 0)
    def _():
        rows = lax.broadcasted_iota(jnp.int32, xb_ref.shape[1:], 0)
        pieces = []
        for h in range(N_PIECES):
            xh = xb_ref[h]
            pieces.append(_unpack_pairs(jnp.where(rows < nv, xh, jnp.zeros_like(xh))))
        hid = _silu(_mm_pieces(pieces, wg_ref.at[0])) * _mm_pieces(pieces, wu_ref.at[0])
        _split_pieces(_pack_pairs(_mm(hid, wd_ref[0])), y_out)


def _experts_call(xb, block_e, nvalid, wg, wu, wd):
    _, R, _ = xb.shape
    D = D_MODEL
    blk = (N_PIECES, EXPERT_BLOCK, PIECE)
    grid_spec = pltpu.PrefetchScalarGridSpec(
        num_scalar_prefetch=2,
        grid=(R // EXPERT_BLOCK,),
        in_specs=[pl.BlockSpec(blk, lambda b, be, nv: (0, b, 0)),
                  pl.BlockSpec((1, D, EXPERT_DIM), lambda b, be, nv: (be[b], 0, 0)),
                  pl.BlockSpec((1, D, EXPERT_DIM), lambda b, be, nv: (be[b], 0, 0)),
                  pl.BlockSpec((1, EXPERT_DIM, D), lambda b, be, nv: (be[b], 0, 0))],
        out_specs=pl.BlockSpec(blk, lambda b, be, nv: (0, b, 0)),
    )
    return pl.pallas_call(
        _experts_kernel,
        grid_spec=grid_spec,
        out_shape=jax.ShapeDtypeStruct((N_PIECES, R, PIECE), jnp.int32),
        compiler_params=_params("arbitrary"),
        name="moe_experts",
    )(block_e, nvalid, xb, wg, wu, wd)


def _combine_kernel(x_ref, ufp_ref, yg_ref, ew_ref, gf_ref, sg_ref, su_ref, sd_ref, g_ref, b_ref, o_ref):
    pieces = [_unpack_pairs(ufp_ref[h]) for h in range(N_PIECES)]
    hs = _silu(_mm_pieces(pieces, sg_ref)) * _mm_pieces(pieces, su_ref)
    f = _mm(hs, sd_ref[...])
    ew = ew_ref[...]
    lane = lax.broadcasted_iota(jnp.int32, ew.shape, 1)
    acc = [[jnp.zeros((x_ref.shape[0], PIECE), F32) for _ in range(N_PIECES)] for _ in range(2)]
    for k in range(TOP_K):
        wk = jnp.sum(jnp.where(lane == k, ew, jnp.zeros_like(ew)), axis=1, keepdims=True)
        for h in range(N_PIECES):
            ylo, yhi = _unpack_pairs(yg_ref[h, k])
            acc[0][h] = acc[0][h] + wk * ylo
            acc[1][h] = acc[1][h] + wk * yhi
    f = f + jnp.concatenate(acc[0] + acc[1], axis=1)
    o_ref[...] = _layernorm(DEEPNORM_ALPHA * x_ref[...] + gf_ref[0] * f, g_ref[...], b_ref[...])


def _combine_call(x1, ufp, yg, ew, mod_l, sg, su, sd, g, b, B, S, C):
    T, D = x1.shape
    tpb, ctiles = S // TM, C // TM
    row = lambda i: (i, 0)
    const = lambda i: (0, 0)
    return pl.pallas_call(
        _combine_kernel,
        grid=(T // TM,),
        in_specs=[pl.BlockSpec((TM, D), row), pl.BlockSpec((N_PIECES, TM, PIECE), lambda i: (0, i, 0)),
                  pl.BlockSpec((N_PIECES, TOP_K, TM, PIECE), lambda i: (0, 0, i, 0)),
                  pl.BlockSpec((TM, LANES), row),
                  pl.BlockSpec((1, 1, D), _mod_index(tpb, ctiles, 5)),
                  pl.BlockSpec((D, SHARED_DIM), const), pl.BlockSpec((D, SHARED_DIM), const),
                  pl.BlockSpec((SHARED_DIM, D), const),
                  pl.BlockSpec((1, D), const), pl.BlockSpec((1, D), const)],
        out_specs=pl.BlockSpec((TM, D), row),
        out_shape=jax.ShapeDtypeStruct((T, D), F32),
        compiler_params=_params("parallel"),
        name="moe_combine_norm",
    )(x1, ufp, yg, ew, mod_l, sg, su, sd, g, b)


def _moe_routed(ufp, eidx_t, rank_t, counts, wg, wu, wd):
    T = ufp.shape[1]
    n_blocks = -(-(T * TOP_K + N_EXPERTS * (EXPERT_BLOCK - 1)) // EXPERT_BLOCK)
    n_rows = n_blocks * EXPERT_BLOCK
    padded = (counts + EXPERT_BLOCK - 1) // EXPERT_BLOCK * EXPERT_BLOCK
    pad_end = jnp.cumsum(padded)
    start_pad = pad_end - padded
    experts = jnp.arange(N_EXPERTS, dtype=jnp.int32)

    def lookup(table, idx):
        sel = idx[None] == experts.reshape((N_EXPERTS,) + (1,) * idx.ndim)
        return jnp.sum(jnp.where(sel, table.reshape((N_EXPERTS,) + (1,) * idx.ndim), 0), axis=0)

    dest_t = lookup(start_pad, eidx_t) + rank_t
    blk = jnp.arange(n_blocks, dtype=jnp.int32) * EXPERT_BLOCK
    block_e = jnp.minimum(jnp.sum((blk[:, None] >= pad_end[None, :]).astype(jnp.int32), axis=1), N_EXPERTS - 1)
    nvalid = jnp.clip(lookup(counts, block_e) - (blk - lookup(start_pad, block_e)), 0, EXPERT_BLOCK)
    tok = jnp.broadcast_to(jnp.arange(T, dtype=jnp.int32), (TOP_K, T))
    row_tok = (jnp.arange(n_rows, dtype=jnp.int32) % T).at[dest_t.reshape(-1)].set(
        tok.reshape(-1), unique_indices=True)
    piece = jnp.arange(N_PIECES, dtype=jnp.int32)
    src = (piece[:, None] * T + row_tok[None, :]).reshape(1, N_PIECES * n_rows)
    xb = _sc_gather_rows(ufp.reshape(N_PIECES * T, PIECE), src).reshape(N_PIECES, n_rows, PIECE)
    yb = _experts_call(xb, block_e.astype(jnp.int32), nvalid.astype(jnp.int32), wg, wu, wd)
    back = (piece[:, None, None] * n_rows + dest_t[None]).reshape(1, N_PIECES * TOP_K * T)
    yg = _sc_gather_rows(yb.reshape(N_PIECES * n_rows, PIECE), back)
    return yg.reshape(N_PIECES, TOP_K, T, PIECE)


def _rot_cols(w, half):
    return jnp.concatenate([-w[:, half:], w[:, :half]], axis=1)


def _prep_w_in(w):
    d = w.shape[0]
    offs = np.cumsum((0,) + IN_SIZES)
    cq, ckv, kr, dqkv, da, db, dz, gq, gk, gv, gates = (w[:, offs[t]:offs[t + 1]] for t in range(len(IN_SIZES)))
    z = lambda n: jnp.zeros((d, n), w.dtype)
    krg = jnp.concatenate([z(64), kr, z(32)], 1)
    krr = jnp.concatenate([z(64), _rot_cols(kr, MLA_ROPE // 2), z(32)], 1)
    ab = jnp.concatenate([da, db, z(LANES - 4 * DN_HEADS)], 1)
    hd = GQA_HEAD_DIM
    gq_rot = jnp.concatenate([_rot_cols(gq[:, h * hd:(h + 1) * hd], hd // 2) for h in range(GQA_HEADS)], 1)
    dup = lambda t: jnp.concatenate([t[:, 0:hd], t[:, 0:hd], t[:, hd:2 * hd], t[:, hd:2 * hd]], 1)
    gk_rot = jnp.concatenate([_rot_cols(gk[:, h * hd:(h + 1) * hd], hd // 2) for h in range(GQA_KV_HEADS)], 1)
    cat = jnp.concatenate([cq, ckv, krg, krr, dqkv, ab, dz, gq, gq_rot, dup(gk), dup(gk_rot), dup(gv), gates], 1)
    assert cat.shape[1] == NZ
    return cat.astype(MXU_DTYPE)


def _prep_w_uq(w):
    d = w.shape[0]
    hw = MLA_NOPE + MLA_ROPE
    a, b = [], []
    for h in range(MLA_HEADS):
        wh = w[:, h * hw:(h + 1) * hw]
        a += [wh, jnp.zeros((d, LANES - hw), w.dtype)]
        b += [jnp.zeros((d, MLA_NOPE), w.dtype), _rot_cols(wh[:, MLA_NOPE:], MLA_ROPE // 2),
              jnp.zeros((d, LANES - hw), w.dtype)]
    return jnp.concatenate(a + b, 1).astype(MXU_DTYPE)


def _prep_w_ukv(w):
    d = w.shape[0]
    hw = MLA_NOPE + MLA_V
    kpart, vpart = [], []
    for h in range(MLA_HEADS):
        wh = w[:, h * hw:(h + 1) * hw]
        kpart += [wh[:, :MLA_NOPE], jnp.zeros((d, LANES - MLA_NOPE), w.dtype)]
        vpart += [wh[:, MLA_NOPE:]]
    return jnp.concatenate(kpart + vpart, 1).astype(MXU_DTYPE)


def _rope_tables(n_rows, C):
    row = jnp.repeat(jnp.arange(n_rows, dtype=F32), GRID_W)
    col = jnp.tile(jnp.arange(GRID_W, dtype=F32), n_rows)

    def angles(dim):
        n = dim // 4
        inv = ROPE_BASE ** (-jnp.arange(n, dtype=F32) / n)
        return jnp.concatenate([row[:, None] * inv, col[:, None] * inv], axis=-1)

    def with_ctx(cos, sin):
        return (jnp.concatenate([jnp.ones((C, LANES), F32), cos], 0),
                jnp.concatenate([jnp.zeros((C, LANES), F32), sin], 0))

    L = n_rows * GRID_W
    am = angles(MLA_ROPE)
    one, zero = jnp.ones((L, MLA_NOPE), F32), jnp.zeros((L, MLA_NOPE), F32)
    cm = jnp.concatenate([one, jnp.cos(am), jnp.cos(am), one[:, :32]], 1)
    sm = jnp.concatenate([zero, jnp.sin(am), jnp.sin(am), zero[:, :32]], 1)
    ag = angles(GQA_HEAD_DIM)
    cg = jnp.tile(jnp.cos(ag), (1, 4))
    sg = jnp.tile(jnp.sin(ag), (1, 4))
    return with_ctx(cm, sm) + with_ctx(cg, sg)


def kernel(x, c, ctx, c_ctx, w_ada, b_ada, w_in, mla_q_norm, mla_kv_norm, w_uq, w_ukv, dn_conv, dn_a_log, dn_dt_bias, dn_norm, gqa_sink, w_o_mla, w_o_dn, w_o_gqa, w_out, ln1_g, ln1_b, w_router, router_bias, w_exp_gate, w_exp_up, w_exp_down, w_sh_gate, w_sh_up, w_sh_down, ln2_g, ln2_b):
    B, L, D = x.shape
    C = ctx.shape[1]
    S = C + L
    nl = w_in.shape[0]
    assert D == D_MODEL and nl == DEPTH and B <= CTX_MOD_ROW
    assert C % TM == 0 and L % TM == 0 and L % GRID_W == 0 and L >= 3 * WINDOW
    cast = lambda t: t.astype(MXU_DTYPE)

    cc = jnp.zeros((MOD_ROWS, D), F32).at[0:B].set(c).at[CTX_MOD_ROW].set(c_ctx)
    mods = _ada_call(cc, w_ada, b_ada)
    tabs = _rope_tables(L // GRID_W, C)
    dn_consts = _dn_constants()
    xs = jnp.concatenate([ctx, x], axis=1).reshape(B * S, D)

    for l in range(nl):
        mod_l = mods[l].reshape(MOD_ROWS * 6, 1, D)
        q, k, v, dqkv, ab, dz, gq, gkv, gates = _inproj_call(
            xs, mod_l, _prep_w_in(w_in[l]), _prep_w_uq(w_uq[l]), _prep_w_ukv(w_ukv[l]),
            mla_q_norm[l].reshape(1, -1), mla_kv_norm[l].reshape(1, -1), tabs, B, S, C)
        omla = _mla_call(q, k, v, B, S, C)
        conv8 = jnp.zeros((8, 3 * DN_WIDTH), F32).at[0:DN_CONV].set(dn_conv[l])
        gp = (jnp.zeros((8, LANES), F32).at[0, 0:2 * DN_HEADS].set(dn_a_log[l].reshape(-1))
              .at[1, 0:2 * DN_HEADS].set(dn_dt_bias[l].reshape(-1)))
        local = _dnlocal_call(dqkv, ab, conv8, gp, dn_consts, B, S, C)
        of, ob = _dnscan_call(local, B, S, C)
        ogqa = _gqa_call(gq, gkv, gqa_sink[l], B, S, C)
        wr = jnp.zeros((LANES, D), F32).at[0:N_EXPERTS].set(w_router[l].T)
        rb = jnp.zeros((LANES, 1), F32).at[0:N_EXPERTS, 0].set(router_bias[l])
        x1, ufp, eidx_t, rank_t, ew, cnt = _merge_call(
            xs, mod_l, omla, of, ob, dz, ogqa, gates,
            cast(w_o_mla[l]), cast(w_o_dn[l]), cast(w_o_gqa[l]), cast(w_out[l]),
            jnp.tile(dn_norm[l], DN_HEADS).reshape(1, DN_WIDTH), dn_consts[0],
            ln1_g[l].reshape(1, D), ln1_b[l].reshape(1, D), cast(wr), rb, B, S, C)
        yg = _moe_routed(ufp, eidx_t, rank_t, cnt[:, 0].astype(jnp.int32),
                         cast(w_exp_gate[l]), cast(w_exp_up[l]), cast(w_exp_down[l]))
        xs = _combine_call(x1, ufp, yg, ew, mod_l, cast(w_sh_gate[l]), cast(w_sh_up[l]), cast(w_sh_down[l]),
                           ln2_g[l].reshape(1, D), ln2_b[l].reshape(1, D), B, S, C)
    return xs.reshape(B, S, D)[:, C:, :]
```

```python
import functools

import numpy as np
import jax
import jax.numpy as jnp
from jax import lax
from jax.experimental import pallas as pl
from jax.experimental.pallas import tpu as pltpu
from jax.experimental.pallas import tpu_sc as plsc

F32 = jnp.float32
MXU_DTYPE = jnp.bfloat16

D_MODEL = 1024
DEPTH = 4
GRID_W = 64
NORM_EPS = 1e-6
ROPE_BASE = 10000.0
DEEPNORM_ALPHA = (2.0 * DEPTH) ** 0.25

MLA_HEADS = 8
MLA_Q_LORA = 256
MLA_KV_LORA = 128
MLA_NOPE = 64
MLA_ROPE = 32
MLA_V = 64
MLA_SCALE = (MLA_NOPE + MLA_ROPE) ** -0.5
LOG2E = float(np.log2(np.e))

DN_HEADS = 8
DN_HEAD_DIM = 64
DN_WIDTH = DN_HEADS * DN_HEAD_DIM
DN_CONV = 5
DN_CHUNK = 64

GQA_HEADS = 8
GQA_KV_HEADS = 2
GQA_HEAD_DIM = 64
GQA_SCALE = GQA_HEAD_DIM ** -0.5
WINDOW = 128

N_EXPERTS = 64
TOP_K = 8
N_GROUPS = 8
TOPK_GROUPS = 4
EXPERT_DIM = 256
SHARED_DIM = 256
ROUTED_SCALE = 2.5

IN_SIZES = (MLA_Q_LORA, MLA_KV_LORA, MLA_ROPE,
            3 * DN_WIDTH, 2 * DN_HEADS, 2 * DN_HEADS, DN_WIDTH,
            GQA_HEADS * GQA_HEAD_DIM, GQA_KV_HEADS * GQA_HEAD_DIM, GQA_KV_HEADS * GQA_HEAD_DIM,
            3 * D_MODEL)

LANES = 128
TM = 256
MOD_ROWS = 16
CTX_MOD_ROW = 8

OFF_A = 0
OFF_DQKV = 640
OFF_AB = OFF_DQKV + 3 * DN_WIDTH
OFF_DZ = OFF_AB + LANES
OFF_GQ = OFF_DZ + DN_WIDTH
OFF_GK = OFF_GQ + 1024
OFF_GATES = OFF_GK + 768
NZ = OFF_GATES + 3 * D_MODEL

VMEM_LIMIT = 56 * 1024 * 1024


def _mm(a, b):
    return jnp.dot(a.astype(MXU_DTYPE), b.astype(MXU_DTYPE), preferred_element_type=F32)


def _mm_nt(a, b):
    return lax.dot_general(a.astype(MXU_DTYPE), b.astype(MXU_DTYPE), (((1,), (1,)), ((), ())),
                           preferred_element_type=F32)


def _mm_tn(a, b):
    return lax.dot_general(a.astype(MXU_DTYPE), b.astype(MXU_DTYPE), (((0,), (0,)), ((), ())),
                           preferred_element_type=F32)


def _bmm(a, b):
    return jnp.einsum('cik,ckj->cij', a.astype(MXU_DTYPE), b.astype(MXU_DTYPE), preferred_element_type=F32)


def _bmm_nt(a, b):
    return jnp.einsum('cik,cjk->cij', a.astype(MXU_DTYPE), b.astype(MXU_DTYPE), preferred_element_type=F32)


def _split3(x):
    hi = x.astype(jnp.bfloat16).astype(F32)
    r = x - hi
    mid = r.astype(jnp.bfloat16).astype(F32)
    lo = (r - mid).astype(jnp.bfloat16).astype(F32)
    return hi, mid, lo


def _exact_mm(x, m01):
    hi, mid, lo = _split3(x)
    return _mm(hi, m01) + _mm(mid, m01) + _mm(lo, m01)


def _exact_mm_left(m01, x):
    hi, mid, lo = _split3(x)
    return _mm(m01, hi) + _mm(m01, mid) + _mm(m01, lo)


def _silu(x):
    return x * jax.nn.sigmoid(x)


def _layernorm(v, g, b):
    mu = jnp.mean(v, -1, keepdims=True)
    d = v - mu
    var = jnp.mean(d * d, -1, keepdims=True)
    return d * lax.rsqrt(var + NORM_EPS) * g + b


def _mod_index(tiles_per_b, ctx_tiles, k):
    def index(i):
        row = jnp.where((i % tiles_per_b) < ctx_tiles, CTX_MOD_ROW, i // tiles_per_b)
        return (row * 6 + k, 0, 0)
    return index


def _params(*sem):
    return pltpu.CompilerParams(dimension_semantics=sem, vmem_limit_bytes=VMEM_LIMIT)


def _ada_kernel(c_ref, w_ref, b_ref, o_ref):
    o_ref[0] = _mm(_silu(c_ref[...]), w_ref[0]) + b_ref[0]


def _ada_call(cc, w_ada, b_ada):
    nl, d, n6 = w_ada.shape
    tn = 1536
    return pl.pallas_call(
        _ada_kernel,
        grid=(nl, n6 // tn),
        in_specs=[pl.BlockSpec((MOD_ROWS, d), lambda l, j: (0, 0)),
                  pl.BlockSpec((1, d, tn), lambda l, j: (l, 0, j)),
                  pl.BlockSpec((1, 1, tn), lambda l, j: (l, 0, j))],
        out_specs=pl.BlockSpec((1, MOD_ROWS, tn), lambda l, j: (l, 0, j)),
        out_shape=jax.ShapeDtypeStruct((nl, MOD_ROWS, n6), F32),
        compiler_params=_params("parallel", "parallel"),
        name="ada_mod",
    )(cc, w_ada, b_ada.reshape(nl, 1, n6))


def _inproj_kernel(x_ref, sh_ref, sc_ref, w_ref, wuq_ref, wukv_ref, qn_ref, kvn_ref,
                   cm_ref, sm_ref, cg_ref, sg_ref,
                   q_out, k_out, v_out, dqkv_out, ab_out, dz_out, gq_out, gkv_out, gates_out):
    u = (x_ref[...] * (1.0 + sc_ref[0]) + sh_ref[0]).astype(MXU_DTYPE)

    def z(off, width):
        return jnp.dot(u, w_ref[:, off:off + width], preferred_element_type=F32)

    def rms(v, g):
        return v * lax.rsqrt(jnp.mean(v * v, -1, keepdims=True) + NORM_EPS) * g

    cm, sm, cg, sg = cm_ref[...], sm_ref[...], cg_ref[...], sg_ref[...]

    za = z(OFF_A, 640)
    qq = _mm(rms(za[:, 0:256], qn_ref[...]), wuq_ref[...])
    kvv = _mm(rms(za[:, 256:384], kvn_ref[...]), wukv_ref[...])
    k_rope = za[:, 384:512] * cm + za[:, 512:640] * sm
    for h in range(MLA_HEADS):
        sl = slice(h * LANES, (h + 1) * LANES)
        qa = qq[:, h * LANES:(h + 1) * LANES]
        qb = qq[:, 1024 + h * LANES:1024 + (h + 1) * LANES]
        q_out[:, sl] = ((qa * cm + qb * sm) * (MLA_SCALE * LOG2E)).astype(q_out.dtype)
        k_out[:, sl] = (kvv[:, sl] + k_rope).astype(k_out.dtype)
    v_out[...] = kvv[:, 1024:1536].astype(v_out.dtype)

    for t in range(3):
        dqkv_out[:, t * 512:(t + 1) * 512] = z(OFF_DQKV + t * 512, 512).astype(dqkv_out.dtype)
    ab_out[...] = z(OFF_AB, LANES)
    dz_out[...] = z(OFF_DZ, DN_WIDTH).astype(dz_out.dtype)

    zq = z(OFF_GQ, 1024)
    for p in range(4):
        sl = slice(p * LANES, (p + 1) * LANES)
        gq_out[:, sl] = ((zq[:, sl] * cg + zq[:, 512 + p * LANES:512 + (p + 1) * LANES] * sg)
                         * GQA_SCALE).astype(gq_out.dtype)
    zk = z(OFF_GK, 768)
    for j in range(2):
        sl = slice(j * LANES, (j + 1) * LANES)
        gkv_out[:, sl] = (zk[:, sl] * cg + zk[:, 256 + j * LANES:256 + (j + 1) * LANES] * sg).astype(gkv_out.dtype)
    gkv_out[:, 256:512] = zk[:, 512:768].astype(gkv_out.dtype)

    for t in range(3):
        gates_out[:, t * 1024:(t + 1) * 1024] = jax.nn.sigmoid(z(OFF_GATES + t * 1024, 1024)).astype(gates_out.dtype)


def _inproj_call(xs, mod_l, w_cat, wuq_cat, wukv_cat, qn, kvn, tabs, B, S, C):
    T, D = xs.shape
    tpb, ctiles = S // TM, C // TM
    act = MXU_DTYPE
    row = lambda i: (i, 0)
    const = lambda i: (0, 0)
    tab = lambda i: (i % tpb, 0)
    widths = (1024, 1024, 512, 1536, LANES, 512, 512, 512, 3072)
    dtypes = (act, act, act, act, F32, act, act, act, act)
    return pl.pallas_call(
        _inproj_kernel,
        grid=(T // TM,),
        in_specs=[pl.BlockSpec((TM, D), row),
                  pl.BlockSpec((1, 1, D), _mod_index(tpb, ctiles, 0)),
                  pl.BlockSpec((1, 1, D), _mod_index(tpb, ctiles, 1)),
                  pl.BlockSpec((D, NZ), const),
                  pl.BlockSpec((MLA_Q_LORA, 2048), const),
                  pl.BlockSpec((MLA_KV_LORA, 1536), const),
                  pl.BlockSpec((1, MLA_Q_LORA), const),
                  pl.BlockSpec((1, MLA_KV_LORA), const)]
                 + [pl.BlockSpec((TM, LANES), tab)] * 4,
        out_specs=[pl.BlockSpec((TM, w), row) for w in widths],
        out_shape=[jax.ShapeDtypeStruct((T, w), dt) for w, dt in zip(widths, dtypes)],
        compiler_params=_params("parallel"),
        name="in_proj",
    )(xs, mod_l, mod_l, w_cat, wuq_cat, wukv_cat, qn, kvn, *tabs)


def _mla_kernel(q_ref, k_ref, v_ref, o_ref, *, n_ctx, n_all):
    i = pl.program_id(2)
    tq = q_ref.shape[1]
    left = lax.broadcasted_iota(jnp.int32, (tq, LANES), 1) < MLA_V

    def attend(nk):
        v = v_ref[0, 0:nk, :]
        v_left = lax.broadcasted_iota(jnp.int32, v.shape, 1) < MLA_V
        ones = jnp.ones_like(v)
        outs = []
        for hh in range(2):
            q = q_ref[0, :, hh * LANES:(hh + 1) * LANES]
            k = k_ref[0, 0:nk, hh * LANES:(hh + 1) * LANES]
            s = _mm_nt(q, k)
            p = jnp.exp2((s - jnp.max(s, -1, keepdims=True)).astype(MXU_DTYPE))
            vh = jnp.where(v_left, v, ones) if hh == 0 else jnp.where(v_left, ones, v)
            o = _mm(p, vh)
            outs.append(o / pltpu.roll(o, MLA_V, axis=1))
        o_ref[0] = jnp.where(left, outs[0], outs[1]).astype(o_ref.dtype)

    ctx_tiles = n_ctx // tq

    @pl.when(i < ctx_tiles)
    def _():
        attend(n_ctx)

    @pl.when(i >= ctx_tiles)
    def _():
        attend(n_all)


def _mla_call(q, k, v, B, S, C):
    tq = 256
    q3, k3, v3 = q.reshape(B, S, 1024), k.reshape(B, S, 1024), v.reshape(B, S, 512)
    out = pl.pallas_call(
        functools.partial(_mla_kernel, n_ctx=C, n_all=S),
        grid=(B, MLA_HEADS // 2, S // tq),
        in_specs=[pl.BlockSpec((1, tq, 2 * LANES), lambda b, j, i: (b, i, j)),
                  pl.BlockSpec((1, S, 2 * LANES), lambda b, j, i: (b, 0, j)),
                  pl.BlockSpec((1, S, LANES), lambda b, j, i: (b, 0, j))],
        out_specs=pl.BlockSpec((1, tq, LANES), lambda b, j, i: (b, i, j)),
        out_shape=jax.ShapeDtypeStruct((B, S, 512), MXU_DTYPE),
        compiler_params=_params("parallel", "parallel", "arbitrary"),
        name="mla_attn",
    )(q3, k3, v3)
    return out.reshape(B * S, 512)


def _gqa_kernel(sink_ref, q_ref, kv_ref, o_ref, *, n_ctx, n_all):
    i = pl.program_id(1)
    qb_rows = q_ref.shape[1]
    span = qb_rows + 2 * WINDOW
    ctx_blocks = n_ctx // qb_rows
    lane = lax.broadcasted_iota(jnp.int32, (qb_rows, LANES), 1)
    left = lane < GQA_HEAD_DIM

    def run(latent):
        if latent:
            qb = i - ctx_blocks
            ws = pl.multiple_of(jnp.minimum((qb + 1) * qb_rows, n_all - span), qb_rows)
            q_pos = qb * qb_rows + lax.broadcasted_iota(jnp.int32, (qb_rows, span), 0)
            k_pos = ws - n_ctx + lax.broadcasted_iota(jnp.int32, (qb_rows, span), 1)
            ok = (jnp.abs(k_pos - q_pos) <= WINDOW) & (k_pos >= 0)
        for p in range(4):
            j = p // 2
            qp = q_ref[0, :, p * LANES:(p + 1) * LANES]
            kc = kv_ref[0, 0:n_ctx, j * LANES:(j + 1) * LANES]
            vc = kv_ref[0, 0:n_ctx, 256 + j * LANES:256 + (j + 1) * LANES]
            if latent:
                kl = kv_ref[0, pl.ds(ws, span), j * LANES:(j + 1) * LANES]
                vl = kv_ref[0, pl.ds(ws, span), 256 + j * LANES:256 + (j + 1) * LANES]
            outs = []
            for r in range(2):
                qm = jnp.where(left if r == 0 else ~left, qp, jnp.zeros_like(qp))
                sink = sink_ref[2 * p + r]
                s_ctx = _mm_nt(qm, kc)
                m = jnp.maximum(jnp.max(s_ctx, -1, keepdims=True), sink)
                if latent:
                    s_loc = jnp.where(ok, _mm_nt(qm, kl), -jnp.inf)
                    m = jnp.maximum(m, jnp.max(s_loc, -1, keepdims=True))
                p_ctx = jnp.exp(s_ctx - m)
                l = jnp.sum(p_ctx, -1, keepdims=True) + jnp.exp(sink - m)
                o = _mm(p_ctx, vc)
                if latent:
                    p_loc = jnp.exp(s_loc - m)
                    l = l + jnp.sum(p_loc, -1, keepdims=True)
                    o = o + _mm(p_loc, vl)
                outs.append(o / l)
            o_ref[0, :, p * LANES:(p + 1) * LANES] = jnp.where(left, outs[0], outs[1]).astype(o_ref.dtype)

    @pl.when(i < ctx_blocks)
    def _():
        run(False)

    @pl.when(i >= ctx_blocks)
    def _():
        run(True)


def _gqa_call(gq, gkv, sink, B, S, C):
    qb_rows = 128
    out = pl.pallas_call(
        functools.partial(_gqa_kernel, n_ctx=C, n_all=S),
        grid=(B, S // qb_rows),
        in_specs=[pl.BlockSpec(memory_space=pltpu.SMEM),
                  pl.BlockSpec((1, qb_rows, 512), lambda b, i: (b, i, 0)),
                  pl.BlockSpec((1, S, 512), lambda b, i: (b, 0, 0))],
        out_specs=pl.BlockSpec((1, qb_rows, 512), lambda b, i: (b, i, 0)),
        out_shape=jax.ShapeDtypeStruct((B, S, 512), MXU_DTYPE),
        compiler_params=_params("parallel", "arbitrary"),
        name="gqa_attn",
    )(sink, gq.reshape(B, S, 512), gkv.reshape(B, S, 512))
    return out.reshape(B * S, 512)


DN_TR = 256
DN_HALO = 16
DN_CPT = DN_TR // DN_CHUNK


def _stack(x, left):
    z = jnp.zeros_like(x)
    return jnp.concatenate([jnp.where(left, x, z), jnp.where(left, z, x)], axis=1)


def _dnlocal_kernel(main_ref, prev_ref, next_ref, ab_ref, conv_ref, gp_ref, bd_ref, trif_ref, trib_ref,
                    eg_ref, eb_ref,
                    u_out, w_out, qg_out, kg_out, qk_out, gl_out, pad_ref, *, ctx_tiles, n_tiles):
    i = pl.program_id(1)
    tr = DN_TR
    first = (i == 0) | (i == ctx_tiles)
    last = (i == ctx_tiles - 1) | (i == n_tiles - 1)
    xp = prev_ref[0].astype(F32)
    xn = next_ref[0].astype(F32)
    pad_ref[0:DN_HALO, :] = jnp.where(first, jnp.zeros_like(xp), xp)
    pad_ref[DN_HALO:DN_HALO + tr, :] = main_ref[0].astype(F32)
    pad_ref[DN_HALO + tr:, :] = jnp.where(last, jnp.zeros_like(xn), xn)
    y = jnp.zeros((tr, 3 * DN_WIDTH), F32)
    for t in range(DN_CONV):
        y = y + conv_ref[t:t + 1, :] * pad_ref[pl.ds(DN_HALO - DN_CONV // 2 + t, tr), :]
    y = _silu(y)
    q, k, v = y[:, 0:512], y[:, 512:1024], y[:, 1024:1536]
    bd = bd_ref[...]
    q = q * lax.rsqrt(_exact_mm(q * q, bd) + 1e-6) * (DN_HEAD_DIM ** -0.5)
    k = k * lax.rsqrt(_exact_mm(k * k, bd) + 1e-6)

    ab = ab_ref[0]
    g = -jnp.exp(gp_ref[0:1, :]) * jax.nn.softplus(ab + gp_ref[1:2, :])
    beta = jax.nn.sigmoid(ab)
    lane = lax.broadcasted_iota(jnp.int32, (tr, LANES), 1)
    gc = jnp.where(lane < DN_HEADS, _exact_mm_left(trif_ref[...], g), _exact_mm_left(trib_ref[...], g))
    gcx_all = _exact_mm(gc, eg_ref[...])
    bx_all = _exact_mm(beta, eb_ref[...])

    c = DN_CPT
    lane3 = lax.broadcasted_iota(jnp.int32, (1, 1, LANES), 2)
    left = (lane3 % LANES) < DN_HEAD_DIM
    tpos = lane3 % DN_HEAD_DIM
    lane6 = lax.broadcasted_iota(jnp.int32, (1, 1, 2 * LANES), 2)
    left6 = (lane6 % LANES) < DN_HEAD_DIM
    ri = lax.broadcasted_iota(jnp.int32, (1, DN_CHUNK, LANES), 1)
    cj = lax.broadcasted_iota(jnp.int32, (1, DN_CHUNK, LANES), 2) % DN_HEAD_DIM
    one = jnp.ones((), F32)
    zero = jnp.zeros((), F32)

    for d in range(2):
        incl = (ri >= cj) if d == 0 else (ri <= cj)
        strict = (ri > cj) if d == 0 else (ri < cj)
        for j in range(4):
            off = d * 512 + j * LANES
            gcx = gcx_all[:, off:off + LANES].reshape(c, DN_CHUNK, LANES)
            bx = bx_all[:, off:off + LANES].reshape(c, DN_CHUNK, LANES)
            qp = q[:, j * LANES:(j + 1) * LANES].reshape(c, DN_CHUNK, LANES)
            kp = k[:, j * LANES:(j + 1) * LANES].reshape(c, DN_CHUNK, LANES)
            vp = v[:, j * LANES:(j + 1) * LANES].reshape(c, DN_CHUNK, LANES)
            gl = gcx[:, DN_CHUNK - 1:DN_CHUNK, :] if d == 0 else gcx[:, 0:1, :]
            eg = jnp.exp(gcx)
            kb = kp * bx
            kst = _stack(kp, left)
            kk = _bmm_nt(kb, kst)
            qk = _bmm_nt(qp, kst)
            hi, mid, lo = _split3(gcx)
            a6 = jnp.where(tpos == 0, hi, jnp.where(tpos == 1, mid, jnp.where(tpos == 2, lo,
                           jnp.where(tpos < 6, one, zero))))
            b6 = jnp.where(tpos < 3, one, jnp.where(tpos == 3, -hi, jnp.where(tpos == 4, -mid,
                           jnp.where(tpos == 5, -lo, zero))))
            diff = _bmm_nt(a6, _stack(b6, left))
            dm = jnp.exp(jnp.where(incl, diff, -jnp.inf))
            x = -jnp.where(strict, kk * dm, zero)
            r = x
            for _ in range(5):
                x = _bmm(x, _stack(x, left))
                r = r + x + _bmm(r, _stack(x, left))
            rhs = jnp.concatenate([vp * bx, kb * eg], axis=-1)
            sol = rhs + _bmm(r, _stack(rhs, left6))
            sl = slice(j * LANES, (j + 1) * LANES)
            u_out[0, d, :, sl] = sol[:, :, 0:LANES].reshape(tr, LANES)
            w_out[0, d, :, sl] = sol[:, :, LANES:].reshape(tr, LANES).astype(w_out.dtype)
            qg_out[0, d, :, sl] = (qp * eg).reshape(tr, LANES).astype(qg_out.dtype)
            kg_out[0, d, :, sl] = (kp * jnp.exp(gl - gcx)).reshape(tr, LANES).astype(kg_out.dtype)
            qk_out[0, d, :, sl] = (qk * dm).reshape(tr, LANES).astype(qk_out.dtype)
            gl_out[0, d, :, :, sl] = jnp.exp(gl)


def _dn_constants():
    idx = np.arange(DN_TR)
    same = (idx[:, None] // DN_CHUNK) == (idx[None, :] // DN_CHUNK)
    trif = (same & (idx[None, :] <= idx[:, None])).astype(np.float32)
    trib = (same & (idx[None, :] >= idx[:, None])).astype(np.float32)
    h = np.arange(512)
    bd = ((h[:, None] // DN_HEAD_DIM) == (h[None, :] // DN_HEAD_DIM)).astype(np.float32)
    col = np.arange(LANES)[:, None]
    out = np.arange(1024)[None, :]
    unit = (out // 512) * DN_HEADS + (out % 512) // DN_HEAD_DIM
    eg = (col == unit).astype(np.float32)
    eb = (col == unit + 2 * DN_HEADS).astype(np.float32)
    return tuple(jnp.asarray(a, MXU_DTYPE) for a in (bd, trif, trib, eg, eb))


def _dnlocal_call(dqkv, ab, conv_w, gp, consts, B, S, C):
    tr = DN_TR
    n_tiles = S // tr
    hb = tr // DN_HALO
    n_hblk = S // DN_HALO
    bd, trif, trib, eg, eb = consts
    const2 = lambda b, i: (0, 0)
    big = lambda b, i: (b, 0, i, 0)
    act = MXU_DTYPE
    shp = (B, 2, S, 512)
    return pl.pallas_call(
        functools.partial(_dnlocal_kernel, ctx_tiles=C // tr, n_tiles=n_tiles),
        grid=(B, n_tiles),
        in_specs=[pl.BlockSpec((1, tr, 1536), lambda b, i: (b, i, 0)),
                  pl.BlockSpec((1, DN_HALO, 1536), lambda b, i: (b, jnp.maximum(i * hb - 1, 0), 0)),
                  pl.BlockSpec((1, DN_HALO, 1536), lambda b, i: (b, jnp.minimum((i + 1) * hb, n_hblk - 1), 0)),
                  pl.BlockSpec((1, tr, LANES), lambda b, i: (b, i, 0)),
                  pl.BlockSpec((8, 1536), const2),
                  pl.BlockSpec((8, LANES), const2),
                  pl.BlockSpec((512, 512), const2),
                  pl.BlockSpec((tr, tr), const2),
                  pl.BlockSpec((tr, tr), const2),
                  pl.BlockSpec((LANES, 1024), const2),
                  pl.BlockSpec((LANES, 1024), const2)],
        out_specs=[pl.BlockSpec((1, 2, tr, 512), big)] * 5
                  + [pl.BlockSpec((1, 2, DN_CPT, 1, 512), lambda b, i: (b, 0, i, 0, 0))],
        out_shape=[jax.ShapeDtypeStruct(shp, F32)] + [jax.ShapeDtypeStruct(shp, act)] * 4
                  + [jax.ShapeDtypeStruct((B, 2, S // DN_CHUNK, 1, 512), F32)],
        scratch_shapes=[pltpu.VMEM((tr + 2 * DN_HALO, 1536), F32)],
        compiler_params=_params("parallel", "parallel"),
        name="dn_local",
    )(dqkv.reshape(B, S, 1536), dqkv.reshape(B, S, 1536), dqkv.reshape(B, S, 1536), ab.reshape(B, S, LANES),
      conv_w, gp, bd, trif, trib, eg, eb)


def _dnscan_kernel(uf, wf, qgf, kgf, qkf, glf, ub, wb, qgb, kgb, qkb, glb, of_out, ob_out, s_ref):
    n = pl.program_id(1)

    @pl.when(n == 0)
    def _():
        s_ref[...] = jnp.zeros_like(s_ref)

    lane = lax.broadcasted_iota(jnp.int32, (1, LANES), 1)
    left = lane < DN_HEAD_DIM
    row = lax.broadcasted_iota(jnp.int32, (LANES, LANES), 0)
    col = lax.broadcasted_iota(jnp.int32, (LANES, LANES), 1)
    same_head = (row < DN_HEAD_DIM) == (col < DN_HEAD_DIM)
    dirs = ((uf, wf, qgf, kgf, qkf, glf, of_out), (ub, wb, qgb, kgb, qkb, glb, ob_out))
    for bb in range(uf.shape[0]):
        for d, (u_r, w_r, qg_r, kg_r, qk_r, gl_r, o_r) in enumerate(dirs):
            for j in range(4):
                sl = slice(j * LANES, (j + 1) * LANES)
                si = (bb * 2 + d) * 4 + j
                st = s_ref[si]
                vn = u_r[bb, 0, :, sl] - _mm(w_r[bb, 0, :, sl], st)
                z = jnp.zeros_like(vn)
                vst = jnp.concatenate([jnp.where(left, vn, z), jnp.where(left, z, vn)], axis=0)
                o_r[bb, :, sl] = _mm(qg_r[bb, 0, :, sl], st) + _mm(qk_r[bb, 0, :, sl], vst)
                upd = _mm_tn(kg_r[bb, 0, :, sl], vn)
                s_ref[si] = st * gl_r[bb, 0, 0, :, sl] + jnp.where(same_head, upd, jnp.zeros_like(upd))


DN_SCAN_BATCH = 4


def _dnscan_call(local, B, S, C):
    u, w, qg, kg, qk, gl = local
    nch, nc = S // DN_CHUNK, C // DN_CHUNK
    bb = DN_SCAN_BATCH if B % DN_SCAN_BATCH == 0 else 1

    def bidx(n):
        return jnp.where(n < nc, nc - 1 - n, nch - 1 + nc - n)

    fspec = pl.BlockSpec((bb, 1, DN_CHUNK, 512), lambda b, n: (b, 0, n, 0))
    bspec = pl.BlockSpec((bb, 1, DN_CHUNK, 512), lambda b, n: (b, 1, bidx(n), 0))
    fgl = pl.BlockSpec((bb, 1, 1, 1, 512), lambda b, n: (b, 0, n, 0, 0))
    bgl = pl.BlockSpec((bb, 1, 1, 1, 512), lambda b, n: (b, 1, bidx(n), 0, 0))
    return pl.pallas_call(
        _dnscan_kernel,
        grid=(B // bb, nch),
        in_specs=[fspec] * 5 + [fgl] + [bspec] * 5 + [bgl],
        out_specs=[pl.BlockSpec((bb, DN_CHUNK, 512), lambda b, n: (b, n, 0)),
                   pl.BlockSpec((bb, DN_CHUNK, 512), lambda b, n: (b, bidx(n), 0))],
        out_shape=[jax.ShapeDtypeStruct((B, S, 512), F32)] * 2,
        scratch_shapes=[pltpu.VMEM((bb * 8, LANES, LANES), F32)],
        compiler_params=_params("parallel", "arbitrary"),
        name="dn_scan",
    )(u, w, qg, kg, qk, gl, u, w, qg, kg, qk, gl)


def _pack_pairs(v):
    w = v.shape[1] // 2
    bits = lax.bitcast_convert_type(v.astype(jnp.bfloat16).astype(F32), jnp.int32)
    return lax.shift_right_logical(bits[:, :w], 16) | bits[:, w:]


def _unpack_pairs(p):
    lo = lax.bitcast_convert_type(lax.shift_left(p, 16), F32)
    hi = lax.bitcast_convert_type(p & jnp.int32(-65536), F32)
    return lo, hi


def _merge_kernel(x_ref, ga_ref, shf_ref, scf_ref, omla_ref, of_ref, ob_ref, dz_ref, ogqa_ref, gates_ref,
                  wo1_ref, wo2_ref, wo3_ref, wout_ref, dnn_ref, bd_ref, lng_ref, lnb_ref, wr_ref, rb_ref,
                  x1_out, ufp_out, eidx_out, rank_out, ew_out, cnt_out, cnt_ref):
    @pl.when(pl.program_id(0) == 0)
    def _():
        cnt_ref[...] = jnp.zeros_like(cnt_ref)

    o = of_ref[...] + ob_ref[...]
    ms = _exact_mm(o * o, bd_ref[...]) * (1.0 / DN_HEAD_DIM)
    dn = o * lax.rsqrt(ms + NORM_EPS) * dnn_ref[...] * _silu(dz_ref[...].astype(F32))
    g1 = gates_ref[:, 0:1024].astype(F32)
    g2 = gates_ref[:, 1024:2048].astype(F32)
    g3 = gates_ref[:, 2048:3072].astype(F32)
    m = (g1 * _mm(omla_ref[...], wo1_ref[...]) + g2 * _mm(dn, wo2_ref[...])
         + g3 * _mm(ogqa_ref[...], wo3_ref[...]))
    y = _mm(m, wout_ref[...])
    x1 = _layernorm(DEEPNORM_ALPHA * x_ref[...] + ga_ref[0] * y, lng_ref[...], lnb_ref[...])
    x1_out[...] = x1
    uf = x1 * (1.0 + scf_ref[0]) + shf_ref[0]
    _split_pieces(_pack_pairs(uf), ufp_out)

    tm = uf.shape[0]
    scores = jax.nn.sigmoid(_mm_nt(wr_ref[...], uf))[0:N_EXPERTS]
    sel = scores + rb_ref[0:N_EXPERTS, :]
    gsz = N_EXPERTS // N_GROUPS
    neg = jnp.full((), -jnp.inf, F32)
    sel3 = sel.reshape(N_GROUPS, gsz, tm)
    mem = lax.broadcasted_iota(jnp.int32, (N_GROUPS, gsz, tm), 1)
    m1 = jnp.max(sel3, 1, keepdims=True)
    i1 = jnp.min(jnp.where(sel3 == m1, mem, gsz), 1, keepdims=True)
    m2 = jnp.max(jnp.where(mem == i1, neg, sel3), 1, keepdims=True)
    gs = (m1 + m2).reshape(N_GROUPS, tm)
    gi = lax.broadcasted_iota(jnp.int32, (N_GROUPS, tm), 0)
    grank = jnp.zeros((N_GROUPS, tm), jnp.int32)
    for gp in range(N_GROUPS):
        other = gs[gp:gp + 1, :]
        beats = (other > gs) | ((other == gs) & (gp < gi))
        grank = grank + beats.astype(jnp.int32)
    gsel = (grank < TOPK_GROUPS).reshape(N_GROUPS, 1, tm)
    cur = jnp.where(gsel, sel3, neg).reshape(N_EXPERTS, tm)
    ei = lax.broadcasted_iota(jnp.int32, (N_EXPERTS, tm), 0)
    zero = jnp.zeros((N_EXPERTS, tm), F32)
    one = jnp.ones((N_EXPERTS, tm), F32)
    chosen = zero
    picks = []
    for _ in range(TOP_K):
        mx = jnp.max(cur, 0, keepdims=True)
        ix = jnp.min(jnp.where(cur == mx, ei, N_EXPERTS), 0, keepdims=True)
        pick = ei == ix
        picks.append((ix, pick))
        chosen = chosen + jnp.where(pick, one, zero)
        cur = jnp.where(pick, neg, cur)

    r_i = lax.broadcasted_iota(jnp.int32, (tm, tm), 0)
    c_i = lax.broadcasted_iota(jnp.int32, (tm, tm), 1)
    before = jnp.where(r_i < c_i, 1.0, 0.0)
    pos = cnt_ref[:, 0:1] + _mm(chosen, before)
    cnt_new = cnt_ref[...] + jnp.sum(chosen, 1, keepdims=True)
    cnt_ref[...] = cnt_new
    cnt_out[...] = cnt_new

    w_rows = [jnp.sum(jnp.where(pick, scores, zero), 0, keepdims=True) for _, pick in picks]
    wsum = w_rows[0]
    for w_k in w_rows[1:]:
        wsum = wsum + w_k
    eidx_out[...] = jnp.concatenate([ix for ix, _ in picks], axis=0)
    rank_out[...] = jnp.concatenate(
        [jnp.sum(jnp.where(pick, pos, zero), 0, keepdims=True) for _, pick in picks], axis=0).astype(jnp.int32)
    w8 = jnp.concatenate([w_k / wsum * ROUTED_SCALE for w_k in w_rows], axis=0)
    ew_out[...] = jnp.concatenate([w8, jnp.zeros((LANES - TOP_K, tm), F32)], axis=0).T


def _merge_call(xs, mod_l, omla, of, ob, dz, ogqa, gates, wo1, wo2, wo3, wout, dnn, bd, lng, lnb, wr, rb, B, S, C):
    T, D = xs.shape
    tpb, ctiles = S // TM, C // TM
    row = lambda i: (i, 0)
    const = lambda i: (0, 0)
    modspec = lambda k: pl.BlockSpec((1, 1, D), _mod_index(tpb, ctiles, k))
    return pl.pallas_call(
        _merge_kernel,
        grid=(T // TM,),
        in_specs=[pl.BlockSpec((TM, D), row), modspec(2), modspec(3), modspec(4),
                  pl.BlockSpec((TM, 512), row), pl.BlockSpec((TM, 512), row), pl.BlockSpec((TM, 512), row),
                  pl.BlockSpec((TM, 512), row), pl.BlockSpec((TM, 512), row), pl.BlockSpec((TM, 3072), row),
                  pl.BlockSpec((512, D), const), pl.BlockSpec((512, D), const), pl.BlockSpec((512, D), const),
                  pl.BlockSpec((D, D), const), pl.BlockSpec((1, 512), const), pl.BlockSpec((512, 512), const),
                  pl.BlockSpec((1, D), const), pl.BlockSpec((1, D), const),
                  pl.BlockSpec((LANES, D), const), pl.BlockSpec((LANES, 1), const)],
        out_specs=[pl.BlockSpec((TM, D), row), pl.BlockSpec((N_PIECES, TM, PIECE), lambda i: (0, i, 0)),
                   pl.BlockSpec((TOP_K, TM), lambda i: (0, i)), pl.BlockSpec((TOP_K, TM), lambda i: (0, i)),
                   pl.BlockSpec((TM, LANES), row), pl.BlockSpec((N_EXPERTS, LANES), const)],
        out_shape=[jax.ShapeDtypeStruct((T, D), F32), jax.ShapeDtypeStruct((N_PIECES, T, PIECE), jnp.int32),
                   jax.ShapeDtypeStruct((TOP_K, T), jnp.int32), jax.ShapeDtypeStruct((TOP_K, T), jnp.int32),
                   jax.ShapeDtypeStruct((T, LANES), F32), jax.ShapeDtypeStruct((N_EXPERTS, LANES), F32)],
        scratch_shapes=[pltpu.VMEM((N_EXPERTS, LANES), F32)],
        compiler_params=_params("arbitrary"),
        name="merge_norm_route",
    )(xs, mod_l, mod_l, mod_l, omla, of.reshape(T, 512), ob.reshape(T, 512), dz, ogqa, gates,
      wo1, wo2, wo3, wout, dnn, bd, lng, lnb, wr, rb)


EXPERT_BLOCK = 512
SC_WINDOW = 128
N_PIECES = 2
PIECE = D_MODEL // 2 // N_PIECES


def _split_pieces(packed, out_ref):
    for h in range(N_PIECES):
        out_ref[h] = packed[:, h * PIECE:(h + 1) * PIECE]


def _mm_pieces(pieces, w):
    acc = None
    for h, (lo, hi) in enumerate(pieces):
        t = (_mm(lo, w[h * PIECE:(h + 1) * PIECE, :])
             + _mm(hi, w[D_MODEL // 2 + h * PIECE:D_MODEL // 2 + (h + 1) * PIECE, :]))
        acc = t if acc is None else acc + t
    return acc


def _sc_mesh():
    return plsc.VectorSubcoreMesh(core_axis_name="c", subcore_axis_name="s")


def _sc_gather_rows(y, idx):
    n = idx.shape[1]
    W = y.shape[1]

    @pl.kernel(out_type=jax.ShapeDtypeStruct((n, W), y.dtype), mesh=_sc_mesh(), scratch_types=[])
    def gather(y_hbm, i_hbm, o_hbm):
        def body(i_vmem, o_vmem):
            pltpu.sync_copy(y_hbm.at[i_vmem.at[0]], o_vmem)

        pltpu.emit_pipeline(
            body,
            grid=(n // SC_WINDOW,),
            in_specs=[pl.BlockSpec((1, SC_WINDOW), lambda i: (0, i))],
            out_specs=[pl.BlockSpec((SC_WINDOW, W), lambda i: (i, 0))],
            core_axis_name=("c", "s"),
            dimension_semantics=(pltpu.PARALLEL,),
        )(i_hbm, o_hbm)

    return gather(y, idx)


def _experts_kernel(be_ref, nv_ref, xb_ref, wg_ref, wu_ref, wd_ref, y_out):
    b = pl.program_id(0)
    nv = nv_ref[b]

    @pl.when(nv > 0)
    def _():
        rows = lax.broadcasted_iota(jnp.int32, xb_ref.shape[1:], 0)
        pieces = []
        for h in range(N_PIECES):
            xh = xb_ref[h]
            pieces.append(_unpack_pairs(jnp.where(rows < nv, xh, jnp.zeros_like(xh))))
        hid = _silu(_mm_pieces(pieces, wg_ref.at[0])) * _mm_pieces(pieces, wu_ref.at[0])
        _split_pieces(_pack_pairs(_mm(hid, wd_ref[0])), y_out)


def _experts_call(xb, block_e, nvalid, wg, wu, wd):
    _, R, _ = xb.shape
    D = D_MODEL
    blk = (N_PIECES, EXPERT_BLOCK, PIECE)
    grid_spec = pltpu.PrefetchScalarGridSpec(
        num_scalar_prefetch=2,
        grid=(R // EXPERT_BLOCK,),
        in_specs=[pl.BlockSpec(blk, lambda b, be, nv: (0, b, 0)),
                  pl.BlockSpec((1, D, EXPERT_DIM), lambda b, be, nv: (be[b], 0, 0)),
                  pl.BlockSpec((1, D, EXPERT_DIM), lambda b, be, nv: (be[b], 0, 0)),
                  pl.BlockSpec((1, EXPERT_DIM, D), lambda b, be, nv: (be[b], 0, 0))],
        out_specs=pl.BlockSpec(blk, lambda b, be, nv: (0, b, 0)),
    )
    return pl.pallas_call(
        _experts_kernel,
        grid_spec=grid_spec,
        out_shape=jax.ShapeDtypeStruct((N_PIECES, R, PIECE), jnp.int32),
        compiler_params=_params("arbitrary"),
        name="moe_experts",
    )(block_e, nvalid, xb, wg, wu, wd)


def _combine_kernel(x_ref, ufp_ref, yg_ref, ew_ref, gf_ref, sg_ref, su_ref, sd_ref, g_ref, b_ref, o_ref):
    pieces = [_unpack_pairs(ufp_ref[h]) for h in range(N_PIECES)]
    hs = _silu(_mm_pieces(pieces, sg_ref)) * _mm_pieces(pieces, su_ref)
    f = _mm(hs, sd_ref[...])
    ew = ew_ref[...]
    lane = lax.broadcasted_iota(jnp.int32, ew.shape, 1)
    acc = [[jnp.zeros((x_ref.shape[0], PIECE), F32) for _ in range(N_PIECES)] for _ in range(2)]
    for k in range(TOP_K):
        wk = jnp.sum(jnp.where(lane == k, ew, jnp.zeros_like(ew)), axis=1, keepdims=True)
        for h in range(N_PIECES):
            ylo, yhi = _unpack_pairs(yg_ref[h, k])
            acc[0][h] = acc[0][h] + wk * ylo
            acc[1][h] = acc[1][h] + wk * yhi
    f = f + jnp.concatenate(acc[0] + acc[1], axis=1)
    o_ref[...] = _layernorm(DEEPNORM_ALPHA * x_ref[...] + gf_ref[0] * f, g_ref[...], b_ref[...])


def _combine_call(x1, ufp, yg, ew, mod_l, sg, su, sd, g, b, B, S, C):
    T, D = x1.shape
    tpb, ctiles = S // TM, C // TM
    row = lambda i: (i, 0)
    const = lambda i: (0, 0)
    return pl.pallas_call(
        _combine_kernel,
        grid=(T // TM,),
        in_specs=[pl.BlockSpec((TM, D), row), pl.BlockSpec((N_PIECES, TM, PIECE), lambda i: (0, i, 0)),
                  pl.BlockSpec((N_PIECES, TOP_K, TM, PIECE), lambda i: (0, 0, i, 0)),
                  pl.BlockSpec((TM, LANES), row),
                  pl.BlockSpec((1, 1, D), _mod_index(tpb, ctiles, 5)),
                  pl.BlockSpec((D, SHARED_DIM), const), pl.BlockSpec((D, SHARED_DIM), const),
                  pl.BlockSpec((SHARED_DIM, D), const),
                  pl.BlockSpec((1, D), const), pl.BlockSpec((1, D), const)],
        out_specs=pl.BlockSpec((TM, D), row),
        out_shape=jax.ShapeDtypeStruct((T, D), F32),
        compiler_params=_params("parallel"),
        name="moe_combine_norm",
    )(x1, ufp, yg, ew, mod_l, sg, su, sd, g, b)


def _moe_routed(ufp, eidx_t, rank_t, counts, wg, wu, wd):
    T = ufp.shape[1]
    n_blocks = -(-(T * TOP_K + N_EXPERTS * (EXPERT_BLOCK - 1)) // EXPERT_BLOCK)
    n_rows = n_blocks * EXPERT_BLOCK
    padded = (counts + EXPERT_BLOCK - 1) // EXPERT_BLOCK * EXPERT_BLOCK
    pad_end = jnp.cumsum(padded)
    start_pad = pad_end - padded
    experts = jnp.arange(N_EXPERTS, dtype=jnp.int32)

    def lookup(table, idx):
        sel = idx[None] == experts.reshape((N_EXPERTS,) + (1,) * idx.ndim)
        return jnp.sum(jnp.where(sel, table.reshape((N_EXPERTS,) + (1,) * idx.ndim), 0), axis=0)

    dest_t = lookup(start_pad, eidx_t) + rank_t
    blk = jnp.arange(n_blocks, dtype=jnp.int32) * EXPERT_BLOCK
    block_e = jnp.minimum(jnp.sum((blk[:, None] >= pad_end[None, :]).astype(jnp.int32), axis=1), N_EXPERTS - 1)
    nvalid = jnp.clip(lookup(counts, block_e) - (blk - lookup(start_pad, block_e)), 0, EXPERT_BLOCK)
    tok = jnp.broadcast_to(jnp.arange(T, dtype=jnp.int32), (TOP_K, T))
    row_tok = (jnp.arange(n_rows, dtype=jnp.int32) % T).at[dest_t.reshape(-1)].set(
        tok.reshape(-1), unique_indices=True)
    piece = jnp.arange(N_PIECES, dtype=jnp.int32)
    src = (piece[:, None] * T + row_tok[None, :]).reshape(1, N_PIECES * n_rows)
    xb = _sc_gather_rows(ufp.reshape(N_PIECES * T, PIECE), src).reshape(N_PIECES, n_rows, PIECE)
    yb = _experts_call(xb, block_e.astype(jnp.int32), nvalid.astype(jnp.int32), wg, wu, wd)
    back = (piece[:, None, None] * n_rows + dest_t[None]).reshape(1, N_PIECES * TOP_K * T)
    yg = _sc_gather_rows(yb.reshape(N_PIECES * n_rows, PIECE), back)
    return yg.reshape(N_PIECES, TOP_K, T, PIECE)


def _rot_cols(w, half):
    return jnp.concatenate([-w[:, half:], w[:, :half]], axis=1)


def _prep_w_in(w):
    d = w.shape[0]
    offs = np.cumsum((0,) + IN_SIZES)
    cq, ckv, kr, dqkv, da, db, dz, gq, gk, gv, gates = (w[:, offs[t]:offs[t + 1]] for t in range(len(IN_SIZES)))
    z = lambda n: jnp.zeros((d, n), w.dtype)
    krg = jnp.concatenate([z(64), kr, z(32)], 1)
    krr = jnp.concatenate([z(64), _rot_cols(kr, MLA_ROPE // 2), z(32)], 1)
    ab = jnp.concatenate([da, db, z(LANES - 4 * DN_HEADS)], 1)
    hd = GQA_HEAD_DIM
    gq_rot = jnp.concatenate([_rot_cols(gq[:, h * hd:(h + 1) * hd], hd // 2) for h in range(GQA_HEADS)], 1)
    dup = lambda t: jnp.concatenate([t[:, 0:hd], t[:, 0:hd], t[:, hd:2 * hd], t[:, hd:2 * hd]], 1)
    gk_rot = jnp.concatenate([_rot_cols(gk[:, h * hd:(h + 1) * hd], hd // 2) for h in range(GQA_KV_HEADS)], 1)
    cat = jnp.concatenate([cq, ckv, krg, krr, dqkv, ab, dz, gq, gq_rot, dup(gk), dup(gk_rot), dup(gv), gates], 1)
    assert cat.shape[1] == NZ
    return cat.astype(MXU_DTYPE)


def _prep_w_uq(w):
    d = w.shape[0]
    hw = MLA_NOPE + MLA_ROPE
    a, b = [], []
    for h in range(MLA_HEADS):
        wh = w[:, h * hw:(h + 1) * hw]
        a += [wh, jnp.zeros((d, LANES - hw), w.dtype)]
        b += [jnp.zeros((d, MLA_NOPE), w.dtype), _rot_cols(wh[:, MLA_NOPE:], MLA_ROPE // 2),
              jnp.zeros((d, LANES - hw), w.dtype)]
    return jnp.concatenate(a + b, 1).astype(MXU_DTYPE)


def _prep_w_ukv(w):
    d = w.shape[0]
    hw = MLA_NOPE + MLA_V
    kpart, vpart = [], []
    for h in range(MLA_HEADS):
        wh = w[:, h * hw:(h + 1) * hw]
        kpart += [wh[:, :MLA_NOPE], jnp.zeros((d, LANES - MLA_NOPE), w.dtype)]
        vpart += [wh[:, MLA_NOPE:]]
    return jnp.concatenate(kpart + vpart, 1).astype(MXU_DTYPE)


def _rope_tables(n_rows, C):
    row = jnp.repeat(jnp.arange(n_rows, dtype=F32), GRID_W)
    col = jnp.tile(jnp.arange(GRID_W, dtype=F32), n_rows)

    def angles(dim):
        n = dim // 4
        inv = ROPE_BASE ** (-jnp.arange(n, dtype=F32) / n)
        return jnp.concatenate([row[:, None] * inv, col[:, None] * inv], axis=-1)

    def with_ctx(cos, sin):
        return (jnp.concatenate([jnp.ones((C, LANES), F32), cos], 0),
                jnp.concatenate([jnp.zeros((C, LANES), F32), sin], 0))

    L = n_rows * GRID_W
    am = angles(MLA_ROPE)
    one, zero = jnp.ones((L, MLA_NOPE), F32), jnp.zeros((L, MLA_NOPE), F32)
    cm = jnp.concatenate([one, jnp.cos(am), jnp.cos(am), one[:, :32]], 1)
    sm = jnp.concatenate([zero, jnp.sin(am), jnp.sin(am), zero[:, :32]], 1)
    ag = angles(GQA_HEAD_DIM)
    cg = jnp.tile(jnp.cos(ag), (1, 4))
    sg = jnp.tile(jnp.sin(ag), (1, 4))
    return with_ctx(cm, sm) + with_ctx(cg, sg)


def kernel(x, c, ctx, c_ctx, w_ada, b_ada, w_in, mla_q_norm, mla_kv_norm, w_uq, w_ukv, dn_conv, dn_a_log, dn_dt_bias, dn_norm, gqa_sink, w_o_mla, w_o_dn, w_o_gqa, w_out, ln1_g, ln1_b, w_router, router_bias, w_exp_gate, w_exp_up, w_exp_down, w_sh_gate, w_sh_up, w_sh_down, ln2_g, ln2_b):
    B, L, D = x.shape
    C = ctx.shape[1]
    S = C + L
    nl = w_in.shape[0]
    assert D == D_MODEL and nl == DEPTH and B <= CTX_MOD_ROW
    assert C % TM == 0 and L % TM == 0 and L % GRID_W == 0 and L >= 3 * WINDOW
    cast = lambda t: t.astype(MXU_DTYPE)

    cc = jnp.zeros((MOD_ROWS, D), F32).at[0:B].set(c).at[CTX_MOD_ROW].set(c_ctx)
    mods = _ada_call(cc, w_ada, b_ada)
    tabs = _rope_tables(L // GRID_W, C)
    dn_consts = _dn_constants()
    xs = jnp.concatenate([ctx, x], axis=1).reshape(B * S, D)

    for l in range(nl):
        mod_l = mods[l].reshape(MOD_ROWS * 6, 1, D)
        q, k, v, dqkv, ab, dz, gq, gkv, gates = _inproj_call(
            xs, mod_l, _prep_w_in(w_in[l]), _prep_w_uq(w_uq[l]), _prep_w_ukv(w_ukv[l]),
            mla_q_norm[l].reshape(1, -1), mla_kv_norm[l].reshape(1, -1), tabs, B, S, C)
        omla = _mla_call(q, k, v, B, S, C)
        conv8 = jnp.zeros((8, 3 * DN_WIDTH), F32).at[0:DN_CONV].set(dn_conv[l])
        gp = (jnp.zeros((8, LANES), F32).at[0, 0:2 * DN_HEADS].set(dn_a_log[l].reshape(-1))
              .at[1, 0:2 * DN_HEADS].set(dn_dt_bias[l].reshape(-1)))
        local = _dnlocal_call(dqkv, ab, conv8, gp, dn_consts, B, S, C)
        of, ob = _dnscan_call(local, B, S, C)
        ogqa = _gqa_call(gq, gkv, gqa_sink[l], B, S, C)
        wr = jnp.zeros((LANES, D), F32).at[0:N_EXPERTS].set(w_router[l].T)
        rb = jnp.zeros((LANES, 1), F32).at[0:N_EXPERTS, 0].set(router_bias[l])
        x1, ufp, eidx_t, rank_t, ew, cnt = _merge_call(
            xs, mod_l, omla, of, ob, dz, ogqa, gates,
            cast(w_o_mla[l]), cast(w_o_dn[l]), cast(w_o_gqa[l]), cast(w_out[l]),
            jnp.tile(dn_norm[l], DN_HEADS).reshape(1, DN_WIDTH), dn_consts[0],
            ln1_g[l].reshape(1, D), ln1_b[l].reshape(1, D), cast(wr), rb, B, S, C)
        yg = _moe_routed(ufp, eidx_t, rank_t, cnt[:, 0].astype(jnp.int32),
                         cast(w_exp_gate[l]), cast(w_exp_up[l]), cast(w_exp_down[l]))
        xs = _combine_call(x1, ufp, yg, ew, mod_l, cast(w_sh_gate[l]), cast(w_sh_up[l]), cast(w_sh_down[l]),
                           ln2_g[l].reshape(1, D), ln2_b[l].reshape(1, D), B, S, C)
    return xs.reshape(B, S, D)[:, C:, :]
```

```python
import functools

import numpy as np
import jax
import jax.numpy as jnp
from jax import lax
from jax.experimental import pallas as pl
from jax.experimental.pallas import tpu as pltpu
from jax.experimental.pallas import tpu_sc as plsc

F32 = jnp.float32
MXU_DTYPE = jnp.bfloat16

D_MODEL = 1024
DEPTH = 4
GRID_W = 64
NORM_EPS = 1e-6
ROPE_BASE = 10000.0
DEEPNORM_ALPHA = (2.0 * DEPTH) ** 0.25

MLA_HEADS = 8
MLA_Q_LORA = 256
MLA_KV_LORA = 128
MLA_NOPE = 64
MLA_ROPE = 32
MLA_V = 64
MLA_SCALE = (MLA_NOPE + MLA_ROPE) ** -0.5
LOG2E = float(np.log2(np.e))

DN_HEADS = 8
DN_HEAD_DIM = 64
DN_WIDTH = DN_HEADS * DN_HEAD_DIM
DN_CONV = 5
DN_CHUNK = 64

GQA_HEADS = 8
GQA_KV_HEADS = 2
GQA_HEAD_DIM = 64
GQA_SCALE = GQA_HEAD_DIM ** -0.5
WINDOW = 128

N_EXPERTS = 64
TOP_K = 8
N_GROUPS = 8
TOPK_GROUPS = 4
EXPERT_DIM = 256
SHARED_DIM = 256
ROUTED_SCALE = 2.5

IN_SIZES = (MLA_Q_LORA, MLA_KV_LORA, MLA_ROPE,
            3 * DN_WIDTH, 2 * DN_HEADS, 2 * DN_HEADS, DN_WIDTH,
            GQA_HEADS * GQA_HEAD_DIM, GQA_KV_HEADS * GQA_HEAD_DIM, GQA_KV_HEADS * GQA_HEAD_DIM,
            3 * D_MODEL)

LANES = 128
TM = 256
MOD_ROWS = 16
CTX_MOD_ROW = 8

OFF_A = 0
OFF_DQKV = 640
OFF_AB = OFF_DQKV + 3 * DN_WIDTH
OFF_DZ = OFF_AB + LANES
OFF_GQ = OFF_DZ + DN_WIDTH
OFF_GK = OFF_GQ + 1024
OFF_GATES = OFF_GK + 768
NZ = OFF_GATES + 3 * D_MODEL

VMEM_LIMIT = 56 * 1024 * 1024


def _mm(a, b):
    return jnp.dot(a.astype(MXU_DTYPE), b.astype(MXU_DTYPE), preferred_element_type=F32)


def _mm_nt(a, b):
    return lax.dot_general(a.astype(MXU_DTYPE), b.astype(MXU_DTYPE), (((1,), (1,)), ((), ())),
                           preferred_element_type=F32)


def _mm_tn(a, b):
    return lax.dot_general(a.astype(MXU_DTYPE), b.astype(MXU_DTYPE), (((0,), (0,)), ((), ())),
                           preferred_element_type=F32)


def _bmm(a, b):
    return jnp.einsum('cik,ckj->cij', a.astype(MXU_DTYPE), b.astype(MXU_DTYPE), preferred_element_type=F32)


def _bmm_nt(a, b):
    return jnp.einsum('cik,cjk->cij', a.astype(MXU_DTYPE), b.astype(MXU_DTYPE), preferred_element_type=F32)


def _split3(x):
    hi = x.astype(jnp.bfloat16).astype(F32)
    r = x - hi
    mid = r.astype(jnp.bfloat16).astype(F32)
    lo = (r - mid).astype(jnp.bfloat16).astype(F32)
    return hi, mid, lo


def _exact_mm(x, m01):
    hi, mid, lo = _split3(x)
    return _mm(hi, m01) + _mm(mid, m01) + _mm(lo, m01)


def _exact_mm_left(m01, x):
    hi, mid, lo = _split3(x)
    return _mm(m01, hi) + _mm(m01, mid) + _mm(m01, lo)


def _silu(x):
    return x * jax.nn.sigmoid(x)


def _layernorm(v, g, b):
    mu = jnp.mean(v, -1, keepdims=True)
    d = v - mu
    var = jnp.mean(d * d, -1, keepdims=True)
    return d * lax.rsqrt(var + NORM_EPS) * g + b


def _mod_index(tiles_per_b, ctx_tiles, k):
    def index(i):
        row = jnp.where((i % tiles_per_b) < ctx_tiles, CTX_MOD_ROW, i // tiles_per_b)
        return (row * 6 + k, 0, 0)
    return index


def _params(*sem):
    return pltpu.CompilerParams(dimension_semantics=sem, vmem_limit_bytes=VMEM_LIMIT)


def _ada_kernel(c_ref, w_ref, b_ref, o_ref):
    o_ref[0] = _mm(_silu(c_ref[...]), w_ref[0]) + b_ref[0]


def _ada_call(cc, w_ada, b_ada):
    nl, d, n6 = w_ada.shape
    tn = 1536
    return pl.pallas_call(
        _ada_kernel,
        grid=(nl, n6 // tn),
        in_specs=[pl.BlockSpec((MOD_ROWS, d), lambda l, j: (0, 0)),
                  pl.BlockSpec((1, d, tn), lambda l, j: (l, 0, j)),
                  pl.BlockSpec((1, 1, tn), lambda l, j: (l, 0, j))],
        out_specs=pl.BlockSpec((1, MOD_ROWS, tn), lambda l, j: (l, 0, j)),
        out_shape=jax.ShapeDtypeStruct((nl, MOD_ROWS, n6), F32),
        compiler_params=_params("parallel", "parallel"),
        name="ada_mod",
    )(cc, w_ada, b_ada.reshape(nl, 1, n6))


def _inproj_kernel(x_ref, sh_ref, sc_ref, w_ref, wuq_ref, wukv_ref, qn_ref, kvn_ref,
                   cm_ref, sm_ref, cg_ref, sg_ref,
                   q_out, k_out, v_out, dqkv_out, ab_out, dz_out, gq_out, gkv_out, gates_out):
    u = (x_ref[...] * (1.0 + sc_ref[0]) + sh_ref[0]).astype(MXU_DTYPE)

    def z(off, width):
        return jnp.dot(u, w_ref[:, off:off + width], preferred_element_type=F32)

    def rms(v, g):
        return v * lax.rsqrt(jnp.mean(v * v, -1, keepdims=True) + NORM_EPS) * g

    cm, sm, cg, sg = cm_ref[...], sm_ref[...], cg_ref[...], sg_ref[...]

    za = z(OFF_A, 640)
    qq = _mm(rms(za[:, 0:256], qn_ref[...]), wuq_ref[...])
    kvv = _mm(rms(za[:, 256:384], kvn_ref[...]), wukv_ref[...])
    k_rope = za[:, 384:512] * cm + za[:, 512:640] * sm
    for h in range(MLA_HEADS):
        sl = slice(h * LANES, (h + 1) * LANES)
        qa = qq[:, h * LANES:(h + 1) * LANES]
        qb = qq[:, 1024 + h * LANES:1024 + (h + 1) * LANES]
        q_out[:, sl] = ((qa * cm + qb * sm) * (MLA_SCALE * LOG2E)).astype(q_out.dtype)
        k_out[:, sl] = (kvv[:, sl] + k_rope).astype(k_out.dtype)
    v_out[...] = kvv[:, 1024:1536].astype(v_out.dtype)

    for t in range(3):
        dqkv_out[:, t * 512:(t + 1) * 512] = z(OFF_DQKV + t * 512, 512).astype(dqkv_out.dtype)
    ab_out[...] = z(OFF_AB, LANES)
    dz_out[...] = z(OFF_DZ, DN_WIDTH).astype(dz_out.dtype)

    zq = z(OFF_GQ, 1024)
    for p in range(4):
        sl = slice(p * LANES, (p + 1) * LANES)
        gq_out[:, sl] = ((zq[:, sl] * cg + zq[:, 512 + p * LANES:512 + (p + 1) * LANES] * sg)
                         * (GQA_SCALE * LOG2E)).astype(gq_out.dtype)
    zk = z(OFF_GK, 768)
    for j in range(2):
        sl = slice(j * LANES, (j + 1) * LANES)
        gkv_out[:, sl] = (zk[:, sl] * cg + zk[:, 256 + j * LANES:256 + (j + 1) * LANES] * sg).astype(gkv_out.dtype)
    gkv_out[:, 256:512] = zk[:, 512:768].astype(gkv_out.dtype)

    for t in range(3):
        gates_out[:, t * 1024:(t + 1) * 1024] = jax.nn.sigmoid(z(OFF_GATES + t * 1024, 1024)).astype(gates_out.dtype)


def _inproj_call(xs, mod_l, w_cat, wuq_cat, wukv_cat, qn, kvn, tabs, B, S, C):
    T, D = xs.shape
    tpb, ctiles = S // TM, C // TM
    act = MXU_DTYPE
    row = lambda i: (i, 0)
    const = lambda i: (0, 0)
    tab = lambda i: (i % tpb, 0)
    widths = (1024, 1024, 512, 1536, LANES, 512, 512, 512, 3072)
    dtypes = (act, act, act, act, F32, act, act, act, act)
    return pl.pallas_call(
        _inproj_kernel,
        grid=(T // TM,),
        in_specs=[pl.BlockSpec((TM, D), row),
                  pl.BlockSpec((1, 1, D), _mod_index(tpb, ctiles, 0)),
                  pl.BlockSpec((1, 1, D), _mod_index(tpb, ctiles, 1)),
                  pl.BlockSpec((D, NZ), const),
                  pl.BlockSpec((MLA_Q_LORA, 2048), const),
                  pl.BlockSpec((MLA_KV_LORA, 1536), const),
                  pl.BlockSpec((1, MLA_Q_LORA), const),
                  pl.BlockSpec((1, MLA_KV_LORA), const)]
                 + [pl.BlockSpec((TM, LANES), tab)] * 4,
        out_specs=[pl.BlockSpec((TM, w), row) for w in widths],
        out_shape=[jax.ShapeDtypeStruct((T, w), dt) for w, dt in zip(widths, dtypes)],
        compiler_params=_params("parallel"),
        name="in_proj",
    )(xs, mod_l, mod_l, w_cat, wuq_cat, wukv_cat, qn, kvn, *tabs)


def _mla_kernel(q_ref, k_ref, v_ref, o_ref, *, n_ctx, n_all):
    i = pl.program_id(2)
    tq = q_ref.shape[1]
    left = lax.broadcasted_iota(jnp.int32, (tq, LANES), 1) < MLA_V

    def attend(nk):
        v = v_ref[0, 0:nk, :]
        v_left = lax.broadcasted_iota(jnp.int32, v.shape, 1) < MLA_V
        ones = jnp.ones_like(v)
        outs = []
        for hh in range(2):
            q = q_ref[0, :, hh * LANES:(hh + 1) * LANES]
            k = k_ref[0, 0:nk, hh * LANES:(hh + 1) * LANES]
            s = _mm_nt(q, k)
            p = jnp.exp2((s - jnp.max(s, -1, keepdims=True)).astype(MXU_DTYPE))
            vh = jnp.where(v_left, v, ones) if hh == 0 else jnp.where(v_left, ones, v)
            o = _mm(p, vh)
            outs.append(o / pltpu.roll(o, MLA_V, axis=1))
        o_ref[0] = jnp.where(left, outs[0], outs[1]).astype(o_ref.dtype)

    ctx_tiles = n_ctx // tq

    @pl.when(i < ctx_tiles)
    def _():
        attend(n_ctx)

    @pl.when(i >= ctx_tiles)
    def _():
        attend(n_all)


def _mla_call(q, k, v, B, S, C):
    tq = 256
    q3, k3, v3 = q.reshape(B, S, 1024), k.reshape(B, S, 1024), v.reshape(B, S, 512)
    out = pl.pallas_call(
        functools.partial(_mla_kernel, n_ctx=C, n_all=S),
        grid=(B, MLA_HEADS // 2, S // tq),
        in_specs=[pl.BlockSpec((1, tq, 2 * LANES), lambda b, j, i: (b, i, j)),
                  pl.BlockSpec((1, S, 2 * LANES), lambda b, j, i: (b, 0, j)),
                  pl.BlockSpec((1, S, LANES), lambda b, j, i: (b, 0, j))],
        out_specs=pl.BlockSpec((1, tq, LANES), lambda b, j, i: (b, i, j)),
        out_shape=jax.ShapeDtypeStruct((B, S, 512), MXU_DTYPE),
        compiler_params=_params("parallel", "parallel", "arbitrary"),
        name="mla_attn",
    )(q3, k3, v3)
    return out.reshape(B * S, 512)


def _gqa_kernel(sink_ref, q_ref, kv_ref, o_ref, *, n_ctx, n_all):
    i = pl.program_id(1)
    qb_rows = q_ref.shape[1]
    span = qb_rows + 2 * WINDOW
    ctx_blocks = n_ctx // qb_rows
    group = GQA_HEADS // GQA_KV_HEADS
    left = lax.broadcasted_iota(jnp.int32, (qb_rows, LANES), 1) < GQA_HEAD_DIM

    def run(latent):
        nk = n_ctx + span if latent else n_ctx
        rows = group * qb_rows
        if latent:
            qb = i - ctx_blocks
            ws = pl.multiple_of(jnp.minimum((qb + 1) * qb_rows, n_all - span), qb_rows)
            col = lax.broadcasted_iota(jnp.int32, (rows, nk), 1)
            q_pos = qb * qb_rows + lax.broadcasted_iota(jnp.int32, (rows, nk), 0) % qb_rows
            k_pos = ws - 2 * n_ctx + col
            ok = (col < n_ctx) | ((jnp.abs(k_pos - q_pos) <= WINDOW) & (k_pos >= 0))
        head_of_row = lax.broadcasted_iota(jnp.int32, (rows, 1), 0) // qb_rows
        res = []
        for j in range(GQA_KV_HEADS):
            parts = []
            for g in range(group):
                qp = q_ref[0, :, (2 * j + g // 2) * LANES:(2 * j + g // 2 + 1) * LANES]
                parts.append(jnp.where(left if g % 2 == 0 else ~left, qp, jnp.zeros_like(qp)))
            q4 = jnp.concatenate(parts, axis=0)
            kc = kv_ref[0, 0:n_ctx, j * LANES:(j + 1) * LANES]
            vc = kv_ref[0, 0:n_ctx, 256 + j * LANES:256 + (j + 1) * LANES]
            if latent:
                kc = jnp.concatenate([kc, kv_ref[0, pl.ds(ws, span), j * LANES:(j + 1) * LANES]], axis=0)
                vc = jnp.concatenate([vc, kv_ref[0, pl.ds(ws, span), 256 + j * LANES:256 + (j + 1) * LANES]], axis=0)
            sink = jnp.zeros((rows, 1), F32)
            for g in range(group):
                sink = jnp.where(head_of_row == g, sink_ref[group * j + g] * LOG2E, sink)
            s = _mm_nt(q4, kc)
            if latent:
                s = jnp.where(ok, s, -jnp.inf)
            m = jnp.maximum(jnp.max(s, -1, keepdims=True), sink)
            p = jnp.exp2((s - m).astype(MXU_DTYPE))
            v_left = lax.broadcasted_iota(jnp.int32, vc.shape, 1) < GQA_HEAD_DIM
            o = _mm(p, jnp.where(v_left, vc, jnp.ones_like(vc)))
            res.append(o / (pltpu.roll(o, GQA_HEAD_DIM, axis=1) + jnp.exp2(sink - m)))
        for j in range(GQA_KV_HEADS):
            for pp in range(group // 2):
                r0 = res[j][(2 * pp) * qb_rows:(2 * pp + 1) * qb_rows]
                r1 = res[j][(2 * pp + 1) * qb_rows:(2 * pp + 2) * qb_rows]
                pair = 2 * j + pp
                o_ref[0, :, pair * LANES:(pair + 1) * LANES] = jnp.where(
                    left, r0, pltpu.roll(r1, GQA_HEAD_DIM, axis=1)).astype(o_ref.dtype)

    @pl.when(i < ctx_blocks)
    def _():
        run(False)

    @pl.when(i >= ctx_blocks)
    def _():
        run(True)


def _gqa_call(gq, gkv, sink, B, S, C):
    qb_rows = 128
    out = pl.pallas_call(
        functools.partial(_gqa_kernel, n_ctx=C, n_all=S),
        grid=(B, S // qb_rows),
        in_specs=[pl.BlockSpec(memory_space=pltpu.SMEM),
                  pl.BlockSpec((1, qb_rows, 512), lambda b, i: (b, i, 0)),
                  pl.BlockSpec((1, S, 512), lambda b, i: (b, 0, 0))],
        out_specs=pl.BlockSpec((1, qb_rows, 512), lambda b, i: (b, i, 0)),
        out_shape=jax.ShapeDtypeStruct((B, S, 512), MXU_DTYPE),
        compiler_params=_params("parallel", "arbitrary"),
        name="gqa_attn",
    )(sink, gq.reshape(B, S, 512), gkv.reshape(B, S, 512))
    return out.reshape(B * S, 512)


DN_TR = 256
DN_HALO = 16
DN_CPT = DN_TR // DN_CHUNK


def _stack(x, left):
    z = jnp.zeros_like(x)
    return jnp.concatenate([jnp.where(left, x, z), jnp.where(left, z, x)], axis=1)


def _dnlocal_kernel(main_ref, prev_ref, next_ref, ab_ref, conv_ref, gp_ref, bd_ref, trif_ref, trib_ref,
                    eg_ref, eb_ref,
                    u_out, w_out, qg_out, kg_out, qk_out, gl_out, pad_ref, *, ctx_tiles, n_tiles):
    i = pl.program_id(1)
    tr = DN_TR
    first = (i == 0) | (i == ctx_tiles)
    last = (i == ctx_tiles - 1) | (i == n_tiles - 1)
    xp = prev_ref[0].astype(F32)
    xn = next_ref[0].astype(F32)
    pad_ref[0:DN_HALO, :] = jnp.where(first, jnp.zeros_like(xp), xp)
    pad_ref[DN_HALO:DN_HALO + tr, :] = main_ref[0].astype(F32)
    pad_ref[DN_HALO + tr:, :] = jnp.where(last, jnp.zeros_like(xn), xn)
    y = jnp.zeros((tr, 3 * DN_WIDTH), F32)
    for t in range(DN_CONV):
        y = y + conv_ref[t:t + 1, :] * pad_ref[pl.ds(DN_HALO - DN_CONV // 2 + t, tr), :]
    y = _silu(y)
    q, k, v = y[:, 0:512], y[:, 512:1024], y[:, 1024:1536]
    bd = bd_ref[...]
    q = q * lax.rsqrt(_exact_mm(q * q, bd) + 1e-6) * (DN_HEAD_DIM ** -0.5)
    k = k * lax.rsqrt(_exact_mm(k * k, bd) + 1e-6)

    ab = ab_ref[0]
    g = -jnp.exp(gp_ref[0:1, :]) * jax.nn.softplus(ab + gp_ref[1:2, :])
    beta = jax.nn.sigmoid(ab)
    lane = lax.broadcasted_iota(jnp.int32, (tr, LANES), 1)
    gc = jnp.where(lane < DN_HEADS, _exact_mm_left(trif_ref[...], g), _exact_mm_left(trib_ref[...], g))
    gcx_all = _exact_mm(gc, eg_ref[...])
    bx_all = _exact_mm(beta, eb_ref[...])

    c = DN_CPT
    lane3 = lax.broadcasted_iota(jnp.int32, (1, 1, LANES), 2)
    left = (lane3 % LANES) < DN_HEAD_DIM
    tpos = lane3 % DN_HEAD_DIM
    lane6 = lax.broadcasted_iota(jnp.int32, (1, 1, 2 * LANES), 2)
    left6 = (lane6 % LANES) < DN_HEAD_DIM
    ri = lax.broadcasted_iota(jnp.int32, (1, DN_CHUNK, LANES), 1)
    cj = lax.broadcasted_iota(jnp.int32, (1, DN_CHUNK, LANES), 2) % DN_HEAD_DIM
    one = jnp.ones((), F32)
    zero = jnp.zeros((), F32)

    units = [(d, j) for d in range(2) for j in range(4)]
    st = {}
    for d, j in units:
        off = d * 512 + j * LANES
        gcx = gcx_all[:, off:off + LANES].reshape(c, DN_CHUNK, LANES)
        bx = bx_all[:, off:off + LANES].reshape(c, DN_CHUNK, LANES)
        qp = q[:, j * LANES:(j + 1) * LANES].reshape(c, DN_CHUNK, LANES)
        kp = k[:, j * LANES:(j + 1) * LANES].reshape(c, DN_CHUNK, LANES)
        vp = v[:, j * LANES:(j + 1) * LANES].reshape(c, DN_CHUNK, LANES)
        gl = gcx[:, DN_CHUNK - 1:DN_CHUNK, :] if d == 0 else gcx[:, 0:1, :]
        kb = kp * bx
        kq = _bmm_nt(jnp.concatenate([kb, qp], axis=1), _stack(kp, left))
        hi, mid, lo = _split3(gcx)
        a6 = jnp.where(tpos == 0, hi, jnp.where(tpos == 1, mid, jnp.where(tpos == 2, lo,
                       jnp.where(tpos < 6, one, zero))))
        b6 = jnp.where(tpos < 3, one, jnp.where(tpos == 3, -hi, jnp.where(tpos == 4, -mid,
                       jnp.where(tpos == 5, -lo, zero))))
        diff = _bmm_nt(a6, _stack(b6, left))
        st[d, j] = dict(gcx=gcx, bx=bx, qp=qp, kp=kp, vp=vp, gl=gl, kb=kb, kq=kq, diff=diff)
    for d, j in units:
        u = st[d, j]
        incl = (ri >= cj) if d == 0 else (ri <= cj)
        strict = (ri > cj) if d == 0 else (ri < cj)
        dm = jnp.exp(jnp.where(incl, u["diff"], -jnp.inf))
        u["qkm"] = u["kq"][:, DN_CHUNK:, :] * dm
        u["x"] = -jnp.where(strict, u["kq"][:, 0:DN_CHUNK, :] * dm, zero)
        u["r"] = u["x"]
    for d, j in units:
        u = st[d, j]
        u["x"] = _bmm(u["x"], _stack(u["x"], left))
    for level in range(5):
        for d, j in units:
            u = st[d, j]
            xs = _stack(u["x"], left)
            if level < 4:
                m = _bmm(jnp.concatenate([u["r"], u["x"]], axis=1), xs)
                u["r"] = u["r"] + u["x"] + m[:, 0:DN_CHUNK, :]
                u["x"] = m[:, DN_CHUNK:, :]
            else:
                u["r"] = u["r"] + u["x"] + _bmm(u["r"], xs)
    for d, j in units:
        u = st[d, j]
        eg = jnp.exp(u["gcx"])
        rhs = jnp.concatenate([u["vp"] * u["bx"], u["kb"] * eg], axis=-1)
        sol = rhs + _bmm(u["r"], _stack(rhs, left6))
        sl = slice(j * LANES, (j + 1) * LANES)
        u_out[0, d, :, sl] = sol[:, :, 0:LANES].reshape(tr, LANES)
        w_out[0, d, :, sl] = sol[:, :, LANES:].reshape(tr, LANES).astype(w_out.dtype)
        qg_out[0, d, :, sl] = (u["qp"] * eg).reshape(tr, LANES).astype(qg_out.dtype)
        kg_out[0, d, :, sl] = (u["kp"] * jnp.exp(u["gl"] - u["gcx"])).reshape(tr, LANES).astype(kg_out.dtype)
        qk_out[0, d, :, sl] = u["qkm"].reshape(tr, LANES).astype(qk_out.dtype)
        gl_out[0, d, :, :, sl] = jnp.exp(u["gl"])


def _dn_constants():
    idx = np.arange(DN_TR)
    same = (idx[:, None] // DN_CHUNK) == (idx[None, :] // DN_CHUNK)
    trif = (same & (idx[None, :] <= idx[:, None])).astype(np.float32)
    trib = (same & (idx[None, :] >= idx[:, None])).astype(np.float32)
    h = np.arange(512)
    bd = ((h[:, None] // DN_HEAD_DIM) == (h[None, :] // DN_HEAD_DIM)).astype(np.float32)
    col = np.arange(LANES)[:, None]
    out = np.arange(1024)[None, :]
    unit = (out // 512) * DN_HEADS + (out % 512) // DN_HEAD_DIM
    eg = (col == unit).astype(np.float32)
    eb = (col == unit + 2 * DN_HEADS).astype(np.float32)
    return tuple(jnp.asarray(a, MXU_DTYPE) for a in (bd, trif, trib, eg, eb))


def _dnlocal_call(dqkv, ab, conv_w, gp, consts, B, S, C):
    tr = DN_TR
    n_tiles = S // tr
    hb = tr // DN_HALO
    n_hblk = S // DN_HALO
    bd, trif, trib, eg, eb = consts
    const2 = lambda b, i: (0, 0)
    big = lambda b, i: (b, 0, i, 0)
    act = MXU_DTYPE
    shp = (B, 2, S, 512)
    return pl.pallas_call(
        functools.partial(_dnlocal_kernel, ctx_tiles=C // tr, n_tiles=n_tiles),
        grid=(B, n_tiles),
        in_specs=[pl.BlockSpec((1, tr, 1536), lambda b, i: (b, i, 0)),
                  pl.BlockSpec((1, DN_HALO, 1536), lambda b, i: (b, jnp.maximum(i * hb - 1, 0), 0)),
                  pl.BlockSpec((1, DN_HALO, 1536), lambda b, i: (b, jnp.minimum((i + 1) * hb, n_hblk - 1), 0)),
                  pl.BlockSpec((1, tr, LANES), lambda b, i: (b, i, 0)),
                  pl.BlockSpec((8, 1536), const2),
                  pl.BlockSpec((8, LANES), const2),
                  pl.BlockSpec((512, 512), const2),
                  pl.BlockSpec((tr, tr), const2),
                  pl.BlockSpec((tr, tr), const2),
                  pl.BlockSpec((LANES, 1024), const2),
                  pl.BlockSpec((LANES, 1024), const2)],
        out_specs=[pl.BlockSpec((1, 2, tr, 512), big)] * 5
                  + [pl.BlockSpec((1, 2, DN_CPT, 1, 512), lambda b, i: (b, 0, i, 0, 0))],
        out_shape=[jax.ShapeDtypeStruct(shp, F32)] + [jax.ShapeDtypeStruct(shp, act)] * 4
                  + [jax.ShapeDtypeStruct((B, 2, S // DN_CHUNK, 1, 512), F32)],
        scratch_shapes=[pltpu.VMEM((tr + 2 * DN_HALO, 1536), F32)],
        compiler_params=_params("parallel", "parallel"),
        name="dn_local",
    )(dqkv.reshape(B, S, 1536), dqkv.reshape(B, S, 1536), dqkv.reshape(B, S, 1536), ab.reshape(B, S, LANES),
      conv_w, gp, bd, trif, trib, eg, eb)


def _dnscan_kernel(uf, wf, qgf, kgf, qkf, glf, ub, wb, qgb, kgb, qkb, glb, of_out, ob_out, s_ref):
    n = pl.program_id(1)
    lane = lax.broadcasted_iota(jnp.int32, (1, LANES), 1)
    left = lane < DN_HEAD_DIM
    row = lax.broadcasted_iota(jnp.int32, (LANES, LANES), 0)
    col = lax.broadcasted_iota(jnp.int32, (LANES, LANES), 1)
    same_head = (row < DN_HEAD_DIM) == (col < DN_HEAD_DIM)
    dirs = ((uf, wf, qgf, kgf, qkf, glf, of_out), (ub, wb, qgb, kgb, qkb, glb, ob_out))
    units = [(bb, d, j) for bb in range(uf.shape[0]) for d in range(2) for j in range(4)]
    sidx = lambda bb, d, j: (bb * 2 + d) * 4 + j
    sl = lambda j: slice(j * LANES, (j + 1) * LANES)
    started = n > 0
    st = {t: jnp.where(started, s_ref[sidx(*t)], jnp.zeros((LANES, LANES), F32)) for t in units}
    pre = {(bb, d, j): _mm(dirs[d][1][bb, 0, :, sl(j)], st[bb, d, j]) for bb, d, j in units}
    o1 = {(bb, d, j): _mm(dirs[d][2][bb, 0, :, sl(j)], st[bb, d, j]) for bb, d, j in units}
    vn = {(bb, d, j): dirs[d][0][bb, 0, :, sl(j)] - pre[bb, d, j] for bb, d, j in units}
    for bb, d, j in units:
        v = vn[bb, d, j]
        z = jnp.zeros_like(v)
        vst = jnp.concatenate([jnp.where(left, v, z), jnp.where(left, z, v)], axis=0)
        dirs[d][6][bb, :, sl(j)] = o1[bb, d, j] + _mm(dirs[d][4][bb, 0, :, sl(j)], vst)
    for bb, d, j in units:
        upd = _mm_tn(dirs[d][3][bb, 0, :, sl(j)], vn[bb, d, j])
        s_ref[sidx(bb, d, j)] = (st[bb, d, j] * dirs[d][5][bb, 0, 0, :, sl(j)]
                                 + jnp.where(same_head, upd, jnp.zeros_like(upd)))


DN_SCAN_BATCH = 4


def _dnscan_call(local, B, S, C):
    u, w, qg, kg, qk, gl = local
    nch, nc = S // DN_CHUNK, C // DN_CHUNK
    bb = DN_SCAN_BATCH if B % DN_SCAN_BATCH == 0 else 1

    def bidx(n):
        return jnp.where(n < nc, nc - 1 - n, nch - 1 + nc - n)

    fspec = pl.BlockSpec((bb, 1, DN_CHUNK, 512), lambda b, n: (b, 0, n, 0))
    bspec = pl.BlockSpec((bb, 1, DN_CHUNK, 512), lambda b, n: (b, 1, bidx(n), 0))
    fgl = pl.BlockSpec((bb, 1, 1, 1, 512), lambda b, n: (b, 0, n, 0, 0))
    bgl = pl.BlockSpec((bb, 1, 1, 1, 512), lambda b, n: (b, 1, bidx(n), 0, 0))
    return pl.pallas_call(
        _dnscan_kernel,
        grid=(B // bb, nch),
        in_specs=[fspec] * 5 + [fgl] + [bspec] * 5 + [bgl],
        out_specs=[pl.BlockSpec((bb, DN_CHUNK, 512), lambda b, n: (b, n, 0)),
                   pl.BlockSpec((bb, DN_CHUNK, 512), lambda b, n: (b, bidx(n), 0))],
        out_shape=[jax.ShapeDtypeStruct((B, S, 512), F32)] * 2,
        scratch_shapes=[pltpu.VMEM((bb * 8, LANES, LANES), F32)],
        compiler_params=_params("parallel", "arbitrary"),
        name="dn_scan",
    )(u, w, qg, kg, qk, gl, u, w, qg, kg, qk, gl)


def _pack_pairs(v):
    w = v.shape[1] // 2
    bits = lax.bitcast_convert_type(v.astype(jnp.bfloat16).astype(F32), jnp.int32)
    return lax.shift_right_logical(bits[:, :w], 16) | bits[:, w:]


def _unpack_pairs(p):
    lo = lax.bitcast_convert_type(lax.shift_left(p, 16), F32)
    hi = lax.bitcast_convert_type(p & jnp.int32(-65536), F32)
    return lo, hi


def _merge_kernel(x_ref, ga_ref, shf_ref, scf_ref, omla_ref, of_ref, ob_ref, dz_ref, ogqa_ref, gates_ref,
                  wo1_ref, wo2_ref, wo3_ref, wout_ref, dnn_ref, bd_ref, lng_ref, lnb_ref, wr_ref, rb_ref,
                  x1_out, ufp_out, eidx_out, rank_out, ew_out, cnt_out, cnt_ref):
    @pl.when(pl.program_id(0) == 0)
    def _():
        cnt_ref[...] = jnp.zeros_like(cnt_ref)

    o = of_ref[...] + ob_ref[...]
    ms = _exact_mm(o * o, bd_ref[...]) * (1.0 / DN_HEAD_DIM)
    dn = o * lax.rsqrt(ms + NORM_EPS) * dnn_ref[...] * _silu(dz_ref[...].astype(F32))
    g1 = gates_ref[:, 0:1024].astype(F32)
    g2 = gates_ref[:, 1024:2048].astype(F32)
    g3 = gates_ref[:, 2048:3072].astype(F32)
    m = (g1 * _mm(omla_ref[...], wo1_ref[...]) + g2 * _mm(dn, wo2_ref[...])
         + g3 * _mm(ogqa_ref[...], wo3_ref[...]))
    y = _mm(m, wout_ref[...])
    x1 = _layernorm(DEEPNORM_ALPHA * x_ref[...] + ga_ref[0] * y, lng_ref[...], lnb_ref[...])
    x1_out[...] = x1
    uf = x1 * (1.0 + scf_ref[0]) + shf_ref[0]
    _split_pieces(_pack_pairs(uf), ufp_out)

    tm = uf.shape[0]
    scores = jax.nn.sigmoid(_mm_nt(wr_ref[...], uf))[0:N_EXPERTS]
    sel = scores + rb_ref[0:N_EXPERTS, :]
    gsz = N_EXPERTS // N_GROUPS
    neg = jnp.full((), -jnp.inf, F32)
    sel3 = sel.reshape(N_GROUPS, gsz, tm)
    mem = lax.broadcasted_iota(jnp.int32, (N_GROUPS, gsz, tm), 1)
    m1 = jnp.max(sel3, 1, keepdims=True)
    i1 = jnp.min(jnp.where(sel3 == m1, mem, gsz), 1, keepdims=True)
    m2 = jnp.max(jnp.where(mem == i1, neg, sel3), 1, keepdims=True)
    gs = (m1 + m2).reshape(N_GROUPS, tm)
    gi = lax.broadcasted_iota(jnp.int32, (N_GROUPS, tm), 0)
    grank = jnp.zeros((N_GROUPS, tm), jnp.int32)
    for gp in range(N_GROUPS):
        other = gs[gp:gp + 1, :]
        beats = (other > gs) | ((other == gs) & (gp < gi))
        grank = grank + beats.astype(jnp.int32)
    gsel = (grank < TOPK_GROUPS).reshape(N_GROUPS, 1, tm)
    cur = jnp.where(gsel, sel3, neg).reshape(N_EXPERTS, tm)
    ei = lax.broadcasted_iota(jnp.int32, (N_EXPERTS, tm), 0)
    zero = jnp.zeros((N_EXPERTS, tm), F32)
    one = jnp.ones((N_EXPERTS, tm), F32)
    chosen = zero
    picks = []
    for _ in range(TOP_K):
        mx = jnp.max(cur, 0, keepdims=True)
        ix = jnp.min(jnp.where(cur == mx, ei, N_EXPERTS), 0, keepdims=True)
        pick = ei == ix
        picks.append((ix, pick))
        chosen = chosen + jnp.where(pick, one, zero)
        cur = jnp.where(pick, neg, cur)

    r_i = lax.broadcasted_iota(jnp.int32, (tm, tm), 0)
    c_i = lax.broadcasted_iota(jnp.int32, (tm, tm), 1)
    before = jnp.where(r_i < c_i, 1.0, 0.0)
    pos = cnt_ref[:, 0:1] + _mm(chosen, before)
    cnt_new = cnt_ref[...] + jnp.sum(chosen, 1, keepdims=True)
    cnt_ref[...] = cnt_new
    cnt_out[...] = cnt_new

    w_rows = [jnp.sum(jnp.where(pick, scores, zero), 0, keepdims=True) for _, pick in picks]
    wsum = w_rows[0]
    for w_k in w_rows[1:]:
        wsum = wsum + w_k
    eidx_out[...] = jnp.concatenate([ix for ix, _ in picks], axis=0)
    rank_out[...] = jnp.concatenate(
        [jnp.sum(jnp.where(pick, pos, zero), 0, keepdims=True) for _, pick in picks], axis=0).astype(jnp.int32)
    w8 = jnp.concatenate([w_k / wsum * ROUTED_SCALE for w_k in w_rows], axis=0)
    ew_out[...] = jnp.concatenate([w8, jnp.zeros((LANES - TOP_K, tm), F32)], axis=0).T


def _merge_call(xs, mod_l, omla, of, ob, dz, ogqa, gates, wo1, wo2, wo3, wout, dnn, bd, lng, lnb, wr, rb, B, S, C):
    T, D = xs.shape
    tpb, ctiles = S // TM, C // TM
    row = lambda i: (i, 0)
    const = lambda i: (0, 0)
    modspec = lambda k: pl.BlockSpec((1, 1, D), _mod_index(tpb, ctiles, k))
    return pl.pallas_call(
        _merge_kernel,
        grid=(T // TM,),
        in_specs=[pl.BlockSpec((TM, D), row), modspec(2), modspec(3), modspec(4),
                  pl.BlockSpec((TM, 512), row), pl.BlockSpec((TM, 512), row), pl.BlockSpec((TM, 512), row),
                  pl.BlockSpec((TM, 512), row), pl.BlockSpec((TM, 512), row), pl.BlockSpec((TM, 3072), row),
                  pl.BlockSpec((512, D), const), pl.BlockSpec((512, D), const), pl.BlockSpec((512, D), const),
                  pl.BlockSpec((D, D), const), pl.BlockSpec((1, 512), const), pl.BlockSpec((512, 512), const),
                  pl.BlockSpec((1, D), const), pl.BlockSpec((1, D), const),
                  pl.BlockSpec((LANES, D), const), pl.BlockSpec((LANES, 1), const)],
        out_specs=[pl.BlockSpec((TM, D), row), pl.BlockSpec((N_PIECES, TM, PIECE), lambda i: (0, i, 0)),
                   pl.BlockSpec((TOP_K, TM), lambda i: (0, i)), pl.BlockSpec((TOP_K, TM), lambda i: (0, i)),
                   pl.BlockSpec((TM, LANES), row), pl.BlockSpec((N_EXPERTS, LANES), const)],
        out_shape=[jax.ShapeDtypeStruct((T, D), F32), jax.ShapeDtypeStruct((N_PIECES, T, PIECE), jnp.int32),
                   jax.ShapeDtypeStruct((TOP_K, T), jnp.int32), jax.ShapeDtypeStruct((TOP_K, T), jnp.int32),
                   jax.ShapeDtypeStruct((T, LANES), F32), jax.ShapeDtypeStruct((N_EXPERTS, LANES), F32)],
        scratch_shapes=[pltpu.VMEM((N_EXPERTS, LANES), F32)],
        compiler_params=_params("arbitrary"),
        name="merge_norm_route",
    )(xs, mod_l, mod_l, mod_l, omla, of.reshape(T, 512), ob.reshape(T, 512), dz, ogqa, gates,
      wo1, wo2, wo3, wout, dnn, bd, lng, lnb, wr, rb)


EXPERT_BLOCK = 512
SC_WINDOW = 128
N_PIECES = 2
PIECE = D_MODEL // 2 // N_PIECES


def _split_pieces(packed, out_ref):
    for h in range(N_PIECES):
        out_ref[h] = packed[:, h * PIECE:(h + 1) * PIECE]


def _mm_pieces(pieces, w):
    acc = None
    for h, (lo, hi) in enumerate(pieces):
        t = (_mm(lo, w[h * PIECE:(h + 1) * PIECE, :])
             + _mm(hi, w[D_MODEL // 2 + h * PIECE:D_MODEL // 2 + (h + 1) * PIECE, :]))
        acc = t if acc is None else acc + t
    return acc


def _sc_mesh():
    return plsc.VectorSubcoreMesh(core_axis_name="c", subcore_axis_name="s")


def _sc_gather_rows(y, idx):
    n = idx.shape[1]
    W = y.shape[1]

    @pl.kernel(out_type=jax.ShapeDtypeStruct((n, W), y.dtype), mesh=_sc_mesh(), scratch_types=[])
    def gather(y_hbm, i_hbm, o_hbm):
        def body(i_vmem, o_vmem):
            pltpu.sync_copy(y_hbm.at[i_vmem.at[0]], o_vmem)

        pltpu.emit_pipeline(
            body,
            grid=(n // SC_WINDOW,),
            in_specs=[pl.BlockSpec((1, SC_WINDOW), lambda i: (0, i))],
            out_specs=[pl.BlockSpec((SC_WINDOW, W), lambda i: (i, 0))],
            core_axis_name=("c", "s"),
            dimension_semantics=(pltpu.PARALLEL,),
        )(i_hbm, o_hbm)

    return gather(y, idx)


def _experts_kernel(be_ref, nv_ref, xb_ref, wg_ref, wu_ref, wd_ref, y_out):
    b = pl.program_id(0)
    nv = nv_ref[b]

    @pl.when(nv > 0)
    def _():
        rows = lax.broadcasted_iota(jnp.int32, xb_ref.shape[1:], 0)
        pieces = []
        for h in range(N_PIECES):
            xh = xb_ref[h]
            pieces.append(_unpack_pairs(jnp.where(rows < nv, xh, jnp.zeros_like(xh))))
        hid = _silu(_mm_pieces(pieces, wg_ref.at[0])) * _mm_pieces(pieces, wu_ref.at[0])
        _split_pieces(_pack_pairs(_mm(hid, wd_ref[0])), y_out)


def _experts_call(xb, block_e, nvalid, wg, wu, wd):
    _, R, _ = xb.shape
    D = D_MODEL
    blk = (N_PIECES, EXPERT_BLOCK, PIECE)
    grid_spec = pltpu.PrefetchScalarGridSpec(
        num_scalar_prefetch=2,
        grid=(R // EXPERT_BLOCK,),
        in_specs=[pl.BlockSpec(blk, lambda b, be, nv: (0, b, 0)),
                  pl.BlockSpec((1, D, EXPERT_DIM), lambda b, be, nv: (be[b], 0, 0)),
                  pl.BlockSpec((1, D, EXPERT_DIM), lambda b, be, nv: (be[b], 0, 0)),
                  pl.BlockSpec((1, EXPERT_DIM, D), lambda b, be, nv: (be[b], 0, 0))],
        out_specs=pl.BlockSpec(blk, lambda b, be, nv: (0, b, 0)),
    )
    return pl.pallas_call(
        _experts_kernel,
        grid_spec=grid_spec,
        out_shape=jax.ShapeDtypeStruct((N_PIECES, R, PIECE), jnp.int32),
        compiler_params=_params("arbitrary"),
        name="moe_experts",
    )(block_e, nvalid, xb, wg, wu, wd)


def _combine_kernel(x_ref, ufp_ref, yg_ref, ew_ref, gf_ref, sg_ref, su_ref, sd_ref, g_ref, b_ref, o_ref):
    pieces = [_unpack_pairs(ufp_ref[h]) for h in range(N_PIECES)]
    hs = _silu(_mm_pieces(pieces, sg_ref)) * _mm_pieces(pieces, su_ref)
    f = _mm(hs, sd_ref[...])
    ew = ew_ref[...]
    lane = lax.broadcasted_iota(jnp.int32, ew.shape, 1)
    acc = [[jnp.zeros((x_ref.shape[0], PIECE), F32) for _ in range(N_PIECES)] for _ in range(2)]
    for k in range(TOP_K):
        wk = jnp.sum(jnp.where(lane == k, ew, jnp.zeros_like(ew)), axis=1, keepdims=True)
        for h in range(N_PIECES):
            ylo, yhi = _unpack_pairs(yg_ref[h, k])
            acc[0][h] = acc[0][h] + wk * ylo
            acc[1][h] = acc[1][h] + wk * yhi
    f = f + jnp.concatenate(acc[0] + acc[1], axis=1)
    o_ref[...] = _layernorm(DEEPNORM_ALPHA * x_ref[...] + gf_ref[0] * f, g_ref[...], b_ref[...])


def _combine_call(x1, ufp, yg, ew, mod_l, sg, su, sd, g, b, B, S, C):
    T, D = x1.shape
    tpb, ctiles = S // TM, C // TM
    row = lambda i: (i, 0)
    const = lambda i: (0, 0)
    return pl.pallas_call(
        _combine_kernel,
        grid=(T // TM,),
        in_specs=[pl.BlockSpec((TM, D), row), pl.BlockSpec((N_PIECES, TM, PIECE), lambda i: (0, i, 0)),
                  pl.BlockSpec((N_PIECES, TOP_K, TM, PIECE), lambda i: (0, 0, i, 0)),
                  pl.BlockSpec((TM, LANES), row),
                  pl.BlockSpec((1, 1, D), _mod_index(tpb, ctiles, 5)),
                  pl.BlockSpec((D, SHARED_DIM), const), pl.BlockSpec((D, SHARED_DIM), const),
                  pl.BlockSpec((SHARED_DIM, D), const),
                  pl.BlockSpec((1, D), const), pl.BlockSpec((1, D), const)],
        out_specs=pl.BlockSpec((TM, D), row),
        out_shape=jax.ShapeDtypeStruct((T, D), F32),
        compiler_params=_params("parallel"),
        name="moe_combine_norm",
    )(x1, ufp, yg, ew, mod_l, sg, su, sd, g, b)


def _moe_routed(ufp, eidx_t, rank_t, counts, wg, wu, wd):
    T = ufp.shape[1]
    n_blocks = -(-(T * TOP_K + N_EXPERTS * (EXPERT_BLOCK - 1)) // EXPERT_BLOCK)
    n_rows = n_blocks * EXPERT_BLOCK
    padded = (counts + EXPERT_BLOCK - 1) // EXPERT_BLOCK * EXPERT_BLOCK
    pad_end = jnp.cumsum(padded)
    start_pad = pad_end - padded
    experts = jnp.arange(N_EXPERTS, dtype=jnp.int32)

    def lookup(table, idx):
        sel = idx[None] == experts.reshape((N_EXPERTS,) + (1,) * idx.ndim)
        return jnp.sum(jnp.where(sel, table.reshape((N_EXPERTS,) + (1,) * idx.ndim), 0), axis=0)

    dest_t = lookup(start_pad, eidx_t) + rank_t
    blk = jnp.arange(n_blocks, dtype=jnp.int32) * EXPERT_BLOCK
    block_e = jnp.minimum(jnp.sum((blk[:, None] >= pad_end[None, :]).astype(jnp.int32), axis=1), N_EXPERTS - 1)
    nvalid = jnp.clip(lookup(counts, block_e) - (blk - lookup(start_pad, block_e)), 0, EXPERT_BLOCK)
    tok = jnp.broadcast_to(jnp.arange(T, dtype=jnp.int32), (TOP_K, T))
    row_tok = (jnp.arange(n_rows, dtype=jnp.int32) % T).at[dest_t.reshape(-1)].set(
        tok.reshape(-1), unique_indices=True)
    piece = jnp.arange(N_PIECES, dtype=jnp.int32)
    src = (piece[:, None] * T + row_tok[None, :]).reshape(1, N_PIECES * n_rows)
    xb = _sc_gather_rows(ufp.reshape(N_PIECES * T, PIECE), src).reshape(N_PIECES, n_rows, PIECE)
    yb = _experts_call(xb, block_e.astype(jnp.int32), nvalid.astype(jnp.int32), wg, wu, wd)
    back = (piece[:, None, None] * n_rows + dest_t[None]).reshape(1, N_PIECES * TOP_K * T)
    yg = _sc_gather_rows(yb.reshape(N_PIECES * n_rows, PIECE), back)
    return yg.reshape(N_PIECES, TOP_K, T, PIECE)


def _rot_cols(w, half):
    return jnp.concatenate([-w[:, half:], w[:, :half]], axis=1)


def _prep_w_in(w):
    d = w.shape[0]
    offs = np.cumsum((0,) + IN_SIZES)
    cq, ckv, kr, dqkv, da, db, dz, gq, gk, gv, gates = (w[:, offs[t]:offs[t + 1]] for t in range(len(IN_SIZES)))
    z = lambda n: jnp.zeros((d, n), w.dtype)
    krg = jnp.concatenate([z(64), kr, z(32)], 1)
    krr = jnp.concatenate([z(64), _rot_cols(kr, MLA_ROPE // 2), z(32)], 1)
    ab = jnp.concatenate([da, db, z(LANES - 4 * DN_HEADS)], 1)
    hd = GQA_HEAD_DIM
    gq_rot = jnp.concatenate([_rot_cols(gq[:, h * hd:(h + 1) * hd], hd // 2) for h in range(GQA_HEADS)], 1)
    dup = lambda t: jnp.concatenate([t[:, 0:hd], t[:, 0:hd], t[:, hd:2 * hd], t[:, hd:2 * hd]], 1)
    gk_rot = jnp.concatenate([_rot_cols(gk[:, h * hd:(h + 1) * hd], hd // 2) for h in range(GQA_KV_HEADS)], 1)
    cat = jnp.concatenate([cq, ckv, krg, krr, dqkv, ab, dz, gq, gq_rot, dup(gk), dup(gk_rot), dup(gv), gates], 1)
    assert cat.shape[1] == NZ
    return cat.astype(MXU_DTYPE)


def _prep_w_uq(w):
    d = w.shape[0]
    hw = MLA_NOPE + MLA_ROPE
    a, b = [], []
    for h in range(MLA_HEADS):
        wh = w[:, h * hw:(h + 1) * hw]
        a += [wh, jnp.zeros((d, LANES - hw), w.dtype)]
        b += [jnp.zeros((d, MLA_NOPE), w.dtype), _rot_cols(wh[:, MLA_NOPE:], MLA_ROPE // 2),
              jnp.zeros((d, LANES - hw), w.dtype)]
    return jnp.concatenate(a + b, 1).astype(MXU_DTYPE)


def _prep_w_ukv(w):
    d = w.shape[0]
    hw = MLA_NOPE + MLA_V
    kpart, vpart = [], []
    for h in range(MLA_HEADS):
        wh = w[:, h * hw:(h + 1) * hw]
        kpart += [wh[:, :MLA_NOPE], jnp.zeros((d, LANES - MLA_NOPE), w.dtype)]
        vpart += [wh[:, MLA_NOPE:]]
    return jnp.concatenate(kpart + vpart, 1).astype(MXU_DTYPE)


def _rope_tables(n_rows, C):
    row = jnp.repeat(jnp.arange(n_rows, dtype=F32), GRID_W)
    col = jnp.tile(jnp.arange(GRID_W, dtype=F32), n_rows)

    def angles(dim):
        n = dim // 4
        inv = ROPE_BASE ** (-jnp.arange(n, dtype=F32) / n)
        return jnp.concatenate([row[:, None] * inv, col[:, None] * inv], axis=-1)

    def with_ctx(cos, sin):
        return (jnp.concatenate([jnp.ones((C, LANES), F32), cos], 0),
                jnp.concatenate([jnp.zeros((C, LANES), F32), sin], 0))

    L = n_rows * GRID_W
    am = angles(MLA_ROPE)
    one, zero = jnp.ones((L, MLA_NOPE), F32), jnp.zeros((L, MLA_NOPE), F32)
    cm = jnp.concatenate([one, jnp.cos(am), jnp.cos(am), one[:, :32]], 1)
    sm = jnp.concatenate([zero, jnp.sin(am), jnp.sin(am), zero[:, :32]], 1)
    ag = angles(GQA_HEAD_DIM)
    cg = jnp.tile(jnp.cos(ag), (1, 4))
    sg = jnp.tile(jnp.sin(ag), (1, 4))
    return with_ctx(cm, sm) + with_ctx(cg, sg)


def kernel(x, c, ctx, c_ctx, w_ada, b_ada, w_in, mla_q_norm, mla_kv_norm, w_uq, w_ukv, dn_conv, dn_a_log, dn_dt_bias, dn_norm, gqa_sink, w_o_mla, w_o_dn, w_o_gqa, w_out, ln1_g, ln1_b, w_router, router_bias, w_exp_gate, w_exp_up, w_exp_down, w_sh_gate, w_sh_up, w_sh_down, ln2_g, ln2_b):
    B, L, D = x.shape
    C = ctx.shape[1]
    S = C + L
    nl = w_in.shape[0]
    assert D == D_MODEL and nl == DEPTH and B <= CTX_MOD_ROW
    assert C % TM == 0 and L % TM == 0 and L % GRID_W == 0 and L >= 3 * WINDOW
    cast = lambda t: t.astype(MXU_DTYPE)

    cc = jnp.zeros((MOD_ROWS, D), F32).at[0:B].set(c).at[CTX_MOD_ROW].set(c_ctx)
    mods = _ada_call(cc, w_ada, b_ada)
    tabs = _rope_tables(L // GRID_W, C)
    dn_consts = _dn_constants()
    xs = jnp.concatenate([ctx, x], axis=1).reshape(B * S, D)

    for l in range(nl):
        mod_l = mods[l].reshape(MOD_ROWS * 6, 1, D)
        q, k, v, dqkv, ab, dz, gq, gkv, gates = _inproj_call(
            xs, mod_l, _prep_w_in(w_in[l]), _prep_w_uq(w_uq[l]), _prep_w_ukv(w_ukv[l]),
            mla_q_norm[l].reshape(1, -1), mla_kv_norm[l].reshape(1, -1), tabs, B, S, C)
        omla = _mla_call(q, k, v, B, S, C)
        conv8 = jnp.zeros((8, 3 * DN_WIDTH), F32).at[0:DN_CONV].set(dn_conv[l])
        gp = (jnp.zeros((8, LANES), F32).at[0, 0:2 * DN_HEADS].set(dn_a_log[l].reshape(-1))
              .at[1, 0:2 * DN_HEADS].set(dn_dt_bias[l].reshape(-1)))
        local = _dnlocal_call(dqkv, ab, conv8, gp, dn_consts, B, S, C)
        of, ob = _dnscan_call(local, B, S, C)
        ogqa = _gqa_call(gq, gkv, gqa_sink[l], B, S, C)
        wr = jnp.zeros((LANES, D), F32).at[0:N_EXPERTS].set(w_router[l].T)
        rb = jnp.zeros((LANES, 1), F32).at[0:N_EXPERTS, 0].set(router_bias[l])
        x1, ufp, eidx_t, rank_t, ew, cnt = _merge_call(
            xs, mod_l, omla, of, ob, dz, ogqa, gates,
            cast(w_o_mla[l]), cast(w_o_dn[l]), cast(w_o_gqa[l]), cast(w_out[l]),
            jnp.tile(dn_norm[l], DN_HEADS).reshape(1, DN_WIDTH), dn_consts[0],
            ln1_g[l].reshape(1, D), ln1_b[l].reshape(1, D), cast(wr), rb, B, S, C)
        yg = _moe_routed(ufp, eidx_t, rank_t, cnt[:, 0].astype(jnp.int32),
                         cast(w_exp_gate[l]), cast(w_exp_up[l]), cast(w_exp_down[l]))
        xs = _combine_call(x1, ufp, yg, ew, mod_l, cast(w_sh_gate[l]), cast(w_sh_up[l]), cast(w_sh_down[l]),
                           ln2_g[l].reshape(1, D), ln2_b[l].reshape(1, D), B, S, C)
    return xs.reshape(B, S, D)[:, C:, :]
```

```python
import functools

import numpy as np
import jax
import jax.numpy as jnp
from jax import lax
from jax.experimental import pallas as pl
from jax.experimental.pallas import tpu as pltpu
from jax.experimental.pallas import tpu_sc as plsc

F32 = jnp.float32
MXU_DTYPE = jnp.bfloat16

D_MODEL = 1024
DEPTH = 4
GRID_W = 64
NORM_EPS = 1e-6
ROPE_BASE = 10000.0
DEEPNORM_ALPHA = (2.0 * DEPTH) ** 0.25

MLA_HEADS = 8
MLA_Q_LORA = 256
MLA_KV_LORA = 128
MLA_NOPE = 64
MLA_ROPE = 32
MLA_V = 64
MLA_SCALE = (MLA_NOPE + MLA_ROPE) ** -0.5
LOG2E = float(np.log2(np.e))

DN_HEADS = 8
DN_HEAD_DIM = 64
DN_WIDTH = DN_HEADS * DN_HEAD_DIM
DN_CONV = 5
DN_CHUNK = 64

GQA_HEADS = 8
GQA_KV_HEADS = 2
GQA_HEAD_DIM = 64
GQA_SCALE = GQA_HEAD_DIM ** -0.5
WINDOW = 128

N_EXPERTS = 64
TOP_K = 8
N_GROUPS = 8
TOPK_GROUPS = 4
EXPERT_DIM = 256
SHARED_DIM = 256
ROUTED_SCALE = 2.5

IN_SIZES = (MLA_Q_LORA, MLA_KV_LORA, MLA_ROPE,
            3 * DN_WIDTH, 2 * DN_HEADS, 2 * DN_HEADS, DN_WIDTH,
            GQA_HEADS * GQA_HEAD_DIM, GQA_KV_HEADS * GQA_HEAD_DIM, GQA_KV_HEADS * GQA_HEAD_DIM,
            3 * D_MODEL)

LANES = 128
TM = 256
MOD_ROWS = 16
CTX_MOD_ROW = 8

OFF_A = 0
OFF_DQKV = 640
OFF_AB = OFF_DQKV + 3 * DN_WIDTH
OFF_DZ = OFF_AB + LANES
OFF_GQ = OFF_DZ + DN_WIDTH
OFF_GK = OFF_GQ + 1024
OFF_GATES = OFF_GK + 768
NZ = OFF_GATES + 3 * D_MODEL

VMEM_LIMIT = 56 * 1024 * 1024


def _mm(a, b):
    return jnp.dot(a.astype(MXU_DTYPE), b.astype(MXU_DTYPE), preferred_element_type=F32)


def _mm_nt(a, b):
    return lax.dot_general(a.astype(MXU_DTYPE), b.astype(MXU_DTYPE), (((1,), (1,)), ((), ())),
                           preferred_element_type=F32)


def _mm_tn(a, b):
    return lax.dot_general(a.astype(MXU_DTYPE), b.astype(MXU_DTYPE), (((0,), (0,)), ((), ())),
                           preferred_element_type=F32)


def _bmm(a, b):
    return jnp.einsum('cik,ckj->cij', a.astype(MXU_DTYPE), b.astype(MXU_DTYPE), preferred_element_type=F32)


def _bmm_nt(a, b):
    return jnp.einsum('cik,cjk->cij', a.astype(MXU_DTYPE), b.astype(MXU_DTYPE), preferred_element_type=F32)


def _split3(x):
    hi = x.astype(jnp.bfloat16).astype(F32)
    r = x - hi
    mid = r.astype(jnp.bfloat16).astype(F32)
    lo = (r - mid).astype(jnp.bfloat16).astype(F32)
    return hi, mid, lo


def _exact_mm(x, m01):
    hi, mid, lo = _split3(x)
    return _mm(hi, m01) + _mm(mid, m01) + _mm(lo, m01)


def _exact_mm_left(m01, x):
    hi, mid, lo = _split3(x)
    return _mm(m01, hi) + _mm(m01, mid) + _mm(m01, lo)


def _silu(x):
    return x * jax.nn.sigmoid(x)


def _layernorm(v, g, b):
    mu = jnp.mean(v, -1, keepdims=True)
    d = v - mu
    var = jnp.mean(d * d, -1, keepdims=True)
    return d * lax.rsqrt(var + NORM_EPS) * g + b


def _mod_index(tiles_per_b, ctx_tiles, k):
    def index(i):
        row = jnp.where((i % tiles_per_b) < ctx_tiles, CTX_MOD_ROW, i // tiles_per_b)
        return (row * 6 + k, 0, 0)
    return index


def _params(*sem):
    return pltpu.CompilerParams(dimension_semantics=sem, vmem_limit_bytes=VMEM_LIMIT)


def _ada_kernel(c_ref, w_ref, b_ref, o_ref):
    o_ref[0] = _mm(_silu(c_ref[...]), w_ref[0]) + b_ref[0]


def _ada_call(cc, w_ada, b_ada):
    nl, d, n6 = w_ada.shape
    tn = 1536
    return pl.pallas_call(
        _ada_kernel,
        grid=(nl, n6 // tn),
        in_specs=[pl.BlockSpec((MOD_ROWS, d), lambda l, j: (0, 0)),
                  pl.BlockSpec((1, d, tn), lambda l, j: (l, 0, j)),
                  pl.BlockSpec((1, 1, tn), lambda l, j: (l, 0, j))],
        out_specs=pl.BlockSpec((1, MOD_ROWS, tn), lambda l, j: (l, 0, j)),
        out_shape=jax.ShapeDtypeStruct((nl, MOD_ROWS, n6), F32),
        compiler_params=_params("parallel", "parallel"),
        name="ada_mod",
    )(cc, w_ada, b_ada.reshape(nl, 1, n6))


def _inproj_kernel(x_ref, sh_ref, sc_ref, w_ref, wuq_ref, wukv_ref, qn_ref, kvn_ref,
                   cm_ref, sm_ref, cg_ref, sg_ref,
                   q_out, k_out, v_out, dqkv_out, ab_out, dz_out, gq_out, gkv_out, gates_out):
    u = (x_ref[...] * (1.0 + sc_ref[0]) + sh_ref[0]).astype(MXU_DTYPE)

    def z(off, width):
        return jnp.dot(u, w_ref[:, off:off + width], preferred_element_type=F32)

    def rms(v, g):
        return v * lax.rsqrt(jnp.mean(v * v, -1, keepdims=True) + NORM_EPS) * g

    cm, sm, cg, sg = cm_ref[...], sm_ref[...], cg_ref[...], sg_ref[...]

    za = z(OFF_A, 640)
    qq = _mm(rms(za[:, 0:256], qn_ref[...]), wuq_ref[...])
    kvv = _mm(rms(za[:, 256:384], kvn_ref[...]), wukv_ref[...])
    k_rope = za[:, 384:512] * cm + za[:, 512:640] * sm
    for h in range(MLA_HEADS):
        sl = slice(h * LANES, (h + 1) * LANES)
        qa = qq[:, h * LANES:(h + 1) * LANES]
        qb = qq[:, 1024 + h * LANES:1024 + (h + 1) * LANES]
        q_out[:, sl] = ((qa * cm + qb * sm) * (MLA_SCALE * LOG2E)).astype(q_out.dtype)
        k_out[:, sl] = (kvv[:, sl] + k_rope).astype(k_out.dtype)
    v_out[...] = kvv[:, 1024:1536].astype(v_out.dtype)

    for t in range(3):
        dqkv_out[:, t * 512:(t + 1) * 512] = z(OFF_DQKV + t * 512, 512).astype(dqkv_out.dtype)
    ab_out[...] = z(OFF_AB, LANES)
    dz_out[...] = z(OFF_DZ, DN_WIDTH).astype(dz_out.dtype)

    zq = z(OFF_GQ, 1024)
    for p in range(4):
        sl = slice(p * LANES, (p + 1) * LANES)
        gq_out[:, sl] = ((zq[:, sl] * cg + zq[:, 512 + p * LANES:512 + (p + 1) * LANES] * sg)
                         * (GQA_SCALE * LOG2E)).astype(gq_out.dtype)
    zk = z(OFF_GK, 768)
    for j in range(2):
        sl = slice(j * LANES, (j + 1) * LANES)
        gkv_out[:, sl] = (zk[:, sl] * cg + zk[:, 256 + j * LANES:256 + (j + 1) * LANES] * sg).astype(gkv_out.dtype)
    gkv_out[:, 256:512] = zk[:, 512:768].astype(gkv_out.dtype)

    for t in range(3):
        gates_out[:, t * 1024:(t + 1) * 1024] = jax.nn.sigmoid(z(OFF_GATES + t * 1024, 1024)).astype(gates_out.dtype)


def _inproj_call(xs, mod_l, w_cat, wuq_cat, wukv_cat, qn, kvn, tabs, B, S, C):
    T, D = xs.shape
    tpb, ctiles = S // TM, C // TM
    act = MXU_DTYPE
    row = lambda i: (i, 0)
    const = lambda i: (0, 0)
    tab = lambda i: (i % tpb, 0)
    widths = (1024, 1024, 512, 1536, LANES, 512, 512, 512, 3072)
    dtypes = (act, act, act, act, F32, act, act, act, act)
    return pl.pallas_call(
        _inproj_kernel,
        grid=(T // TM,),
        in_specs=[pl.BlockSpec((TM, D), row),
                  pl.BlockSpec((1, 1, D), _mod_index(tpb, ctiles, 0)),
                  pl.BlockSpec((1, 1, D), _mod_index(tpb, ctiles, 1)),
                  pl.BlockSpec((D, NZ), const),
                  pl.BlockSpec((MLA_Q_LORA, 2048), const),
                  pl.BlockSpec((MLA_KV_LORA, 1536), const),
                  pl.BlockSpec((1, MLA_Q_LORA), const),
                  pl.BlockSpec((1, MLA_KV_LORA), const)]
                 + [pl.BlockSpec((TM, LANES), tab)] * 4,
        out_specs=[pl.BlockSpec((TM, w), row) for w in widths],
        out_shape=[jax.ShapeDtypeStruct((T, w), dt) for w, dt in zip(widths, dtypes)],
        compiler_params=_params("parallel"),
        name="in_proj",
    )(xs, mod_l, mod_l, w_cat, wuq_cat, wukv_cat, qn, kvn, *tabs)


def _mla_kernel(q_ref, k_ref, v_ref, o_ref, *, n_ctx, n_all):
    i = pl.program_id(2)
    tq = q_ref.shape[1]
    left = lax.broadcasted_iota(jnp.int32, (tq, LANES), 1) < MLA_V

    def attend(nk, ctx_rows):
        v = v_ref[0, 0:nk, :]
        v_left = lax.broadcasted_iota(jnp.int32, v.shape, 1) < MLA_V
        ones = jnp.ones_like(v)
        s = [_mm_nt(q_ref[0, :, hh * LANES:(hh + 1) * LANES], k_ref[0, 0:nk, hh * LANES:(hh + 1) * LANES])
             for hh in range(2)]
        if ctx_rows:
            row = lax.broadcasted_iota(jnp.int32, (tq, nk), 0)
            col = lax.broadcasted_iota(jnp.int32, (tq, nk), 1)
            ok = (row >= ctx_rows) | (col < ctx_rows)
            s = [jnp.where(ok, x, -jnp.inf) for x in s]
        p = [jnp.exp2((x - jnp.max(x, -1, keepdims=True)).astype(MXU_DTYPE)) for x in s]
        o = [_mm(p[0], jnp.where(v_left, v, ones)), _mm(p[1], jnp.where(v_left, ones, v))]
        outs = [x / pltpu.roll(x, MLA_V, axis=1) for x in o]
        o_ref[0] = jnp.where(left, outs[0], outs[1]).astype(o_ref.dtype)

    if tq > n_ctx:
        @pl.when(i == 0)
        def _():
            attend(n_all, n_ctx)

        @pl.when(i > 0)
        def _():
            attend(n_all, 0)
    else:
        ctx_tiles = n_ctx // tq

        @pl.when(i < ctx_tiles)
        def _():
            attend(n_ctx, 0)

        @pl.when(i >= ctx_tiles)
        def _():
            attend(n_all, 0)


MLA_TQ = (544, 384, 256)


def _mla_call(q, k, v, B, S, C):
    tq = next(t for t in MLA_TQ if S % t == 0)
    assert C % tq == 0 or tq > C
    q3, k3, v3 = q.reshape(B, S, 1024), k.reshape(B, S, 1024), v.reshape(B, S, 512)
    out = pl.pallas_call(
        functools.partial(_mla_kernel, n_ctx=C, n_all=S),
        grid=(B, MLA_HEADS // 2, S // tq),
        in_specs=[pl.BlockSpec((1, tq, 2 * LANES), lambda b, j, i: (b, i, j)),
                  pl.BlockSpec((1, S, 2 * LANES), lambda b, j, i: (b, 0, j)),
                  pl.BlockSpec((1, S, LANES), lambda b, j, i: (b, 0, j))],
        out_specs=pl.BlockSpec((1, tq, LANES), lambda b, j, i: (b, i, j)),
        out_shape=jax.ShapeDtypeStruct((B, S, 512), MXU_DTYPE),
        compiler_params=_params("parallel", "parallel", "arbitrary"),
        name="mla_attn",
    )(q3, k3, v3)
    return out.reshape(B * S, 512)


def _gqa_kernel(sink_ref, q_ref, kv_ref, o_ref, *, n_ctx, n_all):
    i = pl.program_id(1)
    qb_rows = q_ref.shape[1]
    span = qb_rows + 2 * WINDOW
    ctx_blocks = n_ctx // qb_rows
    group = GQA_HEADS // GQA_KV_HEADS
    left = lax.broadcasted_iota(jnp.int32, (qb_rows, LANES), 1) < GQA_HEAD_DIM

    def run(latent):
        nk = n_ctx + span if latent else n_ctx
        rows = group * qb_rows
        if latent:
            qb = i - ctx_blocks
            ws = pl.multiple_of(jnp.minimum((qb + 1) * qb_rows, n_all - span), qb_rows)
            col = lax.broadcasted_iota(jnp.int32, (rows, nk), 1)
            q_pos = qb * qb_rows + lax.broadcasted_iota(jnp.int32, (rows, nk), 0) % qb_rows
            k_pos = ws - 2 * n_ctx + col
            ok = (col < n_ctx) | ((jnp.abs(k_pos - q_pos) <= WINDOW) & (k_pos >= 0))
        head_of_row = lax.broadcasted_iota(jnp.int32, (rows, 1), 0) // qb_rows
        res = []
        for j in range(GQA_KV_HEADS):
            parts = []
            for g in range(group):
                qp = q_ref[0, :, (2 * j + g // 2) * LANES:(2 * j + g // 2 + 1) * LANES]
                parts.append(jnp.where(left if g % 2 == 0 else ~left, qp, jnp.zeros_like(qp)))
            q4 = jnp.concatenate(parts, axis=0)
            kc = kv_ref[0, 0:n_ctx, j * LANES:(j + 1) * LANES]
            vc = kv_ref[0, 0:n_ctx, 256 + j * LANES:256 + (j + 1) * LANES]
            if latent:
                kc = jnp.concatenate([kc, kv_ref[0, pl.ds(ws, span), j * LANES:(j + 1) * LANES]], axis=0)
                vc = jnp.concatenate([vc, kv_ref[0, pl.ds(ws, span), 256 + j * LANES:256 + (j + 1) * LANES]], axis=0)
            sink = jnp.zeros((rows, 1), F32)
            for g in range(group):
                sink = jnp.where(head_of_row == g, sink_ref[group * j + g] * LOG2E, sink)
            s = _mm_nt(q4, kc)
            if latent:
                s = jnp.where(ok, s, -jnp.inf)
            m = jnp.maximum(jnp.max(s, -1, keepdims=True), sink)
            p = jnp.exp2((s - m).astype(MXU_DTYPE))
            v_left = lax.broadcasted_iota(jnp.int32, vc.shape, 1) < GQA_HEAD_DIM
            o = _mm(p, jnp.where(v_left, vc, jnp.ones_like(vc)))
            res.append(o / (pltpu.roll(o, GQA_HEAD_DIM, axis=1) + jnp.exp2(sink - m)))
        for j in range(GQA_KV_HEADS):
            for pp in range(group // 2):
                r0 = res[j][(2 * pp) * qb_rows:(2 * pp + 1) * qb_rows]
                r1 = res[j][(2 * pp + 1) * qb_rows:(2 * pp + 2) * qb_rows]
                pair = 2 * j + pp
                o_ref[0, :, pair * LANES:(pair + 1) * LANES] = jnp.where(
                    left, r0, pltpu.roll(r1, GQA_HEAD_DIM, axis=1)).astype(o_ref.dtype)

    @pl.when(i < ctx_blocks)
    def _():
        run(False)

    @pl.when(i >= ctx_blocks)
    def _():
        run(True)


def _gqa_call(gq, gkv, sink, B, S, C):
    qb_rows = 128
    out = pl.pallas_call(
        functools.partial(_gqa_kernel, n_ctx=C, n_all=S),
        grid=(B, S // qb_rows),
        in_specs=[pl.BlockSpec(memory_space=pltpu.SMEM),
                  pl.BlockSpec((1, qb_rows, 512), lambda b, i: (b, i, 0)),
                  pl.BlockSpec((1, S, 512), lambda b, i: (b, 0, 0))],
        out_specs=pl.BlockSpec((1, qb_rows, 512), lambda b, i: (b, i, 0)),
        out_shape=jax.ShapeDtypeStruct((B, S, 512), MXU_DTYPE),
        compiler_params=_params("parallel", "arbitrary"),
        name="gqa_attn",
    )(sink, gq.reshape(B, S, 512), gkv.reshape(B, S, 512))
    return out.reshape(B * S, 512)


DN_TR = 256
DN_HALO = 16
DN_CPT = DN_TR // DN_CHUNK


def _stack(x, left):
    z = jnp.zeros_like(x)
    return jnp.concatenate([jnp.where(left, x, z), jnp.where(left, z, x)], axis=1)


def _dnlocal_kernel(main_ref, prev_ref, next_ref, ab_ref, conv_ref, gp_ref, bd_ref, trif_ref, trib_ref,
                    eg_ref, eb_ref,
                    u_out, w_out, qg_out, kg_out, qk_out, gl_out, pad_ref, *, ctx_tiles, n_tiles):
    i = pl.program_id(1)
    tr = DN_TR
    first = (i == 0) | (i == ctx_tiles)
    last = (i == ctx_tiles - 1) | (i == n_tiles - 1)
    xp = prev_ref[0].astype(F32)
    xn = next_ref[0].astype(F32)
    pad_ref[0:DN_HALO, :] = jnp.where(first, jnp.zeros_like(xp), xp)
    pad_ref[DN_HALO:DN_HALO + tr, :] = main_ref[0].astype(F32)
    pad_ref[DN_HALO + tr:, :] = jnp.where(last, jnp.zeros_like(xn), xn)
    y = jnp.zeros((tr, 3 * DN_WIDTH), F32)
    for t in range(DN_CONV):
        y = y + conv_ref[t:t + 1, :] * pad_ref[pl.ds(DN_HALO - DN_CONV // 2 + t, tr), :]
    y = _silu(y)
    q, k, v = y[:, 0:512], y[:, 512:1024], y[:, 1024:1536]
    bd = bd_ref[...]
    q = q * lax.rsqrt(_exact_mm(q * q, bd) + 1e-6) * (DN_HEAD_DIM ** -0.5)
    k = k * lax.rsqrt(_exact_mm(k * k, bd) + 1e-6)

    ab = ab_ref[0]
    g = -jnp.exp(gp_ref[0:1, :]) * jax.nn.softplus(ab + gp_ref[1:2, :])
    beta = jax.nn.sigmoid(ab)
    lane = lax.broadcasted_iota(jnp.int32, (tr, LANES), 1)
    gc = jnp.where(lane < DN_HEADS, _exact_mm_left(trif_ref[...], g), _exact_mm_left(trib_ref[...], g))
    gcx_all = _exact_mm(gc, eg_ref[...])
    bx_all = _exact_mm(beta, eb_ref[...])

    c = DN_CPT
    lane3 = lax.broadcasted_iota(jnp.int32, (1, 1, LANES), 2)
    left = (lane3 % LANES) < DN_HEAD_DIM
    tpos = lane3 % DN_HEAD_DIM
    lane6 = lax.broadcasted_iota(jnp.int32, (1, 1, 2 * LANES), 2)
    left6 = (lane6 % LANES) < DN_HEAD_DIM
    ri = lax.broadcasted_iota(jnp.int32, (1, DN_CHUNK, LANES), 1)
    cj = lax.broadcasted_iota(jnp.int32, (1, DN_CHUNK, LANES), 2) % DN_HEAD_DIM
    one = jnp.ones((), F32)
    zero = jnp.zeros((), F32)

    units = [(d, j) for d in range(2) for j in range(4)]
    st = {}
    for d, j in units:
        off = d * 512 + j * LANES
        gcx = gcx_all[:, off:off + LANES].reshape(c, DN_CHUNK, LANES)
        bx = bx_all[:, off:off + LANES].reshape(c, DN_CHUNK, LANES)
        qp = q[:, j * LANES:(j + 1) * LANES].reshape(c, DN_CHUNK, LANES)
        kp = k[:, j * LANES:(j + 1) * LANES].reshape(c, DN_CHUNK, LANES)
        vp = v[:, j * LANES:(j + 1) * LANES].reshape(c, DN_CHUNK, LANES)
        gl = gcx[:, DN_CHUNK - 1:DN_CHUNK, :] if d == 0 else gcx[:, 0:1, :]
        kb = kp * bx
        kq = _bmm_nt(jnp.concatenate([kb, qp], axis=1), _stack(kp, left))
        hi, mid, lo = _split3(gcx)
        a6 = jnp.where(tpos == 0, hi, jnp.where(tpos == 1, mid, jnp.where(tpos == 2, lo,
                       jnp.where(tpos < 6, one, zero))))
        b6 = jnp.where(tpos < 3, one, jnp.where(tpos == 3, -hi, jnp.where(tpos == 4, -mid,
                       jnp.where(tpos == 5, -lo, zero))))
        diff = _bmm_nt(a6, _stack(b6, left))
        st[d, j] = dict(gcx=gcx, bx=bx, qp=qp, kp=kp, vp=vp, gl=gl, kb=kb, kq=kq, diff=diff)
    for d, j in units:
        u = st[d, j]
        incl = (ri >= cj) if d == 0 else (ri <= cj)
        strict = (ri > cj) if d == 0 else (ri < cj)
        dm = jnp.exp(jnp.where(incl, u["diff"], -jnp.inf))
        u["qkm"] = u["kq"][:, DN_CHUNK:, :] * dm
        u["x"] = -jnp.where(strict, u["kq"][:, 0:DN_CHUNK, :] * dm, zero)
        u["r"] = u["x"]
    for d, j in units:
        u = st[d, j]
        u["x"] = _bmm(u["x"], _stack(u["x"], left))
    for level in range(5):
        for d, j in units:
            u = st[d, j]
            xs = _stack(u["x"], left)
            if level < 4:
                m = _bmm(jnp.concatenate([u["r"], u["x"]], axis=1), xs)
                u["r"] = u["r"] + u["x"] + m[:, 0:DN_CHUNK, :]
                u["x"] = m[:, DN_CHUNK:, :]
            else:
                u["r"] = u["r"] + u["x"] + _bmm(u["r"], xs)
    for d, j in units:
        u = st[d, j]
        eg = jnp.exp(u["gcx"])
        rhs = jnp.concatenate([u["vp"] * u["bx"], u["kb"] * eg], axis=-1)
        sol = rhs + _bmm(u["r"], _stack(rhs, left6))
        sl = slice(j * LANES, (j + 1) * LANES)
        u_out[0, d, :, sl] = sol[:, :, 0:LANES].reshape(tr, LANES)
        w_out[0, d, :, sl] = sol[:, :, LANES:].reshape(tr, LANES).astype(w_out.dtype)
        qg_out[0, d, :, sl] = (u["qp"] * eg).reshape(tr, LANES).astype(qg_out.dtype)
        kg_out[0, d, :, sl] = (u["kp"] * jnp.exp(u["gl"] - u["gcx"])).reshape(tr, LANES).astype(kg_out.dtype)
        qk_out[0, d, :, sl] = u["qkm"].reshape(tr, LANES).astype(qk_out.dtype)
        gl_out[0, d, :, :, sl] = jnp.exp(u["gl"])


def _dn_constants():
    idx = np.arange(DN_TR)
    same = (idx[:, None] // DN_CHUNK) == (idx[None, :] // DN_CHUNK)
    trif = (same & (idx[None, :] <= idx[:, None])).astype(np.float32)
    trib = (same & (idx[None, :] >= idx[:, None])).astype(np.float32)
    h = np.arange(512)
    bd = ((h[:, None] // DN_HEAD_DIM) == (h[None, :] // DN_HEAD_DIM)).astype(np.float32)
    col = np.arange(LANES)[:, None]
    out = np.arange(1024)[None, :]
    unit = (out // 512) * DN_HEADS + (out % 512) // DN_HEAD_DIM
    eg = (col == unit).astype(np.float32)
    eb = (col == unit + 2 * DN_HEADS).astype(np.float32)
    return tuple(jnp.asarray(a, MXU_DTYPE) for a in (bd, trif, trib, eg, eb))


def _dnlocal_call(dqkv, ab, conv_w, gp, consts, B, S, C):
    tr = DN_TR
    n_tiles = S // tr
    hb = tr // DN_HALO
    n_hblk = S // DN_HALO
    bd, trif, trib, eg, eb = consts
    const2 = lambda b, i: (0, 0)
    big = lambda b, i: (b, 0, i, 0)
    act = MXU_DTYPE
    shp = (B, 2, S, 512)
    return pl.pallas_call(
        functools.partial(_dnlocal_kernel, ctx_tiles=C // tr, n_tiles=n_tiles),
        grid=(B, n_tiles),
        in_specs=[pl.BlockSpec((1, tr, 1536), lambda b, i: (b, i, 0)),
                  pl.BlockSpec((1, DN_HALO, 1536), lambda b, i: (b, jnp.maximum(i * hb - 1, 0), 0)),
                  pl.BlockSpec((1, DN_HALO, 1536), lambda b, i: (b, jnp.minimum((i + 1) * hb, n_hblk - 1), 0)),
                  pl.BlockSpec((1, tr, LANES), lambda b, i: (b, i, 0)),
                  pl.BlockSpec((8, 1536), const2),
                  pl.BlockSpec((8, LANES), const2),
                  pl.BlockSpec((512, 512), const2),
                  pl.BlockSpec((tr, tr), const2),
                  pl.BlockSpec((tr, tr), const2),
                  pl.BlockSpec((LANES, 1024), const2),
                  pl.BlockSpec((LANES, 1024), const2)],
        out_specs=[pl.BlockSpec((1, 2, tr, 512), big)] * 5
                  + [pl.BlockSpec((1, 2, DN_CPT, 1, 512), lambda b, i: (b, 0, i, 0, 0))],
        out_shape=[jax.ShapeDtypeStruct(shp, F32)] + [jax.ShapeDtypeStruct(shp, act)] * 4
                  + [jax.ShapeDtypeStruct((B, 2, S // DN_CHUNK, 1, 512), F32)],
        scratch_shapes=[pltpu.VMEM((tr + 2 * DN_HALO, 1536), F32)],
        compiler_params=_params("parallel", "parallel"),
        name="dn_local",
    )(dqkv.reshape(B, S, 1536), dqkv.reshape(B, S, 1536), dqkv.reshape(B, S, 1536), ab.reshape(B, S, LANES),
      conv_w, gp, bd, trif, trib, eg, eb)


def _dnscan_kernel(uf, wf, qgf, kgf, qkf, glf, ub, wb, qgb, kgb, qkb, glb, of_out, ob_out, s_ref):
    n = pl.program_id(1)
    lane = lax.broadcasted_iota(jnp.int32, (1, LANES), 1)
    left = lane < DN_HEAD_DIM
    row = lax.broadcasted_iota(jnp.int32, (LANES, LANES), 0)
    col = lax.broadcasted_iota(jnp.int32, (LANES, LANES), 1)
    same_head = (row < DN_HEAD_DIM) == (col < DN_HEAD_DIM)
    dirs = ((uf, wf, qgf, kgf, qkf, glf, of_out), (ub, wb, qgb, kgb, qkb, glb, ob_out))
    units = [(bb, d, j) for bb in range(uf.shape[0]) for d in range(2) for j in range(4)]
    sidx = lambda bb, d, j: (bb * 2 + d) * 4 + j
    sl = lambda j: slice(j * LANES, (j + 1) * LANES)
    started = n > 0
    st = {t: jnp.where(started, s_ref[sidx(*t)], jnp.zeros((LANES, LANES), F32)) for t in units}
    pre = {(bb, d, j): _mm(dirs[d][1][bb, 0, :, sl(j)], st[bb, d, j]) for bb, d, j in units}
    o1 = {(bb, d, j): _mm(dirs[d][2][bb, 0, :, sl(j)], st[bb, d, j]) for bb, d, j in units}
    vn = {(bb, d, j): dirs[d][0][bb, 0, :, sl(j)] - pre[bb, d, j] for bb, d, j in units}
    for bb, d, j in units:
        v = vn[bb, d, j]
        z = jnp.zeros_like(v)
        vst = jnp.concatenate([jnp.where(left, v, z), jnp.where(left, z, v)], axis=0)
        dirs[d][6][bb, :, sl(j)] = o1[bb, d, j] + _mm(dirs[d][4][bb, 0, :, sl(j)], vst)
    for bb, d, j in units:
        upd = _mm_tn(dirs[d][3][bb, 0, :, sl(j)], vn[bb, d, j])
        s_ref[sidx(bb, d, j)] = (st[bb, d, j] * dirs[d][5][bb, 0, 0, :, sl(j)]
                                 + jnp.where(same_head, upd, jnp.zeros_like(upd)))


DN_SCAN_BATCH = 4


def _dnscan_call(local, B, S, C):
    u, w, qg, kg, qk, gl = local
    nch, nc = S // DN_CHUNK, C // DN_CHUNK
    bb = DN_SCAN_BATCH if B % DN_SCAN_BATCH == 0 else 1

    def bidx(n):
        return jnp.where(n < nc, nc - 1 - n, nch - 1 + nc - n)

    fspec = pl.BlockSpec((bb, 1, DN_CHUNK, 512), lambda b, n: (b, 0, n, 0))
    bspec = pl.BlockSpec((bb, 1, DN_CHUNK, 512), lambda b, n: (b, 1, bidx(n), 0))
    fgl = pl.BlockSpec((bb, 1, 1, 1, 512), lambda b, n: (b, 0, n, 0, 0))
    bgl = pl.BlockSpec((bb, 1, 1, 1, 512), lambda b, n: (b, 1, bidx(n), 0, 0))
    return pl.pallas_call(
        _dnscan_kernel,
        grid=(B // bb, nch),
        in_specs=[fspec] * 5 + [fgl] + [bspec] * 5 + [bgl],
        out_specs=[pl.BlockSpec((bb, DN_CHUNK, 512), lambda b, n: (b, n, 0)),
                   pl.BlockSpec((bb, DN_CHUNK, 512), lambda b, n: (b, bidx(n), 0))],
        out_shape=[jax.ShapeDtypeStruct((B, S, 512), F32)] * 2,
        scratch_shapes=[pltpu.VMEM((bb * 8, LANES, LANES), F32)],
        compiler_params=_params("parallel", "arbitrary"),
        name="dn_scan",
    )(u, w, qg, kg, qk, gl, u, w, qg, kg, qk, gl)


def _pack_pairs(v):
    w = v.shape[1] // 2
    bits = lax.bitcast_convert_type(v.astype(jnp.bfloat16).astype(F32), jnp.int32)
    return lax.shift_right_logical(bits[:, :w], 16) | bits[:, w:]


def _unpack_pairs(p):
    lo = lax.bitcast_convert_type(lax.shift_left(p, 16), F32)
    hi = lax.bitcast_convert_type(p & jnp.int32(-65536), F32)
    return lo, hi


def _merge_kernel(x_ref, ga_ref, shf_ref, scf_ref, omla_ref, of_ref, ob_ref, dz_ref, ogqa_ref, gates_ref,
                  wo1_ref, wo2_ref, wo3_ref, wout_ref, dnn_ref, bd_ref, lng_ref, lnb_ref, wr_ref, rb_ref,
                  x1_out, ufp_out, eidx_out, rank_out, ew_out, cnt_out, cnt_ref):
    @pl.when(pl.program_id(0) == 0)
    def _():
        cnt_ref[...] = jnp.zeros_like(cnt_ref)

    o = of_ref[...] + ob_ref[...]
    ms = _exact_mm(o * o, bd_ref[...]) * (1.0 / DN_HEAD_DIM)
    dn = o * lax.rsqrt(ms + NORM_EPS) * dnn_ref[...] * _silu(dz_ref[...].astype(F32))
    g1 = gates_ref[:, 0:1024].astype(F32)
    g2 = gates_ref[:, 1024:2048].astype(F32)
    g3 = gates_ref[:, 2048:3072].astype(F32)
    m = (g1 * _mm(omla_ref[...], wo1_ref[...]) + g2 * _mm(dn, wo2_ref[...])
         + g3 * _mm(ogqa_ref[...], wo3_ref[...]))
    y = _mm(m, wout_ref[...])
    x1 = _layernorm(DEEPNORM_ALPHA * x_ref[...] + ga_ref[0] * y, lng_ref[...], lnb_ref[...])
    x1_out[...] = x1
    uf = x1 * (1.0 + scf_ref[0]) + shf_ref[0]
    _split_pieces(_pack_pairs(uf), ufp_out)

    tm = uf.shape[0]
    scores = jax.nn.sigmoid(_mm_nt(wr_ref[...], uf))[0:N_EXPERTS]
    sel = scores + rb_ref[0:N_EXPERTS, :]
    gsz = N_EXPERTS // N_GROUPS
    neg = jnp.full((), -jnp.inf, F32)
    sel3 = sel.reshape(N_GROUPS, gsz, tm)
    mem = lax.broadcasted_iota(jnp.int32, (N_GROUPS, gsz, tm), 1)
    m1 = jnp.max(sel3, 1, keepdims=True)
    i1 = jnp.min(jnp.where(sel3 == m1, mem, gsz), 1, keepdims=True)
    m2 = jnp.max(jnp.where(mem == i1, neg, sel3), 1, keepdims=True)
    gs = (m1 + m2).reshape(N_GROUPS, tm)
    gi = lax.broadcasted_iota(jnp.int32, (N_GROUPS, tm), 0)
    grank = jnp.zeros((N_GROUPS, tm), jnp.int32)
    for gp in range(N_GROUPS):
        other = gs[gp:gp + 1, :]
        beats = (other > gs) | ((other == gs) & (gp < gi))
        grank = grank + beats.astype(jnp.int32)
    gsel = (grank < TOPK_GROUPS).reshape(N_GROUPS, 1, tm)
    cur = jnp.where(gsel, sel3, neg).reshape(N_EXPERTS, tm)
    ei = lax.broadcasted_iota(jnp.int32, (N_EXPERTS, tm), 0)
    zero = jnp.zeros((N_EXPERTS, tm), F32)
    one = jnp.ones((N_EXPERTS, tm), F32)
    chosen = zero
    picks = []
    for _ in range(TOP_K):
        mx = jnp.max(cur, 0, keepdims=True)
        ix = jnp.min(jnp.where(cur == mx, ei, N_EXPERTS), 0, keepdims=True)
        pick = ei == ix
        picks.append((ix, pick))
        chosen = chosen + jnp.where(pick, one, zero)
        cur = jnp.where(pick, neg, cur)

    r_i = lax.broadcasted_iota(jnp.int32, (tm, tm), 0)
    c_i = lax.broadcasted_iota(jnp.int32, (tm, tm), 1)
    before = jnp.where(r_i < c_i, 1.0, 0.0)
    pos = cnt_ref[:, 0:1] + _mm(chosen, before)
    cnt_new = cnt_ref[...] + jnp.sum(chosen, 1, keepdims=True)
    cnt_ref[...] = cnt_new
    cnt_out[...] = cnt_new

    w_rows = [jnp.sum(jnp.where(pick, scores, zero), 0, keepdims=True) for _, pick in picks]
    wsum = w_rows[0]
    for w_k in w_rows[1:]:
        wsum = wsum + w_k
    eidx_out[...] = jnp.concatenate([ix for ix, _ in picks], axis=0)
    rank_out[...] = jnp.concatenate(
        [jnp.sum(jnp.where(pick, pos, zero), 0, keepdims=True) for _, pick in picks], axis=0).astype(jnp.int32)
    w8 = jnp.concatenate([w_k / wsum * ROUTED_SCALE for w_k in w_rows], axis=0)
    ew_out[...] = jnp.concatenate([w8, jnp.zeros((LANES - TOP_K, tm), F32)], axis=0).T


def _merge_call(xs, mod_l, omla, of, ob, dz, ogqa, gates, wo1, wo2, wo3, wout, dnn, bd, lng, lnb, wr, rb, B, S, C):
    T, D = xs.shape
    tpb, ctiles = S // TM, C // TM
    row = lambda i: (i, 0)
    const = lambda i: (0, 0)
    modspec = lambda k: pl.BlockSpec((1, 1, D), _mod_index(tpb, ctiles, k))
    return pl.pallas_call(
        _merge_kernel,
        grid=(T // TM,),
        in_specs=[pl.BlockSpec((TM, D), row), modspec(2), modspec(3), modspec(4),
                  pl.BlockSpec((TM, 512), row), pl.BlockSpec((TM, 512), row), pl.BlockSpec((TM, 512), row),
                  pl.BlockSpec((TM, 512), row), pl.BlockSpec((TM, 512), row), pl.BlockSpec((TM, 3072), row),
                  pl.BlockSpec((512, D), const), pl.BlockSpec((512, D), const), pl.BlockSpec((512, D), const),
                  pl.BlockSpec((D, D), const), pl.BlockSpec((1, 512), const), pl.BlockSpec((512, 512), const),
                  pl.BlockSpec((1, D), const), pl.BlockSpec((1, D), const),
                  pl.BlockSpec((LANES, D), const), pl.BlockSpec((LANES, 1), const)],
        out_specs=[pl.BlockSpec((TM, D), row), pl.BlockSpec((N_PIECES, TM, PIECE), lambda i: (0, i, 0)),
                   pl.BlockSpec((TOP_K, TM), lambda i: (0, i)), pl.BlockSpec((TOP_K, TM), lambda i: (0, i)),
                   pl.BlockSpec((TM, LANES), row), pl.BlockSpec((N_EXPERTS, LANES), const)],
        out_shape=[jax.ShapeDtypeStruct((T, D), F32), jax.ShapeDtypeStruct((N_PIECES, T, PIECE), jnp.int32),
                   jax.ShapeDtypeStruct((TOP_K, T), jnp.int32), jax.ShapeDtypeStruct((TOP_K, T), jnp.int32),
                   jax.ShapeDtypeStruct((T, LANES), F32), jax.ShapeDtypeStruct((N_EXPERTS, LANES), F32)],
        scratch_shapes=[pltpu.VMEM((N_EXPERTS, LANES), F32)],
        compiler_params=_params("arbitrary"),
        name="merge_norm_route",
    )(xs, mod_l, mod_l, mod_l, omla, of.reshape(T, 512), ob.reshape(T, 512), dz, ogqa, gates,
      wo1, wo2, wo3, wout, dnn, bd, lng, lnb, wr, rb)


EXPERT_BLOCK = 512
SC_WINDOW = 128
N_PIECES = 2
PIECE = D_MODEL // 2 // N_PIECES


def _split_pieces(packed, out_ref):
    for h in range(N_PIECES):
        out_ref[h] = packed[:, h * PIECE:(h + 1) * PIECE]


def _mm_pieces(pieces, w):
    acc = None
    for h, (lo, hi) in enumerate(pieces):
        t = (_mm(lo, w[h * PIECE:(h + 1) * PIECE, :])
             + _mm(hi, w[D_MODEL // 2 + h * PIECE:D_MODEL // 2 + (h + 1) * PIECE, :]))
        acc = t if acc is None else acc + t
    return acc


def _sc_mesh():
    return plsc.VectorSubcoreMesh(core_axis_name="c", subcore_axis_name="s")


def _sc_gather_rows(y, idx):
    n = idx.shape[1]
    W = y.shape[1]

    @pl.kernel(out_type=jax.ShapeDtypeStruct((n, W), y.dtype), mesh=_sc_mesh(), scratch_types=[])
    def gather(y_hbm, i_hbm, o_hbm):
        def body(i_vmem, o_vmem):
            pltpu.sync_copy(y_hbm.at[i_vmem.at[0]], o_vmem)

        pltpu.emit_pipeline(
            body,
            grid=(n // SC_WINDOW,),
            in_specs=[pl.BlockSpec((1, SC_WINDOW), lambda i: (0, i))],
            out_specs=[pl.BlockSpec((SC_WINDOW, W), lambda i: (i, 0))],
            core_axis_name=("c", "s"),
            dimension_semantics=(pltpu.PARALLEL,),
        )(i_hbm, o_hbm)

    return gather(y, idx)


def _experts_kernel(be_ref, nv_ref, xb_ref, wg_ref, wu_ref, wd_ref, y_out):
    b = pl.program_id(0)
    nv = nv_ref[b]

    @pl.when(nv > 0)
    def _():
        rows = lax.broadcasted_iota(jnp.int32, xb_ref.shape[1:], 0)
        pieces = []
        for h in range(N_PIECES):
            xh = xb_ref[h]
            pieces.append(_unpack_pairs(jnp.where(rows < nv, xh, jnp.zeros_like(xh))))
        hid = _silu(_mm_pieces(pieces, wg_ref.at[0])) * _mm_pieces(pieces, wu_ref.at[0])
        _split_pieces(_pack_pairs(_mm(hid, wd_ref[0])), y_out)


def _experts_call(xb, block_e, nvalid, wg, wu, wd):
    _, R, _ = xb.shape
    D = D_MODEL
    blk = (N_PIECES, EXPERT_BLOCK, PIECE)
    grid_spec = pltpu.PrefetchScalarGridSpec(
        num_scalar_prefetch=2,
        grid=(R // EXPERT_BLOCK,),
        in_specs=[pl.BlockSpec(blk, lambda b, be, nv: (0, b, 0)),
                  pl.BlockSpec((1, D, EXPERT_DIM), lambda b, be, nv: (be[b], 0, 0)),
                  pl.BlockSpec((1, D, EXPERT_DIM), lambda b, be, nv: (be[b], 0, 0)),
                  pl.BlockSpec((1, EXPERT_DIM, D), lambda b, be, nv: (be[b], 0, 0))],
        out_specs=pl.BlockSpec(blk, lambda b, be, nv: (0, b, 0)),
    )
    return pl.pallas_call(
        _experts_kernel,
        grid_spec=grid_spec,
        out_shape=jax.ShapeDtypeStruct((N_PIECES, R, PIECE), jnp.int32),
        compiler_params=_params("arbitrary"),
        name="moe_experts",
    )(block_e, nvalid, xb, wg, wu, wd)


def _combine_kernel(x_ref, ufp_ref, yg_ref, ew_ref, gf_ref, sg_ref, su_ref, sd_ref, g_ref, b_ref, o_ref):
    pieces = [_unpack_pairs(ufp_ref[h]) for h in range(N_PIECES)]
    hs = _silu(_mm_pieces(pieces, sg_ref)) * _mm_pieces(pieces, su_ref)
    f = _mm(hs, sd_ref[...])
    ew = ew_ref[...]
    lane = lax.broadcasted_iota(jnp.int32, ew.shape, 1)
    acc = [[jnp.zeros((x_ref.shape[0], PIECE), F32) for _ in range(N_PIECES)] for _ in range(2)]
    for k in range(TOP_K):
        wk = jnp.sum(jnp.where(lane == k, ew, jnp.zeros_like(ew)), axis=1, keepdims=True)
        for h in range(N_PIECES):
            ylo, yhi = _unpack_pairs(yg_ref[h, k])
            acc[0][h] = acc[0][h] + wk * ylo
            acc[1][h] = acc[1][h] + wk * yhi
    f = f + jnp.concatenate(acc[0] + acc[1], axis=1)
    o_ref[...] = _layernorm(DEEPNORM_ALPHA * x_ref[...] + gf_ref[0] * f, g_ref[...], b_ref[...])


def _combine_call(x1, ufp, yg, ew, mod_l, sg, su, sd, g, b, B, S, C):
    T, D = x1.shape
    tpb, ctiles = S // TM, C // TM
    row = lambda i: (i, 0)
    const = lambda i: (0, 0)
    return pl.pallas_call(
        _combine_kernel,
        grid=(T // TM,),
        in_specs=[pl.BlockSpec((TM, D), row), pl.BlockSpec((N_PIECES, TM, PIECE), lambda i: (0, i, 0)),
                  pl.BlockSpec((N_PIECES, TOP_K, TM, PIECE), lambda i: (0, 0, i, 0)),
                  pl.BlockSpec((TM, LANES), row),
                  pl.BlockSpec((1, 1, D), _mod_index(tpb, ctiles, 5)),
                  pl.BlockSpec((D, SHARED_DIM), const), pl.BlockSpec((D, SHARED_DIM), const),
                  pl.BlockSpec((SHARED_DIM, D), const),
                  pl.BlockSpec((1, D), const), pl.BlockSpec((1, D), const)],
        out_specs=pl.BlockSpec((TM, D), row),
        out_shape=jax.ShapeDtypeStruct((T, D), F32),
        compiler_params=_params("parallel"),
        name="moe_combine_norm",
    )(x1, ufp, yg, ew, mod_l, sg, su, sd, g, b)


def _moe_routed(ufp, eidx_t, rank_t, counts, wg, wu, wd):
    T = ufp.shape[1]
    n_blocks = -(-(T * TOP_K + N_EXPERTS * (EXPERT_BLOCK - 1)) // EXPERT_BLOCK)
    n_rows = n_blocks * EXPERT_BLOCK
    padded = (counts + EXPERT_BLOCK - 1) // EXPERT_BLOCK * EXPERT_BLOCK
    pad_end = jnp.cumsum(padded)
    start_pad = pad_end - padded
    experts = jnp.arange(N_EXPERTS, dtype=jnp.int32)

    def lookup(table, idx):
        sel = idx[None] == experts.reshape((N_EXPERTS,) + (1,) * idx.ndim)
        return jnp.sum(jnp.where(sel, table.reshape((N_EXPERTS,) + (1,) * idx.ndim), 0), axis=0)

    dest_t = lookup(start_pad, eidx_t) + rank_t
    blk = jnp.arange(n_blocks, dtype=jnp.int32) * EXPERT_BLOCK
    block_e = jnp.minimum(jnp.sum((blk[:, None] >= pad_end[None, :]).astype(jnp.int32), axis=1), N_EXPERTS - 1)
    nvalid = jnp.clip(lookup(counts, block_e) - (blk - lookup(start_pad, block_e)), 0, EXPERT_BLOCK)
    tok = jnp.broadcast_to(jnp.arange(T, dtype=jnp.int32), (TOP_K, T))
    row_tok = (jnp.arange(n_rows, dtype=jnp.int32) % T).at[dest_t.reshape(-1)].set(
        tok.reshape(-1), unique_indices=True)
    piece = jnp.arange(N_PIECES, dtype=jnp.int32)
    src = (piece[:, None] * T + row_tok[None, :]).reshape(1, N_PIECES * n_rows)
    xb = _sc_gather_rows(ufp.reshape(N_PIECES * T, PIECE), src).reshape(N_PIECES, n_rows, PIECE)
    yb = _experts_call(xb, block_e.astype(jnp.int32), nvalid.astype(jnp.int32), wg, wu, wd)
    back = (piece[:, None, None] * n_rows + dest_t[None]).reshape(1, N_PIECES * TOP_K * T)
    yg = _sc_gather_rows(yb.reshape(N_PIECES * n_rows, PIECE), back)
    return yg.reshape(N_PIECES, TOP_K, T, PIECE)


def _rot_cols(w, half):
    return jnp.concatenate([-w[:, half:], w[:, :half]], axis=1)


def _prep_w_in(w):
    d = w.shape[0]
    offs = np.cumsum((0,) + IN_SIZES)
    cq, ckv, kr, dqkv, da, db, dz, gq, gk, gv, gates = (w[:, offs[t]:offs[t + 1]] for t in range(len(IN_SIZES)))
    z = lambda n: jnp.zeros((d, n), w.dtype)
    krg = jnp.concatenate([z(64), kr, z(32)], 1)
    krr = jnp.concatenate([z(64), _rot_cols(kr, MLA_ROPE // 2), z(32)], 1)
    ab = jnp.concatenate([da, db, z(LANES - 4 * DN_HEADS)], 1)
    hd = GQA_HEAD_DIM
    gq_rot = jnp.concatenate([_rot_cols(gq[:, h * hd:(h + 1) * hd], hd // 2) for h in range(GQA_HEADS)], 1)
    dup = lambda t: jnp.concatenate([t[:, 0:hd], t[:, 0:hd], t[:, hd:2 * hd], t[:, hd:2 * hd]], 1)
    gk_rot = jnp.concatenate([_rot_cols(gk[:, h * hd:(h + 1) * hd], hd // 2) for h in range(GQA_KV_HEADS)], 1)
    cat = jnp.concatenate([cq, ckv, krg, krr, dqkv, ab, dz, gq, gq_rot, dup(gk), dup(gk_rot), dup(gv), gates], 1)
    assert cat.shape[1] == NZ
    return cat.astype(MXU_DTYPE)


def _prep_w_uq(w):
    d = w.shape[0]
    hw = MLA_NOPE + MLA_ROPE
    a, b = [], []
    for h in range(MLA_HEADS):
        wh = w[:, h * hw:(h + 1) * hw]
        a += [wh, jnp.zeros((d, LANES - hw), w.dtype)]
        b += [jnp.zeros((d, MLA_NOPE), w.dtype), _rot_cols(wh[:, MLA_NOPE:], MLA_ROPE // 2),
              jnp.zeros((d, LANES - hw), w.dtype)]
    return jnp.concatenate(a + b, 1).astype(MXU_DTYPE)


def _prep_w_ukv(w):
    d = w.shape[0]
    hw = MLA_NOPE + MLA_V
    kpart, vpart = [], []
    for h in range(MLA_HEADS):
        wh = w[:, h * hw:(h + 1) * hw]
        kpart += [wh[:, :MLA_NOPE], jnp.zeros((d, LANES - MLA_NOPE), w.dtype)]
        vpart += [wh[:, MLA_NOPE:]]
    return jnp.concatenate(kpart + vpart, 1).astype(MXU_DTYPE)


def _rope_tables(n_rows, C):
    row = jnp.repeat(jnp.arange(n_rows, dtype=F32), GRID_W)
    col = jnp.tile(jnp.arange(GRID_W, dtype=F32), n_rows)

    def angles(dim):
        n = dim // 4
        inv = ROPE_BASE ** (-jnp.arange(n, dtype=F32) / n)
        return jnp.concatenate([row[:, None] * inv, col[:, None] * inv], axis=-1)

    def with_ctx(cos, sin):
        return (jnp.concatenate([jnp.ones((C, LANES), F32), cos], 0),
                jnp.concatenate([jnp.zeros((C, LANES), F32), sin], 0))

    L = n_rows * GRID_W
    am = angles(MLA_ROPE)
    one, zero = jnp.ones((L, MLA_NOPE), F32), jnp.zeros((L, MLA_NOPE), F32)
    cm = jnp.concatenate([one, jnp.cos(am), jnp.cos(am), one[:, :32]], 1)
    sm = jnp.concatenate([zero, jnp.sin(am), jnp.sin(am), zero[:, :32]], 1)
    ag = angles(GQA_HEAD_DIM)
    cg = jnp.tile(jnp.cos(ag), (1, 4))
    sg = jnp.tile(jnp.sin(ag), (1, 4))
    return with_ctx(cm, sm) + with_ctx(cg, sg)


def kernel(x, c, ctx, c_ctx, w_ada, b_ada, w_in, mla_q_norm, mla_kv_norm, w_uq, w_ukv, dn_conv, dn_a_log, dn_dt_bias, dn_norm, gqa_sink, w_o_mla, w_o_dn, w_o_gqa, w_out, ln1_g, ln1_b, w_router, router_bias, w_exp_gate, w_exp_up, w_exp_down, w_sh_gate, w_sh_up, w_sh_down, ln2_g, ln2_b):
    B, L, D = x.shape
    C = ctx.shape[1]
    S = C + L
    nl = w_in.shape[0]
    assert D == D_MODEL and nl == DEPTH and B <= CTX_MOD_ROW
    assert C % TM == 0 and L % TM == 0 and L % GRID_W == 0 and L >= 3 * WINDOW
    cast = lambda t: t.astype(MXU_DTYPE)

    cc = jnp.zeros((MOD_ROWS, D), F32).at[0:B].set(c).at[CTX_MOD_ROW].set(c_ctx)
    mods = _ada_call(cc, w_ada, b_ada)
    tabs = _rope_tables(L // GRID_W, C)
    dn_consts = _dn_constants()
    xs = jnp.concatenate([ctx, x], axis=1).reshape(B * S, D)

    for l in range(nl):
        mod_l = mods[l].reshape(MOD_ROWS * 6, 1, D)
        q, k, v, dqkv, ab, dz, gq, gkv, gates = _inproj_call(
            xs, mod_l, _prep_w_in(w_in[l]), _prep_w_uq(w_uq[l]), _prep_w_ukv(w_ukv[l]),
            mla_q_norm[l].reshape(1, -1), mla_kv_norm[l].reshape(1, -1), tabs, B, S, C)
        omla = _mla_call(q, k, v, B, S, C)
        conv8 = jnp.zeros((8, 3 * DN_WIDTH), F32).at[0:DN_CONV].set(dn_conv[l])
        gp = (jnp.zeros((8, LANES), F32).at[0, 0:2 * DN_HEADS].set(dn_a_log[l].reshape(-1))
              .at[1, 0:2 * DN_HEADS].set(dn_dt_bias[l].reshape(-1)))
        local = _dnlocal_call(dqkv, ab, conv8, gp, dn_consts, B, S, C)
        of, ob = _dnscan_call(local, B, S, C)
        ogqa = _gqa_call(gq, gkv, gqa_sink[l], B, S, C)
        wr = jnp.zeros((LANES, D), F32).at[0:N_EXPERTS].set(w_router[l].T)
        rb = jnp.zeros((LANES, 1), F32).at[0:N_EXPERTS, 0].set(router_bias[l])
        x1, ufp, eidx_t, rank_t, ew, cnt = _merge_call(
            xs, mod_l, omla, of, ob, dz, ogqa, gates,
            cast(w_o_mla[l]), cast(w_o_dn[l]), cast(w_o_gqa[l]), cast(w_out[l]),
            jnp.tile(dn_norm[l], DN_HEADS).reshape(1, DN_WIDTH), dn_consts[0],
            ln1_g[l].reshape(1, D), ln1_b[l].reshape(1, D), cast(wr), rb, B, S, C)
        yg = _moe_routed(ufp, eidx_t, rank_t, cnt[:, 0].astype(jnp.int32),
                         cast(w_exp_gate[l]), cast(w_exp_up[l]), cast(w_exp_down[l]))
        xs = _combine_call(x1, ufp, yg, ew, mod_l, cast(w_sh_gate[l]), cast(w_sh_up[l]), cast(w_sh_down[l]),
                           ln2_g[l].reshape(1, D), ln2_b[l].reshape(1, D), B, S, C)
    return xs.reshape(B, S, D)[:, C:, :]
```

```python
import functools

import numpy as np
import jax
import jax.numpy as jnp
from jax import lax
from jax.experimental import pallas as pl
from jax.experimental.pallas import tpu as pltpu
from jax.experimental.pallas import tpu_sc as plsc

F32 = jnp.float32
MXU_DTYPE = jnp.bfloat16

D_MODEL = 1024
DEPTH = 4
GRID_W = 64
NORM_EPS = 1e-6
ROPE_BASE = 10000.0
DEEPNORM_ALPHA = (2.0 * DEPTH) ** 0.25

MLA_HEADS = 8
MLA_Q_LORA = 256
MLA_KV_LORA = 128
MLA_NOPE = 64
MLA_ROPE = 32
MLA_V = 64
MLA_SCALE = (MLA_NOPE + MLA_ROPE) ** -0.5
LOG2E = float(np.log2(np.e))

DN_HEADS = 8
DN_HEAD_DIM = 64
DN_WIDTH = DN_HEADS * DN_HEAD_DIM
DN_CONV = 5
DN_CHUNK = 64

GQA_HEADS = 8
GQA_KV_HEADS = 2
GQA_HEAD_DIM = 64
GQA_SCALE = GQA_HEAD_DIM ** -0.5
WINDOW = 128

N_EXPERTS = 64
TOP_K = 8
N_GROUPS = 8
TOPK_GROUPS = 4
EXPERT_DIM = 256
SHARED_DIM = 256
ROUTED_SCALE = 2.5

IN_SIZES = (MLA_Q_LORA, MLA_KV_LORA, MLA_ROPE,
            3 * DN_WIDTH, 2 * DN_HEADS, 2 * DN_HEADS, DN_WIDTH,
            GQA_HEADS * GQA_HEAD_DIM, GQA_KV_HEADS * GQA_HEAD_DIM, GQA_KV_HEADS * GQA_HEAD_DIM,
            3 * D_MODEL)

LANES = 128
TM = 256
MOD_ROWS = 16
CTX_MOD_ROW = 8

OFF_A = 0
OFF_DQKV = 640
OFF_AB = OFF_DQKV + 3 * DN_WIDTH
OFF_DZ = OFF_AB + LANES
OFF_GQ = OFF_DZ + DN_WIDTH
OFF_GK = OFF_GQ + 1024
OFF_GATES = OFF_GK + 768
NZ = OFF_GATES + 3 * D_MODEL

VMEM_LIMIT = 56 * 1024 * 1024


def _mm(a, b):
    return jnp.dot(a.astype(MXU_DTYPE), b.astype(MXU_DTYPE), preferred_element_type=F32)


def _mm_nt(a, b):
    return lax.dot_general(a.astype(MXU_DTYPE), b.astype(MXU_DTYPE), (((1,), (1,)), ((), ())),
                           preferred_element_type=F32)


def _mm_tn(a, b):
    return lax.dot_general(a.astype(MXU_DTYPE), b.astype(MXU_DTYPE), (((0,), (0,)), ((), ())),
                           preferred_element_type=F32)


def _bmm(a, b):
    return jnp.einsum('cik,ckj->cij', a.astype(MXU_DTYPE), b.astype(MXU_DTYPE), preferred_element_type=F32)


def _bmm_nt(a, b):
    return jnp.einsum('cik,cjk->cij', a.astype(MXU_DTYPE), b.astype(MXU_DTYPE), preferred_element_type=F32)


def _split3(x):
    hi = x.astype(jnp.bfloat16).astype(F32)
    r = x - hi
    mid = r.astype(jnp.bfloat16).astype(F32)
    lo = (r - mid).astype(jnp.bfloat16).astype(F32)
    return hi, mid, lo


def _exact_mm(x, m01):
    hi, mid, lo = _split3(x)
    return _mm(hi, m01) + _mm(mid, m01) + _mm(lo, m01)


def _exact_mm_left(m01, x):
    hi, mid, lo = _split3(x)
    return _mm(m01, hi) + _mm(m01, mid) + _mm(m01, lo)


def _silu(x):
    return x * jax.nn.sigmoid(x)


def _layernorm(v, g, b):
    mu = jnp.mean(v, -1, keepdims=True)
    d = v - mu
    var = jnp.mean(d * d, -1, keepdims=True)
    return d * lax.rsqrt(var + NORM_EPS) * g + b


def _mod_index(tiles_per_b, ctx_tiles, k):
    def index(i):
        row = jnp.where((i % tiles_per_b) < ctx_tiles, CTX_MOD_ROW, i // tiles_per_b)
        return (row * 6 + k, 0, 0)
    return index


def _params(*sem):
    return pltpu.CompilerParams(dimension_semantics=sem, vmem_limit_bytes=VMEM_LIMIT)


def _ada_kernel(c_ref, w_ref, b_ref, o_ref):
    o_ref[0] = _mm(_silu(c_ref[...]), w_ref[0]) + b_ref[0]


def _ada_call(cc, w_ada, b_ada):
    nl, d, n6 = w_ada.shape
    tn = 1536
    return pl.pallas_call(
        _ada_kernel,
        grid=(nl, n6 // tn),
        in_specs=[pl.BlockSpec((MOD_ROWS, d), lambda l, j: (0, 0)),
                  pl.BlockSpec((1, d, tn), lambda l, j: (l, 0, j)),
                  pl.BlockSpec((1, 1, tn), lambda l, j: (l, 0, j))],
        out_specs=pl.BlockSpec((1, MOD_ROWS, tn), lambda l, j: (l, 0, j)),
        out_shape=jax.ShapeDtypeStruct((nl, MOD_ROWS, n6), F32),
        compiler_params=_params("parallel", "parallel"),
        name="ada_mod",
    )(cc, w_ada, b_ada.reshape(nl, 1, n6))


def _inproj_kernel(x_ref, sh_ref, sc_ref, w_ref, wuq_ref, wukv_ref, qn_ref, kvn_ref,
                   cm_ref, sm_ref, cg_ref, sg_ref,
                   q_out, k_out, v_out, dqkv_out, ab_out, dz_out, gq_out, gkv_out, gates_out):
    u = (x_ref[...] * (1.0 + sc_ref[0]) + sh_ref[0]).astype(MXU_DTYPE)

    def z(off, width):
        return jnp.dot(u, w_ref[:, off:off + width], preferred_element_type=F32)

    def rms(v, g):
        return v * lax.rsqrt(jnp.mean(v * v, -1, keepdims=True) + NORM_EPS) * g

    cm, sm, cg, sg = cm_ref[...], sm_ref[...], cg_ref[...], sg_ref[...]

    za = z(OFF_A, 640)
    qq = _mm(rms(za[:, 0:256], qn_ref[...]), wuq_ref[...])
    kvv = _mm(rms(za[:, 256:384], kvn_ref[...]), wukv_ref[...])
    k_rope = za[:, 384:512] * cm + za[:, 512:640] * sm
    for h in range(MLA_HEADS):
        sl = slice(h * LANES, (h + 1) * LANES)
        qa = qq[:, h * LANES:(h + 1) * LANES]
        qb = qq[:, 1024 + h * LANES:1024 + (h + 1) * LANES]
        q_out[:, sl] = ((qa * cm + qb * sm) * (MLA_SCALE * LOG2E)).astype(q_out.dtype)
        k_out[:, sl] = (kvv[:, sl] + k_rope).astype(k_out.dtype)
    v_out[...] = kvv[:, 1024:1536].astype(v_out.dtype)

    for t in range(3):
        dqkv_out[:, t * 512:(t + 1) * 512] = z(OFF_DQKV + t * 512, 512).astype(dqkv_out.dtype)
    ab_out[...] = z(OFF_AB, LANES)
    dz_out[...] = z(OFF_DZ, DN_WIDTH).astype(dz_out.dtype)

    zq = z(OFF_GQ, 1024)
    for p in range(4):
        sl = slice(p * LANES, (p + 1) * LANES)
        gq_out[:, sl] = ((zq[:, sl] * cg + zq[:, 512 + p * LANES:512 + (p + 1) * LANES] * sg)
                         * (GQA_SCALE * LOG2E)).astype(gq_out.dtype)
    zk = z(OFF_GK, 768)
    for j in range(2):
        sl = slice(j * LANES, (j + 1) * LANES)
        gkv_out[:, sl] = (zk[:, sl] * cg + zk[:, 256 + j * LANES:256 + (j + 1) * LANES] * sg).astype(gkv_out.dtype)
    gkv_out[:, 256:512] = zk[:, 512:768].astype(gkv_out.dtype)

    for t in range(3):
        gates_out[:, t * 1024:(t + 1) * 1024] = jax.nn.sigmoid(z(OFF_GATES + t * 1024, 1024)).astype(gates_out.dtype)


def _inproj_call(xs, mod_l, w_cat, wuq_cat, wukv_cat, qn, kvn, tabs, B, S, C):
    T, D = xs.shape
    tpb, ctiles = S // TM, C // TM
    act = MXU_DTYPE
    row = lambda i: (i, 0)
    const = lambda i: (0, 0)
    tab = lambda i: (i % tpb, 0)
    widths = (1024, 1024, 512, 1536, LANES, 512, 512, 512, 3072)
    dtypes = (act, act, act, act, F32, act, act, act, act)
    return pl.pallas_call(
        _inproj_kernel,
        grid=(T // TM,),
        in_specs=[pl.BlockSpec((TM, D), row),
                  pl.BlockSpec((1, 1, D), _mod_index(tpb, ctiles, 0)),
                  pl.BlockSpec((1, 1, D), _mod_index(tpb, ctiles, 1)),
                  pl.BlockSpec((D, NZ), const),
                  pl.BlockSpec((MLA_Q_LORA, 2048), const),
                  pl.BlockSpec((MLA_KV_LORA, 1536), const),
                  pl.BlockSpec((1, MLA_Q_LORA), const),
                  pl.BlockSpec((1, MLA_KV_LORA), const)]
                 + [pl.BlockSpec((TM, LANES), tab)] * 4,
        out_specs=[pl.BlockSpec((TM, w), row) for w in widths],
        out_shape=[jax.ShapeDtypeStruct((T, w), dt) for w, dt in zip(widths, dtypes)],
        compiler_params=_params("parallel"),
        name="in_proj",
    )(xs, mod_l, mod_l, w_cat, wuq_cat, wukv_cat, qn, kvn, *tabs)


def _mla_kernel(q_ref, k_ref, v_ref, o_ref, *, n_ctx, n_all):
    i = pl.program_id(2)
    tq = q_ref.shape[1]
    left = lax.broadcasted_iota(jnp.int32, (tq, LANES), 1) < MLA_V

    def attend(nk, ctx_rows):
        v = v_ref[0, 0:nk, :]
        v_left = lax.broadcasted_iota(jnp.int32, v.shape, 1) < MLA_V
        ones = jnp.ones_like(v)
        s = [_mm_nt(q_ref[0, :, hh * LANES:(hh + 1) * LANES], k_ref[0, 0:nk, hh * LANES:(hh + 1) * LANES])
             for hh in range(2)]
        if ctx_rows:
            row = lax.broadcasted_iota(jnp.int32, (tq, nk), 0)
            col = lax.broadcasted_iota(jnp.int32, (tq, nk), 1)
            ok = (row >= ctx_rows) | (col < ctx_rows)
            s = [jnp.where(ok, x, -jnp.inf) for x in s]
        p = [jnp.exp2((x - jnp.max(x, -1, keepdims=True)).astype(MXU_DTYPE)) for x in s]
        o = [_mm(p[0], jnp.where(v_left, v, ones)), _mm(p[1], jnp.where(v_left, ones, v))]
        outs = [x / pltpu.roll(x, MLA_V, axis=1) for x in o]
        o_ref[0] = jnp.where(left, outs[0], outs[1]).astype(o_ref.dtype)

    if tq > n_ctx:
        @pl.when(i == 0)
        def _():
            attend(n_all, n_ctx)

        @pl.when(i > 0)
        def _():
            attend(n_all, 0)
    else:
        ctx_tiles = n_ctx // tq

        @pl.when(i < ctx_tiles)
        def _():
            attend(n_ctx, 0)

        @pl.when(i >= ctx_tiles)
        def _():
            attend(n_all, 0)


MLA_TQ = (544, 384, 256)


def _mla_call(q, k, v, B, S, C):
    tq = next(t for t in MLA_TQ if S % t == 0)
    assert C % tq == 0 or tq > C
    q3, k3, v3 = q.reshape(B, S, 1024), k.reshape(B, S, 1024), v.reshape(B, S, 512)
    out = pl.pallas_call(
        functools.partial(_mla_kernel, n_ctx=C, n_all=S),
        grid=(B, MLA_HEADS // 2, S // tq),
        in_specs=[pl.BlockSpec((1, tq, 2 * LANES), lambda b, j, i: (b, i, j)),
                  pl.BlockSpec((1, S, 2 * LANES), lambda b, j, i: (b, 0, j)),
                  pl.BlockSpec((1, S, LANES), lambda b, j, i: (b, 0, j))],
        out_specs=pl.BlockSpec((1, tq, LANES), lambda b, j, i: (b, i, j)),
        out_shape=jax.ShapeDtypeStruct((B, S, 512), MXU_DTYPE),
        compiler_params=_params("parallel", "parallel", "arbitrary"),
        name="mla_attn",
    )(q3, k3, v3)
    return out.reshape(B * S, 512)


def _gqa_kernel(sink_ref, q_ref, kv_ref, o_ref, *, n_ctx, n_all):
    i = pl.program_id(1)
    qb_rows = q_ref.shape[1]
    span = qb_rows + 2 * WINDOW
    ctx_blocks = n_ctx // qb_rows
    group = GQA_HEADS // GQA_KV_HEADS
    left = lax.broadcasted_iota(jnp.int32, (qb_rows, LANES), 1) < GQA_HEAD_DIM

    def run(latent):
        nk = n_ctx + span if latent else n_ctx
        rows = group * qb_rows
        if latent:
            qb = i - ctx_blocks
            ws = pl.multiple_of(jnp.minimum((qb + 1) * qb_rows, n_all - span), qb_rows)
            col = lax.broadcasted_iota(jnp.int32, (rows, nk), 1)
            q_pos = qb * qb_rows + lax.broadcasted_iota(jnp.int32, (rows, nk), 0) % qb_rows
            k_pos = ws - 2 * n_ctx + col
            ok = (col < n_ctx) | ((jnp.abs(k_pos - q_pos) <= WINDOW) & (k_pos >= 0))
        head_of_row = lax.broadcasted_iota(jnp.int32, (rows, 1), 0) // qb_rows
        res = []
        for j in range(GQA_KV_HEADS):
            parts = []
            for g in range(group):
                qp = q_ref[0, :, (2 * j + g // 2) * LANES:(2 * j + g // 2 + 1) * LANES]
                parts.append(jnp.where(left if g % 2 == 0 else ~left, qp, jnp.zeros_like(qp)))
            q4 = jnp.concatenate(parts, axis=0)
            kc = kv_ref[0, 0:n_ctx, j * LANES:(j + 1) * LANES]
            vc = kv_ref[0, 0:n_ctx, 256 + j * LANES:256 + (j + 1) * LANES]
            if latent:
                kc = jnp.concatenate([kc, kv_ref[0, pl.ds(ws, span), j * LANES:(j + 1) * LANES]], axis=0)
                vc = jnp.concatenate([vc, kv_ref[0, pl.ds(ws, span), 256 + j * LANES:256 + (j + 1) * LANES]], axis=0)
            sink = jnp.zeros((rows, 1), F32)
            for g in range(group):
                sink = jnp.where(head_of_row == g, sink_ref[group * j + g] * LOG2E, sink)
            s = _mm_nt(q4, kc)
            if latent:
                s = jnp.where(ok, s, -jnp.inf)
            m = jnp.maximum(jnp.max(s, -1, keepdims=True), sink)
            p = jnp.exp2((s - m).astype(MXU_DTYPE))
            v_left = lax.broadcasted_iota(jnp.int32, vc.shape, 1) < GQA_HEAD_DIM
            o = _mm(p, jnp.where(v_left, vc, jnp.ones_like(vc)))
            res.append(o / (pltpu.roll(o, GQA_HEAD_DIM, axis=1) + jnp.exp2(sink - m)))
        for j in range(GQA_KV_HEADS):
            for pp in range(group // 2):
                r0 = res[j][(2 * pp) * qb_rows:(2 * pp + 1) * qb_rows]
                r1 = res[j][(2 * pp + 1) * qb_rows:(2 * pp + 2) * qb_rows]
                pair = 2 * j + pp
                o_ref[0, :, pair * LANES:(pair + 1) * LANES] = jnp.where(
                    left, r0, pltpu.roll(r1, GQA_HEAD_DIM, axis=1)).astype(o_ref.dtype)

    @pl.when(i < ctx_blocks)
    def _():
        run(False)

    @pl.when(i >= ctx_blocks)
    def _():
        run(True)


def _gqa_call(gq, gkv, sink, B, S, C):
    qb_rows = 128
    out = pl.pallas_call(
        functools.partial(_gqa_kernel, n_ctx=C, n_all=S),
        grid=(B, S // qb_rows),
        in_specs=[pl.BlockSpec(memory_space=pltpu.SMEM),
                  pl.BlockSpec((1, qb_rows, 512), lambda b, i: (b, i, 0)),
                  pl.BlockSpec((1, S, 512), lambda b, i: (b, 0, 0))],
        out_specs=pl.BlockSpec((1, qb_rows, 512), lambda b, i: (b, i, 0)),
        out_shape=jax.ShapeDtypeStruct((B, S, 512), MXU_DTYPE),
        compiler_params=_params("parallel", "arbitrary"),
        name="gqa_attn",
    )(sink, gq.reshape(B, S, 512), gkv.reshape(B, S, 512))
    return out.reshape(B * S, 512)


DN_TR = 256
DN_HALO = 16
DN_CPT = DN_TR // DN_CHUNK


def _stack(x, left):
    z = jnp.zeros_like(x)
    return jnp.concatenate([jnp.where(left, x, z), jnp.where(left, z, x)], axis=1)


def _dnlocal_kernel(main_ref, prev_ref, next_ref, ab_ref, conv_ref, gp_ref, bd_ref, trif_ref, trib_ref,
                    eg_ref, eb_ref,
                    u_out, w_out, qg_out, kg_out, qk_out, gl_out, pad_ref, *, ctx_tiles, n_tiles):
    i = pl.program_id(1)
    tr = DN_TR
    first = (i == 0) | (i == ctx_tiles)
    last = (i == ctx_tiles - 1) | (i == n_tiles - 1)
    xp = prev_ref[0].astype(F32)
    xn = next_ref[0].astype(F32)
    pad_ref[0:DN_HALO, :] = jnp.where(first, jnp.zeros_like(xp), xp)
    pad_ref[DN_HALO:DN_HALO + tr, :] = main_ref[0].astype(F32)
    pad_ref[DN_HALO + tr:, :] = jnp.where(last, jnp.zeros_like(xn), xn)
    y = jnp.zeros((tr, 3 * DN_WIDTH), F32)
    for t in range(DN_CONV):
        y = y + conv_ref[t:t + 1, :] * pad_ref[pl.ds(DN_HALO - DN_CONV // 2 + t, tr), :]
    y = _silu(y)
    q, k, v = y[:, 0:512], y[:, 512:1024], y[:, 1024:1536]
    bd = bd_ref[...]
    q = q * lax.rsqrt(_exact_mm(q * q, bd) + 1e-6) * (DN_HEAD_DIM ** -0.5)
    k = k * lax.rsqrt(_exact_mm(k * k, bd) + 1e-6)

    ab = ab_ref[0]
    g = -jnp.exp(gp_ref[0:1, :]) * jax.nn.softplus(ab + gp_ref[1:2, :])
    beta = jax.nn.sigmoid(ab)
    lane = lax.broadcasted_iota(jnp.int32, (tr, LANES), 1)
    gc = jnp.where(lane < DN_HEADS, _exact_mm_left(trif_ref[...], g), _exact_mm_left(trib_ref[...], g))
    gcx_all = _exact_mm(gc, eg_ref[...])
    bx_all = _exact_mm(beta, eb_ref[...])

    c = DN_CPT
    lane3 = lax.broadcasted_iota(jnp.int32, (1, 1, LANES), 2)
    left = (lane3 % LANES) < DN_HEAD_DIM
    tpos = lane3 % DN_HEAD_DIM
    lane6 = lax.broadcasted_iota(jnp.int32, (1, 1, 2 * LANES), 2)
    left6 = (lane6 % LANES) < DN_HEAD_DIM
    ri = lax.broadcasted_iota(jnp.int32, (1, DN_CHUNK, LANES), 1)
    cj = lax.broadcasted_iota(jnp.int32, (1, DN_CHUNK, LANES), 2) % DN_HEAD_DIM
    one = jnp.ones((), F32)
    zero = jnp.zeros((), F32)

    units = [(d, j) for d in range(2) for j in range(4)]
    st = {}
    for d, j in units:
        off = d * 512 + j * LANES
        gcx = gcx_all[:, off:off + LANES].reshape(c, DN_CHUNK, LANES)
        bx = bx_all[:, off:off + LANES].reshape(c, DN_CHUNK, LANES)
        qp = q[:, j * LANES:(j + 1) * LANES].reshape(c, DN_CHUNK, LANES)
        kp = k[:, j * LANES:(j + 1) * LANES].reshape(c, DN_CHUNK, LANES)
        vp = v[:, j * LANES:(j + 1) * LANES].reshape(c, DN_CHUNK, LANES)
        gl = gcx[:, DN_CHUNK - 1:DN_CHUNK, :] if d == 0 else gcx[:, 0:1, :]
        kb = kp * bx
        kq = _bmm_nt(jnp.concatenate([kb, qp], axis=1), _stack(kp, left))
        hi, mid, lo = _split3(gcx)
        a6 = jnp.where(tpos == 0, hi, jnp.where(tpos == 1, mid, jnp.where(tpos == 2, lo,
                       jnp.where(tpos < 6, one, zero))))
        b6 = jnp.where(tpos < 3, one, jnp.where(tpos == 3, -hi, jnp.where(tpos == 4, -mid,
                       jnp.where(tpos == 5, -lo, zero))))
        diff = _bmm_nt(a6, _stack(b6, left))
        st[d, j] = dict(gcx=gcx, bx=bx, qp=qp, kp=kp, vp=vp, gl=gl, kb=kb, kq=kq, diff=diff)
    for d, j in units:
        u = st[d, j]
        incl = (ri >= cj) if d == 0 else (ri <= cj)
        strict = (ri > cj) if d == 0 else (ri < cj)
        dm = jnp.exp(jnp.where(incl, u["diff"], -jnp.inf))
        u["qkm"] = u["kq"][:, DN_CHUNK:, :] * dm
        u["x"] = -jnp.where(strict, u["kq"][:, 0:DN_CHUNK, :] * dm, zero)
        u["r"] = u["x"]
    for d, j in units:
        u = st[d, j]
        u["x"] = _bmm(u["x"], _stack(u["x"], left))
    for level in range(5):
        for d, j in units:
            u = st[d, j]
            xs = _stack(u["x"], left)
            if level < 4:
                m = _bmm(jnp.concatenate([u["r"], u["x"]], axis=1), xs)
                u["r"] = u["r"] + u["x"] + m[:, 0:DN_CHUNK, :]
                u["x"] = m[:, DN_CHUNK:, :]
            else:
                u["r"] = u["r"] + u["x"] + _bmm(u["r"], xs)
    for d, j in units:
        u = st[d, j]
        eg = jnp.exp(u["gcx"])
        rhs = jnp.concatenate([u["vp"] * u["bx"], u["kb"] * eg], axis=-1)
        sol = rhs + _bmm(u["r"], _stack(rhs, left6))
        sl = slice(j * LANES, (j + 1) * LANES)
        u_out[0, d, :, sl] = sol[:, :, 0:LANES].reshape(tr, LANES)
        w_out[0, d, :, sl] = sol[:, :, LANES:].reshape(tr, LANES).astype(w_out.dtype)
        qg_out[0, d, :, sl] = (u["qp"] * eg).reshape(tr, LANES).astype(qg_out.dtype)
        kg_out[0, d, :, sl] = (u["kp"] * jnp.exp(u["gl"] - u["gcx"])).reshape(tr, LANES).astype(kg_out.dtype)
        qk_out[0, d, :, sl] = u["qkm"].reshape(tr, LANES).astype(qk_out.dtype)
        gl_out[0, d, :, :, sl] = jnp.exp(u["gl"])


def _dn_constants():
    idx = np.arange(DN_TR)
    same = (idx[:, None] // DN_CHUNK) == (idx[None, :] // DN_CHUNK)
    trif = (same & (idx[None, :] <= idx[:, None])).astype(np.float32)
    trib = (same & (idx[None, :] >= idx[:, None])).astype(np.float32)
    h = np.arange(512)
    bd = ((h[:, None] // DN_HEAD_DIM) == (h[None, :] // DN_HEAD_DIM)).astype(np.float32)
    col = np.arange(LANES)[:, None]
    out = np.arange(1024)[None, :]
    unit = (out // 512) * DN_HEADS + (out % 512) // DN_HEAD_DIM
    eg = (col == unit).astype(np.float32)
    eb = (col == unit + 2 * DN_HEADS).astype(np.float32)
    return tuple(jnp.asarray(a, MXU_DTYPE) for a in (bd, trif, trib, eg, eb))


def _dnlocal_call(dqkv, ab, conv_w, gp, consts, B, S, C):
    tr = DN_TR
    n_tiles = S // tr
    hb = tr // DN_HALO
    n_hblk = S // DN_HALO
    bd, trif, trib, eg, eb = consts
    const2 = lambda b, i: (0, 0)
    big = lambda b, i: (b, 0, i, 0)
    act = MXU_DTYPE
    shp = (B, 2, S, 512)
    return pl.pallas_call(
        functools.partial(_dnlocal_kernel, ctx_tiles=C // tr, n_tiles=n_tiles),
        grid=(B, n_tiles),
        in_specs=[pl.BlockSpec((1, tr, 1536), lambda b, i: (b, i, 0)),
                  pl.BlockSpec((1, DN_HALO, 1536), lambda b, i: (b, jnp.maximum(i * hb - 1, 0), 0)),
                  pl.BlockSpec((1, DN_HALO, 1536), lambda b, i: (b, jnp.minimum((i + 1) * hb, n_hblk - 1), 0)),
                  pl.BlockSpec((1, tr, LANES), lambda b, i: (b, i, 0)),
                  pl.BlockSpec((8, 1536), const2),
                  pl.BlockSpec((8, LANES), const2),
                  pl.BlockSpec((512, 512), const2),
                  pl.BlockSpec((tr, tr), const2),
                  pl.BlockSpec((tr, tr), const2),
                  pl.BlockSpec((LANES, 1024), const2),
                  pl.BlockSpec((LANES, 1024), const2)],
        out_specs=[pl.BlockSpec((1, 2, tr, 512), big)] * 5
                  + [pl.BlockSpec((1, 2, DN_CPT, 1, 512), lambda b, i: (b, 0, i, 0, 0))],
        out_shape=[jax.ShapeDtypeStruct(shp, F32)] + [jax.ShapeDtypeStruct(shp, act)] * 4
                  + [jax.ShapeDtypeStruct((B, 2, S // DN_CHUNK, 1, 512), F32)],
        scratch_shapes=[pltpu.VMEM((tr + 2 * DN_HALO, 1536), F32)],
        compiler_params=_params("parallel", "parallel"),
        name="dn_local",
    )(dqkv.reshape(B, S, 1536), dqkv.reshape(B, S, 1536), dqkv.reshape(B, S, 1536), ab.reshape(B, S, LANES),
      conv_w, gp, bd, trif, trib, eg, eb)


def _dnscan_kernel(uf, wf, qgf, kgf, qkf, glf, ub, wb, qgb, kgb, qkb, glb, of_out, ob_out, s_ref):
    n = pl.program_id(1)
    lane = lax.broadcasted_iota(jnp.int32, (1, LANES), 1)
    left = lane < DN_HEAD_DIM
    row = lax.broadcasted_iota(jnp.int32, (LANES, LANES), 0)
    col = lax.broadcasted_iota(jnp.int32, (LANES, LANES), 1)
    same_head = (row < DN_HEAD_DIM) == (col < DN_HEAD_DIM)
    dirs = ((uf, wf, qgf, kgf, qkf, glf, of_out), (ub, wb, qgb, kgb, qkb, glb, ob_out))
    units = [(bb, d, j) for bb in range(uf.shape[0]) for d in range(2) for j in range(4)]
    sidx = lambda bb, d, j: (bb * 2 + d) * 4 + j
    sl = lambda j: slice(j * LANES, (j + 1) * LANES)
    started = n > 0
    st = {t: jnp.where(started, s_ref[sidx(*t)], jnp.zeros((LANES, LANES), F32)) for t in units}
    pre = {(bb, d, j): _mm(dirs[d][1][bb, 0, :, sl(j)], st[bb, d, j]) for bb, d, j in units}
    o1 = {(bb, d, j): _mm(dirs[d][2][bb, 0, :, sl(j)], st[bb, d, j]) for bb, d, j in units}
    vn = {(bb, d, j): dirs[d][0][bb, 0, :, sl(j)] - pre[bb, d, j] for bb, d, j in units}
    for bb, d, j in units:
        v = vn[bb, d, j]
        z = jnp.zeros_like(v)
        vst = jnp.concatenate([jnp.where(left, v, z), jnp.where(left, z, v)], axis=0)
        dirs[d][6][bb, :, sl(j)] = o1[bb, d, j] + _mm(dirs[d][4][bb, 0, :, sl(j)], vst)
    for bb, d, j in units:
        upd = _mm_tn(dirs[d][3][bb, 0, :, sl(j)], vn[bb, d, j])
        s_ref[sidx(bb, d, j)] = (st[bb, d, j] * dirs[d][5][bb, 0, 0, :, sl(j)]
                                 + jnp.where(same_head, upd, jnp.zeros_like(upd)))


DN_SCAN_BATCH = 4


def _dnscan_call(local, B, S, C):
    u, w, qg, kg, qk, gl = local
    nch, nc = S // DN_CHUNK, C // DN_CHUNK
    bb = DN_SCAN_BATCH if B % DN_SCAN_BATCH == 0 else 1

    def bidx(n):
        return jnp.where(n < nc, nc - 1 - n, nch - 1 + nc - n)

    fspec = pl.BlockSpec((bb, 1, DN_CHUNK, 512), lambda b, n: (b, 0, n, 0))
    bspec = pl.BlockSpec((bb, 1, DN_CHUNK, 512), lambda b, n: (b, 1, bidx(n), 0))
    fgl = pl.BlockSpec((bb, 1, 1, 1, 512), lambda b, n: (b, 0, n, 0, 0))
    bgl = pl.BlockSpec((bb, 1, 1, 1, 512), lambda b, n: (b, 1, bidx(n), 0, 0))
    return pl.pallas_call(
        _dnscan_kernel,
        grid=(B // bb, nch),
        in_specs=[fspec] * 5 + [fgl] + [bspec] * 5 + [bgl],
        out_specs=[pl.BlockSpec((bb, DN_CHUNK, 512), lambda b, n: (b, n, 0)),
                   pl.BlockSpec((bb, DN_CHUNK, 512), lambda b, n: (b, bidx(n), 0))],
        out_shape=[jax.ShapeDtypeStruct((B, S, 512), F32)] * 2,
        scratch_shapes=[pltpu.VMEM((bb * 8, LANES, LANES), F32)],
        compiler_params=_params("parallel", "arbitrary"),
        name="dn_scan",
    )(u, w, qg, kg, qk, gl, u, w, qg, kg, qk, gl)


def _pack_pairs(v):
    w = v.shape[1] // 2
    bits = lax.bitcast_convert_type(v.astype(jnp.bfloat16).astype(F32), jnp.int32)
    return lax.shift_right_logical(bits[:, :w], 16) | bits[:, w:]


def _unpack_pairs(p):
    lo = lax.bitcast_convert_type(lax.shift_left(p, 16), F32)
    hi = lax.bitcast_convert_type(p & jnp.int32(-65536), F32)
    return lo, hi


def _merge_kernel(x_ref, ga_ref, shf_ref, scf_ref, omla_ref, of_ref, ob_ref, dz_ref, ogqa_ref, gates_ref,
                  wo1_ref, wo2_ref, wo3_ref, wout_ref, dnn_ref, bd_ref, lng_ref, lnb_ref, wr_ref, rb_ref,
                  x1_out, ufp_out, eidx_out, rank_out, ew_out, cnt_out, cnt_ref):
    @pl.when(pl.program_id(0) == 0)
    def _():
        cnt_ref[...] = jnp.zeros_like(cnt_ref)

    o = of_ref[...] + ob_ref[...]
    ms = _exact_mm(o * o, bd_ref[...]) * (1.0 / DN_HEAD_DIM)
    dn = o * lax.rsqrt(ms + NORM_EPS) * dnn_ref[...] * _silu(dz_ref[...].astype(F32))
    g1 = gates_ref[:, 0:1024].astype(F32)
    g2 = gates_ref[:, 1024:2048].astype(F32)
    g3 = gates_ref[:, 2048:3072].astype(F32)
    m = (g1 * _mm(omla_ref[...], wo1_ref[...]) + g2 * _mm(dn, wo2_ref[...])
         + g3 * _mm(ogqa_ref[...], wo3_ref[...]))
    y = _mm(m, wout_ref[...])
    x1 = _layernorm(DEEPNORM_ALPHA * x_ref[...] + ga_ref[0] * y, lng_ref[...], lnb_ref[...])
    x1_out[...] = x1
    uf = x1 * (1.0 + scf_ref[0]) + shf_ref[0]
    _split_pieces(_pack_pairs(uf), ufp_out)

    tm = uf.shape[0]
    scores = jax.nn.sigmoid(_mm_nt(wr_ref[...], uf))[0:N_EXPERTS]
    sel = scores + rb_ref[0:N_EXPERTS, :]
    gsz = N_EXPERTS // N_GROUPS
    neg = jnp.full((), -jnp.inf, F32)
    sel3 = sel.reshape(N_GROUPS, gsz, tm)
    mem = lax.broadcasted_iota(jnp.int32, (N_GROUPS, gsz, tm), 1)
    m1 = jnp.max(sel3, 1, keepdims=True)
    i1 = jnp.min(jnp.where(sel3 == m1, mem, gsz), 1, keepdims=True)
    m2 = jnp.max(jnp.where(mem == i1, neg, sel3), 1, keepdims=True)
    gs = (m1 + m2).reshape(N_GROUPS, tm)
    gi = lax.broadcasted_iota(jnp.int32, (N_GROUPS, tm), 0)
    grank = jnp.zeros((N_GROUPS, tm), jnp.int32)
    for gp in range(N_GROUPS):
        other = gs[gp:gp + 1, :]
        beats = (other > gs) | ((other == gs) & (gp < gi))
        grank = grank + beats.astype(jnp.int32)
    gsel = (grank < TOPK_GROUPS).reshape(N_GROUPS, 1, tm)
    cur = jnp.where(gsel, sel3, neg).reshape(N_EXPERTS, tm)
    ei = lax.broadcasted_iota(jnp.int32, (N_EXPERTS, tm), 0)
    zero = jnp.zeros((N_EXPERTS, tm), F32)
    one = jnp.ones((N_EXPERTS, tm), F32)
    chosen = zero
    picks = []
    for _ in range(TOP_K):
        mx = jnp.max(cur, 0, keepdims=True)
        ix = jnp.min(jnp.where(cur == mx, ei, N_EXPERTS), 0, keepdims=True)
        pick = ei == ix
        picks.append((ix, pick))
        chosen = chosen + jnp.where(pick, one, zero)
        cur = jnp.where(pick, neg, cur)

    r_i = lax.broadcasted_iota(jnp.int32, (tm, tm), 0)
    c_i = lax.broadcasted_iota(jnp.int32, (tm, tm), 1)
    before = jnp.where(r_i < c_i, 1.0, 0.0)
    pos = cnt_ref[:, 0:1] + _mm(chosen, before)
    cnt_new = cnt_ref[...] + jnp.sum(chosen, 1, keepdims=True)
    cnt_ref[...] = cnt_new
    cnt_out[...] = cnt_new

    w_rows = [jnp.sum(jnp.where(pick, scores, zero), 0, keepdims=True) for _, pick in picks]
    wsum = w_rows[0]
    for w_k in w_rows[1:]:
        wsum = wsum + w_k
    eidx_out[...] = jnp.concatenate([ix for ix, _ in picks], axis=0)
    rank_out[...] = jnp.concatenate(
        [jnp.sum(jnp.where(pick, pos, zero), 0, keepdims=True) for _, pick in picks], axis=0).astype(jnp.int32)
    w8 = jnp.concatenate([w_k / wsum * ROUTED_SCALE for w_k in w_rows], axis=0)
    ew_out[...] = jnp.concatenate([w8, jnp.zeros((LANES - TOP_K, tm), F32)], axis=0).T


def _merge_call(xs, mod_l, omla, of, ob, dz, ogqa, gates, wo1, wo2, wo3, wout, dnn, bd, lng, lnb, wr, rb, B, S, C):
    T, D = xs.shape
    tpb, ctiles = S // TM, C // TM
    row = lambda i: (i, 0)
    const = lambda i: (0, 0)
    modspec = lambda k: pl.BlockSpec((1, 1, D), _mod_index(tpb, ctiles, k))
    return pl.pallas_call(
        _merge_kernel,
        grid=(T // TM,),
        in_specs=[pl.BlockSpec((TM, D), row), modspec(2), modspec(3), modspec(4),
                  pl.BlockSpec((TM, 512), row), pl.BlockSpec((TM, 512), row), pl.BlockSpec((TM, 512), row),
                  pl.BlockSpec((TM, 512), row), pl.BlockSpec((TM, 512), row), pl.BlockSpec((TM, 3072), row),
                  pl.BlockSpec((512, D), const), pl.BlockSpec((512, D), const), pl.BlockSpec((512, D), const),
                  pl.BlockSpec((D, D), const), pl.BlockSpec((1, 512), const), pl.BlockSpec((512, 512), const),
                  pl.BlockSpec((1, D), const), pl.BlockSpec((1, D), const),
                  pl.BlockSpec((LANES, D), const), pl.BlockSpec((LANES, 1), const)],
        out_specs=[pl.BlockSpec((TM, D), row), pl.BlockSpec((N_PIECES, TM, PIECE), lambda i: (0, i, 0)),
                   pl.BlockSpec((TOP_K, TM), lambda i: (0, i)), pl.BlockSpec((TOP_K, TM), lambda i: (0, i)),
                   pl.BlockSpec((TM, LANES), row), pl.BlockSpec((N_EXPERTS, LANES), const)],
        out_shape=[jax.ShapeDtypeStruct((T, D), F32), jax.ShapeDtypeStruct((N_PIECES, T, PIECE), jnp.int32),
                   jax.ShapeDtypeStruct((TOP_K, T), jnp.int32), jax.ShapeDtypeStruct((TOP_K, T), jnp.int32),
                   jax.ShapeDtypeStruct((T, LANES), F32), jax.ShapeDtypeStruct((N_EXPERTS, LANES), F32)],
        scratch_shapes=[pltpu.VMEM((N_EXPERTS, LANES), F32)],
        compiler_params=_params("arbitrary"),
        name="merge_norm_route",
    )(xs, mod_l, mod_l, mod_l, omla, of.reshape(T, 512), ob.reshape(T, 512), dz, ogqa, gates,
      wo1, wo2, wo3, wout, dnn, bd, lng, lnb, wr, rb)


EXPERT_BLOCK = 512
SC_WINDOW = 128
N_PIECES = 2
PIECE = D_MODEL // 2 // N_PIECES


def _split_pieces(packed, out_ref):
    for h in range(N_PIECES):
        out_ref[h] = packed[:, h * PIECE:(h + 1) * PIECE]


def _mm_pieces(pieces, w):
    acc = None
    for h, (lo, hi) in enumerate(pieces):
        t = (_mm(lo, w[h * PIECE:(h + 1) * PIECE, :])
             + _mm(hi, w[D_MODEL // 2 + h * PIECE:D_MODEL // 2 + (h + 1) * PIECE, :]))
        acc = t if acc is None else acc + t
    return acc


def _sc_mesh():
    return plsc.VectorSubcoreMesh(core_axis_name="c", subcore_axis_name="s")


def _sc_gather_rows(y, idx):
    n = idx.shape[1]
    W = y.shape[1]

    @pl.kernel(out_type=jax.ShapeDtypeStruct((n, W), y.dtype), mesh=_sc_mesh(), scratch_types=[])
    def gather(y_hbm, i_hbm, o_hbm):
        def body(i_vmem, o_vmem):
            pltpu.sync_copy(y_hbm.at[i_vmem.at[0]], o_vmem)

        pltpu.emit_pipeline(
            body,
            grid=(n // SC_WINDOW,),
            in_specs=[pl.BlockSpec((1, SC_WINDOW), lambda i: (0, i))],
            out_specs=[pl.BlockSpec((SC_WINDOW, W), lambda i: (i, 0))],
            core_axis_name=("c", "s"),
            dimension_semantics=(pltpu.PARALLEL,),
        )(i_hbm, o_hbm)

    return gather(y, idx)


SC_LANES = 16
SC_WORKERS = 32
SC_CHUNK = 2176


def _sc_invert_rows(dest, default, n_tok):
    n_rows = default.shape[0]
    per_w = n_rows // SC_WORKERS
    n_k = dest.shape[0] // n_tok
    assert n_rows % (SC_WORKERS * SC_LANES) == 0 and n_tok % SC_CHUNK == 0

    @pl.kernel(out_type=jax.ShapeDtypeStruct((n_rows,), jnp.int32), mesh=_sc_mesh(),
               scratch_types=[pltpu.VMEM((per_w,), jnp.int32), pltpu.VMEM((SC_CHUNK,), jnp.int32)],
               compiler_params=pltpu.CompilerParams(needs_layout_passes=False))
    def invert(dest_hbm, dflt_hbm, out_hbm, rows_v, dest_v):
        wid = lax.axis_index("s") * 2 + lax.axis_index("c")
        base = wid * per_w
        pltpu.sync_copy(dflt_hbm.at[pl.ds(base, per_w)], rows_v)
        lanes = lax.iota(jnp.int32, SC_LANES)
        for k in range(n_k):
            @pl.loop(0, n_tok // SC_CHUNK)
            def _(c):
                pltpu.sync_copy(dest_hbm.at[pl.ds(k * n_tok + c * SC_CHUNK, SC_CHUNK)], dest_v)

                @pl.loop(0, SC_CHUNK, step=SC_LANES)
                def _(o):
                    local = dest_v[pl.ds(o, SC_LANES)] - base
                    mine = (local >= 0) & (local < per_w)
                    plsc.store_scatter(rows_v, [jnp.where(mine, local, 0)], c * SC_CHUNK + o + lanes, mask=mine)

        pltpu.sync_copy(rows_v, out_hbm.at[pl.ds(base, per_w)])

    return invert(dest, default)


def _experts_kernel(be_ref, nv_ref, xb_ref, wg_ref, wu_ref, wd_ref, y_out):
    b = pl.program_id(0)
    nv = nv_ref[b]

    @pl.when(nv > 0)
    def _():
        rows = lax.broadcasted_iota(jnp.int32, xb_ref.shape[1:], 0)
        pieces = []
        for h in range(N_PIECES):
            xh = xb_ref[h]
            pieces.append(_unpack_pairs(jnp.where(rows < nv, xh, jnp.zeros_like(xh))))
        hid = _silu(_mm_pieces(pieces, wg_ref.at[0, 0])) * _mm_pieces(pieces, wu_ref.at[0, 0])
        _split_pieces(_pack_pairs(_mm(hid, wd_ref[0, 0])), y_out)


def _experts_call(xb, block_e, nvalid, wg, wu, wd, layer):
    _, R, _ = xb.shape
    D = D_MODEL
    blk = (N_PIECES, EXPERT_BLOCK, PIECE)
    grid_spec = pltpu.PrefetchScalarGridSpec(
        num_scalar_prefetch=2,
        grid=(R // EXPERT_BLOCK,),
        in_specs=[pl.BlockSpec(blk, lambda b, be, nv: (0, b, 0)),
                  pl.BlockSpec((1, 1, D, EXPERT_DIM), lambda b, be, nv: (layer, be[b], 0, 0)),
                  pl.BlockSpec((1, 1, D, EXPERT_DIM), lambda b, be, nv: (layer, be[b], 0, 0)),
                  pl.BlockSpec((1, 1, EXPERT_DIM, D), lambda b, be, nv: (layer, be[b], 0, 0))],
        out_specs=pl.BlockSpec(blk, lambda b, be, nv: (0, b, 0)),
    )
    return pl.pallas_call(
        _experts_kernel,
        grid_spec=grid_spec,
        out_shape=jax.ShapeDtypeStruct((N_PIECES, R, PIECE), jnp.int32),
        compiler_params=_params("arbitrary"),
        name="moe_experts",
    )(block_e, nvalid, xb, wg, wu, wd)


def _combine_kernel(x_ref, ufp_ref, yg_ref, ew_ref, gf_ref, sg_ref, su_ref, sd_ref, g_ref, b_ref, o_ref):
    pieces = [_unpack_pairs(ufp_ref[h]) for h in range(N_PIECES)]
    hs = _silu(_mm_pieces(pieces, sg_ref)) * _mm_pieces(pieces, su_ref)
    f = _mm(hs, sd_ref[...])
    ew = ew_ref[...]
    lane = lax.broadcasted_iota(jnp.int32, ew.shape, 1)
    acc = [[jnp.zeros((x_ref.shape[0], PIECE), F32) for _ in range(N_PIECES)] for _ in range(2)]
    for k in range(TOP_K):
        wk = jnp.sum(jnp.where(lane == k, ew, jnp.zeros_like(ew)), axis=1, keepdims=True)
        for h in range(N_PIECES):
            ylo, yhi = _unpack_pairs(yg_ref[h, k])
            acc[0][h] = acc[0][h] + wk * ylo
            acc[1][h] = acc[1][h] + wk * yhi
    f = f + jnp.concatenate(acc[0] + acc[1], axis=1)
    o_ref[...] = _layernorm(DEEPNORM_ALPHA * x_ref[...] + gf_ref[0] * f, g_ref[...], b_ref[...])


def _combine_call(x1, ufp, yg, ew, mod_l, sg, su, sd, g, b, B, S, C):
    T, D = x1.shape
    tpb, ctiles = S // TM, C // TM
    row = lambda i: (i, 0)
    const = lambda i: (0, 0)
    return pl.pallas_call(
        _combine_kernel,
        grid=(T // TM,),
        in_specs=[pl.BlockSpec((TM, D), row), pl.BlockSpec((N_PIECES, TM, PIECE), lambda i: (0, i, 0)),
                  pl.BlockSpec((N_PIECES, TOP_K, TM, PIECE), lambda i: (0, 0, i, 0)),
                  pl.BlockSpec((TM, LANES), row),
                  pl.BlockSpec((1, 1, D), _mod_index(tpb, ctiles, 5)),
                  pl.BlockSpec((D, SHARED_DIM), const), pl.BlockSpec((D, SHARED_DIM), const),
                  pl.BlockSpec((SHARED_DIM, D), const),
                  pl.BlockSpec((1, D), const), pl.BlockSpec((1, D), const)],
        out_specs=pl.BlockSpec((TM, D), row),
        out_shape=jax.ShapeDtypeStruct((T, D), F32),
        compiler_params=_params("parallel"),
        name="moe_combine_norm",
    )(x1, ufp, yg, ew, mod_l, sg, su, sd, g, b)


def _moe_routed(ufp, eidx_t, rank_t, counts, wg, wu, wd, layer):
    T = ufp.shape[1]
    n_blocks = -(-(T * TOP_K + N_EXPERTS * (EXPERT_BLOCK - 1)) // EXPERT_BLOCK)
    n_rows = n_blocks * EXPERT_BLOCK
    padded = (counts + EXPERT_BLOCK - 1) // EXPERT_BLOCK * EXPERT_BLOCK
    pad_end = jnp.cumsum(padded)
    start_pad = pad_end - padded
    experts = jnp.arange(N_EXPERTS, dtype=jnp.int32)

    def lookup(table, idx):
        sel = idx[None] == experts.reshape((N_EXPERTS,) + (1,) * idx.ndim)
        return jnp.sum(jnp.where(sel, table.reshape((N_EXPERTS,) + (1,) * idx.ndim), 0), axis=0)

    dest_t = lookup(start_pad, eidx_t) + rank_t
    blk = jnp.arange(n_blocks, dtype=jnp.int32) * EXPERT_BLOCK
    block_e = jnp.minimum(jnp.sum((blk[:, None] >= pad_end[None, :]).astype(jnp.int32), axis=1), N_EXPERTS - 1)
    nvalid = jnp.clip(lookup(counts, block_e) - (blk - lookup(start_pad, block_e)), 0, EXPERT_BLOCK)
    row_tok = _sc_invert_rows(dest_t.reshape(-1), jnp.arange(n_rows, dtype=jnp.int32) % T, T)
    piece = jnp.arange(N_PIECES, dtype=jnp.int32)
    src = (piece[:, None] * T + row_tok[None, :]).reshape(1, N_PIECES * n_rows)
    xb = _sc_gather_rows(ufp.reshape(N_PIECES * T, PIECE), src).reshape(N_PIECES, n_rows, PIECE)
    yb = _experts_call(xb, block_e.astype(jnp.int32), nvalid.astype(jnp.int32), wg, wu, wd, layer)
    back = (piece[:, None, None] * n_rows + dest_t[None]).reshape(1, N_PIECES * TOP_K * T)
    yg = _sc_gather_rows(yb.reshape(N_PIECES * n_rows, PIECE), back)
    return yg.reshape(N_PIECES, TOP_K, T, PIECE)


def _rot_cols(w, half):
    return jnp.concatenate([-w[:, half:], w[:, :half]], axis=1)


def _prep_w_in(w):
    d = w.shape[0]
    offs = np.cumsum((0,) + IN_SIZES)
    cq, ckv, kr, dqkv, da, db, dz, gq, gk, gv, gates = (w[:, offs[t]:offs[t + 1]] for t in range(len(IN_SIZES)))
    z = lambda n: jnp.zeros((d, n), w.dtype)
    krg = jnp.concatenate([z(64), kr, z(32)], 1)
    krr = jnp.concatenate([z(64), _rot_cols(kr, MLA_ROPE // 2), z(32)], 1)
    ab = jnp.concatenate([da, db, z(LANES - 4 * DN_HEADS)], 1)
    hd = GQA_HEAD_DIM
    gq_rot = jnp.concatenate([_rot_cols(gq[:, h * hd:(h + 1) * hd], hd // 2) for h in range(GQA_HEADS)], 1)
    dup = lambda t: jnp.concatenate([t[:, 0:hd], t[:, 0:hd], t[:, hd:2 * hd], t[:, hd:2 * hd]], 1)
    gk_rot = jnp.concatenate([_rot_cols(gk[:, h * hd:(h + 1) * hd], hd // 2) for h in range(GQA_KV_HEADS)], 1)
    cat = jnp.concatenate([cq, ckv, krg, krr, dqkv, ab, dz, gq, gq_rot, dup(gk), dup(gk_rot), dup(gv), gates], 1)
    assert cat.shape[1] == NZ
    return cat.astype(MXU_DTYPE)


def _prep_w_uq(w):
    d = w.shape[0]
    hw = MLA_NOPE + MLA_ROPE
    a, b = [], []
    for h in range(MLA_HEADS):
        wh = w[:, h * hw:(h + 1) * hw]
        a += [wh, jnp.zeros((d, LANES - hw), w.dtype)]
        b += [jnp.zeros((d, MLA_NOPE), w.dtype), _rot_cols(wh[:, MLA_NOPE:], MLA_ROPE // 2),
              jnp.zeros((d, LANES - hw), w.dtype)]
    return jnp.concatenate(a + b, 1).astype(MXU_DTYPE)


def _prep_w_ukv(w):
    d = w.shape[0]
    hw = MLA_NOPE + MLA_V
    kpart, vpart = [], []
    for h in range(MLA_HEADS):
        wh = w[:, h * hw:(h + 1) * hw]
        kpart += [wh[:, :MLA_NOPE], jnp.zeros((d, LANES - MLA_NOPE), w.dtype)]
        vpart += [wh[:, MLA_NOPE:]]
    return jnp.concatenate(kpart + vpart, 1).astype(MXU_DTYPE)


def _rope_tables(n_rows, C):
    row = jnp.repeat(jnp.arange(n_rows, dtype=F32), GRID_W)
    col = jnp.tile(jnp.arange(GRID_W, dtype=F32), n_rows)

    def angles(dim):
        n = dim // 4
        inv = ROPE_BASE ** (-jnp.arange(n, dtype=F32) / n)
        return jnp.concatenate([row[:, None] * inv, col[:, None] * inv], axis=-1)

    def with_ctx(cos, sin):
        return (jnp.concatenate([jnp.ones((C, LANES), F32), cos], 0),
                jnp.concatenate([jnp.zeros((C, LANES), F32), sin], 0))

    L = n_rows * GRID_W
    am = angles(MLA_ROPE)
    one, zero = jnp.ones((L, MLA_NOPE), F32), jnp.zeros((L, MLA_NOPE), F32)
    cm = jnp.concatenate([one, jnp.cos(am), jnp.cos(am), one[:, :32]], 1)
    sm = jnp.concatenate([zero, jnp.sin(am), jnp.sin(am), zero[:, :32]], 1)
    ag = angles(GQA_HEAD_DIM)
    cg = jnp.tile(jnp.cos(ag), (1, 4))
    sg = jnp.tile(jnp.sin(ag), (1, 4))
    return with_ctx(cm, sm) + with_ctx(cg, sg)


def kernel(x, c, ctx, c_ctx, w_ada, b_ada, w_in, mla_q_norm, mla_kv_norm, w_uq, w_ukv, dn_conv, dn_a_log, dn_dt_bias, dn_norm, gqa_sink, w_o_mla, w_o_dn, w_o_gqa, w_out, ln1_g, ln1_b, w_router, router_bias, w_exp_gate, w_exp_up, w_exp_down, w_sh_gate, w_sh_up, w_sh_down, ln2_g, ln2_b):
    B, L, D = x.shape
    C = ctx.shape[1]
    S = C + L
    nl = w_in.shape[0]
    assert D == D_MODEL and nl == DEPTH and B <= CTX_MOD_ROW
    assert C % TM == 0 and L % TM == 0 and L % GRID_W == 0 and L >= 3 * WINDOW
    cast = lambda t: t.astype(MXU_DTYPE)

    cc = jnp.zeros((MOD_ROWS, D), F32).at[0:B].set(c).at[CTX_MOD_ROW].set(c_ctx)
    mods = _ada_call(cc, w_ada, b_ada)
    tabs = _rope_tables(L // GRID_W, C)
    dn_consts = _dn_constants()
    xs = jnp.concatenate([ctx, x], axis=1).reshape(B * S, D)

    for l in range(nl):
        mod_l = mods[l].reshape(MOD_ROWS * 6, 1, D)
        q, k, v, dqkv, ab, dz, gq, gkv, gates = _inproj_call(
            xs, mod_l, _prep_w_in(w_in[l]), _prep_w_uq(w_uq[l]), _prep_w_ukv(w_ukv[l]),
            mla_q_norm[l].reshape(1, -1), mla_kv_norm[l].reshape(1, -1), tabs, B, S, C)
        omla = _mla_call(q, k, v, B, S, C)
        conv8 = jnp.zeros((8, 3 * DN_WIDTH), F32).at[0:DN_CONV].set(dn_conv[l])
        gp = (jnp.zeros((8, LANES), F32).at[0, 0:2 * DN_HEADS].set(dn_a_log[l].reshape(-1))
              .at[1, 0:2 * DN_HEADS].set(dn_dt_bias[l].reshape(-1)))
        local = _dnlocal_call(dqkv, ab, conv8, gp, dn_consts, B, S, C)
        of, ob = _dnscan_call(local, B, S, C)
        ogqa = _gqa_call(gq, gkv, gqa_sink[l], B, S, C)
        wr = jnp.zeros((LANES, D), F32).at[0:N_EXPERTS].set(w_router[l].T)
        rb = jnp.zeros((LANES, 1), F32).at[0:N_EXPERTS, 0].set(router_bias[l])
        x1, ufp, eidx_t, rank_t, ew, cnt = _merge_call(
            xs, mod_l, omla, of, ob, dz, ogqa, gates,
            cast(w_o_mla[l]), cast(w_o_dn[l]), cast(w_o_gqa[l]), cast(w_out[l]),
            jnp.tile(dn_norm[l], DN_HEADS).reshape(1, DN_WIDTH), dn_consts[0],
            ln1_g[l].reshape(1, D), ln1_b[l].reshape(1, D), cast(wr), rb, B, S, C)
        yg = _moe_routed(ufp, eidx_t, rank_t, cnt[:, 0].astype(jnp.int32),
                         w_exp_gate, w_exp_up, w_exp_down, l)
        xs = _combine_call(x1, ufp, yg, ew, mod_l, cast(w_sh_gate[l]), cast(w_sh_up[l]), cast(w_sh_down[l]),
                           ln2_g[l].reshape(1, D), ln2_b[l].reshape(1, D), B, S, C)
    return xs.reshape(B, S, D)[:, C:, :]
```

```python
import functools

import numpy as np
import jax
import jax.numpy as jnp
from jax import lax
from jax.experimental import pallas as pl
from jax.experimental.pallas import tpu as pltpu
from jax.experimental.pallas import tpu_sc as plsc

F32 = jnp.float32
MXU_DTYPE = jnp.bfloat16

D_MODEL = 1024
DEPTH = 4
GRID_W = 64
NORM_EPS = 1e-6
ROPE_BASE = 10000.0
DEEPNORM_ALPHA = (2.0 * DEPTH) ** 0.25

MLA_HEADS = 8
MLA_Q_LORA = 256
MLA_KV_LORA = 128
MLA_NOPE = 64
MLA_ROPE = 32
MLA_V = 64
MLA_SCALE = (MLA_NOPE + MLA_ROPE) ** -0.5
LOG2E = float(np.log2(np.e))

DN_HEADS = 8
DN_HEAD_DIM = 64
DN_WIDTH = DN_HEADS * DN_HEAD_DIM
DN_CONV = 5
DN_CHUNK = 64

GQA_HEADS = 8
GQA_KV_HEADS = 2
GQA_HEAD_DIM = 64
GQA_SCALE = GQA_HEAD_DIM ** -0.5
WINDOW = 128

N_EXPERTS = 64
TOP_K = 8
N_GROUPS = 8
TOPK_GROUPS = 4
EXPERT_DIM = 256
SHARED_DIM = 256
ROUTED_SCALE = 2.5

IN_SIZES = (MLA_Q_LORA, MLA_KV_LORA, MLA_ROPE,
            3 * DN_WIDTH, 2 * DN_HEADS, 2 * DN_HEADS, DN_WIDTH,
            GQA_HEADS * GQA_HEAD_DIM, GQA_KV_HEADS * GQA_HEAD_DIM, GQA_KV_HEADS * GQA_HEAD_DIM,
            3 * D_MODEL)

LANES = 128
TM = 256
MOD_ROWS = 16
CTX_MOD_ROW = 8

OFF_A = 0
OFF_DQKV = 640
OFF_AB = OFF_DQKV + 3 * DN_WIDTH
OFF_DZ = OFF_AB + LANES
OFF_GQ = OFF_DZ + DN_WIDTH
OFF_GK = OFF_GQ + 1024
OFF_GATES = OFF_GK + 768
NZ = OFF_GATES + 3 * D_MODEL

VMEM_LIMIT = 56 * 1024 * 1024


def _mm(a, b):
    return jnp.dot(a.astype(MXU_DTYPE), b.astype(MXU_DTYPE), preferred_element_type=F32)


def _mm_nt(a, b):
    return lax.dot_general(a.astype(MXU_DTYPE), b.astype(MXU_DTYPE), (((1,), (1,)), ((), ())),
                           preferred_element_type=F32)


def _mm_tn(a, b):
    return lax.dot_general(a.astype(MXU_DTYPE), b.astype(MXU_DTYPE), (((0,), (0,)), ((), ())),
                           preferred_element_type=F32)


def _bmm(a, b):
    return jnp.einsum('cik,ckj->cij', a.astype(MXU_DTYPE), b.astype(MXU_DTYPE), preferred_element_type=F32)


def _bmm_nt(a, b):
    return jnp.einsum('cik,cjk->cij', a.astype(MXU_DTYPE), b.astype(MXU_DTYPE), preferred_element_type=F32)


def _split3(x):
    hi = x.astype(jnp.bfloat16).astype(F32)
    r = x - hi
    mid = r.astype(jnp.bfloat16).astype(F32)
    lo = (r - mid).astype(jnp.bfloat16).astype(F32)
    return hi, mid, lo


def _exact_mm(x, m01):
    hi, mid, lo = _split3(x)
    return _mm(hi, m01) + _mm(mid, m01) + _mm(lo, m01)


def _exact_mm_left(m01, x):
    hi, mid, lo = _split3(x)
    return _mm(m01, hi) + _mm(m01, mid) + _mm(m01, lo)


def _silu(x):
    return x * jax.nn.sigmoid(x)


def _layernorm(v, g, b):
    mu = jnp.mean(v, -1, keepdims=True)
    d = v - mu
    var = jnp.mean(d * d, -1, keepdims=True)
    return d * lax.rsqrt(var + NORM_EPS) * g + b


def _mod_index(tiles_per_b, ctx_tiles, k):
    def index(i):
        row = jnp.where((i % tiles_per_b) < ctx_tiles, CTX_MOD_ROW, i // tiles_per_b)
        return (row * 6 + k, 0, 0)
    return index


def _params(*sem):
    return pltpu.CompilerParams(dimension_semantics=sem, vmem_limit_bytes=VMEM_LIMIT)


def _ada_kernel(c_ref, w_ref, b_ref, o_ref):
    o_ref[0] = _mm(_silu(c_ref[...]), w_ref[0]) + b_ref[0]


def _ada_call(cc, w_ada, b_ada):
    nl, d, n6 = w_ada.shape
    tn = 1536
    return pl.pallas_call(
        _ada_kernel,
        grid=(nl, n6 // tn),
        in_specs=[pl.BlockSpec((MOD_ROWS, d), lambda l, j: (0, 0)),
                  pl.BlockSpec((1, d, tn), lambda l, j: (l, 0, j)),
                  pl.BlockSpec((1, 1, tn), lambda l, j: (l, 0, j))],
        out_specs=pl.BlockSpec((1, MOD_ROWS, tn), lambda l, j: (l, 0, j)),
        out_shape=jax.ShapeDtypeStruct((nl, MOD_ROWS, n6), F32),
        compiler_params=_params("parallel", "parallel"),
        name="ada_mod",
    )(cc, w_ada, b_ada.reshape(nl, 1, n6))


def _inproj_kernel(x_ref, sh_ref, sc_ref, w_ref, wuq_ref, wukv_ref, qn_ref, kvn_ref,
                   cm_ref, sm_ref, cg_ref, sg_ref,
                   q_out, k_out, v_out, dqkv_out, ab_out, dz_out, gq_out, gkv_out, gates_out):
    u = (x_ref[...] * (1.0 + sc_ref[0]) + sh_ref[0]).astype(MXU_DTYPE)

    def z(off, width):
        return jnp.dot(u, w_ref[:, off:off + width], preferred_element_type=F32)

    def rms(v, g):
        return v * lax.rsqrt(jnp.mean(v * v, -1, keepdims=True) + NORM_EPS) * g

    cm, sm, cg, sg = cm_ref[...], sm_ref[...], cg_ref[...], sg_ref[...]

    za = z(OFF_A, 640)
    qq = _mm(rms(za[:, 0:256], qn_ref[...]), wuq_ref[...])
    kvv = _mm(rms(za[:, 256:384], kvn_ref[...]), wukv_ref[...])
    k_rope = za[:, 384:512] * cm + za[:, 512:640] * sm
    for h in range(MLA_HEADS):
        sl = slice(h * LANES, (h + 1) * LANES)
        qa = qq[:, h * LANES:(h + 1) * LANES]
        qb = qq[:, 1024 + h * LANES:1024 + (h + 1) * LANES]
        q_out[:, sl] = ((qa * cm + qb * sm) * (MLA_SCALE * LOG2E)).astype(q_out.dtype)
        k_out[:, sl] = (kvv[:, sl] + k_rope).astype(k_out.dtype)
    v_out[...] = kvv[:, 1024:1536].astype(v_out.dtype)

    for t in range(3):
        dqkv_out[:, t * 512:(t + 1) * 512] = z(OFF_DQKV + t * 512, 512).astype(dqkv_out.dtype)
    ab_out[...] = z(OFF_AB, LANES)
    dz_out[...] = z(OFF_DZ, DN_WIDTH).astype(dz_out.dtype)

    zq = z(OFF_GQ, 1024)
    for p in range(4):
        sl = slice(p * LANES, (p + 1) * LANES)
        gq_out[:, sl] = ((zq[:, sl] * cg + zq[:, 512 + p * LANES:512 + (p + 1) * LANES] * sg)
                         * (GQA_SCALE * LOG2E)).astype(gq_out.dtype)
    zk = z(OFF_GK, 768)
    for j in range(2):
        sl = slice(j * LANES, (j + 1) * LANES)
        gkv_out[:, sl] = (zk[:, sl] * cg + zk[:, 256 + j * LANES:256 + (j + 1) * LANES] * sg).astype(gkv_out.dtype)
    gkv_out[:, 256:512] = zk[:, 512:768].astype(gkv_out.dtype)

    for t in range(3):
        gates_out[:, t * 1024:(t + 1) * 1024] = jax.nn.sigmoid(z(OFF_GATES + t * 1024, 1024)).astype(gates_out.dtype)


def _inproj_call(xs, mod_l, w_cat, wuq_cat, wukv_cat, qn, kvn, tabs, B, S, C):
    T, D = xs.shape
    tpb, ctiles = S // TM, C // TM
    act = MXU_DTYPE
    row = lambda i: (i, 0)
    const = lambda i: (0, 0)
    tab = lambda i: (i % tpb, 0)
    widths = (1024, 1024, 512, 1536, LANES, 512, 512, 512, 3072)
    dtypes = (act, act, act, act, F32, act, act, act, act)
    return pl.pallas_call(
        _inproj_kernel,
        grid=(T // TM,),
        in_specs=[pl.BlockSpec((TM, D), row),
                  pl.BlockSpec((1, 1, D), _mod_index(tpb, ctiles, 0)),
                  pl.BlockSpec((1, 1, D), _mod_index(tpb, ctiles, 1)),
                  pl.BlockSpec((D, NZ), const),
                  pl.BlockSpec((MLA_Q_LORA, 2048), const),
                  pl.BlockSpec((MLA_KV_LORA, 1536), const),
                  pl.BlockSpec((1, MLA_Q_LORA), const),
                  pl.BlockSpec((1, MLA_KV_LORA), const)]
                 + [pl.BlockSpec((TM, LANES), tab)] * 4,
        out_specs=[pl.BlockSpec((TM, w), row) for w in widths],
        out_shape=[jax.ShapeDtypeStruct((T, w), dt) for w, dt in zip(widths, dtypes)],
        compiler_params=_params("parallel"),
        name="in_proj",
    )(xs, mod_l, mod_l, w_cat, wuq_cat, wukv_cat, qn, kvn, *tabs)


def _mla_kernel(q_ref, k_ref, v_ref, o_ref, *, n_ctx, n_all):
    i = pl.program_id(2)
    tq = q_ref.shape[1]
    left = lax.broadcasted_iota(jnp.int32, (tq, LANES), 1) < MLA_V

    def attend(nk, ctx_rows):
        v = v_ref[0, 0:nk, :]
        v_left = lax.broadcasted_iota(jnp.int32, v.shape, 1) < MLA_V
        ones = jnp.ones_like(v)
        s = [_mm_nt(q_ref[0, :, hh * LANES:(hh + 1) * LANES], k_ref[0, 0:nk, hh * LANES:(hh + 1) * LANES])
             for hh in range(2)]
        if ctx_rows:
            row = lax.broadcasted_iota(jnp.int32, (tq, nk), 0)
            col = lax.broadcasted_iota(jnp.int32, (tq, nk), 1)
            ok = (row >= ctx_rows) | (col < ctx_rows)
            s = [jnp.where(ok, x, -jnp.inf) for x in s]
        p = [jnp.exp2((x - jnp.max(x, -1, keepdims=True)).astype(MXU_DTYPE)) for x in s]
        o = [_mm(p[0], jnp.where(v_left, v, ones)), _mm(p[1], jnp.where(v_left, ones, v))]
        outs = [x / pltpu.roll(x, MLA_V, axis=1) for x in o]
        o_ref[0] = jnp.where(left, outs[0], outs[1]).astype(o_ref.dtype)

    if tq > n_ctx:
        @pl.when(i == 0)
        def _():
            attend(n_all, n_ctx)

        @pl.when(i > 0)
        def _():
            attend(n_all, 0)
    else:
        ctx_tiles = n_ctx // tq

        @pl.when(i < ctx_tiles)
        def _():
            attend(n_ctx, 0)

        @pl.when(i >= ctx_tiles)
        def _():
            attend(n_all, 0)


MLA_TQ = (544, 384, 256)


def _mla_call(q, k, v, B, S, C):
    tq = next(t for t in MLA_TQ if S % t == 0)
    assert C % tq == 0 or tq > C
    q3, k3, v3 = q.reshape(B, S, 1024), k.reshape(B, S, 1024), v.reshape(B, S, 512)
    out = pl.pallas_call(
        functools.partial(_mla_kernel, n_ctx=C, n_all=S),
        grid=(B, MLA_HEADS // 2, S // tq),
        in_specs=[pl.BlockSpec((1, tq, 2 * LANES), lambda b, j, i: (b, i, j)),
                  pl.BlockSpec((1, S, 2 * LANES), lambda b, j, i: (b, 0, j)),
                  pl.BlockSpec((1, S, LANES), lambda b, j, i: (b, 0, j))],
        out_specs=pl.BlockSpec((1, tq, LANES), lambda b, j, i: (b, i, j)),
        out_shape=jax.ShapeDtypeStruct((B, S, 512), MXU_DTYPE),
        compiler_params=_params("parallel", "parallel", "arbitrary"),
        name="mla_attn",
    )(q3, k3, v3)
    return out.reshape(B * S, 512)


def _gqa_kernel(sink_ref, q_ref, kv_ref, o_ref, *, n_ctx, n_all):
    i = pl.program_id(1)
    qb_rows = q_ref.shape[1]
    span = qb_rows + 2 * WINDOW
    ctx_blocks = n_ctx // qb_rows
    group = GQA_HEADS // GQA_KV_HEADS
    left = lax.broadcasted_iota(jnp.int32, (qb_rows, LANES), 1) < GQA_HEAD_DIM

    def run(latent):
        nk = n_ctx + span if latent else n_ctx
        rows = group * qb_rows
        if latent:
            qb = i - ctx_blocks
            ws = pl.multiple_of(jnp.minimum((qb + 1) * qb_rows, n_all - span), qb_rows)
            col = lax.broadcasted_iota(jnp.int32, (rows, nk), 1)
            q_pos = qb * qb_rows + lax.broadcasted_iota(jnp.int32, (rows, nk), 0) % qb_rows
            k_pos = ws - 2 * n_ctx + col
            ok = (col < n_ctx) | ((jnp.abs(k_pos - q_pos) <= WINDOW) & (k_pos >= 0))
        head_of_row = lax.broadcasted_iota(jnp.int32, (rows, 1), 0) // qb_rows
        res = []
        for j in range(GQA_KV_HEADS):
            parts = []
            for g in range(group):
                qp = q_ref[0, :, (2 * j + g // 2) * LANES:(2 * j + g // 2 + 1) * LANES]
                parts.append(jnp.where(left if g % 2 == 0 else ~left, qp, jnp.zeros_like(qp)))
            q4 = jnp.concatenate(parts, axis=0)
            kc = kv_ref[0, 0:n_ctx, j * LANES:(j + 1) * LANES]
            vc = kv_ref[0, 0:n_ctx, 256 + j * LANES:256 + (j + 1) * LANES]
            if latent:
                kc = jnp.concatenate([kc, kv_ref[0, pl.ds(ws, span), j * LANES:(j + 1) * LANES]], axis=0)
                vc = jnp.concatenate([vc, kv_ref[0, pl.ds(ws, span), 256 + j * LANES:256 + (j + 1) * LANES]], axis=0)
            sink = jnp.zeros((rows, 1), F32)
            for g in range(group):
                sink = jnp.where(head_of_row == g, sink_ref[group * j + g] * LOG2E, sink)
            s = _mm_nt(q4, kc)
            if latent:
                s = jnp.where(ok, s, -jnp.inf)
            m = jnp.maximum(jnp.max(s, -1, keepdims=True), sink)
            p = jnp.exp2((s - m).astype(MXU_DTYPE))
            v_left = lax.broadcasted_iota(jnp.int32, vc.shape, 1) < GQA_HEAD_DIM
            o = _mm(p, jnp.where(v_left, vc, jnp.ones_like(vc)))
            res.append(o / (pltpu.roll(o, GQA_HEAD_DIM, axis=1) + jnp.exp2(sink - m)))
        for j in range(GQA_KV_HEADS):
            for pp in range(group // 2):
                r0 = res[j][(2 * pp) * qb_rows:(2 * pp + 1) * qb_rows]
                r1 = res[j][(2 * pp + 1) * qb_rows:(2 * pp + 2) * qb_rows]
                pair = 2 * j + pp
                o_ref[0, :, pair * LANES:(pair + 1) * LANES] = jnp.where(
                    left, r0, pltpu.roll(r1, GQA_HEAD_DIM, axis=1)).astype(o_ref.dtype)

    @pl.when(i < ctx_blocks)
    def _():
        run(False)

    @pl.when(i >= ctx_blocks)
    def _():
        run(True)


def _gqa_call(gq, gkv, sink, B, S, C):
    qb_rows = 128
    out = pl.pallas_call(
        functools.partial(_gqa_kernel, n_ctx=C, n_all=S),
        grid=(B, S // qb_rows),
        in_specs=[pl.BlockSpec(memory_space=pltpu.SMEM),
                  pl.BlockSpec((1, qb_rows, 512), lambda b, i: (b, i, 0)),
                  pl.BlockSpec((1, S, 512), lambda b, i: (b, 0, 0))],
        out_specs=pl.BlockSpec((1, qb_rows, 512), lambda b, i: (b, i, 0)),
        out_shape=jax.ShapeDtypeStruct((B, S, 512), MXU_DTYPE),
        compiler_params=_params("parallel", "arbitrary"),
        name="gqa_attn",
    )(sink, gq.reshape(B, S, 512), gkv.reshape(B, S, 512))
    return out.reshape(B * S, 512)


DN_TR = 256
DN_HALO = 16
DN_CPT = DN_TR // DN_CHUNK


def _stack(x, left):
    z = jnp.zeros_like(x)
    return jnp.concatenate([jnp.where(left, x, z), jnp.where(left, z, x)], axis=1)


def _dnlocal_kernel(main_ref, prev_ref, next_ref, ab_ref, conv_ref, gp_ref, bd_ref, trif_ref, trib_ref,
                    eg_ref, eb_ref,
                    u_out, w_out, qg_out, kg_out, qk_out, gl_out, pad_ref, *, ctx_tiles, n_tiles):
    i = pl.program_id(1)
    tr = DN_TR
    first = (i == 0) | (i == ctx_tiles)
    last = (i == ctx_tiles - 1) | (i == n_tiles - 1)
    xp = prev_ref[0].astype(F32)
    xn = next_ref[0].astype(F32)
    pad_ref[0:DN_HALO, :] = jnp.where(first, jnp.zeros_like(xp), xp)
    pad_ref[DN_HALO:DN_HALO + tr, :] = main_ref[0].astype(F32)
    pad_ref[DN_HALO + tr:, :] = jnp.where(last, jnp.zeros_like(xn), xn)
    y = jnp.zeros((tr, 3 * DN_WIDTH), F32)
    for t in range(DN_CONV):
        y = y + conv_ref[t:t + 1, :] * pad_ref[pl.ds(DN_HALO - DN_CONV // 2 + t, tr), :]
    y = _silu(y)
    q, k, v = y[:, 0:512], y[:, 512:1024], y[:, 1024:1536]
    bd = bd_ref[...]
    q = q * lax.rsqrt(_exact_mm(q * q, bd) + 1e-6) * (DN_HEAD_DIM ** -0.5)
    k = k * lax.rsqrt(_exact_mm(k * k, bd) + 1e-6)

    ab = ab_ref[0]
    g = -jnp.exp(gp_ref[0:1, :]) * jax.nn.softplus(ab + gp_ref[1:2, :])
    beta = jax.nn.sigmoid(ab)
    lane = lax.broadcasted_iota(jnp.int32, (tr, LANES), 1)
    gc = jnp.where(lane < DN_HEADS, _exact_mm_left(trif_ref[...], g), _exact_mm_left(trib_ref[...], g))
    gcx_all = _exact_mm(gc, eg_ref[...])
    bx_all = _exact_mm(beta, eb_ref[...])

    c = DN_CPT
    lane3 = lax.broadcasted_iota(jnp.int32, (1, 1, LANES), 2)
    left = (lane3 % LANES) < DN_HEAD_DIM
    tpos = lane3 % DN_HEAD_DIM
    lane6 = lax.broadcasted_iota(jnp.int32, (1, 1, 2 * LANES), 2)
    left6 = (lane6 % LANES) < DN_HEAD_DIM
    ri = lax.broadcasted_iota(jnp.int32, (1, DN_CHUNK, LANES), 1)
    cj = lax.broadcasted_iota(jnp.int32, (1, DN_CHUNK, LANES), 2) % DN_HEAD_DIM
    one = jnp.ones((), F32)
    zero = jnp.zeros((), F32)

    def process(units):
        st = {}
        for d, j in units:
            off = d * 512 + j * LANES
            gcx = gcx_all[:, off:off + LANES].reshape(c, DN_CHUNK, LANES)
            bx = bx_all[:, off:off + LANES].reshape(c, DN_CHUNK, LANES)
            qp = q[:, j * LANES:(j + 1) * LANES].reshape(c, DN_CHUNK, LANES)
            kp = k[:, j * LANES:(j + 1) * LANES].reshape(c, DN_CHUNK, LANES)
            vp = v[:, j * LANES:(j + 1) * LANES].reshape(c, DN_CHUNK, LANES)
            gl = gcx[:, DN_CHUNK - 1:DN_CHUNK, :] if d == 0 else gcx[:, 0:1, :]
            kb = kp * bx
            kq = _bmm_nt(jnp.concatenate([kb, qp], axis=1), _stack(kp, left))
            hi, mid, lo = _split3(gcx)
            a6 = jnp.where(tpos == 0, hi, jnp.where(tpos == 1, mid, jnp.where(tpos == 2, lo,
                           jnp.where(tpos < 6, one, zero))))
            b6 = jnp.where(tpos < 3, one, jnp.where(tpos == 3, -hi, jnp.where(tpos == 4, -mid,
                           jnp.where(tpos == 5, -lo, zero))))
            diff = _bmm_nt(a6, _stack(b6, left))
            st[d, j] = dict(gcx=gcx, bx=bx, qp=qp, kp=kp, vp=vp, gl=gl, kb=kb, kq=kq, diff=diff)
        for d, j in units:
            u = st[d, j]
            incl = (ri >= cj) if d == 0 else (ri <= cj)
            strict = (ri > cj) if d == 0 else (ri < cj)
            dm = jnp.exp(jnp.where(incl, u["diff"], -jnp.inf))
            u["qkm"] = u["kq"][:, DN_CHUNK:, :] * dm
            u["x"] = -jnp.where(strict, u["kq"][:, 0:DN_CHUNK, :] * dm, zero)
            u["r"] = u["x"]
        for d, j in units:
            u = st[d, j]
            u["x"] = _bmm(u["x"], _stack(u["x"], left))
        for level in range(5):
            for d, j in units:
                u = st[d, j]
                xs = _stack(u["x"], left)
                if level < 4:
                    m = _bmm(jnp.concatenate([u["r"], u["x"]], axis=1), xs)
                    u["r"] = u["r"] + u["x"] + m[:, 0:DN_CHUNK, :]
                    u["x"] = m[:, DN_CHUNK:, :]
                else:
                    u["r"] = u["r"] + u["x"] + _bmm(u["r"], xs)
        for d, j in units:
            u = st[d, j]
            eg = jnp.exp(u["gcx"])
            rhs = jnp.concatenate([u["vp"] * u["bx"], u["kb"] * eg], axis=-1)
            sol = rhs + _bmm(u["r"], _stack(rhs, left6))
            sl = slice(j * LANES, (j + 1) * LANES)
            u_out[0, d, :, sl] = sol[:, :, 0:LANES].reshape(tr, LANES)
            w_out[0, d, :, sl] = sol[:, :, LANES:].reshape(tr, LANES).astype(w_out.dtype)
            qg_out[0, d, :, sl] = (u["qp"] * eg).reshape(tr, LANES).astype(qg_out.dtype)
            kg_out[0, d, :, sl] = (u["kp"] * jnp.exp(u["gl"] - u["gcx"])).reshape(tr, LANES).astype(kg_out.dtype)
            qk_out[0, d, :, sl] = u["qkm"].reshape(tr, LANES).astype(qk_out.dtype)
            gl_out[0, d, :, :, sl] = jnp.exp(u["gl"])

    for d in range(2):
        process([(d, j) for j in range(4)])


def _dn_constants():
    idx = np.arange(DN_TR)
    same = (idx[:, None] // DN_CHUNK) == (idx[None, :] // DN_CHUNK)
    trif = (same & (idx[None, :] <= idx[:, None])).astype(np.float32)
    trib = (same & (idx[None, :] >= idx[:, None])).astype(np.float32)
    h = np.arange(512)
    bd = ((h[:, None] // DN_HEAD_DIM) == (h[None, :] // DN_HEAD_DIM)).astype(np.float32)
    col = np.arange(LANES)[:, None]
    out = np.arange(1024)[None, :]
    unit = (out // 512) * DN_HEADS + (out % 512) // DN_HEAD_DIM
    eg = (col == unit).astype(np.float32)
    eb = (col == unit + 2 * DN_HEADS).astype(np.float32)
    return tuple(jnp.asarray(a, MXU_DTYPE) for a in (bd, trif, trib, eg, eb))


def _dnlocal_call(dqkv, ab, conv_w, gp, consts, B, S, C):
    tr = DN_TR
    n_tiles = S // tr
    hb = tr // DN_HALO
    n_hblk = S // DN_HALO
    bd, trif, trib, eg, eb = consts
    const2 = lambda b, i: (0, 0)
    big = lambda b, i: (b, 0, i, 0)
    act = MXU_DTYPE
    shp = (B, 2, S, 512)
    return pl.pallas_call(
        functools.partial(_dnlocal_kernel, ctx_tiles=C // tr, n_tiles=n_tiles),
        grid=(B, n_tiles),
        in_specs=[pl.BlockSpec((1, tr, 1536), lambda b, i: (b, i, 0)),
                  pl.BlockSpec((1, DN_HALO, 1536), lambda b, i: (b, jnp.maximum(i * hb - 1, 0), 0)),
                  pl.BlockSpec((1, DN_HALO, 1536), lambda b, i: (b, jnp.minimum((i + 1) * hb, n_hblk - 1), 0)),
                  pl.BlockSpec((1, tr, LANES), lambda b, i: (b, i, 0)),
                  pl.BlockSpec((8, 1536), const2),
                  pl.BlockSpec((8, LANES), const2),
                  pl.BlockSpec((512, 512), const2),
                  pl.BlockSpec((tr, tr), const2),
                  pl.BlockSpec((tr, tr), const2),
                  pl.BlockSpec((LANES, 1024), const2),
                  pl.BlockSpec((LANES, 1024), const2)],
        out_specs=[pl.BlockSpec((1, 2, tr, 512), big)] * 5
                  + [pl.BlockSpec((1, 2, DN_CPT, 1, 512), lambda b, i: (b, 0, i, 0, 0))],
        out_shape=[jax.ShapeDtypeStruct(shp, F32)] + [jax.ShapeDtypeStruct(shp, act)] * 4
                  + [jax.ShapeDtypeStruct((B, 2, S // DN_CHUNK, 1, 512), F32)],
        scratch_shapes=[pltpu.VMEM((tr + 2 * DN_HALO, 1536), F32)],
        compiler_params=_params("parallel", "parallel"),
        name="dn_local",
    )(dqkv.reshape(B, S, 1536), dqkv.reshape(B, S, 1536), dqkv.reshape(B, S, 1536), ab.reshape(B, S, LANES),
      conv_w, gp, bd, trif, trib, eg, eb)


def _dnscan_kernel(uf, wf, qgf, kgf, qkf, glf, ub, wb, qgb, kgb, qkb, glb, of_out, ob_out, s_ref):
    n = pl.program_id(1)
    lane = lax.broadcasted_iota(jnp.int32, (1, LANES), 1)
    left = lane < DN_HEAD_DIM
    row = lax.broadcasted_iota(jnp.int32, (LANES, LANES), 0)
    col = lax.broadcasted_iota(jnp.int32, (LANES, LANES), 1)
    same_head = (row < DN_HEAD_DIM) == (col < DN_HEAD_DIM)
    dirs = ((uf, wf, qgf, kgf, qkf, glf, of_out), (ub, wb, qgb, kgb, qkb, glb, ob_out))
    units = [(bb, d, j) for bb in range(uf.shape[0]) for d in range(2) for j in range(4)]
    sidx = lambda bb, d, j: (bb * 2 + d) * 4 + j
    sl = lambda j: slice(j * LANES, (j + 1) * LANES)
    started = n > 0
    st = {t: jnp.where(started, s_ref[sidx(*t)], jnp.zeros((LANES, LANES), F32)) for t in units}
    pre = {(bb, d, j): _mm(dirs[d][1][bb, 0, :, sl(j)], st[bb, d, j]) for bb, d, j in units}
    o1 = {(bb, d, j): _mm(dirs[d][2][bb, 0, :, sl(j)], st[bb, d, j]) for bb, d, j in units}
    vn = {(bb, d, j): dirs[d][0][bb, 0, :, sl(j)] - pre[bb, d, j] for bb, d, j in units}
    for bb, d, j in units:
        v = vn[bb, d, j]
        z = jnp.zeros_like(v)
        vst = jnp.concatenate([jnp.where(left, v, z), jnp.where(left, z, v)], axis=0)
        dirs[d][6][bb, :, sl(j)] = o1[bb, d, j] + _mm(dirs[d][4][bb, 0, :, sl(j)], vst)
    for bb, d, j in units:
        upd = _mm_tn(dirs[d][3][bb, 0, :, sl(j)], vn[bb, d, j])
        s_ref[sidx(bb, d, j)] = (st[bb, d, j] * dirs[d][5][bb, 0, 0, :, sl(j)]
                                 + jnp.where(same_head, upd, jnp.zeros_like(upd)))


DN_SCAN_BATCH = 4


def _dnscan_call(local, B, S, C):
    u, w, qg, kg, qk, gl = local
    nch, nc = S // DN_CHUNK, C // DN_CHUNK
    bb = DN_SCAN_BATCH if B % DN_SCAN_BATCH == 0 else 1

    def bidx(n):
        return jnp.where(n < nc, nc - 1 - n, nch - 1 + nc - n)

    fspec = pl.BlockSpec((bb, 1, DN_CHUNK, 512), lambda b, n: (b, 0, n, 0))
    bspec = pl.BlockSpec((bb, 1, DN_CHUNK, 512), lambda b, n: (b, 1, bidx(n), 0))
    fgl = pl.BlockSpec((bb, 1, 1, 1, 512), lambda b, n: (b, 0, n, 0, 0))
    bgl = pl.BlockSpec((bb, 1, 1, 1, 512), lambda b, n: (b, 1, bidx(n), 0, 0))
    return pl.pallas_call(
        _dnscan_kernel,
        grid=(B // bb, nch),
        in_specs=[fspec] * 5 + [fgl] + [bspec] * 5 + [bgl],
        out_specs=[pl.BlockSpec((bb, DN_CHUNK, 512), lambda b, n: (b, n, 0)),
                   pl.BlockSpec((bb, DN_CHUNK, 512), lambda b, n: (b, bidx(n), 0))],
        out_shape=[jax.ShapeDtypeStruct((B, S, 512), F32)] * 2,
        scratch_shapes=[pltpu.VMEM((bb * 8, LANES, LANES), F32)],
        compiler_params=_params("parallel", "arbitrary"),
        name="dn_scan",
    )(u, w, qg, kg, qk, gl, u, w, qg, kg, qk, gl)


def _pack_pairs(v):
    w = v.shape[1] // 2
    bits = lax.bitcast_convert_type(v.astype(jnp.bfloat16).astype(F32), jnp.int32)
    return lax.shift_right_logical(bits[:, :w], 16) | bits[:, w:]


def _unpack_pairs(p):
    lo = lax.bitcast_convert_type(lax.shift_left(p, 16), F32)
    hi = lax.bitcast_convert_type(p & jnp.int32(-65536), F32)
    return lo, hi


def _merge_kernel(x_ref, ga_ref, shf_ref, scf_ref, omla_ref, of_ref, ob_ref, dz_ref, ogqa_ref, gates_ref,
                  wo1_ref, wo2_ref, wo3_ref, wout_ref, dnn_ref, bd_ref, lng_ref, lnb_ref, wr_ref, rb_ref,
                  x1_out, ufp_out, eidx_out, rank_out, ew_out, cnt_out, cnt_ref):
    @pl.when(pl.program_id(0) == 0)
    def _():
        cnt_ref[...] = jnp.zeros_like(cnt_ref)

    o = of_ref[...] + ob_ref[...]
    ms = _exact_mm(o * o, bd_ref[...]) * (1.0 / DN_HEAD_DIM)
    dn = o * lax.rsqrt(ms + NORM_EPS) * dnn_ref[...] * _silu(dz_ref[...].astype(F32))
    g1 = gates_ref[:, 0:1024].astype(F32)
    g2 = gates_ref[:, 1024:2048].astype(F32)
    g3 = gates_ref[:, 2048:3072].astype(F32)
    m = (g1 * _mm(omla_ref[...], wo1_ref[...]) + g2 * _mm(dn, wo2_ref[...])
         + g3 * _mm(ogqa_ref[...], wo3_ref[...]))
    y = _mm(m, wout_ref[...])
    x1 = _layernorm(DEEPNORM_ALPHA * x_ref[...] + ga_ref[0] * y, lng_ref[...], lnb_ref[...])
    x1_out[...] = x1
    uf = x1 * (1.0 + scf_ref[0]) + shf_ref[0]
    _split_pieces(_pack_pairs(uf), ufp_out)

    tm = uf.shape[0]
    scores = jax.nn.sigmoid(_mm_nt(wr_ref[...], uf))[0:N_EXPERTS]
    sel = scores + rb_ref[0:N_EXPERTS, :]
    gsz = N_EXPERTS // N_GROUPS
    neg = jnp.full((), -jnp.inf, F32)
    sel3 = sel.reshape(N_GROUPS, gsz, tm)
    mem = lax.broadcasted_iota(jnp.int32, (N_GROUPS, gsz, tm), 1)
    m1 = jnp.max(sel3, 1, keepdims=True)
    i1 = jnp.min(jnp.where(sel3 == m1, mem, gsz), 1, keepdims=True)
    m2 = jnp.max(jnp.where(mem == i1, neg, sel3), 1, keepdims=True)
    gs = (m1 + m2).reshape(N_GROUPS, tm)
    gi = lax.broadcasted_iota(jnp.int32, (N_GROUPS, tm), 0)
    grank = jnp.zeros((N_GROUPS, tm), jnp.int32)
    for gp in range(N_GROUPS):
        other = gs[gp:gp + 1, :]
        beats = (other > gs) | ((other == gs) & (gp < gi))
        grank = grank + beats.astype(jnp.int32)
    gsel = (grank < TOPK_GROUPS).reshape(N_GROUPS, 1, tm)
    cur = jnp.where(gsel, sel3, neg).reshape(N_EXPERTS, tm)
    ei = lax.broadcasted_iota(jnp.int32, (N_EXPERTS, tm), 0)
    zero = jnp.zeros((N_EXPERTS, tm), F32)
    one = jnp.ones((N_EXPERTS, tm), F32)
    chosen = zero
    picks = []
    for _ in range(TOP_K):
        mx = jnp.max(cur, 0, keepdims=True)
        ix = jnp.min(jnp.where(cur == mx, ei, N_EXPERTS), 0, keepdims=True)
        pick = ei == ix
        picks.append((ix, pick))
        chosen = chosen + jnp.where(pick, one, zero)
        cur = jnp.where(pick, neg, cur)

    r_i = lax.broadcasted_iota(jnp.int32, (tm, tm), 0)
    c_i = lax.broadcasted_iota(jnp.int32, (tm, tm), 1)
    before = jnp.where(r_i < c_i, 1.0, 0.0)
    pos = cnt_ref[:, 0:1] + _mm(chosen, before)
    cnt_new = cnt_ref[...] + jnp.sum(chosen, 1, keepdims=True)
    cnt_ref[...] = cnt_new
    cnt_out[...] = cnt_new

    w_rows = [jnp.sum(jnp.where(pick, scores, zero), 0, keepdims=True) for _, pick in picks]
    wsum = w_rows[0]
    for w_k in w_rows[1:]:
        wsum = wsum + w_k
    eidx_out[...] = jnp.concatenate([ix for ix, _ in picks], axis=0)
    rank_out[...] = jnp.concatenate(
        [jnp.sum(jnp.where(pick, pos, zero), 0, keepdims=True) for _, pick in picks], axis=0).astype(jnp.int32)
    w8 = jnp.concatenate([w_k / wsum * ROUTED_SCALE for w_k in w_rows], axis=0)
    ew_out[...] = jnp.concatenate([w8, jnp.zeros((LANES - TOP_K, tm), F32)], axis=0).T


def _merge_call(xs, mod_l, omla, of, ob, dz, ogqa, gates, wo1, wo2, wo3, wout, dnn, bd, lng, lnb, wr, rb, B, S, C):
    T, D = xs.shape
    tpb, ctiles = S // TM, C // TM
    row = lambda i: (i, 0)
    const = lambda i: (0, 0)
    modspec = lambda k: pl.BlockSpec((1, 1, D), _mod_index(tpb, ctiles, k))
    return pl.pallas_call(
        _merge_kernel,
        grid=(T // TM,),
        in_specs=[pl.BlockSpec((TM, D), row), modspec(2), modspec(3), modspec(4),
                  pl.BlockSpec((TM, 512), row), pl.BlockSpec((TM, 512), row), pl.BlockSpec((TM, 512), row),
                  pl.BlockSpec((TM, 512), row), pl.BlockSpec((TM, 512), row), pl.BlockSpec((TM, 3072), row),
                  pl.BlockSpec((512, D), const), pl.BlockSpec((512, D), const), pl.BlockSpec((512, D), const),
                  pl.BlockSpec((D, D), const), pl.BlockSpec((1, 512), const), pl.BlockSpec((512, 512), const),
                  pl.BlockSpec((1, D), const), pl.BlockSpec((1, D), const),
                  pl.BlockSpec((LANES, D), const), pl.BlockSpec((LANES, 1), const)],
        out_specs=[pl.BlockSpec((TM, D), row), pl.BlockSpec((N_PIECES, TM, PIECE), lambda i: (0, i, 0)),
                   pl.BlockSpec((TOP_K, TM), lambda i: (0, i)), pl.BlockSpec((TOP_K, TM), lambda i: (0, i)),
                   pl.BlockSpec((TM, LANES), row), pl.BlockSpec((N_EXPERTS, LANES), const)],
        out_shape=[jax.ShapeDtypeStruct((T, D), F32), jax.ShapeDtypeStruct((N_PIECES, T, PIECE), jnp.int32),
                   jax.ShapeDtypeStruct((TOP_K, T), jnp.int32), jax.ShapeDtypeStruct((TOP_K, T), jnp.int32),
                   jax.ShapeDtypeStruct((T, LANES), F32), jax.ShapeDtypeStruct((N_EXPERTS, LANES), F32)],
        scratch_shapes=[pltpu.VMEM((N_EXPERTS, LANES), F32)],
        compiler_params=_params("arbitrary"),
        name="merge_norm_route",
    )(xs, mod_l, mod_l, mod_l, omla, of.reshape(T, 512), ob.reshape(T, 512), dz, ogqa, gates,
      wo1, wo2, wo3, wout, dnn, bd, lng, lnb, wr, rb)


EXPERT_BLOCK = 512
EXPERT_SUB = 128
SC_WINDOW = 128
N_PIECES = 2
PIECE = D_MODEL // 2 // N_PIECES


def _split_pieces(packed, out_ref):
    for h in range(N_PIECES):
        out_ref[h] = packed[:, h * PIECE:(h + 1) * PIECE]


def _mm_pieces(pieces, w):
    acc = None
    for h, (lo, hi) in enumerate(pieces):
        t = (_mm(lo, w[h * PIECE:(h + 1) * PIECE, :])
             + _mm(hi, w[D_MODEL // 2 + h * PIECE:D_MODEL // 2 + (h + 1) * PIECE, :]))
        acc = t if acc is None else acc + t
    return acc


def _sc_mesh():
    return plsc.VectorSubcoreMesh(core_axis_name="c", subcore_axis_name="s")


def _sc_gather_rows(y, idx):
    n = idx.shape[1]
    W = y.shape[1]

    @pl.kernel(out_type=jax.ShapeDtypeStruct((n, W), y.dtype), mesh=_sc_mesh(), scratch_types=[])
    def gather(y_hbm, i_hbm, o_hbm):
        def body(i_vmem, o_vmem):
            pltpu.sync_copy(y_hbm.at[i_vmem.at[0]], o_vmem)

        pltpu.emit_pipeline(
            body,
            grid=(n // SC_WINDOW,),
            in_specs=[pl.BlockSpec((1, SC_WINDOW), lambda i: (0, i))],
            out_specs=[pl.BlockSpec((SC_WINDOW, W), lambda i: (i, 0))],
            core_axis_name=("c", "s"),
            dimension_semantics=(pltpu.PARALLEL,),
        )(i_hbm, o_hbm)

    return gather(y, idx)


SC_LANES = 16
SC_WORKERS = 32
SC_CHUNK = 2176


def _sc_invert_rows(dest, default, n_tok):
    n_rows = default.shape[0]
    per_w = n_rows // SC_WORKERS
    n_k = dest.shape[0] // n_tok
    assert n_rows % (SC_WORKERS * SC_LANES) == 0 and n_tok % SC_CHUNK == 0

    @pl.kernel(out_type=jax.ShapeDtypeStruct((n_rows,), jnp.int32), mesh=_sc_mesh(),
               scratch_types=[pltpu.VMEM((per_w,), jnp.int32), pltpu.VMEM((SC_CHUNK,), jnp.int32)],
               compiler_params=pltpu.CompilerParams(needs_layout_passes=False))
    def invert(dest_hbm, dflt_hbm, out_hbm, rows_v, dest_v):
        wid = lax.axis_index("s") * 2 + lax.axis_index("c")
        base = wid * per_w
        pltpu.sync_copy(dflt_hbm.at[pl.ds(base, per_w)], rows_v)
        lanes = lax.iota(jnp.int32, SC_LANES)
        for k in range(n_k):
            @pl.loop(0, n_tok // SC_CHUNK)
            def _(c):
                pltpu.sync_copy(dest_hbm.at[pl.ds(k * n_tok + c * SC_CHUNK, SC_CHUNK)], dest_v)

                @pl.loop(0, SC_CHUNK, step=SC_LANES)
                def _(o):
                    local = dest_v[pl.ds(o, SC_LANES)] - base
                    mine = (local >= 0) & (local < per_w)
                    plsc.store_scatter(rows_v, [jnp.where(mine, local, 0)], c * SC_CHUNK + o + lanes, mask=mine)

        pltpu.sync_copy(rows_v, out_hbm.at[pl.ds(base, per_w)])

    return invert(dest, default)


def _experts_kernel(be_ref, nv_ref, xb_ref, wg_ref, wu_ref, wd_ref, y_out):
    b = pl.program_id(0)
    nv = nv_ref[b]

    @pl.when(nv > 0)
    def _():
        wg = wg_ref[0, 0].astype(MXU_DTYPE)
        wu = wu_ref[0, 0].astype(MXU_DTYPE)
        wd = wd_ref[0, 0].astype(MXU_DTYPE)
        n_sub = EXPERT_BLOCK // EXPERT_SUB
        rows = lax.broadcasted_iota(jnp.int32, (EXPERT_SUB, PIECE), 0)
        pieces = []
        for s in range(n_sub):
            sl = slice(s * EXPERT_SUB, (s + 1) * EXPERT_SUB)
            ps = []
            for h in range(N_PIECES):
                xh = xb_ref[h, sl, :]
                ps.append(_unpack_pairs(jnp.where(rows + s * EXPERT_SUB < nv, xh, jnp.zeros_like(xh))))
            pieces.append(ps)
        gate = [_mm_pieces(ps, wg) for ps in pieces]
        up = [_mm_pieces(ps, wu) for ps in pieces]
        hid = [_silu(g) * u for g, u in zip(gate, up)]
        out = [_mm(h_, wd) for h_ in hid]
        for s in range(n_sub):
            packed = _pack_pairs(out[s])
            for h in range(N_PIECES):
                y_out[h, s * EXPERT_SUB:(s + 1) * EXPERT_SUB, :] = packed[:, h * PIECE:(h + 1) * PIECE]


def _experts_call(xb, block_e, nvalid, wg, wu, wd, layer):
    _, R, _ = xb.shape
    D = D_MODEL
    blk = (N_PIECES, EXPERT_BLOCK, PIECE)
    grid_spec = pltpu.PrefetchScalarGridSpec(
        num_scalar_prefetch=2,
        grid=(R // EXPERT_BLOCK,),
        in_specs=[pl.BlockSpec(blk, lambda b, be, nv: (0, b, 0)),
                  pl.BlockSpec((1, 1, D, EXPERT_DIM), lambda b, be, nv: (layer, be[b], 0, 0)),
                  pl.BlockSpec((1, 1, D, EXPERT_DIM), lambda b, be, nv: (layer, be[b], 0, 0)),
                  pl.BlockSpec((1, 1, EXPERT_DIM, D), lambda b, be, nv: (layer, be[b], 0, 0))],
        out_specs=pl.BlockSpec(blk, lambda b, be, nv: (0, b, 0)),
    )
    return pl.pallas_call(
        _experts_kernel,
        grid_spec=grid_spec,
        out_shape=jax.ShapeDtypeStruct((N_PIECES, R, PIECE), jnp.int32),
        compiler_params=_params("arbitrary"),
        name="moe_experts",
    )(block_e, nvalid, xb, wg, wu, wd)


def _combine_kernel(x_ref, ufp_ref, yg_ref, ew_ref, gf_ref, sg_ref, su_ref, sd_ref, g_ref, b_ref, o_ref):
    pieces = [_unpack_pairs(ufp_ref[h]) for h in range(N_PIECES)]
    hs = _silu(_mm_pieces(pieces, sg_ref)) * _mm_pieces(pieces, su_ref)
    f = _mm(hs, sd_ref[...])
    ew = ew_ref[...]
    lane = lax.broadcasted_iota(jnp.int32, ew.shape, 1)
    acc = [[jnp.zeros((x_ref.shape[0], PIECE), F32) for _ in range(N_PIECES)] for _ in range(2)]
    for k in range(TOP_K):
        wk = jnp.sum(jnp.where(lane == k, ew, jnp.zeros_like(ew)), axis=1, keepdims=True)
        for h in range(N_PIECES):
            ylo, yhi = _unpack_pairs(yg_ref[h, k])
            acc[0][h] = acc[0][h] + wk * ylo
            acc[1][h] = acc[1][h] + wk * yhi
    f = f + jnp.concatenate(acc[0] + acc[1], axis=1)
    o_ref[...] = _layernorm(DEEPNORM_ALPHA * x_ref[...] + gf_ref[0] * f, g_ref[...], b_ref[...])


def _combine_call(x1, ufp, yg, ew, mod_l, sg, su, sd, g, b, B, S, C):
    T, D = x1.shape
    tpb, ctiles = S // TM, C // TM
    row = lambda i: (i, 0)
    const = lambda i: (0, 0)
    return pl.pallas_call(
        _combine_kernel,
        grid=(T // TM,),
        in_specs=[pl.BlockSpec((TM, D), row), pl.BlockSpec((N_PIECES, TM, PIECE), lambda i: (0, i, 0)),
                  pl.BlockSpec((N_PIECES, TOP_K, TM, PIECE), lambda i: (0, 0, i, 0)),
                  pl.BlockSpec((TM, LANES), row),
                  pl.BlockSpec((1, 1, D), _mod_index(tpb, ctiles, 5)),
                  pl.BlockSpec((D, SHARED_DIM), const), pl.BlockSpec((D, SHARED_DIM), const),
                  pl.BlockSpec((SHARED_DIM, D), const),
                  pl.BlockSpec((1, D), const), pl.BlockSpec((1, D), const)],
        out_specs=pl.BlockSpec((TM, D), row),
        out_shape=jax.ShapeDtypeStruct((T, D), F32),
        compiler_params=_params("parallel"),
        name="moe_combine_norm",
    )(x1, ufp, yg, ew, mod_l, sg, su, sd, g, b)


def _moe_routed(ufp, eidx_t, rank_t, counts, wg, wu, wd, layer):
    T = ufp.shape[1]
    n_blocks = -(-(T * TOP_K + N_EXPERTS * (EXPERT_BLOCK - 1)) // EXPERT_BLOCK)
    n_rows = n_blocks * EXPERT_BLOCK
    padded = (counts + EXPERT_BLOCK - 1) // EXPERT_BLOCK * EXPERT_BLOCK
    pad_end = jnp.cumsum(padded)
    start_pad = pad_end - padded
    experts = jnp.arange(N_EXPERTS, dtype=jnp.int32)

    def lookup(table, idx):
        sel = idx[None] == experts.reshape((N_EXPERTS,) + (1,) * idx.ndim)
        return jnp.sum(jnp.where(sel, table.reshape((N_EXPERTS,) + (1,) * idx.ndim), 0), axis=0)

    dest_t = lookup(start_pad, eidx_t) + rank_t
    blk = jnp.arange(n_blocks, dtype=jnp.int32) * EXPERT_BLOCK
    block_e = jnp.minimum(jnp.sum((blk[:, None] >= pad_end[None, :]).astype(jnp.int32), axis=1), N_EXPERTS - 1)
    nvalid = jnp.clip(lookup(counts, block_e) - (blk - lookup(start_pad, block_e)), 0, EXPERT_BLOCK)
    row_tok = _sc_invert_rows(dest_t.reshape(-1), jnp.arange(n_rows, dtype=jnp.int32) % T, T)
    piece = jnp.arange(N_PIECES, dtype=jnp.int32)
    src = (piece[:, None] * T + row_tok[None, :]).reshape(1, N_PIECES * n_rows)
    xb = _sc_gather_rows(ufp.reshape(N_PIECES * T, PIECE), src).reshape(N_PIECES, n_rows, PIECE)
    yb = _experts_call(xb, block_e.astype(jnp.int32), nvalid.astype(jnp.int32), wg, wu, wd, layer)
    back = (piece[:, None, None] * n_rows + dest_t[None]).reshape(1, N_PIECES * TOP_K * T)
    yg = _sc_gather_rows(yb.reshape(N_PIECES * n_rows, PIECE), back)
    return yg.reshape(N_PIECES, TOP_K, T, PIECE)


def _rot_cols(w, half):
    return jnp.concatenate([-w[:, half:], w[:, :half]], axis=1)


def _prep_w_in(w):
    d = w.shape[0]
    offs = np.cumsum((0,) + IN_SIZES)
    cq, ckv, kr, dqkv, da, db, dz, gq, gk, gv, gates = (w[:, offs[t]:offs[t + 1]] for t in range(len(IN_SIZES)))
    z = lambda n: jnp.zeros((d, n), w.dtype)
    krg = jnp.concatenate([z(64), kr, z(32)], 1)
    krr = jnp.concatenate([z(64), _rot_cols(kr, MLA_ROPE // 2), z(32)], 1)
    ab = jnp.concatenate([da, db, z(LANES - 4 * DN_HEADS)], 1)
    hd = GQA_HEAD_DIM
    gq_rot = jnp.concatenate([_rot_cols(gq[:, h * hd:(h + 1) * hd], hd // 2) for h in range(GQA_HEADS)], 1)
    dup = lambda t: jnp.concatenate([t[:, 0:hd], t[:, 0:hd], t[:, hd:2 * hd], t[:, hd:2 * hd]], 1)
    gk_rot = jnp.concatenate([_rot_cols(gk[:, h * hd:(h + 1) * hd], hd // 2) for h in range(GQA_KV_HEADS)], 1)
    cat = jnp.concatenate([cq, ckv, krg, krr, dqkv, ab, dz, gq, gq_rot, dup(gk), dup(gk_rot), dup(gv), gates], 1)
    assert cat.shape[1] == NZ
    return cat.astype(MXU_DTYPE)


def _prep_w_uq(w):
    d = w.shape[0]
    hw = MLA_NOPE + MLA_ROPE
    a, b = [], []
    for h in range(MLA_HEADS):
        wh = w[:, h * hw:(h + 1) * hw]
        a += [wh, jnp.zeros((d, LANES - hw), w.dtype)]
        b += [jnp.zeros((d, MLA_NOPE), w.dtype), _rot_cols(wh[:, MLA_NOPE:], MLA_ROPE // 2),
              jnp.zeros((d, LANES - hw), w.dtype)]
    return jnp.concatenate(a + b, 1).astype(MXU_DTYPE)


def _prep_w_ukv(w):
    d = w.shape[0]
    hw = MLA_NOPE + MLA_V
    kpart, vpart = [], []
    for h in range(MLA_HEADS):
        wh = w[:, h * hw:(h + 1) * hw]
        kpart += [wh[:, :MLA_NOPE], jnp.zeros((d, LANES - MLA_NOPE), w.dtype)]
        vpart += [wh[:, MLA_NOPE:]]
    return jnp.concatenate(kpart + vpart, 1).astype(MXU_DTYPE)


def _rope_tables(n_rows, C):
    row = jnp.repeat(jnp.arange(n_rows, dtype=F32), GRID_W)
    col = jnp.tile(jnp.arange(GRID_W, dtype=F32), n_rows)

    def angles(dim):
        n = dim // 4
        inv = ROPE_BASE ** (-jnp.arange(n, dtype=F32) / n)
        return jnp.concatenate([row[:, None] * inv, col[:, None] * inv], axis=-1)

    def with_ctx(cos, sin):
        return (jnp.concatenate([jnp.ones((C, LANES), F32), cos], 0),
                jnp.concatenate([jnp.zeros((C, LANES), F32), sin], 0))

    L = n_rows * GRID_W
    am = angles(MLA_ROPE)
    one, zero = jnp.ones((L, MLA_NOPE), F32), jnp.zeros((L, MLA_NOPE), F32)
    cm = jnp.concatenate([one, jnp.cos(am), jnp.cos(am), one[:, :32]], 1)
    sm = jnp.concatenate([zero, jnp.sin(am), jnp.sin(am), zero[:, :32]], 1)
    ag = angles(GQA_HEAD_DIM)
    cg = jnp.tile(jnp.cos(ag), (1, 4))
    sg = jnp.tile(jnp.sin(ag), (1, 4))
    return with_ctx(cm, sm) + with_ctx(cg, sg)


def kernel(x, c, ctx, c_ctx, w_ada, b_ada, w_in, mla_q_norm, mla_kv_norm, w_uq, w_ukv, dn_conv, dn_a_log, dn_dt_bias, dn_norm, gqa_sink, w_o_mla, w_o_dn, w_o_gqa, w_out, ln1_g, ln1_b, w_router, router_bias, w_exp_gate, w_exp_up, w_exp_down, w_sh_gate, w_sh_up, w_sh_down, ln2_g, ln2_b):
    B, L, D = x.shape
    C = ctx.shape[1]
    S = C + L
    nl = w_in.shape[0]
    assert D == D_MODEL and nl == DEPTH and B <= CTX_MOD_ROW
    assert C % TM == 0 and L % TM == 0 and L % GRID_W == 0 and L >= 3 * WINDOW
    cast = lambda t: t.astype(MXU_DTYPE)

    cc = jnp.zeros((MOD_ROWS, D), F32).at[0:B].set(c).at[CTX_MOD_ROW].set(c_ctx)
    mods = _ada_call(cc, w_ada, b_ada)
    tabs = _rope_tables(L // GRID_W, C)
    dn_consts = _dn_constants()
    xs = jnp.concatenate([ctx, x], axis=1).reshape(B * S, D)

    for l in range(nl):
        mod_l = mods[l].reshape(MOD_ROWS * 6, 1, D)
        q, k, v, dqkv, ab, dz, gq, gkv, gates = _inproj_call(
            xs, mod_l, _prep_w_in(w_in[l]), _prep_w_uq(w_uq[l]), _prep_w_ukv(w_ukv[l]),
            mla_q_norm[l].reshape(1, -1), mla_kv_norm[l].reshape(1, -1), tabs, B, S, C)
        omla = _mla_call(q, k, v, B, S, C)
        conv8 = jnp.zeros((8, 3 * DN_WIDTH), F32).at[0:DN_CONV].set(dn_conv[l])
        gp = (jnp.zeros((8, LANES), F32).at[0, 0:2 * DN_HEADS].set(dn_a_log[l].reshape(-1))
              .at[1, 0:2 * DN_HEADS].set(dn_dt_bias[l].reshape(-1)))
        local = _dnlocal_call(dqkv, ab, conv8, gp, dn_consts, B, S, C)
        of, ob = _dnscan_call(local, B, S, C)
        ogqa = _gqa_call(gq, gkv, gqa_sink[l], B, S, C)
        wr = jnp.zeros((LANES, D), F32).at[0:N_EXPERTS].set(w_router[l].T)
        rb = jnp.zeros((LANES, 1), F32).at[0:N_EXPERTS, 0].set(router_bias[l])
        x1, ufp, eidx_t, rank_t, ew, cnt = _merge_call(
            xs, mod_l, omla, of, ob, dz, ogqa, gates,
            cast(w_o_mla[l]), cast(w_o_dn[l]), cast(w_o_gqa[l]), cast(w_out[l]),
            jnp.tile(dn_norm[l], DN_HEADS).reshape(1, DN_WIDTH), dn_consts[0],
            ln1_g[l].reshape(1, D), ln1_b[l].reshape(1, D), cast(wr), rb, B, S, C)
        yg = _moe_routed(ufp, eidx_t, rank_t, cnt[:, 0].astype(jnp.int32),
                         w_exp_gate, w_exp_up, w_exp_down, l)
        xs = _combine_call(x1, ufp, yg, ew, mod_l, cast(w_sh_gate[l]), cast(w_sh_up[l]), cast(w_sh_down[l]),
                           ln2_g[l].reshape(1, D), ln2_b[l].reshape(1, D), B, S, C)
    return xs.reshape(B, S, D)[:, C:, :]
```

```python
import functools

import numpy as np
import jax
import jax.numpy as jnp
from jax import lax
from jax.experimental import pallas as pl
from jax.experimental.pallas import tpu as pltpu
from jax.experimental.pallas import tpu_sc as plsc

F32 = jnp.float32
MXU_DTYPE = jnp.bfloat16

D_MODEL = 1024
DEPTH = 4
GRID_W = 64
NORM_EPS = 1e-6
ROPE_BASE = 10000.0
DEEPNORM_ALPHA = (2.0 * DEPTH) ** 0.25

MLA_HEADS = 8
MLA_Q_LORA = 256
MLA_KV_LORA = 128
MLA_NOPE = 64
MLA_ROPE = 32
MLA_V = 64
MLA_SCALE = (MLA_NOPE + MLA_ROPE) ** -0.5
LOG2E = float(np.log2(np.e))

DN_HEADS = 8
DN_HEAD_DIM = 64
DN_WIDTH = DN_HEADS * DN_HEAD_DIM
DN_CONV = 5
DN_CHUNK = 64

GQA_HEADS = 8
GQA_KV_HEADS = 2
GQA_HEAD_DIM = 64
GQA_SCALE = GQA_HEAD_DIM ** -0.5
WINDOW = 128

N_EXPERTS = 64
TOP_K = 8
N_GROUPS = 8
TOPK_GROUPS = 4
EXPERT_DIM = 256
SHARED_DIM = 256
ROUTED_SCALE = 2.5

IN_SIZES = (MLA_Q_LORA, MLA_KV_LORA, MLA_ROPE,
            3 * DN_WIDTH, 2 * DN_HEADS, 2 * DN_HEADS, DN_WIDTH,
            GQA_HEADS * GQA_HEAD_DIM, GQA_KV_HEADS * GQA_HEAD_DIM, GQA_KV_HEADS * GQA_HEAD_DIM,
            3 * D_MODEL)

LANES = 128
TM = 256
MOD_ROWS = 16
CTX_MOD_ROW = 8

OFF_A = 0
OFF_DQKV = 640
OFF_AB = OFF_DQKV + 3 * DN_WIDTH
OFF_DZ = OFF_AB + LANES
OFF_GQ = OFF_DZ + DN_WIDTH
OFF_GK = OFF_GQ + 1024
OFF_GATES = OFF_GK + 768
NZ = OFF_GATES + 3 * D_MODEL

VMEM_LIMIT = 56 * 1024 * 1024


def _mm(a, b):
    return jnp.dot(a.astype(MXU_DTYPE), b.astype(MXU_DTYPE), preferred_element_type=F32)


def _mm_nt(a, b):
    return lax.dot_general(a.astype(MXU_DTYPE), b.astype(MXU_DTYPE), (((1,), (1,)), ((), ())),
                           preferred_element_type=F32)


def _mm_tn(a, b):
    return lax.dot_general(a.astype(MXU_DTYPE), b.astype(MXU_DTYPE), (((0,), (0,)), ((), ())),
                           preferred_element_type=F32)


def _bmm(a, b):
    return jnp.einsum('cik,ckj->cij', a.astype(MXU_DTYPE), b.astype(MXU_DTYPE), preferred_element_type=F32)


def _bmm_nt(a, b):
    return jnp.einsum('cik,cjk->cij', a.astype(MXU_DTYPE), b.astype(MXU_DTYPE), preferred_element_type=F32)


def _split3(x):
    hi = x.astype(jnp.bfloat16).astype(F32)
    r = x - hi
    mid = r.astype(jnp.bfloat16).astype(F32)
    lo = (r - mid).astype(jnp.bfloat16).astype(F32)
    return hi, mid, lo


def _exact_mm(x, m01):
    hi, mid, lo = _split3(x)
    return _mm(hi, m01) + _mm(mid, m01) + _mm(lo, m01)


def _exact_mm_left(m01, x):
    hi, mid, lo = _split3(x)
    return _mm(m01, hi) + _mm(m01, mid) + _mm(m01, lo)


def _silu(x):
    return x * jax.nn.sigmoid(x)


def _layernorm(v, g, b):
    mu = jnp.mean(v, -1, keepdims=True)
    d = v - mu
    var = jnp.mean(d * d, -1, keepdims=True)
    return d * lax.rsqrt(var + NORM_EPS) * g + b


def _mod_index(tiles_per_b, ctx_tiles, k, first_tile=0):
    def index(i):
        t = i + first_tile
        row = jnp.where((t % tiles_per_b) < ctx_tiles, CTX_MOD_ROW, t // tiles_per_b)
        return (row * 6 + k, 0, 0)
    return index


def _params(*sem):
    return pltpu.CompilerParams(dimension_semantics=sem, vmem_limit_bytes=VMEM_LIMIT)


def _ada_kernel(c_ref, w_ref, b_ref, o_ref):
    o_ref[0] = _mm(_silu(c_ref[...]), w_ref[0]) + b_ref[0]


def _ada_call(cc, w_ada, b_ada):
    nl, d, n6 = w_ada.shape
    tn = 1536
    return pl.pallas_call(
        _ada_kernel,
        grid=(nl, n6 // tn),
        in_specs=[pl.BlockSpec((MOD_ROWS, d), lambda l, j: (0, 0)),
                  pl.BlockSpec((1, d, tn), lambda l, j: (l, 0, j)),
                  pl.BlockSpec((1, 1, tn), lambda l, j: (l, 0, j))],
        out_specs=pl.BlockSpec((1, MOD_ROWS, tn), lambda l, j: (l, 0, j)),
        out_shape=jax.ShapeDtypeStruct((nl, MOD_ROWS, n6), F32),
        compiler_params=_params("parallel", "parallel"),
        name="ada_mod",
    )(cc, w_ada, b_ada.reshape(nl, 1, n6))


def _inproj_kernel(x_ref, sh_ref, sc_ref, w_ref, wuq_ref, wukv_ref, qn_ref, kvn_ref,
                   cm_ref, sm_ref, cg_ref, sg_ref,
                   q_out, k_out, v_out, dqkv_out, ab_out, dz_out, gq_out, gkv_out, gates_out):
    u = (x_ref[...] * (1.0 + sc_ref[0]) + sh_ref[0]).astype(MXU_DTYPE)

    def z(off, width):
        return jnp.dot(u, w_ref[:, off:off + width], preferred_element_type=F32)

    def rms(v, g):
        return v * lax.rsqrt(jnp.mean(v * v, -1, keepdims=True) + NORM_EPS) * g

    cm, sm, cg, sg = cm_ref[...], sm_ref[...], cg_ref[...], sg_ref[...]

    za = z(OFF_A, 640)
    qq = _mm(rms(za[:, 0:256], qn_ref[...]), wuq_ref[...])
    kvv = _mm(rms(za[:, 256:384], kvn_ref[...]), wukv_ref[...])
    k_rope = za[:, 384:512] * cm + za[:, 512:640] * sm
    for h in range(MLA_HEADS):
        sl = slice(h * LANES, (h + 1) * LANES)
        qa = qq[:, h * LANES:(h + 1) * LANES]
        qb = qq[:, 1024 + h * LANES:1024 + (h + 1) * LANES]
        q_out[:, sl] = ((qa * cm + qb * sm) * (MLA_SCALE * LOG2E)).astype(q_out.dtype)
        k_out[:, sl] = (kvv[:, sl] + k_rope).astype(k_out.dtype)
    v_out[...] = kvv[:, 1024:1536].astype(v_out.dtype)

    for t in range(3):
        dqkv_out[:, t * 512:(t + 1) * 512] = z(OFF_DQKV + t * 512, 512).astype(dqkv_out.dtype)
    ab_out[...] = z(OFF_AB, LANES)
    dz_out[...] = z(OFF_DZ, DN_WIDTH).astype(dz_out.dtype)

    zq = z(OFF_GQ, 1024)
    for p in range(4):
        sl = slice(p * LANES, (p + 1) * LANES)
        gq_out[:, sl] = ((zq[:, sl] * cg + zq[:, 512 + p * LANES:512 + (p + 1) * LANES] * sg)
                         * (GQA_SCALE * LOG2E)).astype(gq_out.dtype)
    zk = z(OFF_GK, 768)
    for j in range(2):
        sl = slice(j * LANES, (j + 1) * LANES)
        gkv_out[:, sl] = (zk[:, sl] * cg + zk[:, 256 + j * LANES:256 + (j + 1) * LANES] * sg).astype(gkv_out.dtype)
    gkv_out[:, 256:512] = zk[:, 512:768].astype(gkv_out.dtype)

    for t in range(3):
        gates_out[:, t * 1024:(t + 1) * 1024] = jax.nn.sigmoid(z(OFF_GATES + t * 1024, 1024)).astype(gates_out.dtype)


def _inproj_call(xs, mod_l, w_cat, wuq_cat, wukv_cat, qn, kvn, tabs, B, S, C):
    T, D = xs.shape
    tpb, ctiles = S // TM, C // TM
    act = MXU_DTYPE
    row = lambda i: (i, 0)
    const = lambda i: (0, 0)
    tab = lambda i: (i % tpb, 0)
    widths = (1024, 1024, 512, 1536, LANES, 512, 512, 512, 3072)
    dtypes = (act, act, act, act, F32, act, act, act, act)
    return pl.pallas_call(
        _inproj_kernel,
        grid=(T // TM,),
        in_specs=[pl.BlockSpec((TM, D), row),
                  pl.BlockSpec((1, 1, D), _mod_index(tpb, ctiles, 0)),
                  pl.BlockSpec((1, 1, D), _mod_index(tpb, ctiles, 1)),
                  pl.BlockSpec((D, NZ), const),
                  pl.BlockSpec((MLA_Q_LORA, 2048), const),
                  pl.BlockSpec((MLA_KV_LORA, 1536), const),
                  pl.BlockSpec((1, MLA_Q_LORA), const),
                  pl.BlockSpec((1, MLA_KV_LORA), const)]
                 + [pl.BlockSpec((TM, LANES), tab)] * 4,
        out_specs=[pl.BlockSpec((TM, w), row) for w in widths],
        out_shape=[jax.ShapeDtypeStruct((T, w), dt) for w, dt in zip(widths, dtypes)],
        compiler_params=_params("parallel"),
        name="in_proj",
    )(xs, mod_l, mod_l, w_cat, wuq_cat, wukv_cat, qn, kvn, *tabs)


def _mla_kernel(q_ref, k_ref, v_ref, o_ref, *, n_ctx, n_all):
    i = pl.program_id(2)
    tq = q_ref.shape[1]
    left = lax.broadcasted_iota(jnp.int32, (tq, LANES), 1) < MLA_V

    def attend(nk, ctx_rows):
        v = v_ref[0, 0:nk, :]
        v_left = lax.broadcasted_iota(jnp.int32, v.shape, 1) < MLA_V
        ones = jnp.ones_like(v)
        s = [_mm_nt(q_ref[0, :, hh * LANES:(hh + 1) * LANES], k_ref[0, 0:nk, hh * LANES:(hh + 1) * LANES])
             for hh in range(2)]
        if ctx_rows:
            row = lax.broadcasted_iota(jnp.int32, (tq, nk), 0)
            col = lax.broadcasted_iota(jnp.int32, (tq, nk), 1)
            ok = (row >= ctx_rows) | (col < ctx_rows)
            s = [jnp.where(ok, x, -jnp.inf) for x in s]
        p = [jnp.exp2((x - jnp.max(x, -1, keepdims=True)).astype(MXU_DTYPE)) for x in s]
        o = [_mm(p[0], jnp.where(v_left, v, ones)), _mm(p[1], jnp.where(v_left, ones, v))]
        outs = [x / pltpu.roll(x, MLA_V, axis=1) for x in o]
        o_ref[0] = jnp.where(left, outs[0], outs[1]).astype(o_ref.dtype)

    if tq > n_ctx:
        @pl.when(i == 0)
        def _():
            attend(n_all, n_ctx)

        @pl.when(i > 0)
        def _():
            attend(n_all, 0)
    else:
        ctx_tiles = n_ctx // tq

        @pl.when(i < ctx_tiles)
        def _():
            attend(n_ctx, 0)

        @pl.when(i >= ctx_tiles)
        def _():
            attend(n_all, 0)


MLA_TQ = (544, 384, 256)


def _mla_call(q, k, v, B, S, C):
    tq = next(t for t in MLA_TQ if S % t == 0)
    assert C % tq == 0 or tq > C
    q3, k3, v3 = q.reshape(B, S, 1024), k.reshape(B, S, 1024), v.reshape(B, S, 512)
    out = pl.pallas_call(
        functools.partial(_mla_kernel, n_ctx=C, n_all=S),
        grid=(B, MLA_HEADS // 2, S // tq),
        in_specs=[pl.BlockSpec((1, tq, 2 * LANES), lambda b, j, i: (b, i, j)),
                  pl.BlockSpec((1, S, 2 * LANES), lambda b, j, i: (b, 0, j)),
                  pl.BlockSpec((1, S, LANES), lambda b, j, i: (b, 0, j))],
        out_specs=pl.BlockSpec((1, tq, LANES), lambda b, j, i: (b, i, j)),
        out_shape=jax.ShapeDtypeStruct((B, S, 512), MXU_DTYPE),
        compiler_params=_params("parallel", "parallel", "arbitrary"),
        name="mla_attn",
    )(q3, k3, v3)
    return out.reshape(B * S, 512)


def _gqa_kernel(sink_ref, q_ref, kv_ref, o_ref, *, n_ctx, n_all):
    i = pl.program_id(1)
    qb_rows = q_ref.shape[1]
    span = qb_rows + 2 * WINDOW
    ctx_blocks = n_ctx // qb_rows
    group = GQA_HEADS // GQA_KV_HEADS
    left = lax.broadcasted_iota(jnp.int32, (qb_rows, LANES), 1) < GQA_HEAD_DIM

    def run(latent):
        nk = n_ctx + span if latent else n_ctx
        rows = group * qb_rows
        if latent:
            qb = i - ctx_blocks
            ws = pl.multiple_of(jnp.minimum((qb + 1) * qb_rows, n_all - span), qb_rows)
            col = lax.broadcasted_iota(jnp.int32, (rows, nk), 1)
            q_pos = qb * qb_rows + lax.broadcasted_iota(jnp.int32, (rows, nk), 0) % qb_rows
            k_pos = ws - 2 * n_ctx + col
            ok = (col < n_ctx) | ((jnp.abs(k_pos - q_pos) <= WINDOW) & (k_pos >= 0))
        head_of_row = lax.broadcasted_iota(jnp.int32, (rows, 1), 0) // qb_rows
        res = []
        for j in range(GQA_KV_HEADS):
            parts = []
            for g in range(group):
                qp = q_ref[0, :, (2 * j + g // 2) * LANES:(2 * j + g // 2 + 1) * LANES]
                parts.append(jnp.where(left if g % 2 == 0 else ~left, qp, jnp.zeros_like(qp)))
            q4 = jnp.concatenate(parts, axis=0)
            kc = kv_ref[0, 0:n_ctx, j * LANES:(j + 1) * LANES]
            vc = kv_ref[0, 0:n_ctx, 256 + j * LANES:256 + (j + 1) * LANES]
            if latent:
                kc = jnp.concatenate([kc, kv_ref[0, pl.ds(ws, span), j * LANES:(j + 1) * LANES]], axis=0)
                vc = jnp.concatenate([vc, kv_ref[0, pl.ds(ws, span), 256 + j * LANES:256 + (j + 1) * LANES]], axis=0)
            sink = jnp.zeros((rows, 1), F32)
            for g in range(group):
                sink = jnp.where(head_of_row == g, sink_ref[group * j + g] * LOG2E, sink)
            s = _mm_nt(q4, kc)
            if latent:
                s = jnp.where(ok, s, -jnp.inf)
            m = jnp.maximum(jnp.max(s, -1, keepdims=True), sink)
            p = jnp.exp2((s - m).astype(MXU_DTYPE))
            v_left = lax.broadcasted_iota(jnp.int32, vc.shape, 1) < GQA_HEAD_DIM
            o = _mm(p, jnp.where(v_left, vc, jnp.ones_like(vc)))
            res.append(o / (pltpu.roll(o, GQA_HEAD_DIM, axis=1) + jnp.exp2(sink - m)))
        for j in range(GQA_KV_HEADS):
            for pp in range(group // 2):
                r0 = res[j][(2 * pp) * qb_rows:(2 * pp + 1) * qb_rows]
                r1 = res[j][(2 * pp + 1) * qb_rows:(2 * pp + 2) * qb_rows]
                pair = 2 * j + pp
                o_ref[0, :, pair * LANES:(pair + 1) * LANES] = jnp.where(
                    left, r0, pltpu.roll(r1, GQA_HEAD_DIM, axis=1)).astype(o_ref.dtype)

    @pl.when(i < ctx_blocks)
    def _():
        run(False)

    @pl.when(i >= ctx_blocks)
    def _():
        run(True)


def _gqa_call(gq, gkv, sink, B, S, C):
    qb_rows = 128
    out = pl.pallas_call(
        functools.partial(_gqa_kernel, n_ctx=C, n_all=S),
        grid=(B, S // qb_rows),
        in_specs=[pl.BlockSpec(memory_space=pltpu.SMEM),
                  pl.BlockSpec((1, qb_rows, 512), lambda b, i: (b, i, 0)),
                  pl.BlockSpec((1, S, 512), lambda b, i: (b, 0, 0))],
        out_specs=pl.BlockSpec((1, qb_rows, 512), lambda b, i: (b, i, 0)),
        out_shape=jax.ShapeDtypeStruct((B, S, 512), MXU_DTYPE),
        compiler_params=_params("parallel", "arbitrary"),
        name="gqa_attn",
    )(sink, gq.reshape(B, S, 512), gkv.reshape(B, S, 512))
    return out.reshape(B * S, 512)


DN_TR = 256
DN_HALO = 16
DN_CPT = DN_TR // DN_CHUNK


def _stack(x, left):
    z = jnp.zeros_like(x)
    return jnp.concatenate([jnp.where(left, x, z), jnp.where(left, z, x)], axis=1)


def _dnlocal_kernel(main_ref, prev_ref, next_ref, ab_ref, conv_ref, gp_ref, bd_ref, trif_ref, trib_ref,
                    eg_ref, eb_ref,
                    u_out, w_out, qg_out, kg_out, qk_out, gl_out, pad_ref, *, ctx_tiles, n_tiles):
    i = pl.program_id(1)
    tr = DN_TR
    first = (i == 0) | (i == ctx_tiles)
    last = (i == ctx_tiles - 1) | (i == n_tiles - 1)
    xp = prev_ref[0].astype(F32)
    xn = next_ref[0].astype(F32)
    pad_ref[0:DN_HALO, :] = jnp.where(first, jnp.zeros_like(xp), xp)
    pad_ref[DN_HALO:DN_HALO + tr, :] = main_ref[0].astype(F32)
    pad_ref[DN_HALO + tr:, :] = jnp.where(last, jnp.zeros_like(xn), xn)
    y = jnp.zeros((tr, 3 * DN_WIDTH), F32)
    for t in range(DN_CONV):
        y = y + conv_ref[t:t + 1, :] * pad_ref[pl.ds(DN_HALO - DN_CONV // 2 + t, tr), :]
    y = _silu(y)
    q, k, v = y[:, 0:512], y[:, 512:1024], y[:, 1024:1536]
    bd = bd_ref[...]
    q = q * lax.rsqrt(_exact_mm(q * q, bd) + 1e-6) * (DN_HEAD_DIM ** -0.5)
    k = k * lax.rsqrt(_exact_mm(k * k, bd) + 1e-6)

    ab = ab_ref[0]
    g = -jnp.exp(gp_ref[0:1, :]) * jax.nn.softplus(ab + gp_ref[1:2, :])
    beta = jax.nn.sigmoid(ab)
    lane = lax.broadcasted_iota(jnp.int32, (tr, LANES), 1)
    gc = jnp.where(lane < DN_HEADS, _exact_mm_left(trif_ref[...], g), _exact_mm_left(trib_ref[...], g))
    gcx_all = _exact_mm(gc, eg_ref[...])
    bx_all = _exact_mm(beta, eb_ref[...])

    c = DN_CPT
    lane3 = lax.broadcasted_iota(jnp.int32, (1, 1, LANES), 2)
    left = (lane3 % LANES) < DN_HEAD_DIM
    tpos = lane3 % DN_HEAD_DIM
    lane6 = lax.broadcasted_iota(jnp.int32, (1, 1, 2 * LANES), 2)
    left6 = (lane6 % LANES) < DN_HEAD_DIM
    ri = lax.broadcasted_iota(jnp.int32, (1, DN_CHUNK, LANES), 1)
    cj = lax.broadcasted_iota(jnp.int32, (1, DN_CHUNK, LANES), 2) % DN_HEAD_DIM
    one = jnp.ones((), F32)
    zero = jnp.zeros((), F32)

    def process(units):
        st = {}
        for d, j in units:
            off = d * 512 + j * LANES
            gcx = gcx_all[:, off:off + LANES].reshape(c, DN_CHUNK, LANES)
            bx = bx_all[:, off:off + LANES].reshape(c, DN_CHUNK, LANES)
            qp = q[:, j * LANES:(j + 1) * LANES].reshape(c, DN_CHUNK, LANES)
            kp = k[:, j * LANES:(j + 1) * LANES].reshape(c, DN_CHUNK, LANES)
            vp = v[:, j * LANES:(j + 1) * LANES].reshape(c, DN_CHUNK, LANES)
            gl = gcx[:, DN_CHUNK - 1:DN_CHUNK, :] if d == 0 else gcx[:, 0:1, :]
            kb = kp * bx
            kq = _bmm_nt(jnp.concatenate([kb, qp], axis=1), _stack(kp, left))
            hi, mid, lo = _split3(gcx)
            a6 = jnp.where(tpos == 0, hi, jnp.where(tpos == 1, mid, jnp.where(tpos == 2, lo,
                           jnp.where(tpos < 6, one, zero))))
            b6 = jnp.where(tpos < 3, one, jnp.where(tpos == 3, -hi, jnp.where(tpos == 4, -mid,
                           jnp.where(tpos == 5, -lo, zero))))
            diff = _bmm_nt(a6, _stack(b6, left))
            st[d, j] = dict(gcx=gcx, bx=bx, qp=qp, kp=kp, vp=vp, gl=gl, kb=kb, kq=kq, diff=diff)
        for d, j in units:
            u = st[d, j]
            incl = (ri >= cj) if d == 0 else (ri <= cj)
            strict = (ri > cj) if d == 0 else (ri < cj)
            dm = jnp.exp(jnp.where(incl, u["diff"], -jnp.inf))
            u["qkm"] = u["kq"][:, DN_CHUNK:, :] * dm
            u["x"] = -jnp.where(strict, u["kq"][:, 0:DN_CHUNK, :] * dm, zero)
            u["r"] = u["x"]
        for d, j in units:
            u = st[d, j]
            u["x"] = _bmm(u["x"], _stack(u["x"], left))
        for level in range(5):
            for d, j in units:
                u = st[d, j]
                xs = _stack(u["x"], left)
                if level < 4:
                    m = _bmm(jnp.concatenate([u["r"], u["x"]], axis=1), xs)
                    u["r"] = u["r"] + u["x"] + m[:, 0:DN_CHUNK, :]
                    u["x"] = m[:, DN_CHUNK:, :]
                else:
                    u["r"] = u["r"] + u["x"] + _bmm(u["r"], xs)
        for d, j in units:
            u = st[d, j]
            eg = jnp.exp(u["gcx"])
            rhs = jnp.concatenate([u["vp"] * u["bx"], u["kb"] * eg], axis=-1)
            sol = rhs + _bmm(u["r"], _stack(rhs, left6))
            sl = slice(j * LANES, (j + 1) * LANES)
            u_out[0, d, :, sl] = sol[:, :, 0:LANES].reshape(tr, LANES)
            w_out[0, d, :, sl] = sol[:, :, LANES:].reshape(tr, LANES).astype(w_out.dtype)
            qg_out[0, d, :, sl] = (u["qp"] * eg).reshape(tr, LANES).astype(qg_out.dtype)
            kg_out[0, d, :, sl] = (u["kp"] * jnp.exp(u["gl"] - u["gcx"])).reshape(tr, LANES).astype(kg_out.dtype)
            qk_out[0, d, :, sl] = u["qkm"].reshape(tr, LANES).astype(qk_out.dtype)
            gl_out[0, d, :, :, sl] = jnp.exp(u["gl"])

    process([(d, j) for d in range(2) for j in range(4)])


def _dn_constants():
    idx = np.arange(DN_TR)
    same = (idx[:, None] // DN_CHUNK) == (idx[None, :] // DN_CHUNK)
    trif = (same & (idx[None, :] <= idx[:, None])).astype(np.float32)
    trib = (same & (idx[None, :] >= idx[:, None])).astype(np.float32)
    h = np.arange(512)
    bd = ((h[:, None] // DN_HEAD_DIM) == (h[None, :] // DN_HEAD_DIM)).astype(np.float32)
    col = np.arange(LANES)[:, None]
    out = np.arange(1024)[None, :]
    unit = (out // 512) * DN_HEADS + (out % 512) // DN_HEAD_DIM
    eg = (col == unit).astype(np.float32)
    eb = (col == unit + 2 * DN_HEADS).astype(np.float32)
    return tuple(jnp.asarray(a, MXU_DTYPE) for a in (bd, trif, trib, eg, eb))


def _dnlocal_call(dqkv, ab, conv_w, gp, consts, B, S, C):
    tr = DN_TR
    n_tiles = S // tr
    hb = tr // DN_HALO
    n_hblk = S // DN_HALO
    bd, trif, trib, eg, eb = consts
    const2 = lambda b, i: (0, 0)
    big = lambda b, i: (b, 0, i, 0)
    act = MXU_DTYPE
    shp = (B, 2, S, 512)
    return pl.pallas_call(
        functools.partial(_dnlocal_kernel, ctx_tiles=C // tr, n_tiles=n_tiles),
        grid=(B, n_tiles),
        in_specs=[pl.BlockSpec((1, tr, 1536), lambda b, i: (b, i, 0)),
                  pl.BlockSpec((1, DN_HALO, 1536), lambda b, i: (b, jnp.maximum(i * hb - 1, 0), 0)),
                  pl.BlockSpec((1, DN_HALO, 1536), lambda b, i: (b, jnp.minimum((i + 1) * hb, n_hblk - 1), 0)),
                  pl.BlockSpec((1, tr, LANES), lambda b, i: (b, i, 0)),
                  pl.BlockSpec((8, 1536), const2),
                  pl.BlockSpec((8, LANES), const2),
                  pl.BlockSpec((512, 512), const2),
                  pl.BlockSpec((tr, tr), const2),
                  pl.BlockSpec((tr, tr), const2),
                  pl.BlockSpec((LANES, 1024), const2),
                  pl.BlockSpec((LANES, 1024), const2)],
        out_specs=[pl.BlockSpec((1, 2, tr, 512), big)] * 5
                  + [pl.BlockSpec((1, 2, DN_CPT, 1, 512), lambda b, i: (b, 0, i, 0, 0))],
        out_shape=[jax.ShapeDtypeStruct(shp, F32)] + [jax.ShapeDtypeStruct(shp, act)] * 4
                  + [jax.ShapeDtypeStruct((B, 2, S // DN_CHUNK, 1, 512), F32)],
        scratch_shapes=[pltpu.VMEM((tr + 2 * DN_HALO, 1536), F32)],
        compiler_params=_params("parallel", "parallel"),
        name="dn_local",
    )(dqkv.reshape(B, S, 1536), dqkv.reshape(B, S, 1536), dqkv.reshape(B, S, 1536), ab.reshape(B, S, LANES),
      conv_w, gp, bd, trif, trib, eg, eb)


def _dnscan_kernel(uf, wf, qgf, kgf, qkf, glf, ub, wb, qgb, kgb, qkb, glb, of_out, ob_out, s_ref):
    n = pl.program_id(1)
    lane = lax.broadcasted_iota(jnp.int32, (1, LANES), 1)
    left = lane < DN_HEAD_DIM
    row = lax.broadcasted_iota(jnp.int32, (LANES, LANES), 0)
    col = lax.broadcasted_iota(jnp.int32, (LANES, LANES), 1)
    same_head = (row < DN_HEAD_DIM) == (col < DN_HEAD_DIM)
    dirs = ((uf, wf, qgf, kgf, qkf, glf, of_out), (ub, wb, qgb, kgb, qkb, glb, ob_out))
    units = [(bb, d, j) for bb in range(uf.shape[0]) for d in range(2) for j in range(4)]
    sidx = lambda bb, d, j: (bb * 2 + d) * 4 + j
    sl = lambda j: slice(j * LANES, (j + 1) * LANES)
    started = n > 0
    st = {t: jnp.where(started, s_ref[sidx(*t)], jnp.zeros((LANES, LANES), F32)) for t in units}
    pre = {(bb, d, j): _mm(dirs[d][1][bb, 0, :, sl(j)], st[bb, d, j]) for bb, d, j in units}
    o1 = {(bb, d, j): _mm(dirs[d][2][bb, 0, :, sl(j)], st[bb, d, j]) for bb, d, j in units}
    vn = {(bb, d, j): dirs[d][0][bb, 0, :, sl(j)] - pre[bb, d, j] for bb, d, j in units}
    for bb, d, j in units:
        v = vn[bb, d, j]
        z = jnp.zeros_like(v)
        vst = jnp.concatenate([jnp.where(left, v, z), jnp.where(left, z, v)], axis=0)
        dirs[d][6][bb, :, sl(j)] = o1[bb, d, j] + _mm(dirs[d][4][bb, 0, :, sl(j)], vst)
    for bb, d, j in units:
        upd = _mm_tn(dirs[d][3][bb, 0, :, sl(j)], vn[bb, d, j])
        s_ref[sidx(bb, d, j)] = (st[bb, d, j] * dirs[d][5][bb, 0, 0, :, sl(j)]
                                 + jnp.where(same_head, upd, jnp.zeros_like(upd)))


DN_SCAN_BATCH = 4


def _dnscan_call(local, B, S, C):
    u, w, qg, kg, qk, gl = local
    nch, nc = S // DN_CHUNK, C // DN_CHUNK
    bb = DN_SCAN_BATCH if B % DN_SCAN_BATCH == 0 else 1

    def bidx(n):
        return jnp.where(n < nc, nc - 1 - n, nch - 1 + nc - n)

    fspec = pl.BlockSpec((bb, 1, DN_CHUNK, 512), lambda b, n: (b, 0, n, 0))
    bspec = pl.BlockSpec((bb, 1, DN_CHUNK, 512), lambda b, n: (b, 1, bidx(n), 0))
    fgl = pl.BlockSpec((bb, 1, 1, 1, 512), lambda b, n: (b, 0, n, 0, 0))
    bgl = pl.BlockSpec((bb, 1, 1, 1, 512), lambda b, n: (b, 1, bidx(n), 0, 0))
    return pl.pallas_call(
        _dnscan_kernel,
        grid=(B // bb, nch),
        in_specs=[fspec] * 5 + [fgl] + [bspec] * 5 + [bgl],
        out_specs=[pl.BlockSpec((bb, DN_CHUNK, 512), lambda b, n: (b, n, 0)),
                   pl.BlockSpec((bb, DN_CHUNK, 512), lambda b, n: (b, bidx(n), 0))],
        out_shape=[jax.ShapeDtypeStruct((B, S, 512), F32)] * 2,
        scratch_shapes=[pltpu.VMEM((bb * 8, LANES, LANES), F32)],
        compiler_params=_params("parallel", "arbitrary"),
        name="dn_scan",
    )(u, w, qg, kg, qk, gl, u, w, qg, kg, qk, gl)


def _pack_pairs(v):
    w = v.shape[1] // 2
    bits = lax.bitcast_convert_type(v.astype(jnp.bfloat16).astype(F32), jnp.int32)
    return lax.shift_right_logical(bits[:, :w], 16) | bits[:, w:]


def _unpack_pairs(p):
    lo = lax.bitcast_convert_type(lax.shift_left(p, 16), F32)
    hi = lax.bitcast_convert_type(p & jnp.int32(-65536), F32)
    return lo, hi


def _merge_kernel(x_ref, ga_ref, shf_ref, scf_ref, omla_ref, of_ref, ob_ref, dz_ref, ogqa_ref, gates_ref,
                  wo1_ref, wo2_ref, wo3_ref, wout_ref, dnn_ref, bd_ref, lng_ref, lnb_ref, wr_ref, rb_ref,
                  x1_out, ufp_out, eidx_out, rank_out, ew_out, cnt_out, cnt_ref):
    @pl.when(pl.program_id(0) == 0)
    def _():
        cnt_ref[...] = jnp.zeros_like(cnt_ref)

    o = of_ref[...] + ob_ref[...]
    ms = _exact_mm(o * o, bd_ref[...]) * (1.0 / DN_HEAD_DIM)
    dn = o * lax.rsqrt(ms + NORM_EPS) * dnn_ref[...] * _silu(dz_ref[...].astype(F32))
    g1 = gates_ref[:, 0:1024].astype(F32)
    g2 = gates_ref[:, 1024:2048].astype(F32)
    g3 = gates_ref[:, 2048:3072].astype(F32)
    m = (g1 * _mm(omla_ref[...], wo1_ref[...]) + g2 * _mm(dn, wo2_ref[...])
         + g3 * _mm(ogqa_ref[...], wo3_ref[...]))
    y = _mm(m, wout_ref[...])
    x1 = _layernorm(DEEPNORM_ALPHA * x_ref[...] + ga_ref[0] * y, lng_ref[...], lnb_ref[...])
    x1_out[...] = x1
    uf = x1 * (1.0 + scf_ref[0]) + shf_ref[0]
    _split_pieces(_pack_pairs(uf), ufp_out)

    tm = uf.shape[0]
    scores = jax.nn.sigmoid(_mm_nt(wr_ref[...], uf))[0:N_EXPERTS]
    sel = scores + rb_ref[0:N_EXPERTS, :]
    gsz = N_EXPERTS // N_GROUPS
    neg = jnp.full((), -jnp.inf, F32)
    sel3 = sel.reshape(N_GROUPS, gsz, tm)
    mem = lax.broadcasted_iota(jnp.int32, (N_GROUPS, gsz, tm), 1)
    m1 = jnp.max(sel3, 1, keepdims=True)
    i1 = jnp.min(jnp.where(sel3 == m1, mem, gsz), 1, keepdims=True)
    m2 = jnp.max(jnp.where(mem == i1, neg, sel3), 1, keepdims=True)
    gs = (m1 + m2).reshape(N_GROUPS, tm)
    gi = lax.broadcasted_iota(jnp.int32, (N_GROUPS, tm), 0)
    grank = jnp.zeros((N_GROUPS, tm), jnp.int32)
    for gp in range(N_GROUPS):
        other = gs[gp:gp + 1, :]
        beats = (other > gs) | ((other == gs) & (gp < gi))
        grank = grank + beats.astype(jnp.int32)
    gsel = (grank < TOPK_GROUPS).reshape(N_GROUPS, 1, tm)
    cur = jnp.where(gsel, sel3, neg).reshape(N_EXPERTS, tm)
    ei = lax.broadcasted_iota(jnp.int32, (N_EXPERTS, tm), 0)
    zero = jnp.zeros((N_EXPERTS, tm), F32)
    one = jnp.ones((N_EXPERTS, tm), F32)
    chosen = zero
    picks = []
    for _ in range(TOP_K):
        mx = jnp.max(cur, 0, keepdims=True)
        ix = jnp.min(jnp.where(cur == mx, ei, N_EXPERTS), 0, keepdims=True)
        pick = ei == ix
        picks.append((ix, pick))
        chosen = chosen + jnp.where(pick, one, zero)
        cur = jnp.where(pick, neg, cur)

    r_i = lax.broadcasted_iota(jnp.int32, (tm, tm), 0)
    c_i = lax.broadcasted_iota(jnp.int32, (tm, tm), 1)
    before = jnp.where(r_i < c_i, 1.0, 0.0)
    pos = cnt_ref[:, 0:1] + _mm(chosen, before)
    cnt_new = cnt_ref[...] + jnp.sum(chosen, 1, keepdims=True)
    cnt_ref[...] = cnt_new
    cnt_out[...] = cnt_new

    w_rows = [jnp.sum(jnp.where(pick, scores, zero), 0, keepdims=True) for _, pick in picks]
    wsum = w_rows[0]
    for w_k in w_rows[1:]:
        wsum = wsum + w_k
    eidx_out[...] = jnp.concatenate([ix for ix, _ in picks], axis=0)
    rank_out[...] = jnp.concatenate(
        [jnp.sum(jnp.where(pick, pos, zero), 0, keepdims=True) for _, pick in picks], axis=0).astype(jnp.int32)
    w8 = jnp.concatenate([w_k / wsum * ROUTED_SCALE for w_k in w_rows], axis=0)
    ew_out[...] = jnp.concatenate([w8, jnp.zeros((LANES - TOP_K, tm), F32)], axis=0).T


def _merge_call(xs, mod_l, omla, of, ob, dz, ogqa, gates, wo1, wo2, wo3, wout, dnn, bd, lng, lnb, wr, rb, B, S, C,
                part, n_parts):
    t_all, D = xs.shape
    T = t_all // n_parts
    first = part * (T // TM)
    tpb, ctiles = S // TM, C // TM
    row = lambda i: (i, 0)
    src = lambda i: (i + first, 0)
    const = lambda i: (0, 0)
    modspec = lambda k: pl.BlockSpec((1, 1, D), _mod_index(tpb, ctiles, k, first))
    return pl.pallas_call(
        _merge_kernel,
        grid=(T // TM,),
        in_specs=[pl.BlockSpec((TM, D), src), modspec(2), modspec(3), modspec(4),
                  pl.BlockSpec((TM, 512), src), pl.BlockSpec((TM, 512), src), pl.BlockSpec((TM, 512), src),
                  pl.BlockSpec((TM, 512), src), pl.BlockSpec((TM, 512), src), pl.BlockSpec((TM, 3072), src),
                  pl.BlockSpec((512, D), const), pl.BlockSpec((512, D), const), pl.BlockSpec((512, D), const),
                  pl.BlockSpec((D, D), const), pl.BlockSpec((1, 512), const), pl.BlockSpec((512, 512), const),
                  pl.BlockSpec((1, D), const), pl.BlockSpec((1, D), const),
                  pl.BlockSpec((LANES, D), const), pl.BlockSpec((LANES, 1), const)],
        out_specs=[pl.BlockSpec((TM, D), row), pl.BlockSpec((N_PIECES, TM, PIECE), lambda i: (0, i, 0)),
                   pl.BlockSpec((TOP_K, TM), lambda i: (0, i)), pl.BlockSpec((TOP_K, TM), lambda i: (0, i)),
                   pl.BlockSpec((TM, LANES), row), pl.BlockSpec((N_EXPERTS, LANES), const)],
        out_shape=[jax.ShapeDtypeStruct((T, D), F32), jax.ShapeDtypeStruct((N_PIECES, T, PIECE), jnp.int32),
                   jax.ShapeDtypeStruct((TOP_K, T), jnp.int32), jax.ShapeDtypeStruct((TOP_K, T), jnp.int32),
                   jax.ShapeDtypeStruct((T, LANES), F32), jax.ShapeDtypeStruct((N_EXPERTS, LANES), F32)],
        scratch_shapes=[pltpu.VMEM((N_EXPERTS, LANES), F32)],
        compiler_params=_params("arbitrary"),
        name="merge_norm_route",
    )(xs, mod_l, mod_l, mod_l, omla, of.reshape(t_all, 512), ob.reshape(t_all, 512), dz, ogqa, gates,
      wo1, wo2, wo3, wout, dnn, bd, lng, lnb, wr, rb)


EXPERT_BLOCK = 512
MOE_PARTS = 2
SC_WINDOW = 128
N_PIECES = 2
PIECE = D_MODEL // 2 // N_PIECES


def _split_pieces(packed, out_ref):
    for h in range(N_PIECES):
        out_ref[h] = packed[:, h * PIECE:(h + 1) * PIECE]


def _mm_pieces(pieces, w):
    acc = None
    for h, (lo, hi) in enumerate(pieces):
        t = (_mm(lo, w[h * PIECE:(h + 1) * PIECE, :])
             + _mm(hi, w[D_MODEL // 2 + h * PIECE:D_MODEL // 2 + (h + 1) * PIECE, :]))
        acc = t if acc is None else acc + t
    return acc


def _sc_mesh():
    return plsc.VectorSubcoreMesh(core_axis_name="c", subcore_axis_name="s")


def _sc_gather_rows(y, idx):
    n = idx.shape[1]
    W = y.shape[1]

    @pl.kernel(out_type=jax.ShapeDtypeStruct((n, W), y.dtype), mesh=_sc_mesh(), scratch_types=[])
    def gather(y_hbm, i_hbm, o_hbm):
        def body(i_vmem, o_vmem):
            pltpu.sync_copy(y_hbm.at[i_vmem.at[0]], o_vmem)

        pltpu.emit_pipeline(
            body,
            grid=(n // SC_WINDOW,),
            in_specs=[pl.BlockSpec((1, SC_WINDOW), lambda i: (0, i))],
            out_specs=[pl.BlockSpec((SC_WINDOW, W), lambda i: (i, 0))],
            core_axis_name=("c", "s"),
            dimension_semantics=(pltpu.PARALLEL,),
        )(i_hbm, o_hbm)

    return gather(y, idx)


SC_LANES = 16
SC_WORKERS = 32
SC_CHUNK = 2176


def _sc_invert_rows(dest, default, n_tok):
    n_rows = default.shape[0]
    per_w = n_rows // SC_WORKERS
    n_k = dest.shape[0] // n_tok
    assert n_rows % (SC_WORKERS * SC_LANES) == 0 and n_tok % SC_CHUNK == 0

    @pl.kernel(out_type=jax.ShapeDtypeStruct((n_rows,), jnp.int32), mesh=_sc_mesh(),
               scratch_types=[pltpu.VMEM((per_w,), jnp.int32), pltpu.VMEM((SC_CHUNK,), jnp.int32)],
               compiler_params=pltpu.CompilerParams(needs_layout_passes=False))
    def invert(dest_hbm, dflt_hbm, out_hbm, rows_v, dest_v):
        wid = lax.axis_index("s") * 2 + lax.axis_index("c")
        base = wid * per_w
        pltpu.sync_copy(dflt_hbm.at[pl.ds(base, per_w)], rows_v)
        lanes = lax.iota(jnp.int32, SC_LANES)
        for k in range(n_k):
            @pl.loop(0, n_tok // SC_CHUNK)
            def _(c):
                pltpu.sync_copy(dest_hbm.at[pl.ds(k * n_tok + c * SC_CHUNK, SC_CHUNK)], dest_v)

                @pl.loop(0, SC_CHUNK, step=SC_LANES)
                def _(o):
                    local = dest_v[pl.ds(o, SC_LANES)] - base
                    mine = (local >= 0) & (local < per_w)
                    plsc.store_scatter(rows_v, [jnp.where(mine, local, 0)], c * SC_CHUNK + o + lanes, mask=mine)

        pltpu.sync_copy(rows_v, out_hbm.at[pl.ds(base, per_w)])

    return invert(dest, default)


def _experts_kernel(be_ref, nv_ref, xb_ref, wg_ref, wu_ref, wd_ref, y_out):
    b = pl.program_id(0)
    nv = nv_ref[b]

    @pl.when(nv > 0)
    def _():
        rows = lax.broadcasted_iota(jnp.int32, xb_ref.shape[1:], 0)
        pieces = []
        for h in range(N_PIECES):
            xh = xb_ref[h]
            pieces.append(_unpack_pairs(jnp.where(rows < nv, xh, jnp.zeros_like(xh))))
        hid = _silu(_mm_pieces(pieces, wg_ref.at[0, 0])) * _mm_pieces(pieces, wu_ref.at[0, 0])
        _split_pieces(_pack_pairs(_mm(hid, wd_ref[0, 0])), y_out)


def _experts_call(xb, block_e, nvalid, wg, wu, wd, layer):
    _, R, _ = xb.shape
    D = D_MODEL
    blk = (N_PIECES, EXPERT_BLOCK, PIECE)
    grid_spec = pltpu.PrefetchScalarGridSpec(
        num_scalar_prefetch=2,
        grid=(R // EXPERT_BLOCK,),
        in_specs=[pl.BlockSpec(blk, lambda b, be, nv: (0, b, 0)),
                  pl.BlockSpec((1, 1, D, EXPERT_DIM), lambda b, be, nv: (layer, be[b], 0, 0)),
                  pl.BlockSpec((1, 1, D, EXPERT_DIM), lambda b, be, nv: (layer, be[b], 0, 0)),
                  pl.BlockSpec((1, 1, EXPERT_DIM, D), lambda b, be, nv: (layer, be[b], 0, 0))],
        out_specs=pl.BlockSpec(blk, lambda b, be, nv: (0, b, 0)),
    )
    return pl.pallas_call(
        _experts_kernel,
        grid_spec=grid_spec,
        out_shape=jax.ShapeDtypeStruct((N_PIECES, R, PIECE), jnp.int32),
        compiler_params=_params("arbitrary"),
        name="moe_experts",
    )(block_e, nvalid, xb, wg, wu, wd)


def _combine_kernel(x_ref, ufp_ref, yg_ref, ew_ref, gf_ref, sg_ref, su_ref, sd_ref, g_ref, b_ref, o_ref):
    pieces = [_unpack_pairs(ufp_ref[h]) for h in range(N_PIECES)]
    hs = _silu(_mm_pieces(pieces, sg_ref)) * _mm_pieces(pieces, su_ref)
    f = _mm(hs, sd_ref[...])
    ew = ew_ref[...]
    lane = lax.broadcasted_iota(jnp.int32, ew.shape, 1)
    acc = [[jnp.zeros((x_ref.shape[0], PIECE), F32) for _ in range(N_PIECES)] for _ in range(2)]
    for k in range(TOP_K):
        wk = jnp.sum(jnp.where(lane == k, ew, jnp.zeros_like(ew)), axis=1, keepdims=True)
        for h in range(N_PIECES):
            ylo, yhi = _unpack_pairs(yg_ref[h, k])
            acc[0][h] = acc[0][h] + wk * ylo
            acc[1][h] = acc[1][h] + wk * yhi
    f = f + jnp.concatenate(acc[0] + acc[1], axis=1)
    o_ref[...] = _layernorm(DEEPNORM_ALPHA * x_ref[...] + gf_ref[0] * f, g_ref[...], b_ref[...])


def _combine_call(x1, ufp, yg, ew, mod_l, sg, su, sd, g, b, B, S, C, part):
    T, D = x1.shape
    tpb, ctiles = S // TM, C // TM
    first = part * (T // TM)
    row = lambda i: (i, 0)
    const = lambda i: (0, 0)
    return pl.pallas_call(
        _combine_kernel,
        grid=(T // TM,),
        in_specs=[pl.BlockSpec((TM, D), row), pl.BlockSpec((N_PIECES, TM, PIECE), lambda i: (0, i, 0)),
                  pl.BlockSpec((N_PIECES, TOP_K, TM, PIECE), lambda i: (0, 0, i, 0)),
                  pl.BlockSpec((TM, LANES), row),
                  pl.BlockSpec((1, 1, D), _mod_index(tpb, ctiles, 5, first)),
                  pl.BlockSpec((D, SHARED_DIM), const), pl.BlockSpec((D, SHARED_DIM), const),
                  pl.BlockSpec((SHARED_DIM, D), const),
                  pl.BlockSpec((1, D), const), pl.BlockSpec((1, D), const)],
        out_specs=pl.BlockSpec((TM, D), row),
        out_shape=jax.ShapeDtypeStruct((T, D), F32),
        compiler_params=_params("parallel"),
        name="moe_combine_norm",
    )(x1, ufp, yg, ew, mod_l, sg, su, sd, g, b)


def _moe_dispatch(ufp, eidx_t, rank_t, counts):
    T = ufp.shape[1]
    n_blocks = -(-(T * TOP_K + N_EXPERTS * (EXPERT_BLOCK - 1)) // EXPERT_BLOCK)
    n_rows = n_blocks * EXPERT_BLOCK
    padded = (counts + EXPERT_BLOCK - 1) // EXPERT_BLOCK * EXPERT_BLOCK
    pad_end = jnp.cumsum(padded)
    start_pad = pad_end - padded
    experts = jnp.arange(N_EXPERTS, dtype=jnp.int32)

    def lookup(table, idx):
        sel = idx[None] == experts.reshape((N_EXPERTS,) + (1,) * idx.ndim)
        return jnp.sum(jnp.where(sel, table.reshape((N_EXPERTS,) + (1,) * idx.ndim), 0), axis=0)

    dest_t = lookup(start_pad, eidx_t) + rank_t
    blk = jnp.arange(n_blocks, dtype=jnp.int32) * EXPERT_BLOCK
    block_e = jnp.minimum(jnp.sum((blk[:, None] >= pad_end[None, :]).astype(jnp.int32), axis=1), N_EXPERTS - 1)
    nvalid = jnp.clip(lookup(counts, block_e) - (blk - lookup(start_pad, block_e)), 0, EXPERT_BLOCK)
    row_tok = _sc_invert_rows(dest_t.reshape(-1), jnp.arange(n_rows, dtype=jnp.int32) % T, T)
    piece = jnp.arange(N_PIECES, dtype=jnp.int32)
    src = (piece[:, None] * T + row_tok[None, :]).reshape(1, N_PIECES * n_rows)
    xb = _sc_gather_rows(ufp.reshape(N_PIECES * T, PIECE), src).reshape(N_PIECES, n_rows, PIECE)
    return xb, block_e.astype(jnp.int32), nvalid.astype(jnp.int32), dest_t


def _moe_return(yb, dest_t):
    _, n_rows, _ = yb.shape
    T = dest_t.shape[1]
    piece = jnp.arange(N_PIECES, dtype=jnp.int32)
    back = (piece[:, None, None] * n_rows + dest_t[None]).reshape(1, N_PIECES * TOP_K * T)
    yg = _sc_gather_rows(yb.reshape(N_PIECES * n_rows, PIECE), back)
    return yg.reshape(N_PIECES, TOP_K, T, PIECE)


def _rot_cols(w, half):
    return jnp.concatenate([-w[:, half:], w[:, :half]], axis=1)


def _prep_w_in(w):
    d = w.shape[0]
    offs = np.cumsum((0,) + IN_SIZES)
    cq, ckv, kr, dqkv, da, db, dz, gq, gk, gv, gates = (w[:, offs[t]:offs[t + 1]] for t in range(len(IN_SIZES)))
    z = lambda n: jnp.zeros((d, n), w.dtype)
    krg = jnp.concatenate([z(64), kr, z(32)], 1)
    krr = jnp.concatenate([z(64), _rot_cols(kr, MLA_ROPE // 2), z(32)], 1)
    ab = jnp.concatenate([da, db, z(LANES - 4 * DN_HEADS)], 1)
    hd = GQA_HEAD_DIM
    gq_rot = jnp.concatenate([_rot_cols(gq[:, h * hd:(h + 1) * hd], hd // 2) for h in range(GQA_HEADS)], 1)
    dup = lambda t: jnp.concatenate([t[:, 0:hd], t[:, 0:hd], t[:, hd:2 * hd], t[:, hd:2 * hd]], 1)
    gk_rot = jnp.concatenate([_rot_cols(gk[:, h * hd:(h + 1) * hd], hd // 2) for h in range(GQA_KV_HEADS)], 1)
    cat = jnp.concatenate([cq, ckv, krg, krr, dqkv, ab, dz, gq, gq_rot, dup(gk), dup(gk_rot), dup(gv), gates], 1)
    assert cat.shape[1] == NZ
    return cat.astype(MXU_DTYPE)


def _prep_w_uq(w):
    d = w.shape[0]
    hw = MLA_NOPE + MLA_ROPE
    a, b = [], []
    for h in range(MLA_HEADS):
        wh = w[:, h * hw:(h + 1) * hw]
        a += [wh, jnp.zeros((d, LANES - hw), w.dtype)]
        b += [jnp.zeros((d, MLA_NOPE), w.dtype), _rot_cols(wh[:, MLA_NOPE:], MLA_ROPE // 2),
              jnp.zeros((d, LANES - hw), w.dtype)]
    return jnp.concatenate(a + b, 1).astype(MXU_DTYPE)


def _prep_w_ukv(w):
    d = w.shape[0]
    hw = MLA_NOPE + MLA_V
    kpart, vpart = [], []
    for h in range(MLA_HEADS):
        wh = w[:, h * hw:(h + 1) * hw]
        kpart += [wh[:, :MLA_NOPE], jnp.zeros((d, LANES - MLA_NOPE), w.dtype)]
        vpart += [wh[:, MLA_NOPE:]]
    return jnp.concatenate(kpart + vpart, 1).astype(MXU_DTYPE)


def _rope_tables(n_rows, C):
    row = jnp.repeat(jnp.arange(n_rows, dtype=F32), GRID_W)
    col = jnp.tile(jnp.arange(GRID_W, dtype=F32), n_rows)

    def angles(dim):
        n = dim // 4
        inv = ROPE_BASE ** (-jnp.arange(n, dtype=F32) / n)
        return jnp.concatenate([row[:, None] * inv, col[:, None] * inv], axis=-1)

    def with_ctx(cos, sin):
        return (jnp.concatenate([jnp.ones((C, LANES), F32), cos], 0),
                jnp.concatenate([jnp.zeros((C, LANES), F32), sin], 0))

    L = n_rows * GRID_W
    am = angles(MLA_ROPE)
    one, zero = jnp.ones((L, MLA_NOPE), F32), jnp.zeros((L, MLA_NOPE), F32)
    cm = jnp.concatenate([one, jnp.cos(am), jnp.cos(am), one[:, :32]], 1)
    sm = jnp.concatenate([zero, jnp.sin(am), jnp.sin(am), zero[:, :32]], 1)
    ag = angles(GQA_HEAD_DIM)
    cg = jnp.tile(jnp.cos(ag), (1, 4))
    sg = jnp.tile(jnp.sin(ag), (1, 4))
    return with_ctx(cm, sm) + with_ctx(cg, sg)


def kernel(x, c, ctx, c_ctx, w_ada, b_ada, w_in, mla_q_norm, mla_kv_norm, w_uq, w_ukv, dn_conv, dn_a_log, dn_dt_bias, dn_norm, gqa_sink, w_o_mla, w_o_dn, w_o_gqa, w_out, ln1_g, ln1_b, w_router, router_bias, w_exp_gate, w_exp_up, w_exp_down, w_sh_gate, w_sh_up, w_sh_down, ln2_g, ln2_b):
    B, L, D = x.shape
    C = ctx.shape[1]
    S = C + L
    nl = w_in.shape[0]
    assert D == D_MODEL and nl == DEPTH and B <= CTX_MOD_ROW
    assert C % TM == 0 and L % TM == 0 and L % GRID_W == 0 and L >= 3 * WINDOW
    cast = lambda t: t.astype(MXU_DTYPE)

    cc = jnp.zeros((MOD_ROWS, D), F32).at[0:B].set(c).at[CTX_MOD_ROW].set(c_ctx)
    mods = _ada_call(cc, w_ada, b_ada)
    tabs = _rope_tables(L // GRID_W, C)
    dn_consts = _dn_constants()
    xs = jnp.concatenate([ctx, x], axis=1).reshape(B * S, D)

    for l in range(nl):
        mod_l = mods[l].reshape(MOD_ROWS * 6, 1, D)
        q, k, v, dqkv, ab, dz, gq, gkv, gates = _inproj_call(
            xs, mod_l, _prep_w_in(w_in[l]), _prep_w_uq(w_uq[l]), _prep_w_ukv(w_ukv[l]),
            mla_q_norm[l].reshape(1, -1), mla_kv_norm[l].reshape(1, -1), tabs, B, S, C)
        omla = _mla_call(q, k, v, B, S, C)
        conv8 = jnp.zeros((8, 3 * DN_WIDTH), F32).at[0:DN_CONV].set(dn_conv[l])
        gp = (jnp.zeros((8, LANES), F32).at[0, 0:2 * DN_HEADS].set(dn_a_log[l].reshape(-1))
              .at[1, 0:2 * DN_HEADS].set(dn_dt_bias[l].reshape(-1)))
        local = _dnlocal_call(dqkv, ab, conv8, gp, dn_consts, B, S, C)
        of, ob = _dnscan_call(local, B, S, C)
        ogqa = _gqa_call(gq, gkv, gqa_sink[l], B, S, C)
        wr = jnp.zeros((LANES, D), F32).at[0:N_EXPERTS].set(w_router[l].T)
        rb = jnp.zeros((LANES, 1), F32).at[0:N_EXPERTS, 0].set(router_bias[l])
        merged, routed = [], []
        for part in range(MOE_PARTS):
            m = _merge_call(
                xs, mod_l, omla, of, ob, dz, ogqa, gates,
                cast(w_o_mla[l]), cast(w_o_dn[l]), cast(w_o_gqa[l]), cast(w_out[l]),
                jnp.tile(dn_norm[l], DN_HEADS).reshape(1, DN_WIDTH), dn_consts[0],
                ln1_g[l].reshape(1, D), ln1_b[l].reshape(1, D), cast(wr), rb, B, S, C, part, MOE_PARTS)
            merged.append(m)
            routed.append(_moe_dispatch(m[1], m[2], m[3], m[5][:, 0].astype(jnp.int32)))
        outs = []
        for part in range(MOE_PARTS):
            x1, ufp, _, _, ew, _ = merged[part]
            xb, block_e, nvalid, dest_t = routed[part]
            yb = _experts_call(xb, block_e, nvalid, w_exp_gate, w_exp_up, w_exp_down, l)
            yg = _moe_return(yb, dest_t)
            outs.append(_combine_call(x1, ufp, yg, ew, mod_l, cast(w_sh_gate[l]), cast(w_sh_up[l]),
                                      cast(w_sh_down[l]), ln2_g[l].reshape(1, D), ln2_b[l].reshape(1, D),
                                      B, S, C, part))
        xs = jnp.concatenate(outs, axis=0)
    return xs.reshape(B, S, D)[:, C:, :]
```

```python
import functools

import numpy as np
import jax
import jax.numpy as jnp
from jax import lax
from jax.experimental import pallas as pl
from jax.experimental.pallas import tpu as pltpu
from jax.experimental.pallas import tpu_sc as plsc

F32 = jnp.float32
MXU_DTYPE = jnp.bfloat16

D_MODEL = 1024
DEPTH = 4
GRID_W = 64
NORM_EPS = 1e-6
ROPE_BASE = 10000.0
DEEPNORM_ALPHA = (2.0 * DEPTH) ** 0.25

MLA_HEADS = 8
MLA_Q_LORA = 256
MLA_KV_LORA = 128
MLA_NOPE = 64
MLA_ROPE = 32
MLA_V = 64
MLA_SCALE = (MLA_NOPE + MLA_ROPE) ** -0.5
LOG2E = float(np.log2(np.e))

DN_HEADS = 8
DN_HEAD_DIM = 64
DN_WIDTH = DN_HEADS * DN_HEAD_DIM
DN_CONV = 5
DN_CHUNK = 64

GQA_HEADS = 8
GQA_KV_HEADS = 2
GQA_HEAD_DIM = 64
GQA_SCALE = GQA_HEAD_DIM ** -0.5
WINDOW = 128

N_EXPERTS = 64
TOP_K = 8
N_GROUPS = 8
TOPK_GROUPS = 4
EXPERT_DIM = 256
SHARED_DIM = 256
ROUTED_SCALE = 2.5

IN_SIZES = (MLA_Q_LORA, MLA_KV_LORA, MLA_ROPE,
            3 * DN_WIDTH, 2 * DN_HEADS, 2 * DN_HEADS, DN_WIDTH,
            GQA_HEADS * GQA_HEAD_DIM, GQA_KV_HEADS * GQA_HEAD_DIM, GQA_KV_HEADS * GQA_HEAD_DIM,
            3 * D_MODEL)

LANES = 128
TM = 256
MOD_ROWS = 16
CTX_MOD_ROW = 8

OFF_A = 0
OFF_DQKV = 640
OFF_AB = OFF_DQKV + 3 * DN_WIDTH
OFF_DZ = OFF_AB + LANES
OFF_GQ = OFF_DZ + DN_WIDTH
OFF_GK = OFF_GQ + 1024
OFF_GATES = OFF_GK + 768
NZ = OFF_GATES + 3 * D_MODEL

VMEM_LIMIT = 56 * 1024 * 1024


def _mm(a, b):
    return jnp.dot(a.astype(MXU_DTYPE), b.astype(MXU_DTYPE), preferred_element_type=F32)


def _mm_nt(a, b):
    return lax.dot_general(a.astype(MXU_DTYPE), b.astype(MXU_DTYPE), (((1,), (1,)), ((), ())),
                           preferred_element_type=F32)


def _mm_tn(a, b):
    return lax.dot_general(a.astype(MXU_DTYPE), b.astype(MXU_DTYPE), (((0,), (0,)), ((), ())),
                           preferred_element_type=F32)


def _bmm(a, b):
    return jnp.einsum('cik,ckj->cij', a.astype(MXU_DTYPE), b.astype(MXU_DTYPE), preferred_element_type=F32)


def _bmm_nt(a, b):
    return jnp.einsum('cik,cjk->cij', a.astype(MXU_DTYPE), b.astype(MXU_DTYPE), preferred_element_type=F32)


def _split3(x):
    hi = x.astype(jnp.bfloat16).astype(F32)
    r = x - hi
    mid = r.astype(jnp.bfloat16).astype(F32)
    lo = (r - mid).astype(jnp.bfloat16).astype(F32)
    return hi, mid, lo


def _exact_mm(x, m01):
    hi, mid, lo = _split3(x)
    return _mm(hi, m01) + _mm(mid, m01) + _mm(lo, m01)


def _exact_mm_left(m01, x):
    hi, mid, lo = _split3(x)
    return _mm(m01, hi) + _mm(m01, mid) + _mm(m01, lo)


def _silu(x):
    return x * jax.nn.sigmoid(x)


def _layernorm(v, g, b):
    mu = jnp.mean(v, -1, keepdims=True)
    d = v - mu
    var = jnp.mean(d * d, -1, keepdims=True)
    return d * lax.rsqrt(var + NORM_EPS) * g + b


def _mod_index(tiles_per_b, ctx_tiles, k, first_tile=0, stride=1):
    def index(i):
        t = i * stride + first_tile
        row = jnp.where((t % tiles_per_b) < ctx_tiles, CTX_MOD_ROW, t // tiles_per_b)
        return (row * 6 + k, 0, 0)
    return index


def _params(*sem):
    return pltpu.CompilerParams(dimension_semantics=sem, vmem_limit_bytes=VMEM_LIMIT)


def _ada_kernel(c_ref, w_ref, b_ref, o_ref):
    o_ref[0] = _mm(_silu(c_ref[...]), w_ref[0]) + b_ref[0]


def _ada_call(cc, w_ada, b_ada):
    nl, d, n6 = w_ada.shape
    tn = 1536
    return pl.pallas_call(
        _ada_kernel,
        grid=(nl, n6 // tn),
        in_specs=[pl.BlockSpec((MOD_ROWS, d), lambda l, j: (0, 0)),
                  pl.BlockSpec((1, d, tn), lambda l, j: (l, 0, j)),
                  pl.BlockSpec((1, 1, tn), lambda l, j: (l, 0, j))],
        out_specs=pl.BlockSpec((1, MOD_ROWS, tn), lambda l, j: (l, 0, j)),
        out_shape=jax.ShapeDtypeStruct((nl, MOD_ROWS, n6), F32),
        compiler_params=_params("parallel", "parallel"),
        name="ada_mod",
    )(cc, w_ada, b_ada.reshape(nl, 1, n6))


def _inproj_kernel(x_ref, *refs):
    n = IN_SUB
    mod_refs, refs = refs[:2 * n], refs[2 * n:]
    w_ref, wuq_ref, wukv_ref, qn_ref, kvn_ref = refs[:5]
    tab_refs, outs = refs[5:5 + 4 * n], refs[5 + 4 * n:]
    q_out, k_out, v_out, dqkv_out, ab_out, dz_out, gq_out, gkv_out, gates_out = outs
    u = jnp.concatenate(
        [x_ref[s * TM:(s + 1) * TM, :] * (1.0 + mod_refs[2 * s + 1][0]) + mod_refs[2 * s][0] for s in range(n)],
        axis=0).astype(MXU_DTYPE)

    def z(off, width):
        return jnp.dot(u, w_ref[:, off:off + width], preferred_element_type=F32)

    def rms(v, g):
        return v * lax.rsqrt(jnp.mean(v * v, -1, keepdims=True) + NORM_EPS) * g

    cm, sm, cg, sg = (jnp.concatenate([tab_refs[4 * s + t][...] for s in range(n)], axis=0) for t in range(4))

    za = z(OFF_A, 640)
    qq = _mm(rms(za[:, 0:256], qn_ref[...]), wuq_ref[...])
    kvv = _mm(rms(za[:, 256:384], kvn_ref[...]), wukv_ref[...])
    k_rope = za[:, 384:512] * cm + za[:, 512:640] * sm
    for h in range(MLA_HEADS):
        sl = slice(h * LANES, (h + 1) * LANES)
        qa = qq[:, h * LANES:(h + 1) * LANES]
        qb = qq[:, 1024 + h * LANES:1024 + (h + 1) * LANES]
        q_out[:, sl] = ((qa * cm + qb * sm) * (MLA_SCALE * LOG2E)).astype(q_out.dtype)
        k_out[:, sl] = (kvv[:, sl] + k_rope).astype(k_out.dtype)
    v_out[...] = kvv[:, 1024:1536].astype(v_out.dtype)

    for t in range(3):
        dqkv_out[:, t * 512:(t + 1) * 512] = z(OFF_DQKV + t * 512, 512).astype(dqkv_out.dtype)
    ab_out[...] = z(OFF_AB, LANES)
    dz_out[...] = z(OFF_DZ, DN_WIDTH).astype(dz_out.dtype)

    zq = z(OFF_GQ, 1024)
    for p in range(4):
        sl = slice(p * LANES, (p + 1) * LANES)
        gq_out[:, sl] = ((zq[:, sl] * cg + zq[:, 512 + p * LANES:512 + (p + 1) * LANES] * sg)
                         * (GQA_SCALE * LOG2E)).astype(gq_out.dtype)
    zk = z(OFF_GK, 768)
    for j in range(2):
        sl = slice(j * LANES, (j + 1) * LANES)
        gkv_out[:, sl] = (zk[:, sl] * cg + zk[:, 256 + j * LANES:256 + (j + 1) * LANES] * sg).astype(gkv_out.dtype)
    gkv_out[:, 256:512] = zk[:, 512:768].astype(gkv_out.dtype)

    for t in range(3):
        gates_out[:, t * 1024:(t + 1) * 1024] = jax.nn.sigmoid(z(OFF_GATES + t * 1024, 1024)).astype(gates_out.dtype)


IN_SUB = 2


def _inproj_call(xs, mod_l, w_cat, wuq_cat, wukv_cat, qn, kvn, tabs, B, S, C):
    T, D = xs.shape
    n = IN_SUB
    rows = n * TM
    tpb, ctiles = S // TM, C // TM
    act = MXU_DTYPE
    row = lambda i: (i, 0)
    const = lambda i: (0, 0)
    once = dict(pipeline_mode=pl.Buffered(1))
    widths = (1024, 1024, 512, 1536, LANES, 512, 512, 512, 3072)
    dtypes = (act, act, act, act, F32, act, act, act, act)
    mod_specs, mod_args, tab_specs, tab_args = [], [], [], []
    for s in range(n):
        for k in range(2):
            mod_specs.append(pl.BlockSpec((1, 1, D), _mod_index(tpb, ctiles, k, s, n)))
            mod_args.append(mod_l)
        for t in range(4):
            tab_specs.append(pl.BlockSpec((TM, LANES), lambda i, s=s: ((i * n + s) % tpb, 0)))
            tab_args.append(tabs[t])
    return pl.pallas_call(
        _inproj_kernel,
        grid=(T // rows,),
        in_specs=[pl.BlockSpec((rows, D), row)] + mod_specs
                 + [pl.BlockSpec((D, NZ), const, **once),
                    pl.BlockSpec((MLA_Q_LORA, 2048), const, **once),
                    pl.BlockSpec((MLA_KV_LORA, 1536), const, **once),
                    pl.BlockSpec((1, MLA_Q_LORA), const),
                    pl.BlockSpec((1, MLA_KV_LORA), const)]
                 + tab_specs,
        out_specs=[pl.BlockSpec((rows, w), row) for w in widths],
        out_shape=[jax.ShapeDtypeStruct((T, w), dt) for w, dt in zip(widths, dtypes)],
        compiler_params=_params("parallel"),
        name="in_proj",
    )(xs, *mod_args, w_cat, wuq_cat, wukv_cat, qn, kvn, *tab_args)


def _mla_kernel(q_ref, k_ref, v_ref, o_ref, *, n_ctx, n_all):
    i = pl.program_id(2)
    tq = q_ref.shape[1]
    left = lax.broadcasted_iota(jnp.int32, (tq, LANES), 1) < MLA_V

    def attend(nk, ctx_rows):
        v = v_ref[0, 0:nk, :]
        v_left = lax.broadcasted_iota(jnp.int32, v.shape, 1) < MLA_V
        ones = jnp.ones_like(v)
        s = [_mm_nt(q_ref[0, :, hh * LANES:(hh + 1) * LANES], k_ref[0, 0:nk, hh * LANES:(hh + 1) * LANES])
             for hh in range(2)]
        if ctx_rows:
            row = lax.broadcasted_iota(jnp.int32, (tq, nk), 0)
            col = lax.broadcasted_iota(jnp.int32, (tq, nk), 1)
            ok = (row >= ctx_rows) | (col < ctx_rows)
            s = [jnp.where(ok, x, -jnp.inf) for x in s]
        p = [jnp.exp2((x - jnp.max(x, -1, keepdims=True)).astype(MXU_DTYPE)) for x in s]
        o = [_mm(p[0], jnp.where(v_left, v, ones)), _mm(p[1], jnp.where(v_left, ones, v))]
        outs = [x / pltpu.roll(x, MLA_V, axis=1) for x in o]
        o_ref[0] = jnp.where(left, outs[0], outs[1]).astype(o_ref.dtype)

    if tq > n_ctx:
        @pl.when(i == 0)
        def _():
            attend(n_all, n_ctx)

        @pl.when(i > 0)
        def _():
            attend(n_all, 0)
    else:
        ctx_tiles = n_ctx // tq

        @pl.when(i < ctx_tiles)
        def _():
            attend(n_ctx, 0)

        @pl.when(i >= ctx_tiles)
        def _():
            attend(n_all, 0)


MLA_TQ = (544, 384, 256)


def _mla_call(q, k, v, B, S, C):
    tq = next(t for t in MLA_TQ if S % t == 0)
    assert C % tq == 0 or tq > C
    q3, k3, v3 = q.reshape(B, S, 1024), k.reshape(B, S, 1024), v.reshape(B, S, 512)
    out = pl.pallas_call(
        functools.partial(_mla_kernel, n_ctx=C, n_all=S),
        grid=(B, MLA_HEADS // 2, S // tq),
        in_specs=[pl.BlockSpec((1, tq, 2 * LANES), lambda b, j, i: (b, i, j)),
                  pl.BlockSpec((1, S, 2 * LANES), lambda b, j, i: (b, 0, j)),
                  pl.BlockSpec((1, S, LANES), lambda b, j, i: (b, 0, j))],
        out_specs=pl.BlockSpec((1, tq, LANES), lambda b, j, i: (b, i, j)),
        out_shape=jax.ShapeDtypeStruct((B, S, 512), MXU_DTYPE),
        compiler_params=_params("parallel", "parallel", "arbitrary"),
        name="mla_attn",
    )(q3, k3, v3)
    return out.reshape(B * S, 512)


def _gqa_kernel(sink_ref, q_ref, kv_ref, o_ref, *, n_ctx, n_all):
    i = pl.program_id(1)
    qb_rows = q_ref.shape[1]
    span = qb_rows + 2 * WINDOW
    ctx_blocks = n_ctx // qb_rows
    group = GQA_HEADS // GQA_KV_HEADS
    left = lax.broadcasted_iota(jnp.int32, (qb_rows, LANES), 1) < GQA_HEAD_DIM

    def run(latent):
        nk = n_ctx + span if latent else n_ctx
        rows = group * qb_rows
        if latent:
            qb = i - ctx_blocks
            ws = pl.multiple_of(jnp.minimum((qb + 1) * qb_rows, n_all - span), qb_rows)
            col = lax.broadcasted_iota(jnp.int32, (rows, nk), 1)
            q_pos = qb * qb_rows + lax.broadcasted_iota(jnp.int32, (rows, nk), 0) % qb_rows
            k_pos = ws - 2 * n_ctx + col
            ok = (col < n_ctx) | ((jnp.abs(k_pos - q_pos) <= WINDOW) & (k_pos >= 0))
        head_of_row = lax.broadcasted_iota(jnp.int32, (rows, 1), 0) // qb_rows
        res = []
        for j in range(GQA_KV_HEADS):
            parts = []
            for g in range(group):
                qp = q_ref[0, :, (2 * j + g // 2) * LANES:(2 * j + g // 2 + 1) * LANES]
                parts.append(jnp.where(left if g % 2 == 0 else ~left, qp, jnp.zeros_like(qp)))
            q4 = jnp.concatenate(parts, axis=0)
            kc = kv_ref[0, 0:n_ctx, j * LANES:(j + 1) * LANES]
            vc = kv_ref[0, 0:n_ctx, 256 + j * LANES:256 + (j + 1) * LANES]
            if latent:
                kc = jnp.concatenate([kc, kv_ref[0, pl.ds(ws, span), j * LANES:(j + 1) * LANES]], axis=0)
                vc = jnp.concatenate([vc, kv_ref[0, pl.ds(ws, span), 256 + j * LANES:256 + (j + 1) * LANES]], axis=0)
            sink = jnp.zeros((rows, 1), F32)
            for g in range(group):
                sink = jnp.where(head_of_row == g, sink_ref[group * j + g] * LOG2E, sink)
            s = _mm_nt(q4, kc)
            if latent:
                s = jnp.where(ok, s, -jnp.inf)
            m = jnp.maximum(jnp.max(s, -1, keepdims=True), sink)
            p = jnp.exp2((s - m).astype(MXU_DTYPE))
            v_left = lax.broadcasted_iota(jnp.int32, vc.shape, 1) < GQA_HEAD_DIM
            o = _mm(p, jnp.where(v_left, vc, jnp.ones_like(vc)))
            res.append(o / (pltpu.roll(o, GQA_HEAD_DIM, axis=1) + jnp.exp2(sink - m)))
        for j in range(GQA_KV_HEADS):
            for pp in range(group // 2):
                r0 = res[j][(2 * pp) * qb_rows:(2 * pp + 1) * qb_rows]
                r1 = res[j][(2 * pp + 1) * qb_rows:(2 * pp + 2) * qb_rows]
                pair = 2 * j + pp
                o_ref[0, :, pair * LANES:(pair + 1) * LANES] = jnp.where(
                    left, r0, pltpu.roll(r1, GQA_HEAD_DIM, axis=1)).astype(o_ref.dtype)

    @pl.when(i < ctx_blocks)
    def _():
        run(False)

    @pl.when(i >= ctx_blocks)
    def _():
        run(True)


def _gqa_call(gq, gkv, sink, B, S, C):
    qb_rows = 128
    out = pl.pallas_call(
        functools.partial(_gqa_kernel, n_ctx=C, n_all=S),
        grid=(B, S // qb_rows),
        in_specs=[pl.BlockSpec(memory_space=pltpu.SMEM),
                  pl.BlockSpec((1, qb_rows, 512), lambda b, i: (b, i, 0)),
                  pl.BlockSpec((1, S, 512), lambda b, i: (b, 0, 0))],
        out_specs=pl.BlockSpec((1, qb_rows, 512), lambda b, i: (b, i, 0)),
        out_shape=jax.ShapeDtypeStruct((B, S, 512), MXU_DTYPE),
        compiler_params=_params("parallel", "arbitrary"),
        name="gqa_attn",
    )(sink, gq.reshape(B, S, 512), gkv.reshape(B, S, 512))
    return out.reshape(B * S, 512)


DN_TR = 256
DN_HALO = 16
DN_CPT = DN_TR // DN_CHUNK


def _stack(x, left):
    z = jnp.zeros_like(x)
    return jnp.concatenate([jnp.where(left, x, z), jnp.where(left, z, x)], axis=1)


def _dnlocal_kernel(main_ref, prev_ref, next_ref, ab_ref, conv_ref, gp_ref, bd_ref, trif_ref, trib_ref,
                    eg_ref, eb_ref,
                    u_out, w_out, qg_out, kg_out, qk_out, gl_out, pad_ref, *, ctx_tiles, n_tiles):
    i = pl.program_id(1)
    tr = DN_TR
    first = (i == 0) | (i == ctx_tiles)
    last = (i == ctx_tiles - 1) | (i == n_tiles - 1)
    xp = prev_ref[0].astype(F32)
    xn = next_ref[0].astype(F32)
    pad_ref[0:DN_HALO, :] = jnp.where(first, jnp.zeros_like(xp), xp)
    pad_ref[DN_HALO:DN_HALO + tr, :] = main_ref[0].astype(F32)
    pad_ref[DN_HALO + tr:, :] = jnp.where(last, jnp.zeros_like(xn), xn)
    y = jnp.zeros((tr, 3 * DN_WIDTH), F32)
    for t in range(DN_CONV):
        y = y + conv_ref[t:t + 1, :] * pad_ref[pl.ds(DN_HALO - DN_CONV // 2 + t, tr), :]
    y = _silu(y)
    q, k, v = y[:, 0:512], y[:, 512:1024], y[:, 1024:1536]
    bd = bd_ref[...]
    q = q * lax.rsqrt(_exact_mm(q * q, bd) + 1e-6) * (DN_HEAD_DIM ** -0.5)
    k = k * lax.rsqrt(_exact_mm(k * k, bd) + 1e-6)

    ab = ab_ref[0]
    g = -jnp.exp(gp_ref[0:1, :]) * jax.nn.softplus(ab + gp_ref[1:2, :])
    beta = jax.nn.sigmoid(ab)
    lane = lax.broadcasted_iota(jnp.int32, (tr, LANES), 1)
    gc = jnp.where(lane < DN_HEADS, _exact_mm_left(trif_ref[...], g), _exact_mm_left(trib_ref[...], g))
    gcx_all = _exact_mm(gc, eg_ref[...])
    bx_all = _exact_mm(beta, eb_ref[...])

    c = DN_CPT
    lane3 = lax.broadcasted_iota(jnp.int32, (1, 1, LANES), 2)
    left = (lane3 % LANES) < DN_HEAD_DIM
    tpos = lane3 % DN_HEAD_DIM
    lane6 = lax.broadcasted_iota(jnp.int32, (1, 1, 2 * LANES), 2)
    left6 = (lane6 % LANES) < DN_HEAD_DIM
    ri = lax.broadcasted_iota(jnp.int32, (1, DN_CHUNK, LANES), 1)
    cj = lax.broadcasted_iota(jnp.int32, (1, DN_CHUNK, LANES), 2) % DN_HEAD_DIM
    one = jnp.ones((), F32)
    zero = jnp.zeros((), F32)

    def process(units):
        st = {}
        for d, j in units:
            off = d * 512 + j * LANES
            gcx = gcx_all[:, off:off + LANES].reshape(c, DN_CHUNK, LANES)
            bx = bx_all[:, off:off + LANES].reshape(c, DN_CHUNK, LANES)
            qp = q[:, j * LANES:(j + 1) * LANES].reshape(c, DN_CHUNK, LANES)
            kp = k[:, j * LANES:(j + 1) * LANES].reshape(c, DN_CHUNK, LANES)
            vp = v[:, j * LANES:(j + 1) * LANES].reshape(c, DN_CHUNK, LANES)
            gl = gcx[:, DN_CHUNK - 1:DN_CHUNK, :] if d == 0 else gcx[:, 0:1, :]
            kb = kp * bx
            kq = _bmm_nt(jnp.concatenate([kb, qp], axis=1), _stack(kp, left))
            hi, mid, lo = _split3(gcx)
            a6 = jnp.where(tpos == 0, hi, jnp.where(tpos == 1, mid, jnp.where(tpos == 2, lo,
                           jnp.where(tpos < 6, one, zero))))
            b6 = jnp.where(tpos < 3, one, jnp.where(tpos == 3, -hi, jnp.where(tpos == 4, -mid,
                           jnp.where(tpos == 5, -lo, zero))))
            diff = _bmm_nt(a6, _stack(b6, left))
            st[d, j] = dict(gcx=gcx, bx=bx, qp=qp, kp=kp, vp=vp, gl=gl, kb=kb, kq=kq, diff=diff)
        for d, j in units:
            u = st[d, j]
            incl = (ri >= cj) if d == 0 else (ri <= cj)
            strict = (ri > cj) if d == 0 else (ri < cj)
            dm = jnp.exp(jnp.where(incl, u["diff"], -jnp.inf))
            u["qkm"] = u["kq"][:, DN_CHUNK:, :] * dm
            u["x"] = -jnp.where(strict, u["kq"][:, 0:DN_CHUNK, :] * dm, zero)
            u["r"] = u["x"]
        for d, j in units:
            u = st[d, j]
            u["x"] = _bmm(u["x"], _stack(u["x"], left))
        for level in range(5):
            for d, j in units:
                u = st[d, j]
                xs = _stack(u["x"], left)
                if level < 4:
                    m = _bmm(jnp.concatenate([u["r"], u["x"]], axis=1), xs)
                    u["r"] = u["r"] + u["x"] + m[:, 0:DN_CHUNK, :]
                    u["x"] = m[:, DN_CHUNK:, :]
                else:
                    u["r"] = u["r"] + u["x"] + _bmm(u["r"], xs)
        for d, j in units:
            u = st[d, j]
            eg = jnp.exp(u["gcx"])
            rhs = jnp.concatenate([u["vp"] * u["bx"], u["kb"] * eg], axis=-1)
            sol = rhs + _bmm(u["r"], _stack(rhs, left6))
            sl = slice(j * LANES, (j + 1) * LANES)
            u_out[0, d, :, sl] = sol[:, :, 0:LANES].reshape(tr, LANES)
            w_out[0, d, :, sl] = sol[:, :, LANES:].reshape(tr, LANES).astype(w_out.dtype)
            qg_out[0, d, :, sl] = (u["qp"] * eg).reshape(tr, LANES).astype(qg_out.dtype)
            kg_out[0, d, :, sl] = (u["kp"] * jnp.exp(u["gl"] - u["gcx"])).reshape(tr, LANES).astype(kg_out.dtype)
            qk_out[0, d, :, sl] = u["qkm"].reshape(tr, LANES).astype(qk_out.dtype)
            gl_out[0, d, :, :, sl] = jnp.exp(u["gl"])

    process([(d, j) for d in range(2) for j in range(4)])


def _dn_constants():
    idx = np.arange(DN_TR)
    same = (idx[:, None] // DN_CHUNK) == (idx[None, :] // DN_CHUNK)
    trif = (same & (idx[None, :] <= idx[:, None])).astype(np.float32)
    trib = (same & (idx[None, :] >= idx[:, None])).astype(np.float32)
    h = np.arange(512)
    bd = ((h[:, None] // DN_HEAD_DIM) == (h[None, :] // DN_HEAD_DIM)).astype(np.float32)
    col = np.arange(LANES)[:, None]
    out = np.arange(1024)[None, :]
    unit = (out // 512) * DN_HEADS + (out % 512) // DN_HEAD_DIM
    eg = (col == unit).astype(np.float32)
    eb = (col == unit + 2 * DN_HEADS).astype(np.float32)
    return tuple(jnp.asarray(a, MXU_DTYPE) for a in (bd, trif, trib, eg, eb))


def _dnlocal_call(dqkv, ab, conv_w, gp, consts, B, S, C):
    tr = DN_TR
    n_tiles = S // tr
    hb = tr // DN_HALO
    n_hblk = S // DN_HALO
    bd, trif, trib, eg, eb = consts
    const2 = lambda b, i: (0, 0)
    big = lambda b, i: (b, 0, i, 0)
    act = MXU_DTYPE
    shp = (B, 2, S, 512)
    return pl.pallas_call(
        functools.partial(_dnlocal_kernel, ctx_tiles=C // tr, n_tiles=n_tiles),
        grid=(B, n_tiles),
        in_specs=[pl.BlockSpec((1, tr, 1536), lambda b, i: (b, i, 0)),
                  pl.BlockSpec((1, DN_HALO, 1536), lambda b, i: (b, jnp.maximum(i * hb - 1, 0), 0)),
                  pl.BlockSpec((1, DN_HALO, 1536), lambda b, i: (b, jnp.minimum((i + 1) * hb, n_hblk - 1), 0)),
                  pl.BlockSpec((1, tr, LANES), lambda b, i: (b, i, 0)),
                  pl.BlockSpec((8, 1536), const2),
                  pl.BlockSpec((8, LANES), const2),
                  pl.BlockSpec((512, 512), const2),
                  pl.BlockSpec((tr, tr), const2),
                  pl.BlockSpec((tr, tr), const2),
                  pl.BlockSpec((LANES, 1024), const2),
                  pl.BlockSpec((LANES, 1024), const2)],
        out_specs=[pl.BlockSpec((1, 2, tr, 512), big)] * 5
                  + [pl.BlockSpec((1, 2, DN_CPT, 1, 512), lambda b, i: (b, 0, i, 0, 0))],
        out_shape=[jax.ShapeDtypeStruct(shp, F32)] + [jax.ShapeDtypeStruct(shp, act)] * 4
                  + [jax.ShapeDtypeStruct((B, 2, S // DN_CHUNK, 1, 512), F32)],
        scratch_shapes=[pltpu.VMEM((tr + 2 * DN_HALO, 1536), F32)],
        compiler_params=_params("parallel", "parallel"),
        name="dn_local",
    )(dqkv.reshape(B, S, 1536), dqkv.reshape(B, S, 1536), dqkv.reshape(B, S, 1536), ab.reshape(B, S, LANES),
      conv_w, gp, bd, trif, trib, eg, eb)


def _dnscan_kernel(uf, wf, qgf, kgf, qkf, glf, ub, wb, qgb, kgb, qkb, glb, of_out, ob_out, s_ref):
    n = pl.program_id(1)
    lane = lax.broadcasted_iota(jnp.int32, (1, LANES), 1)
    left = lane < DN_HEAD_DIM
    row = lax.broadcasted_iota(jnp.int32, (LANES, LANES), 0)
    col = lax.broadcasted_iota(jnp.int32, (LANES, LANES), 1)
    same_head = (row < DN_HEAD_DIM) == (col < DN_HEAD_DIM)
    dirs = ((uf, wf, qgf, kgf, qkf, glf, of_out), (ub, wb, qgb, kgb, qkb, glb, ob_out))
    units = [(bb, d, j) for bb in range(uf.shape[0]) for d in range(2) for j in range(4)]
    sidx = lambda bb, d, j: (bb * 2 + d) * 4 + j
    sl = lambda j: slice(j * LANES, (j + 1) * LANES)
    started = n > 0
    st = {t: jnp.where(started, s_ref[sidx(*t)], jnp.zeros((LANES, LANES), F32)) for t in units}
    pre = {(bb, d, j): _mm(dirs[d][1][bb, 0, :, sl(j)], st[bb, d, j]) for bb, d, j in units}
    o1 = {(bb, d, j): _mm(dirs[d][2][bb, 0, :, sl(j)], st[bb, d, j]) for bb, d, j in units}
    vn = {(bb, d, j): dirs[d][0][bb, 0, :, sl(j)] - pre[bb, d, j] for bb, d, j in units}
    for bb, d, j in units:
        v = vn[bb, d, j]
        z = jnp.zeros_like(v)
        vst = jnp.concatenate([jnp.where(left, v, z), jnp.where(left, z, v)], axis=0)
        dirs[d][6][bb, :, sl(j)] = o1[bb, d, j] + _mm(dirs[d][4][bb, 0, :, sl(j)], vst)
    for bb, d, j in units:
        upd = _mm_tn(dirs[d][3][bb, 0, :, sl(j)], vn[bb, d, j])
        s_ref[sidx(bb, d, j)] = (st[bb, d, j] * dirs[d][5][bb, 0, 0, :, sl(j)]
                                 + jnp.where(same_head, upd, jnp.zeros_like(upd)))


DN_SCAN_BATCH = 4


def _dnscan_call(local, B, S, C):
    u, w, qg, kg, qk, gl = local
    nch, nc = S // DN_CHUNK, C // DN_CHUNK
    bb = DN_SCAN_BATCH if B % DN_SCAN_BATCH == 0 else 1

    def bidx(n):
        return jnp.where(n < nc, nc - 1 - n, nch - 1 + nc - n)

    fspec = pl.BlockSpec((bb, 1, DN_CHUNK, 512), lambda b, n: (b, 0, n, 0))
    bspec = pl.BlockSpec((bb, 1, DN_CHUNK, 512), lambda b, n: (b, 1, bidx(n), 0))
    fgl = pl.BlockSpec((bb, 1, 1, 1, 512), lambda b, n: (b, 0, n, 0, 0))
    bgl = pl.BlockSpec((bb, 1, 1, 1, 512), lambda b, n: (b, 1, bidx(n), 0, 0))
    return pl.pallas_call(
        _dnscan_kernel,
        grid=(B // bb, nch),
        in_specs=[fspec] * 5 + [fgl] + [bspec] * 5 + [bgl],
        out_specs=[pl.BlockSpec((bb, DN_CHUNK, 512), lambda b, n: (b, n, 0)),
                   pl.BlockSpec((bb, DN_CHUNK, 512), lambda b, n: (b, bidx(n), 0))],
        out_shape=[jax.ShapeDtypeStruct((B, S, 512), F32)] * 2,
        scratch_shapes=[pltpu.VMEM((bb * 8, LANES, LANES), F32)],
        compiler_params=_params("parallel", "arbitrary"),
        name="dn_scan",
    )(u, w, qg, kg, qk, gl, u, w, qg, kg, qk, gl)


def _pack_pairs(v):
    w = v.shape[1] // 2
    bits = lax.bitcast_convert_type(v.astype(jnp.bfloat16).astype(F32), jnp.int32)
    return lax.shift_right_logical(bits[:, :w], 16) | bits[:, w:]


def _unpack_pairs(p):
    lo = lax.bitcast_convert_type(lax.shift_left(p, 16), F32)
    hi = lax.bitcast_convert_type(p & jnp.int32(-65536), F32)
    return lo, hi


def _merge_kernel(x_ref, ga_ref, shf_ref, scf_ref, omla_ref, of_ref, ob_ref, dz_ref, ogqa_ref, gates_ref,
                  wo1_ref, wo2_ref, wo3_ref, wout_ref, dnn_ref, bd_ref, lng_ref, lnb_ref, wr_ref, rb_ref,
                  x1_out, ufp_out, eidx_out, rank_out, ew_out, cnt_out, cnt_ref):
    @pl.when(pl.program_id(0) == 0)
    def _():
        cnt_ref[...] = jnp.zeros_like(cnt_ref)

    o = of_ref[...] + ob_ref[...]
    ms = _exact_mm(o * o, bd_ref[...]) * (1.0 / DN_HEAD_DIM)
    dn = o * lax.rsqrt(ms + NORM_EPS) * dnn_ref[...] * _silu(dz_ref[...].astype(F32))
    g1 = gates_ref[:, 0:1024].astype(F32)
    g2 = gates_ref[:, 1024:2048].astype(F32)
    g3 = gates_ref[:, 2048:3072].astype(F32)
    m = (g1 * _mm(omla_ref[...], wo1_ref[...]) + g2 * _mm(dn, wo2_ref[...])
         + g3 * _mm(ogqa_ref[...], wo3_ref[...]))
    y = _mm(m, wout_ref[...])
    x1 = _layernorm(DEEPNORM_ALPHA * x_ref[...] + ga_ref[0] * y, lng_ref[...], lnb_ref[...])
    x1_out[...] = x1
    uf = x1 * (1.0 + scf_ref[0]) + shf_ref[0]
    _split_pieces(_pack_pairs(uf), ufp_out)

    tm = uf.shape[0]
    scores = jax.nn.sigmoid(_mm_nt(wr_ref[...], uf))[0:N_EXPERTS]
    sel = scores + rb_ref[0:N_EXPERTS, :]
    gsz = N_EXPERTS // N_GROUPS
    neg = jnp.full((), -jnp.inf, F32)
    sel3 = sel.reshape(N_GROUPS, gsz, tm)
    mem = lax.broadcasted_iota(jnp.int32, (N_GROUPS, gsz, tm), 1)
    m1 = jnp.max(sel3, 1, keepdims=True)
    i1 = jnp.min(jnp.where(sel3 == m1, mem, gsz), 1, keepdims=True)
    m2 = jnp.max(jnp.where(mem == i1, neg, sel3), 1, keepdims=True)
    gs = (m1 + m2).reshape(N_GROUPS, tm)
    gi = lax.broadcasted_iota(jnp.int32, (N_GROUPS, tm), 0)
    grank = jnp.zeros((N_GROUPS, tm), jnp.int32)
    for gp in range(N_GROUPS):
        other = gs[gp:gp + 1, :]
        beats = (other > gs) | ((other == gs) & (gp < gi))
        grank = grank + beats.astype(jnp.int32)
    gsel = (grank < TOPK_GROUPS).reshape(N_GROUPS, 1, tm)
    cur = jnp.where(gsel, sel3, neg).reshape(N_EXPERTS, tm)
    ei = lax.broadcasted_iota(jnp.int32, (N_EXPERTS, tm), 0)
    zero = jnp.zeros((N_EXPERTS, tm), F32)
    one = jnp.ones((N_EXPERTS, tm), F32)
    chosen = zero
    picks = []
    for _ in range(TOP_K):
        mx = jnp.max(cur, 0, keepdims=True)
        ix = jnp.min(jnp.where(cur == mx, ei, N_EXPERTS), 0, keepdims=True)
        pick = ei == ix
        picks.append((ix, pick))
        chosen = chosen + jnp.where(pick, one, zero)
        cur = jnp.where(pick, neg, cur)

    r_i = lax.broadcasted_iota(jnp.int32, (tm, tm), 0)
    c_i = lax.broadcasted_iota(jnp.int32, (tm, tm), 1)
    before = jnp.where(r_i < c_i, 1.0, 0.0)
    pos = cnt_ref[:, 0:1] + _mm(chosen, before)
    cnt_new = cnt_ref[...] + jnp.sum(chosen, 1, keepdims=True)
    cnt_ref[...] = cnt_new
    cnt_out[...] = cnt_new

    w_rows = [jnp.sum(jnp.where(pick, scores, zero), 0, keepdims=True) for _, pick in picks]
    wsum = w_rows[0]
    for w_k in w_rows[1:]:
        wsum = wsum + w_k
    eidx_out[...] = jnp.concatenate([ix for ix, _ in picks], axis=0)
    rank_out[...] = jnp.concatenate(
        [jnp.sum(jnp.where(pick, pos, zero), 0, keepdims=True) for _, pick in picks], axis=0).astype(jnp.int32)
    w8 = jnp.concatenate([w_k / wsum * ROUTED_SCALE for w_k in w_rows], axis=0)
    ew_out[...] = jnp.concatenate([w8, jnp.zeros((LANES - TOP_K, tm), F32)], axis=0).T


def _merge_call(xs, mod_l, omla, of, ob, dz, ogqa, gates, wo1, wo2, wo3, wout, dnn, bd, lng, lnb, wr, rb, B, S, C,
                part, n_parts):
    t_all, D = xs.shape
    T = t_all // n_parts
    first = part * (T // TM)
    tpb, ctiles = S // TM, C // TM
    row = lambda i: (i, 0)
    src = lambda i: (i + first, 0)
    const = lambda i: (0, 0)
    modspec = lambda k: pl.BlockSpec((1, 1, D), _mod_index(tpb, ctiles, k, first))
    return pl.pallas_call(
        _merge_kernel,
        grid=(T // TM,),
        in_specs=[pl.BlockSpec((TM, D), src), modspec(2), modspec(3), modspec(4),
                  pl.BlockSpec((TM, 512), src), pl.BlockSpec((TM, 512), src), pl.BlockSpec((TM, 512), src),
                  pl.BlockSpec((TM, 512), src), pl.BlockSpec((TM, 512), src), pl.BlockSpec((TM, 3072), src),
                  pl.BlockSpec((512, D), const), pl.BlockSpec((512, D), const), pl.BlockSpec((512, D), const),
                  pl.BlockSpec((D, D), const), pl.BlockSpec((1, 512), const), pl.BlockSpec((512, 512), const),
                  pl.BlockSpec((1, D), const), pl.BlockSpec((1, D), const),
                  pl.BlockSpec((LANES, D), const), pl.BlockSpec((LANES, 1), const)],
        out_specs=[pl.BlockSpec((TM, D), row), pl.BlockSpec((N_PIECES, TM, PIECE), lambda i: (0, i, 0)),
                   pl.BlockSpec((TOP_K, TM), lambda i: (0, i)), pl.BlockSpec((TOP_K, TM), lambda i: (0, i)),
                   pl.BlockSpec((TM, LANES), row), pl.BlockSpec((N_EXPERTS, LANES), const)],
        out_shape=[jax.ShapeDtypeStruct((T, D), F32), jax.ShapeDtypeStruct((N_PIECES, T, PIECE), jnp.int32),
                   jax.ShapeDtypeStruct((TOP_K, T), jnp.int32), jax.ShapeDtypeStruct((TOP_K, T), jnp.int32),
                   jax.ShapeDtypeStruct((T, LANES), F32), jax.ShapeDtypeStruct((N_EXPERTS, LANES), F32)],
        scratch_shapes=[pltpu.VMEM((N_EXPERTS, LANES), F32)],
        compiler_params=_params("arbitrary"),
        name="merge_norm_route",
    )(xs, mod_l, mod_l, mod_l, omla, of.reshape(t_all, 512), ob.reshape(t_all, 512), dz, ogqa, gates,
      wo1, wo2, wo3, wout, dnn, bd, lng, lnb, wr, rb)


EXPERT_BLOCK = 512
MOE_PARTS = 2
SC_WINDOW = 128
N_PIECES = 2
PIECE = D_MODEL // 2 // N_PIECES


def _split_pieces(packed, out_ref):
    for h in range(N_PIECES):
        out_ref[h] = packed[:, h * PIECE:(h + 1) * PIECE]


def _mm_pieces(pieces, w):
    acc = None
    for h, (lo, hi) in enumerate(pieces):
        t = (_mm(lo, w[h * PIECE:(h + 1) * PIECE, :])
             + _mm(hi, w[D_MODEL // 2 + h * PIECE:D_MODEL // 2 + (h + 1) * PIECE, :]))
        acc = t if acc is None else acc + t
    return acc


def _sc_mesh():
    return plsc.VectorSubcoreMesh(core_axis_name="c", subcore_axis_name="s")


def _sc_gather_rows(y, idx):
    n = idx.shape[1]
    W = y.shape[1]

    @pl.kernel(out_type=jax.ShapeDtypeStruct((n, W), y.dtype), mesh=_sc_mesh(), scratch_types=[])
    def gather(y_hbm, i_hbm, o_hbm):
        def body(i_vmem, o_vmem):
            pltpu.sync_copy(y_hbm.at[i_vmem.at[0]], o_vmem)

        pltpu.emit_pipeline(
            body,
            grid=(n // SC_WINDOW,),
            in_specs=[pl.BlockSpec((1, SC_WINDOW), lambda i: (0, i))],
            out_specs=[pl.BlockSpec((SC_WINDOW, W), lambda i: (i, 0))],
            core_axis_name=("c", "s"),
            dimension_semantics=(pltpu.PARALLEL,),
        )(i_hbm, o_hbm)

    return gather(y, idx)


SC_LANES = 16
SC_WORKERS = 32
SC_CHUNK = 2176


def _sc_invert_rows(dest, default, n_tok):
    n_rows = default.shape[0]
    per_w = n_rows // SC_WORKERS
    n_k = dest.shape[0] // n_tok
    assert n_rows % (SC_WORKERS * SC_LANES) == 0 and n_tok % SC_CHUNK == 0

    @pl.kernel(out_type=jax.ShapeDtypeStruct((n_rows,), jnp.int32), mesh=_sc_mesh(),
               scratch_types=[pltpu.VMEM((per_w,), jnp.int32), pltpu.VMEM((SC_CHUNK,), jnp.int32)],
               compiler_params=pltpu.CompilerParams(needs_layout_passes=False))
    def invert(dest_hbm, dflt_hbm, out_hbm, rows_v, dest_v):
        wid = lax.axis_index("s") * 2 + lax.axis_index("c")
        base = wid * per_w
        pltpu.sync_copy(dflt_hbm.at[pl.ds(base, per_w)], rows_v)
        lanes = lax.iota(jnp.int32, SC_LANES)
        for k in range(n_k):
            @pl.loop(0, n_tok // SC_CHUNK)
            def _(c):
                pltpu.sync_copy(dest_hbm.at[pl.ds(k * n_tok + c * SC_CHUNK, SC_CHUNK)], dest_v)

                @pl.loop(0, SC_CHUNK, step=SC_LANES)
                def _(o):
                    local = dest_v[pl.ds(o, SC_LANES)] - base
                    mine = (local >= 0) & (local < per_w)
                    plsc.store_scatter(rows_v, [jnp.where(mine, local, 0)], c * SC_CHUNK + o + lanes, mask=mine)

        pltpu.sync_copy(rows_v, out_hbm.at[pl.ds(base, per_w)])

    return invert(dest, default)


def _experts_kernel(be_ref, nv_ref, xb_ref, wg_ref, wu_ref, wd_ref, y_out):
    b = pl.program_id(0)
    nv = nv_ref[b]

    @pl.when(nv > 0)
    def _():
        rows = lax.broadcasted_iota(jnp.int32, xb_ref.shape[1:], 0)
        pieces = []
        for h in range(N_PIECES):
            xh = xb_ref[h]
            pieces.append(_unpack_pairs(jnp.where(rows < nv, xh, jnp.zeros_like(xh))))
        hid = _silu(_mm_pieces(pieces, wg_ref.at[0, 0])) * _mm_pieces(pieces, wu_ref.at[0, 0])
        _split_pieces(_pack_pairs(_mm(hid, wd_ref[0, 0])), y_out)


def _experts_call(xb, block_e, nvalid, wg, wu, wd, layer):
    _, R, _ = xb.shape
    D = D_MODEL
    blk = (N_PIECES, EXPERT_BLOCK, PIECE)
    grid_spec = pltpu.PrefetchScalarGridSpec(
        num_scalar_prefetch=2,
        grid=(R // EXPERT_BLOCK,),
        in_specs=[pl.BlockSpec(blk, lambda b, be, nv: (0, b, 0)),
                  pl.BlockSpec((1, 1, D, EXPERT_DIM), lambda b, be, nv: (layer, be[b], 0, 0)),
                  pl.BlockSpec((1, 1, D, EXPERT_DIM), lambda b, be, nv: (layer, be[b], 0, 0)),
                  pl.BlockSpec((1, 1, EXPERT_DIM, D), lambda b, be, nv: (layer, be[b], 0, 0))],
        out_specs=pl.BlockSpec(blk, lambda b, be, nv: (0, b, 0)),
    )
    return pl.pallas_call(
        _experts_kernel,
        grid_spec=grid_spec,
        out_shape=jax.ShapeDtypeStruct((N_PIECES, R, PIECE), jnp.int32),
        compiler_params=_params("arbitrary"),
        name="moe_experts",
    )(block_e, nvalid, xb, wg, wu, wd)


def _combine_kernel(x_ref, ufp_ref, yg_ref, ew_ref, gf_ref, sg_ref, su_ref, sd_ref, g_ref, b_ref, o_ref):
    pieces = [_unpack_pairs(ufp_ref[h]) for h in range(N_PIECES)]
    hs = _silu(_mm_pieces(pieces, sg_ref)) * _mm_pieces(pieces, su_ref)
    f = _mm(hs, sd_ref[...])
    ew = ew_ref[...]
    lane = lax.broadcasted_iota(jnp.int32, ew.shape, 1)
    acc = [[jnp.zeros((x_ref.shape[0], PIECE), F32) for _ in range(N_PIECES)] for _ in range(2)]
    for k in range(TOP_K):
        wk = jnp.sum(jnp.where(lane == k, ew, jnp.zeros_like(ew)), axis=1, keepdims=True)
        for h in range(N_PIECES):
            ylo, yhi = _unpack_pairs(yg_ref[h, k])
            acc[0][h] = acc[0][h] + wk * ylo
            acc[1][h] = acc[1][h] + wk * yhi
    f = f + jnp.concatenate(acc[0] + acc[1], axis=1)
    o_ref[...] = _layernorm(DEEPNORM_ALPHA * x_ref[...] + gf_ref[0] * f, g_ref[...], b_ref[...])


def _combine_call(x1, ufp, yg, ew, mod_l, sg, su, sd, g, b, B, S, C, part):
    T, D = x1.shape
    tpb, ctiles = S // TM, C // TM
    first = part * (T // TM)
    row = lambda i: (i, 0)
    const = lambda i: (0, 0)
    return pl.pallas_call(
        _combine_kernel,
        grid=(T // TM,),
        in_specs=[pl.BlockSpec((TM, D), row), pl.BlockSpec((N_PIECES, TM, PIECE), lambda i: (0, i, 0)),
                  pl.BlockSpec((N_PIECES, TOP_K, TM, PIECE), lambda i: (0, 0, i, 0)),
                  pl.BlockSpec((TM, LANES), row),
                  pl.BlockSpec((1, 1, D), _mod_index(tpb, ctiles, 5, first)),
                  pl.BlockSpec((D, SHARED_DIM), const), pl.BlockSpec((D, SHARED_DIM), const),
                  pl.BlockSpec((SHARED_DIM, D), const),
                  pl.BlockSpec((1, D), const), pl.BlockSpec((1, D), const)],
        out_specs=pl.BlockSpec((TM, D), row),
        out_shape=jax.ShapeDtypeStruct((T, D), F32),
        compiler_params=_params("parallel"),
        name="moe_combine_norm",
    )(x1, ufp, yg, ew, mod_l, sg, su, sd, g, b)


def _moe_dispatch(ufp, eidx_t, rank_t, counts):
    T = ufp.shape[1]
    n_blocks = -(-(T * TOP_K + N_EXPERTS * (EXPERT_BLOCK - 1)) // EXPERT_BLOCK)
    n_rows = n_blocks * EXPERT_BLOCK
    padded = (counts + EXPERT_BLOCK - 1) // EXPERT_BLOCK * EXPERT_BLOCK
    pad_end = jnp.cumsum(padded)
    start_pad = pad_end - padded
    experts = jnp.arange(N_EXPERTS, dtype=jnp.int32)

    def lookup(table, idx):
        sel = idx[None] == experts.reshape((N_EXPERTS,) + (1,) * idx.ndim)
        return jnp.sum(jnp.where(sel, table.reshape((N_EXPERTS,) + (1,) * idx.ndim), 0), axis=0)

    dest_t = lookup(start_pad, eidx_t) + rank_t
    blk = jnp.arange(n_blocks, dtype=jnp.int32) * EXPERT_BLOCK
    block_e = jnp.minimum(jnp.sum((blk[:, None] >= pad_end[None, :]).astype(jnp.int32), axis=1), N_EXPERTS - 1)
    nvalid = jnp.clip(lookup(counts, block_e) - (blk - lookup(start_pad, block_e)), 0, EXPERT_BLOCK)
    row_tok = _sc_invert_rows(dest_t.reshape(-1), jnp.arange(n_rows, dtype=jnp.int32) % T, T)
    piece = jnp.arange(N_PIECES, dtype=jnp.int32)
    src = (piece[:, None] * T + row_tok[None, :]).reshape(1, N_PIECES * n_rows)
    xb = _sc_gather_rows(ufp.reshape(N_PIECES * T, PIECE), src).reshape(N_PIECES, n_rows, PIECE)
    return xb, block_e.astype(jnp.int32), nvalid.astype(jnp.int32), dest_t


def _moe_return(yb, dest_t):
    _, n_rows, _ = yb.shape
    T = dest_t.shape[1]
    piece = jnp.arange(N_PIECES, dtype=jnp.int32)
    back = (piece[:, None, None] * n_rows + dest_t[None]).reshape(1, N_PIECES * TOP_K * T)
    yg = _sc_gather_rows(yb.reshape(N_PIECES * n_rows, PIECE), back)
    return yg.reshape(N_PIECES, TOP_K, T, PIECE)


def _rot_cols(w, half):
    return jnp.concatenate([-w[:, half:], w[:, :half]], axis=1)


def _prep_w_in(w):
    d = w.shape[0]
    offs = np.cumsum((0,) + IN_SIZES)
    cq, ckv, kr, dqkv, da, db, dz, gq, gk, gv, gates = (w[:, offs[t]:offs[t + 1]] for t in range(len(IN_SIZES)))
    z = lambda n: jnp.zeros((d, n), w.dtype)
    krg = jnp.concatenate([z(64), kr, z(32)], 1)
    krr = jnp.concatenate([z(64), _rot_cols(kr, MLA_ROPE // 2), z(32)], 1)
    ab = jnp.concatenate([da, db, z(LANES - 4 * DN_HEADS)], 1)
    hd = GQA_HEAD_DIM
    gq_rot = jnp.concatenate([_rot_cols(gq[:, h * hd:(h + 1) * hd], hd // 2) for h in range(GQA_HEADS)], 1)
    dup = lambda t: jnp.concatenate([t[:, 0:hd], t[:, 0:hd], t[:, hd:2 * hd], t[:, hd:2 * hd]], 1)
    gk_rot = jnp.concatenate([_rot_cols(gk[:, h * hd:(h + 1) * hd], hd // 2) for h in range(GQA_KV_HEADS)], 1)
    cat = jnp.concatenate([cq, ckv, krg, krr, dqkv, ab, dz, gq, gq_rot, dup(gk), dup(gk_rot), dup(gv), gates], 1)
    assert cat.shape[1] == NZ
    return cat.astype(MXU_DTYPE)


def _prep_w_uq(w):
    d = w.shape[0]
    hw = MLA_NOPE + MLA_ROPE
    a, b = [], []
    for h in range(MLA_HEADS):
        wh = w[:, h * hw:(h + 1) * hw]
        a += [wh, jnp.zeros((d, LANES - hw), w.dtype)]
        b += [jnp.zeros((d, MLA_NOPE), w.dtype), _rot_cols(wh[:, MLA_NOPE:], MLA_ROPE // 2),
              jnp.zeros((d, LANES - hw), w.dtype)]
    return jnp.concatenate(a + b, 1).astype(MXU_DTYPE)


def _prep_w_ukv(w):
    d = w.shape[0]
    hw = MLA_NOPE + MLA_V
    kpart, vpart = [], []
    for h in range(MLA_HEADS):
        wh = w[:, h * hw:(h + 1) * hw]
        kpart += [wh[:, :MLA_NOPE], jnp.zeros((d, LANES - MLA_NOPE), w.dtype)]
        vpart += [wh[:, MLA_NOPE:]]
    return jnp.concatenate(kpart + vpart, 1).astype(MXU_DTYPE)


def _rope_tables(n_rows, C):
    row = jnp.repeat(jnp.arange(n_rows, dtype=F32), GRID_W)
    col = jnp.tile(jnp.arange(GRID_W, dtype=F32), n_rows)

    def angles(dim):
        n = dim // 4
        inv = ROPE_BASE ** (-jnp.arange(n, dtype=F32) / n)
        return jnp.concatenate([row[:, None] * inv, col[:, None] * inv], axis=-1)

    def with_ctx(cos, sin):
        return (jnp.concatenate([jnp.ones((C, LANES), F32), cos], 0),
                jnp.concatenate([jnp.zeros((C, LANES), F32), sin], 0))

    L = n_rows * GRID_W
    am = angles(MLA_ROPE)
    one, zero = jnp.ones((L, MLA_NOPE), F32), jnp.zeros((L, MLA_NOPE), F32)
    cm = jnp.concatenate([one, jnp.cos(am), jnp.cos(am), one[:, :32]], 1)
    sm = jnp.concatenate([zero, jnp.sin(am), jnp.sin(am), zero[:, :32]], 1)
    ag = angles(GQA_HEAD_DIM)
    cg = jnp.tile(jnp.cos(ag), (1, 4))
    sg = jnp.tile(jnp.sin(ag), (1, 4))
    return with_ctx(cm, sm) + with_ctx(cg, sg)


def kernel(x, c, ctx, c_ctx, w_ada, b_ada, w_in, mla_q_norm, mla_kv_norm, w_uq, w_ukv, dn_conv, dn_a_log, dn_dt_bias, dn_norm, gqa_sink, w_o_mla, w_o_dn, w_o_gqa, w_out, ln1_g, ln1_b, w_router, router_bias, w_exp_gate, w_exp_up, w_exp_down, w_sh_gate, w_sh_up, w_sh_down, ln2_g, ln2_b):
    B, L, D = x.shape
    C = ctx.shape[1]
    S = C + L
    nl = w_in.shape[0]
    assert D == D_MODEL and nl == DEPTH and B <= CTX_MOD_ROW
    assert C % TM == 0 and L % TM == 0 and L % GRID_W == 0 and L >= 3 * WINDOW
    cast = lambda t: t.astype(MXU_DTYPE)

    cc = jnp.zeros((MOD_ROWS, D), F32).at[0:B].set(c).at[CTX_MOD_ROW].set(c_ctx)
    mods = _ada_call(cc, w_ada, b_ada)
    tabs = _rope_tables(L // GRID_W, C)
    dn_consts = _dn_constants()
    xs = jnp.concatenate([ctx, x], axis=1).reshape(B * S, D)

    for l in range(nl):
        mod_l = mods[l].reshape(MOD_ROWS * 6, 1, D)
        q, k, v, dqkv, ab, dz, gq, gkv, gates = _inproj_call(
            xs, mod_l, _prep_w_in(w_in[l]), _prep_w_uq(w_uq[l]), _prep_w_ukv(w_ukv[l]),
            mla_q_norm[l].reshape(1, -1), mla_kv_norm[l].reshape(1, -1), tabs, B, S, C)
        omla = _mla_call(q, k, v, B, S, C)
        conv8 = jnp.zeros((8, 3 * DN_WIDTH), F32).at[0:DN_CONV].set(dn_conv[l])
        gp = (jnp.zeros((8, LANES), F32).at[0, 0:2 * DN_HEADS].set(dn_a_log[l].reshape(-1))
              .at[1, 0:2 * DN_HEADS].set(dn_dt_bias[l].reshape(-1)))
        local = _dnlocal_call(dqkv, ab, conv8, gp, dn_consts, B, S, C)
        of, ob = _dnscan_call(local, B, S, C)
        ogqa = _gqa_call(gq, gkv, gqa_sink[l], B, S, C)
        wr = jnp.zeros((LANES, D), F32).at[0:N_EXPERTS].set(w_router[l].T)
        rb = jnp.zeros((LANES, 1), F32).at[0:N_EXPERTS, 0].set(router_bias[l])
        merged, routed = [], []
        for part in range(MOE_PARTS):
            m = _merge_call(
                xs, mod_l, omla, of, ob, dz, ogqa, gates,
                cast(w_o_mla[l]), cast(w_o_dn[l]), cast(w_o_gqa[l]), cast(w_out[l]),
                jnp.tile(dn_norm[l], DN_HEADS).reshape(1, DN_WIDTH), dn_consts[0],
                ln1_g[l].reshape(1, D), ln1_b[l].reshape(1, D), cast(wr), rb, B, S, C, part, MOE_PARTS)
            merged.append(m)
            routed.append(_moe_dispatch(m[1], m[2], m[3], m[5][:, 0].astype(jnp.int32)))
        outs = []
        for part in range(MOE_PARTS):
            x1, ufp, _, _, ew, _ = merged[part]
            xb, block_e, nvalid, dest_t = routed[part]
            yb = _experts_call(xb, block_e, nvalid, w_exp_gate, w_exp_up, w_exp_down, l)
            yg = _moe_return(yb, dest_t)
            outs.append(_combine_call(x1, ufp, yg, ew, mod_l, cast(w_sh_gate[l]), cast(w_sh_up[l]),
                                      cast(w_sh_down[l]), ln2_g[l].reshape(1, D), ln2_b[l].reshape(1, D),
                                      B, S, C, part))
        xs = jnp.concatenate(outs, axis=0)
    return xs.reshape(B, S, D)[:, C:, :]
```

```python
import functools

import numpy as np
import jax
import jax.numpy as jnp
from jax import lax
from jax.experimental import pallas as pl
from jax.experimental.pallas import tpu as pltpu
from jax.experimental.pallas import tpu_sc as plsc

F32 = jnp.float32
MXU_DTYPE = jnp.bfloat16

D_MODEL = 1024
DEPTH = 4
GRID_W = 64
NORM_EPS = 1e-6
ROPE_BASE = 10000.0
DEEPNORM_ALPHA = (2.0 * DEPTH) ** 0.25

MLA_HEADS = 8
MLA_Q_LORA = 256
MLA_KV_LORA = 128
MLA_NOPE = 64
MLA_ROPE = 32
MLA_V = 64
MLA_SCALE = (MLA_NOPE + MLA_ROPE) ** -0.5
LOG2E = float(np.log2(np.e))

DN_HEADS = 8
DN_HEAD_DIM = 64
DN_WIDTH = DN_HEADS * DN_HEAD_DIM
DN_CONV = 5
DN_CHUNK = 64

GQA_HEADS = 8
GQA_KV_HEADS = 2
GQA_HEAD_DIM = 64
GQA_SCALE = GQA_HEAD_DIM ** -0.5
WINDOW = 128

N_EXPERTS = 64
TOP_K = 8
N_GROUPS = 8
TOPK_GROUPS = 4
EXPERT_DIM = 256
SHARED_DIM = 256
ROUTED_SCALE = 2.5

IN_SIZES = (MLA_Q_LORA, MLA_KV_LORA, MLA_ROPE,
            3 * DN_WIDTH, 2 * DN_HEADS, 2 * DN_HEADS, DN_WIDTH,
            GQA_HEADS * GQA_HEAD_DIM, GQA_KV_HEADS * GQA_HEAD_DIM, GQA_KV_HEADS * GQA_HEAD_DIM,
            3 * D_MODEL)

LANES = 128
TM = 256
MOD_ROWS = 16
CTX_MOD_ROW = 8

OFF_A = 0
OFF_DQKV = 640
OFF_AB = OFF_DQKV + 3 * DN_WIDTH
OFF_DZ = OFF_AB + LANES
OFF_GQ = OFF_DZ + DN_WIDTH
OFF_GK = OFF_GQ + 1024
OFF_GATES = OFF_GK + 768
NZ = OFF_GATES + 3 * D_MODEL

VMEM_LIMIT = 56 * 1024 * 1024


def _mm(a, b):
    return jnp.dot(a.astype(MXU_DTYPE), b.astype(MXU_DTYPE), preferred_element_type=F32)


def _mm_nt(a, b):
    return lax.dot_general(a.astype(MXU_DTYPE), b.astype(MXU_DTYPE), (((1,), (1,)), ((), ())),
                           preferred_element_type=F32)


def _mm_tn(a, b):
    return lax.dot_general(a.astype(MXU_DTYPE), b.astype(MXU_DTYPE), (((0,), (0,)), ((), ())),
                           preferred_element_type=F32)


def _bmm(a, b):
    return jnp.einsum('cik,ckj->cij', a.astype(MXU_DTYPE), b.astype(MXU_DTYPE), preferred_element_type=F32)


def _bmm_nt(a, b):
    return jnp.einsum('cik,cjk->cij', a.astype(MXU_DTYPE), b.astype(MXU_DTYPE), preferred_element_type=F32)


def _split3(x):
    hi = x.astype(jnp.bfloat16).astype(F32)
    r = x - hi
    mid = r.astype(jnp.bfloat16).astype(F32)
    lo = (r - mid).astype(jnp.bfloat16).astype(F32)
    return hi, mid, lo


def _exact_mm(x, m01):
    hi, mid, lo = _split3(x)
    return _mm(hi, m01) + _mm(mid, m01) + _mm(lo, m01)


def _exact_mm_left(m01, x):
    hi, mid, lo = _split3(x)
    return _mm(m01, hi) + _mm(m01, mid) + _mm(m01, lo)


def _silu(x):
    return x * jax.nn.sigmoid(x)


def _layernorm(v, g, b):
    mu = jnp.mean(v, -1, keepdims=True)
    d = v - mu
    var = jnp.mean(d * d, -1, keepdims=True)
    return d * lax.rsqrt(var + NORM_EPS) * g + b


def _mod_index(tiles_per_b, ctx_tiles, k):
    def index(i):
        row = jnp.where((i % tiles_per_b) < ctx_tiles, CTX_MOD_ROW, i // tiles_per_b)
        return (row * 6 + k, 0, 0)
    return index


def _params(*sem):
    return pltpu.CompilerParams(dimension_semantics=sem, vmem_limit_bytes=VMEM_LIMIT)


def _ada_kernel(c_ref, w_ref, b_ref, o_ref):
    o_ref[0] = _mm(_silu(c_ref[...]), w_ref[0]) + b_ref[0]


def _ada_call(cc, w_ada, b_ada):
    nl, d, n6 = w_ada.shape
    tn = 1536
    return pl.pallas_call(
        _ada_kernel,
        grid=(nl, n6 // tn),
        in_specs=[pl.BlockSpec((MOD_ROWS, d), lambda l, j: (0, 0)),
                  pl.BlockSpec((1, d, tn), lambda l, j: (l, 0, j)),
                  pl.BlockSpec((1, 1, tn), lambda l, j: (l, 0, j))],
        out_specs=pl.BlockSpec((1, MOD_ROWS, tn), lambda l, j: (l, 0, j)),
        out_shape=jax.ShapeDtypeStruct((nl, MOD_ROWS, n6), F32),
        compiler_params=_params("parallel", "parallel"),
        name="ada_mod",
    )(cc, w_ada, b_ada.reshape(nl, 1, n6))


def _inproj_kernel(x_ref, sh_ref, sc_ref, w_ref, wuq_ref, wukv_ref, qn_ref, kvn_ref,
                   cm_ref, sm_ref, cg_ref, sg_ref,
                   q_out, k_out, v_out, dqkv_out, ab_out, dz_out, gq_out, gkv_out, gates_out):
    u = (x_ref[...] * (1.0 + sc_ref[0]) + sh_ref[0]).astype(MXU_DTYPE)

    def z(off, width):
        return jnp.dot(u, w_ref[:, off:off + width], preferred_element_type=F32)

    def rms(v, g):
        return v * lax.rsqrt(jnp.mean(v * v, -1, keepdims=True) + NORM_EPS) * g

    cm, sm, cg, sg = cm_ref[...], sm_ref[...], cg_ref[...], sg_ref[...]

    za = z(OFF_A, 640)
    qq = _mm(rms(za[:, 0:256], qn_ref[...]), wuq_ref[...])
    kvv = _mm(rms(za[:, 256:384], kvn_ref[...]), wukv_ref[...])
    k_rope = za[:, 384:512] * cm + za[:, 512:640] * sm
    for h in range(MLA_HEADS):
        sl = slice(h * LANES, (h + 1) * LANES)
        qa = qq[:, h * LANES:(h + 1) * LANES]
        qb = qq[:, 1024 + h * LANES:1024 + (h + 1) * LANES]
        q_out[:, sl] = ((qa * cm + qb * sm) * (MLA_SCALE * LOG2E)).astype(q_out.dtype)
        k_out[:, sl] = (kvv[:, sl] + k_rope).astype(k_out.dtype)
    v_out[...] = kvv[:, 1024:1536].astype(v_out.dtype)

    for t in range(3):
        dqkv_out[:, t * 512:(t + 1) * 512] = z(OFF_DQKV + t * 512, 512).astype(dqkv_out.dtype)
    ab_out[...] = z(OFF_AB, LANES)
    dz_out[...] = z(OFF_DZ, DN_WIDTH).astype(dz_out.dtype)

    zq = z(OFF_GQ, 1024)
    for p in range(4):
        sl = slice(p * LANES, (p + 1) * LANES)
        gq_out[:, sl] = ((zq[:, sl] * cg + zq[:, 512 + p * LANES:512 + (p + 1) * LANES] * sg)
                         * (GQA_SCALE * LOG2E)).astype(gq_out.dtype)
    zk = z(OFF_GK, 768)
    for j in range(2):
        sl = slice(j * LANES, (j + 1) * LANES)
        gkv_out[:, sl] = (zk[:, sl] * cg + zk[:, 256 + j * LANES:256 + (j + 1) * LANES] * sg).astype(gkv_out.dtype)
    gkv_out[:, 256:512] = zk[:, 512:768].astype(gkv_out.dtype)

    for t in range(3):
        gates_out[:, t * 1024:(t + 1) * 1024] = jax.nn.sigmoid(z(OFF_GATES + t * 1024, 1024)).astype(gates_out.dtype)


def _inproj_call(xs, mod_l, w_cat, wuq_cat, wukv_cat, qn, kvn, tabs, B, S, C):
    T, D = xs.shape
    tpb, ctiles = S // TM, C // TM
    act = MXU_DTYPE
    row = lambda i: (i, 0)
    const = lambda i: (0, 0)
    tab = lambda i: (i % tpb, 0)
    widths = (1024, 1024, 512, 1536, LANES, 512, 512, 512, 3072)
    dtypes = (act, act, act, act, F32, act, act, act, act)
    return pl.pallas_call(
        _inproj_kernel,
        grid=(T // TM,),
        in_specs=[pl.BlockSpec((TM, D), row),
                  pl.BlockSpec((1, 1, D), _mod_index(tpb, ctiles, 0)),
                  pl.BlockSpec((1, 1, D), _mod_index(tpb, ctiles, 1)),
                  pl.BlockSpec((D, NZ), const),
                  pl.BlockSpec((MLA_Q_LORA, 2048), const),
                  pl.BlockSpec((MLA_KV_LORA, 1536), const),
                  pl.BlockSpec((1, MLA_Q_LORA), const),
                  pl.BlockSpec((1, MLA_KV_LORA), const)]
                 + [pl.BlockSpec((TM, LANES), tab)] * 4,
        out_specs=[pl.BlockSpec((TM, w), row) for w in widths],
        out_shape=[jax.ShapeDtypeStruct((T, w), dt) for w, dt in zip(widths, dtypes)],
        compiler_params=_params("parallel"),
        name="in_proj",
    )(xs, mod_l, mod_l, w_cat, wuq_cat, wukv_cat, qn, kvn, *tabs)


def _mla_kernel(q_ref, k_ref, v_ref, o_ref, *, n_ctx, n_all):
    i = pl.program_id(2)
    tq = q_ref.shape[1]
    left = lax.broadcasted_iota(jnp.int32, (tq, LANES), 1) < MLA_V

    def attend(nk, ctx_rows):
        v = v_ref[0, 0:nk, :]
        v_left = lax.broadcasted_iota(jnp.int32, v.shape, 1) < MLA_V
        ones = jnp.ones_like(v)
        s = [_mm_nt(q_ref[0, :, hh * LANES:(hh + 1) * LANES], k_ref[0, 0:nk, hh * LANES:(hh + 1) * LANES])
             for hh in range(2)]
        if ctx_rows:
            row = lax.broadcasted_iota(jnp.int32, (tq, nk), 0)
            col = lax.broadcasted_iota(jnp.int32, (tq, nk), 1)
            ok = (row >= ctx_rows) | (col < ctx_rows)
            s = [jnp.where(ok, x, -jnp.inf) for x in s]
        p = [jnp.exp2((x - jnp.max(x, -1, keepdims=True)).astype(MXU_DTYPE)) for x in s]
        o = [_mm(p[0], jnp.where(v_left, v, ones)), _mm(p[1], jnp.where(v_left, ones, v))]
        outs = [x / pltpu.roll(x, MLA_V, axis=1) for x in o]
        o_ref[0] = jnp.where(left, outs[0], outs[1]).astype(o_ref.dtype)

    if tq > n_ctx:
        @pl.when(i == 0)
        def _():
            attend(n_all, n_ctx)

        @pl.when(i > 0)
        def _():
            attend(n_all, 0)
    else:
        ctx_tiles = n_ctx // tq

        @pl.when(i < ctx_tiles)
        def _():
            attend(n_ctx, 0)

        @pl.when(i >= ctx_tiles)
        def _():
            attend(n_all, 0)


MLA_TQ = (544, 384, 256)


def _mla_call(q, k, v, B, S, C):
    tq = next(t for t in MLA_TQ if S % t == 0)
    assert C % tq == 0 or tq > C
    q3, k3, v3 = q.reshape(B, S, 1024), k.reshape(B, S, 1024), v.reshape(B, S, 512)
    out = pl.pallas_call(
        functools.partial(_mla_kernel, n_ctx=C, n_all=S),
        grid=(B, MLA_HEADS // 2, S // tq),
        in_specs=[pl.BlockSpec((1, tq, 2 * LANES), lambda b, j, i: (b, i, j)),
                  pl.BlockSpec((1, S, 2 * LANES), lambda b, j, i: (b, 0, j)),
                  pl.BlockSpec((1, S, LANES), lambda b, j, i: (b, 0, j))],
        out_specs=pl.BlockSpec((1, tq, LANES), lambda b, j, i: (b, i, j)),
        out_shape=jax.ShapeDtypeStruct((B, S, 512), MXU_DTYPE),
        compiler_params=_params("parallel", "parallel", "arbitrary"),
        name="mla_attn",
    )(q3, k3, v3)
    return out.reshape(B * S, 512)


def _gqa_kernel(sink_ref, q_ref, kv_ref, o_ref, *, n_ctx, n_all):
    i = pl.program_id(1)
    qb_rows = q_ref.shape[1]
    span = qb_rows + 2 * WINDOW
    ctx_blocks = n_ctx // qb_rows
    group = GQA_HEADS // GQA_KV_HEADS
    left = lax.broadcasted_iota(jnp.int32, (qb_rows, LANES), 1) < GQA_HEAD_DIM

    def run(latent):
        nk = n_ctx + span if latent else n_ctx
        rows = group * qb_rows
        if latent:
            qb = i - ctx_blocks
            ws = pl.multiple_of(jnp.minimum(n_ctx + qb * qb_rows - WINDOW, n_all - span), WINDOW)
            col = lax.broadcasted_iota(jnp.int32, (rows, nk), 1)
            q_pos = qb * qb_rows + lax.broadcasted_iota(jnp.int32, (rows, nk), 0) % qb_rows
            k_pos = ws - 2 * n_ctx + col
            ok = (col < n_ctx) | ((jnp.abs(k_pos - q_pos) <= WINDOW) & (k_pos >= 0))
        head_of_row = lax.broadcasted_iota(jnp.int32, (rows, 1), 0) // qb_rows
        res = []
        for j in range(GQA_KV_HEADS):
            parts = []
            for g in range(group):
                qp = q_ref[0, :, (2 * j + g // 2) * LANES:(2 * j + g // 2 + 1) * LANES]
                parts.append(jnp.where(left if g % 2 == 0 else ~left, qp, jnp.zeros_like(qp)))
            q4 = jnp.concatenate(parts, axis=0)
            kc = kv_ref[0, 0:n_ctx, j * LANES:(j + 1) * LANES]
            vc = kv_ref[0, 0:n_ctx, 256 + j * LANES:256 + (j + 1) * LANES]
            if latent:
                kc = jnp.concatenate([kc, kv_ref[0, pl.ds(ws, span), j * LANES:(j + 1) * LANES]], axis=0)
                vc = jnp.concatenate([vc, kv_ref[0, pl.ds(ws, span), 256 + j * LANES:256 + (j + 1) * LANES]], axis=0)
            sink = jnp.zeros((rows, 1), F32)
            for g in range(group):
                sink = jnp.where(head_of_row == g, sink_ref[group * j + g] * LOG2E, sink)
            s = _mm_nt(q4, kc)
            if latent:
                s = jnp.where(ok, s, -jnp.inf)
            m = jnp.maximum(jnp.max(s, -1, keepdims=True), sink)
            p = jnp.exp2((s - m).astype(MXU_DTYPE))
            v_left = lax.broadcasted_iota(jnp.int32, vc.shape, 1) < GQA_HEAD_DIM
            o = _mm(p, jnp.where(v_left, vc, jnp.ones_like(vc)))
            res.append(o / (pltpu.roll(o, GQA_HEAD_DIM, axis=1) + jnp.exp2(sink - m)))
        for j in range(GQA_KV_HEADS):
            for pp in range(group // 2):
                r0 = res[j][(2 * pp) * qb_rows:(2 * pp + 1) * qb_rows]
                r1 = res[j][(2 * pp + 1) * qb_rows:(2 * pp + 2) * qb_rows]
                pair = 2 * j + pp
                o_ref[0, :, pair * LANES:(pair + 1) * LANES] = jnp.where(
                    left, r0, pltpu.roll(r1, GQA_HEAD_DIM, axis=1)).astype(o_ref.dtype)

    @pl.when(i < ctx_blocks)
    def _():
        run(False)

    @pl.when(i >= ctx_blocks)
    def _():
        run(True)


GQA_QB = 256


def _gqa_call(gq, gkv, sink, B, S, C):
    qb_rows = GQA_QB
    out = pl.pallas_call(
        functools.partial(_gqa_kernel, n_ctx=C, n_all=S),
        grid=(B, S // qb_rows),
        in_specs=[pl.BlockSpec(memory_space=pltpu.SMEM),
                  pl.BlockSpec((1, qb_rows, 512), lambda b, i: (b, i, 0)),
                  pl.BlockSpec((1, S, 512), lambda b, i: (b, 0, 0))],
        out_specs=pl.BlockSpec((1, qb_rows, 512), lambda b, i: (b, i, 0)),
        out_shape=jax.ShapeDtypeStruct((B, S, 512), MXU_DTYPE),
        compiler_params=_params("parallel", "arbitrary"),
        name="gqa_attn",
    )(sink, gq.reshape(B, S, 512), gkv.reshape(B, S, 512))
    return out.reshape(B * S, 512)


DN_TR = 256
DN_HALO = 16
DN_CPT = DN_TR // DN_CHUNK


def _stack(x, left):
    z = jnp.zeros_like(x)
    return jnp.concatenate([jnp.where(left, x, z), jnp.where(left, z, x)], axis=1)


def _dnlocal_kernel(main_ref, prev_ref, next_ref, ab_ref, conv_ref, gp_ref, bd_ref, trif_ref, trib_ref,
                    eg_ref, eb_ref,
                    u_out, w_out, qg_out, kg_out, qk_out, gl_out, pad_ref, *, ctx_tiles, n_tiles):
    i = pl.program_id(1)
    tr = DN_TR
    first = (i == 0) | (i == ctx_tiles)
    last = (i == ctx_tiles - 1) | (i == n_tiles - 1)
    xp = prev_ref[0].astype(F32)
    xn = next_ref[0].astype(F32)
    pad_ref[0:DN_HALO, :] = jnp.where(first, jnp.zeros_like(xp), xp)
    pad_ref[DN_HALO:DN_HALO + tr, :] = main_ref[0].astype(F32)
    pad_ref[DN_HALO + tr:, :] = jnp.where(last, jnp.zeros_like(xn), xn)
    y = jnp.zeros((tr, 3 * DN_WIDTH), F32)
    for t in range(DN_CONV):
        y = y + conv_ref[t:t + 1, :] * pad_ref[pl.ds(DN_HALO - DN_CONV // 2 + t, tr), :]
    y = _silu(y)
    q, k, v = y[:, 0:512], y[:, 512:1024], y[:, 1024:1536]
    bd = bd_ref[...]
    q = q * lax.rsqrt(_exact_mm(q * q, bd) + 1e-6) * (DN_HEAD_DIM ** -0.5)
    k = k * lax.rsqrt(_exact_mm(k * k, bd) + 1e-6)

    ab = ab_ref[0]
    g = -jnp.exp(gp_ref[0:1, :]) * jax.nn.softplus(ab + gp_ref[1:2, :])
    beta = jax.nn.sigmoid(ab)
    lane = lax.broadcasted_iota(jnp.int32, (tr, LANES), 1)
    gc = jnp.where(lane < DN_HEADS, _exact_mm_left(trif_ref[...], g), _exact_mm_left(trib_ref[...], g))
    gcx_all = _exact_mm(gc, eg_ref[...])
    bx_all = _exact_mm(beta, eb_ref[...])

    c = DN_CPT
    lane3 = lax.broadcasted_iota(jnp.int32, (1, 1, LANES), 2)
    left = (lane3 % LANES) < DN_HEAD_DIM
    tpos = lane3 % DN_HEAD_DIM
    lane6 = lax.broadcasted_iota(jnp.int32, (1, 1, 2 * LANES), 2)
    left6 = (lane6 % LANES) < DN_HEAD_DIM
    ri = lax.broadcasted_iota(jnp.int32, (1, DN_CHUNK, LANES), 1)
    cj = lax.broadcasted_iota(jnp.int32, (1, DN_CHUNK, LANES), 2) % DN_HEAD_DIM
    one = jnp.ones((), F32)
    zero = jnp.zeros((), F32)

    units = [(d, j) for d in range(2) for j in range(4)]
    st = {}
    for d, j in units:
        off = d * 512 + j * LANES
        gcx = gcx_all[:, off:off + LANES].reshape(c, DN_CHUNK, LANES)
        bx = bx_all[:, off:off + LANES].reshape(c, DN_CHUNK, LANES)
        qp = q[:, j * LANES:(j + 1) * LANES].reshape(c, DN_CHUNK, LANES)
        kp = k[:, j * LANES:(j + 1) * LANES].reshape(c, DN_CHUNK, LANES)
        vp = v[:, j * LANES:(j + 1) * LANES].reshape(c, DN_CHUNK, LANES)
        gl = gcx[:, DN_CHUNK - 1:DN_CHUNK, :] if d == 0 else gcx[:, 0:1, :]
        kb = kp * bx
        kq = _bmm_nt(jnp.concatenate([kb, qp], axis=1), _stack(kp, left))
        hi, mid, lo = _split3(gcx)
        a6 = jnp.where(tpos == 0, hi, jnp.where(tpos == 1, mid, jnp.where(tpos == 2, lo,
                       jnp.where(tpos < 6, one, zero))))
        b6 = jnp.where(tpos < 3, one, jnp.where(tpos == 3, -hi, jnp.where(tpos == 4, -mid,
                       jnp.where(tpos == 5, -lo, zero))))
        diff = _bmm_nt(a6, _stack(b6, left))
        st[d, j] = dict(gcx=gcx, bx=bx, qp=qp, kp=kp, vp=vp, gl=gl, kb=kb, kq=kq, diff=diff)
    for d, j in units:
        u = st[d, j]
        incl = (ri >= cj) if d == 0 else (ri <= cj)
        strict = (ri > cj) if d == 0 else (ri < cj)
        dm = jnp.exp(jnp.where(incl, u["diff"], -jnp.inf))
        u["qkm"] = u["kq"][:, DN_CHUNK:, :] * dm
        u["x"] = -jnp.where(strict, u["kq"][:, 0:DN_CHUNK, :] * dm, zero)
        u["r"] = u["x"]
    for d, j in units:
        u = st[d, j]
        u["x"] = _bmm(u["x"], _stack(u["x"], left))
    for level in range(5):
        for d, j in units:
            u = st[d, j]
            xs = _stack(u["x"], left)
            if level < 4:
                m = _bmm(jnp.concatenate([u["r"], u["x"]], axis=1), xs)
                u["r"] = u["r"] + u["x"] + m[:, 0:DN_CHUNK, :]
                u["x"] = m[:, DN_CHUNK:, :]
            else:
                u["r"] = u["r"] + u["x"] + _bmm(u["r"], xs)
    for d, j in units:
        u = st[d, j]
        eg = jnp.exp(u["gcx"])
        rhs = jnp.concatenate([u["vp"] * u["bx"], u["kb"] * eg], axis=-1)
        sol = rhs + _bmm(u["r"], _stack(rhs, left6))
        sl = slice(j * LANES, (j + 1) * LANES)
        u_out[0, d, :, sl] = sol[:, :, 0:LANES].reshape(tr, LANES)
        w_out[0, d, :, sl] = sol[:, :, LANES:].reshape(tr, LANES).astype(w_out.dtype)
        qg_out[0, d, :, sl] = (u["qp"] * eg).reshape(tr, LANES).astype(qg_out.dtype)
        kg_out[0, d, :, sl] = (u["kp"] * jnp.exp(u["gl"] - u["gcx"])).reshape(tr, LANES).astype(kg_out.dtype)
        qk_out[0, d, :, sl] = u["qkm"].reshape(tr, LANES).astype(qk_out.dtype)
        gl_out[0, d, :, :, sl] = jnp.exp(u["gl"])


def _dn_constants():
    idx = np.arange(DN_TR)
    same = (idx[:, None] // DN_CHUNK) == (idx[None, :] // DN_CHUNK)
    trif = (same & (idx[None, :] <= idx[:, None])).astype(np.float32)
    trib = (same & (idx[None, :] >= idx[:, None])).astype(np.float32)
    h = np.arange(512)
    bd = ((h[:, None] // DN_HEAD_DIM) == (h[None, :] // DN_HEAD_DIM)).astype(np.float32)
    col = np.arange(LANES)[:, None]
    out = np.arange(1024)[None, :]
    unit = (out // 512) * DN_HEADS + (out % 512) // DN_HEAD_DIM
    eg = (col == unit).astype(np.float32)
    eb = (col == unit + 2 * DN_HEADS).astype(np.float32)
    return tuple(jnp.asarray(a, MXU_DTYPE) for a in (bd, trif, trib, eg, eb))


def _dnlocal_call(dqkv, ab, conv_w, gp, consts, B, S, C):
    tr = DN_TR
    n_tiles = S // tr
    hb = tr // DN_HALO
    n_hblk = S // DN_HALO
    bd, trif, trib, eg, eb = consts
    const2 = lambda b, i: (0, 0)
    big = lambda b, i: (b, 0, i, 0)
    act = MXU_DTYPE
    shp = (B, 2, S, 512)
    return pl.pallas_call(
        functools.partial(_dnlocal_kernel, ctx_tiles=C // tr, n_tiles=n_tiles),
        grid=(B, n_tiles),
        in_specs=[pl.BlockSpec((1, tr, 1536), lambda b, i: (b, i, 0)),
                  pl.BlockSpec((1, DN_HALO, 1536), lambda b, i: (b, jnp.maximum(i * hb - 1, 0), 0)),
                  pl.BlockSpec((1, DN_HALO, 1536), lambda b, i: (b, jnp.minimum((i + 1) * hb, n_hblk - 1), 0)),
                  pl.BlockSpec((1, tr, LANES), lambda b, i: (b, i, 0)),
                  pl.BlockSpec((8, 1536), const2),
                  pl.BlockSpec((8, LANES), const2),
                  pl.BlockSpec((512, 512), const2),
                  pl.BlockSpec((tr, tr), const2),
                  pl.BlockSpec((tr, tr), const2),
                  pl.BlockSpec((LANES, 1024), const2),
                  pl.BlockSpec((LANES, 1024), const2)],
        out_specs=[pl.BlockSpec((1, 2, tr, 512), big)] * 5
                  + [pl.BlockSpec((1, 2, DN_CPT, 1, 512), lambda b, i: (b, 0, i, 0, 0))],
        out_shape=[jax.ShapeDtypeStruct(shp, F32)] + [jax.ShapeDtypeStruct(shp, act)] * 4
                  + [jax.ShapeDtypeStruct((B, 2, S // DN_CHUNK, 1, 512), F32)],
        scratch_shapes=[pltpu.VMEM((tr + 2 * DN_HALO, 1536), F32)],
        compiler_params=_params("parallel", "parallel"),
        name="dn_local",
    )(dqkv.reshape(B, S, 1536), dqkv.reshape(B, S, 1536), dqkv.reshape(B, S, 1536), ab.reshape(B, S, LANES),
      conv_w, gp, bd, trif, trib, eg, eb)


def _dnscan_kernel(uf, wf, qgf, kgf, qkf, glf, ub, wb, qgb, kgb, qkb, glb, of_out, ob_out, s_ref):
    n = pl.program_id(1)
    lane = lax.broadcasted_iota(jnp.int32, (1, LANES), 1)
    left = lane < DN_HEAD_DIM
    row = lax.broadcasted_iota(jnp.int32, (LANES, LANES), 0)
    col = lax.broadcasted_iota(jnp.int32, (LANES, LANES), 1)
    same_head = (row < DN_HEAD_DIM) == (col < DN_HEAD_DIM)
    dirs = ((uf, wf, qgf, kgf, qkf, glf, of_out), (ub, wb, qgb, kgb, qkb, glb, ob_out))
    units = [(bb, d, j) for bb in range(uf.shape[0]) for d in range(2) for j in range(4)]
    sidx = lambda bb, d, j: (bb * 2 + d) * 4 + j
    sl = lambda j: slice(j * LANES, (j + 1) * LANES)
    started = n > 0
    st = {t: jnp.where(started, s_ref[sidx(*t)], jnp.zeros((LANES, LANES), F32)) for t in units}
    pre = {(bb, d, j): _mm(dirs[d][1][bb, 0, :, sl(j)], st[bb, d, j]) for bb, d, j in units}
    o1 = {(bb, d, j): _mm(dirs[d][2][bb, 0, :, sl(j)], st[bb, d, j]) for bb, d, j in units}
    vn = {(bb, d, j): dirs[d][0][bb, 0, :, sl(j)] - pre[bb, d, j] for bb, d, j in units}
    for bb, d, j in units:
        v = vn[bb, d, j]
        z = jnp.zeros_like(v)
        vst = jnp.concatenate([jnp.where(left, v, z), jnp.where(left, z, v)], axis=0)
        dirs[d][6][bb, :, sl(j)] = o1[bb, d, j] + _mm(dirs[d][4][bb, 0, :, sl(j)], vst)
    for bb, d, j in units:
        upd = _mm_tn(dirs[d][3][bb, 0, :, sl(j)], vn[bb, d, j])
        s_ref[sidx(bb, d, j)] = (st[bb, d, j] * dirs[d][5][bb, 0, 0, :, sl(j)]
                                 + jnp.where(same_head, upd, jnp.zeros_like(upd)))


DN_SCAN_BATCH = 8


def _dnscan_call(local, B, S, C):
    u, w, qg, kg, qk, gl = local
    nch, nc = S // DN_CHUNK, C // DN_CHUNK
    bb = DN_SCAN_BATCH if B % DN_SCAN_BATCH == 0 else 1

    def bidx(n):
        return jnp.where(n < nc, nc - 1 - n, nch - 1 + nc - n)

    fspec = pl.BlockSpec((bb, 1, DN_CHUNK, 512), lambda b, n: (b, 0, n, 0))
    bspec = pl.BlockSpec((bb, 1, DN_CHUNK, 512), lambda b, n: (b, 1, bidx(n), 0))
    fgl = pl.BlockSpec((bb, 1, 1, 1, 512), lambda b, n: (b, 0, n, 0, 0))
    bgl = pl.BlockSpec((bb, 1, 1, 1, 512), lambda b, n: (b, 1, bidx(n), 0, 0))
    return pl.pallas_call(
        _dnscan_kernel,
        grid=(B // bb, nch),
        in_specs=[fspec] * 5 + [fgl] + [bspec] * 5 + [bgl],
        out_specs=[pl.BlockSpec((bb, DN_CHUNK, 512), lambda b, n: (b, n, 0)),
                   pl.BlockSpec((bb, DN_CHUNK, 512), lambda b, n: (b, bidx(n), 0))],
        out_shape=[jax.ShapeDtypeStruct((B, S, 512), F32)] * 2,
        scratch_shapes=[pltpu.VMEM((bb * 8, LANES, LANES), F32)],
        compiler_params=_params("parallel", "arbitrary"),
        name="dn_scan",
    )(u, w, qg, kg, qk, gl, u, w, qg, kg, qk, gl)


def _pack_pairs(v):
    w = v.shape[1] // 2
    bits = lax.bitcast_convert_type(v.astype(jnp.bfloat16).astype(F32), jnp.int32)
    return lax.shift_right_logical(bits[:, :w], 16) | bits[:, w:]


def _unpack_pairs(p):
    lo = lax.bitcast_convert_type(lax.shift_left(p, 16), F32)
    hi = lax.bitcast_convert_type(p & jnp.int32(-65536), F32)
    return lo, hi


def _merge_kernel(x_ref, ga_ref, shf_ref, scf_ref, omla_ref, of_ref, ob_ref, dz_ref, ogqa_ref, gates_ref,
                  wo1_ref, wo2_ref, wo3_ref, wout_ref, dnn_ref, bd_ref, lng_ref, lnb_ref, wr_ref, rb_ref,
                  x1_out, ufp_out, eidx_out, rank_out, ew_out, cnt_out, cnt_ref):
    @pl.when(pl.program_id(0) == 0)
    def _():
        cnt_ref[...] = jnp.zeros_like(cnt_ref)

    o = of_ref[...] + ob_ref[...]
    ms = _exact_mm(o * o, bd_ref[...]) * (1.0 / DN_HEAD_DIM)
    dn = o * lax.rsqrt(ms + NORM_EPS) * dnn_ref[...] * _silu(dz_ref[...].astype(F32))
    g1 = gates_ref[:, 0:1024].astype(F32)
    g2 = gates_ref[:, 1024:2048].astype(F32)
    g3 = gates_ref[:, 2048:3072].astype(F32)
    m = (g1 * _mm(omla_ref[...], wo1_ref[...]) + g2 * _mm(dn, wo2_ref[...])
         + g3 * _mm(ogqa_ref[...], wo3_ref[...]))
    y = _mm(m, wout_ref[...])
    x1 = _layernorm(DEEPNORM_ALPHA * x_ref[...] + ga_ref[0] * y, lng_ref[...], lnb_ref[...])
    x1_out[...] = x1
    uf = x1 * (1.0 + scf_ref[0]) + shf_ref[0]
    _split_pieces(_pack_pairs(uf), ufp_out)

    tm = uf.shape[0]
    scores = jax.nn.sigmoid(_mm_nt(wr_ref[...], uf))[0:N_EXPERTS]
    sel = scores + rb_ref[0:N_EXPERTS, :]
    gsz = N_EXPERTS // N_GROUPS
    neg = jnp.full((), -jnp.inf, F32)
    sel3 = sel.reshape(N_GROUPS, gsz, tm)
    mem = lax.broadcasted_iota(jnp.int32, (N_GROUPS, gsz, tm), 1)
    m1 = jnp.max(sel3, 1, keepdims=True)
    i1 = jnp.min(jnp.where(sel3 == m1, mem, gsz), 1, keepdims=True)
    m2 = jnp.max(jnp.where(mem == i1, neg, sel3), 1, keepdims=True)
    gs = (m1 + m2).reshape(N_GROUPS, tm)
    gi = lax.broadcasted_iota(jnp.int32, (N_GROUPS, tm), 0)
    grank = jnp.zeros((N_GROUPS, tm), jnp.int32)
    for gp in range(N_GROUPS):
        other = gs[gp:gp + 1, :]
        beats = (other > gs) | ((other == gs) & (gp < gi))
        grank = grank + beats.astype(jnp.int32)
    gsel = (grank < TOPK_GROUPS).reshape(N_GROUPS, 1, tm)
    cur = jnp.where(gsel, sel3, neg).reshape(N_EXPERTS, tm)
    ei = lax.broadcasted_iota(jnp.int32, (N_EXPERTS, tm), 0)
    zero = jnp.zeros((N_EXPERTS, tm), F32)
    one = jnp.ones((N_EXPERTS, tm), F32)
    chosen = zero
    picks = []
    for _ in range(TOP_K):
        mx = jnp.max(cur, 0, keepdims=True)
        ix = jnp.min(jnp.where(cur == mx, ei, N_EXPERTS), 0, keepdims=True)
        pick = ei == ix
        picks.append((ix, pick))
        chosen = chosen + jnp.where(pick, one, zero)
        cur = jnp.where(pick, neg, cur)

    r_i = lax.broadcasted_iota(jnp.int32, (tm, tm), 0)
    c_i = lax.broadcasted_iota(jnp.int32, (tm, tm), 1)
    before = jnp.where(r_i < c_i, 1.0, 0.0)
    pos = cnt_ref[:, 0:1] + _mm(chosen, before)
    cnt_new = cnt_ref[...] + jnp.sum(chosen, 1, keepdims=True)
    cnt_ref[...] = cnt_new
    cnt_out[...] = cnt_new

    w_rows = [jnp.sum(jnp.where(pick, scores, zero), 0, keepdims=True) for _, pick in picks]
    wsum = w_rows[0]
    for w_k in w_rows[1:]:
        wsum = wsum + w_k
    eidx_out[...] = jnp.concatenate([ix for ix, _ in picks], axis=0)
    rank_out[...] = jnp.concatenate(
        [jnp.sum(jnp.where(pick, pos, zero), 0, keepdims=True) for _, pick in picks], axis=0).astype(jnp.int32)
    w8 = jnp.concatenate([w_k / wsum * ROUTED_SCALE for w_k in w_rows], axis=0)
    ew_out[...] = jnp.concatenate([w8, jnp.zeros((LANES - TOP_K, tm), F32)], axis=0).T


def _merge_call(xs, mod_l, omla, of, ob, dz, ogqa, gates, wo1, wo2, wo3, wout, dnn, bd, lng, lnb, wr, rb, B, S, C):
    T, D = xs.shape
    tpb, ctiles = S // TM, C // TM
    row = lambda i: (i, 0)
    const = lambda i: (0, 0)
    modspec = lambda k: pl.BlockSpec((1, 1, D), _mod_index(tpb, ctiles, k))
    return pl.pallas_call(
        _merge_kernel,
        grid=(T // TM,),
        in_specs=[pl.BlockSpec((TM, D), row), modspec(2), modspec(3), modspec(4),
                  pl.BlockSpec((TM, 512), row), pl.BlockSpec((TM, 512), row), pl.BlockSpec((TM, 512), row),
                  pl.BlockSpec((TM, 512), row), pl.BlockSpec((TM, 512), row), pl.BlockSpec((TM, 3072), row),
                  pl.BlockSpec((512, D), const), pl.BlockSpec((512, D), const), pl.BlockSpec((512, D), const),
                  pl.BlockSpec((D, D), const), pl.BlockSpec((1, 512), const), pl.BlockSpec((512, 512), const),
                  pl.BlockSpec((1, D), const), pl.BlockSpec((1, D), const),
                  pl.BlockSpec((LANES, D), const), pl.BlockSpec((LANES, 1), const)],
        out_specs=[pl.BlockSpec((TM, D), row), pl.BlockSpec((N_PIECES, TM, PIECE), lambda i: (0, i, 0)),
                   pl.BlockSpec((TOP_K, TM), lambda i: (0, i)), pl.BlockSpec((TOP_K, TM), lambda i: (0, i)),
                   pl.BlockSpec((TM, LANES), row), pl.BlockSpec((N_EXPERTS, LANES), const)],
        out_shape=[jax.ShapeDtypeStruct((T, D), F32), jax.ShapeDtypeStruct((N_PIECES, T, PIECE), jnp.int32),
                   jax.ShapeDtypeStruct((TOP_K, T), jnp.int32), jax.ShapeDtypeStruct((TOP_K, T), jnp.int32),
                   jax.ShapeDtypeStruct((T, LANES), F32), jax.ShapeDtypeStruct((N_EXPERTS, LANES), F32)],
        scratch_shapes=[pltpu.VMEM((N_EXPERTS, LANES), F32)],
        compiler_params=_params("arbitrary"),
        name="merge_norm_route",
    )(xs, mod_l, mod_l, mod_l, omla, of.reshape(T, 512), ob.reshape(T, 512), dz, ogqa, gates,
      wo1, wo2, wo3, wout, dnn, bd, lng, lnb, wr, rb)


EXPERT_BLOCK = 512
SC_WINDOW = 128
N_PIECES = 2
PIECE = D_MODEL // 2 // N_PIECES


def _split_pieces(packed, out_ref):
    for h in range(N_PIECES):
        out_ref[h] = packed[:, h * PIECE:(h + 1) * PIECE]


def _mm_pieces(pieces, w):
    acc = None
    for h, (lo, hi) in enumerate(pieces):
        t = (_mm(lo, w[h * PIECE:(h + 1) * PIECE, :])
             + _mm(hi, w[D_MODEL // 2 + h * PIECE:D_MODEL // 2 + (h + 1) * PIECE, :]))
        acc = t if acc is None else acc + t
    return acc


def _sc_mesh():
    return plsc.VectorSubcoreMesh(core_axis_name="c", subcore_axis_name="s")


def _sc_gather_rows(y, idx):
    n = idx.shape[1]
    W = y.shape[1]

    @pl.kernel(out_type=jax.ShapeDtypeStruct((n, W), y.dtype), mesh=_sc_mesh(), scratch_types=[])
    def gather(y_hbm, i_hbm, o_hbm):
        def body(i_vmem, o_vmem):
            pltpu.sync_copy(y_hbm.at[i_vmem.at[0]], o_vmem)

        pltpu.emit_pipeline(
            body,
            grid=(n // SC_WINDOW,),
            in_specs=[pl.BlockSpec((1, SC_WINDOW), lambda i: (0, i))],
            out_specs=[pl.BlockSpec((SC_WINDOW, W), lambda i: (i, 0))],
            core_axis_name=("c", "s"),
            dimension_semantics=(pltpu.PARALLEL,),
        )(i_hbm, o_hbm)

    return gather(y, idx)


SC_LANES = 16
SC_WORKERS = 32
SC_CHUNK = 2176


def _sc_invert_rows(dest, default, n_tok):
    n_rows = default.shape[0]
    per_w = n_rows // SC_WORKERS
    n_k = dest.shape[0] // n_tok
    assert n_rows % (SC_WORKERS * SC_LANES) == 0 and n_tok % SC_CHUNK == 0

    @pl.kernel(out_type=jax.ShapeDtypeStruct((n_rows,), jnp.int32), mesh=_sc_mesh(),
               scratch_types=[pltpu.VMEM((per_w,), jnp.int32), pltpu.VMEM((SC_CHUNK,), jnp.int32)],
               compiler_params=pltpu.CompilerParams(needs_layout_passes=False))
    def invert(dest_hbm, dflt_hbm, out_hbm, rows_v, dest_v):
        wid = lax.axis_index("s") * 2 + lax.axis_index("c")
        base = wid * per_w
        pltpu.sync_copy(dflt_hbm.at[pl.ds(base, per_w)], rows_v)
        lanes = lax.iota(jnp.int32, SC_LANES)
        for k in range(n_k):
            @pl.loop(0, n_tok // SC_CHUNK)
            def _(c):
                pltpu.sync_copy(dest_hbm.at[pl.ds(k * n_tok + c * SC_CHUNK, SC_CHUNK)], dest_v)

                @pl.loop(0, SC_CHUNK, step=SC_LANES)
                def _(o):
                    local = dest_v[pl.ds(o, SC_LANES)] - base
                    mine = (local >= 0) & (local < per_w)
                    plsc.store_scatter(rows_v, [jnp.where(mine, local, 0)], c * SC_CHUNK + o + lanes, mask=mine)

        pltpu.sync_copy(rows_v, out_hbm.at[pl.ds(base, per_w)])

    return invert(dest, default)


def _experts_kernel(be_ref, nv_ref, xb_ref, wg_ref, wu_ref, wd_ref, y_out):
    b = pl.program_id(0)
    nv = nv_ref[b]

    @pl.when(nv > 0)
    def _():
        rows = lax.broadcasted_iota(jnp.int32, xb_ref.shape[1:], 0)
        pieces = []
        for h in range(N_PIECES):
            xh = xb_ref[h]
            pieces.append(_unpack_pairs(jnp.where(rows < nv, xh, jnp.zeros_like(xh))))
        hid = _silu(_mm_pieces(pieces, wg_ref.at[0, 0])) * _mm_pieces(pieces, wu_ref.at[0, 0])
        _split_pieces(_pack_pairs(_mm(hid, wd_ref[0, 0])), y_out)


def _experts_call(xb, block_e, nvalid, wg, wu, wd, layer):
    _, R, _ = xb.shape
    D = D_MODEL
    blk = (N_PIECES, EXPERT_BLOCK, PIECE)
    grid_spec = pltpu.PrefetchScalarGridSpec(
        num_scalar_prefetch=2,
        grid=(R // EXPERT_BLOCK,),
        in_specs=[pl.BlockSpec(blk, lambda b, be, nv: (0, b, 0)),
                  pl.BlockSpec((1, 1, D, EXPERT_DIM), lambda b, be, nv: (layer, be[b], 0, 0)),
                  pl.BlockSpec((1, 1, D, EXPERT_DIM), lambda b, be, nv: (layer, be[b], 0, 0)),
                  pl.BlockSpec((1, 1, EXPERT_DIM, D), lambda b, be, nv: (layer, be[b], 0, 0))],
        out_specs=pl.BlockSpec(blk, lambda b, be, nv: (0, b, 0)),
    )
    return pl.pallas_call(
        _experts_kernel,
        grid_spec=grid_spec,
        out_shape=jax.ShapeDtypeStruct((N_PIECES, R, PIECE), jnp.int32),
        compiler_params=_params("arbitrary"),
        name="moe_experts",
    )(block_e, nvalid, xb, wg, wu, wd)


def _combine_kernel(x_ref, ufp_ref, yg_ref, ew_ref, gf_ref, sg_ref, su_ref, sd_ref, g_ref, b_ref, o_ref):
    pieces = [_unpack_pairs(ufp_ref[h]) for h in range(N_PIECES)]
    hs = _silu(_mm_pieces(pieces, sg_ref)) * _mm_pieces(pieces, su_ref)
    f = _mm(hs, sd_ref[...])
    ew = ew_ref[...]
    lane = lax.broadcasted_iota(jnp.int32, ew.shape, 1)
    acc = [[jnp.zeros((x_ref.shape[0], PIECE), F32) for _ in range(N_PIECES)] for _ in range(2)]
    for k in range(TOP_K):
        wk = jnp.sum(jnp.where(lane == k, ew, jnp.zeros_like(ew)), axis=1, keepdims=True)
        for h in range(N_PIECES):
            ylo, yhi = _unpack_pairs(yg_ref[h, k])
            acc[0][h] = acc[0][h] + wk * ylo
            acc[1][h] = acc[1][h] + wk * yhi
    f = f + jnp.concatenate(acc[0] + acc[1], axis=1)
    o_ref[...] = _layernorm(DEEPNORM_ALPHA * x_ref[...] + gf_ref[0] * f, g_ref[...], b_ref[...])


def _combine_call(x1, ufp, yg, ew, mod_l, sg, su, sd, g, b, B, S, C):
    T, D = x1.shape
    tpb, ctiles = S // TM, C // TM
    row = lambda i: (i, 0)
    const = lambda i: (0, 0)
    return pl.pallas_call(
        _combine_kernel,
        grid=(T // TM,),
        in_specs=[pl.BlockSpec((TM, D), row), pl.BlockSpec((N_PIECES, TM, PIECE), lambda i: (0, i, 0)),
                  pl.BlockSpec((N_PIECES, TOP_K, TM, PIECE), lambda i: (0, 0, i, 0)),
                  pl.BlockSpec((TM, LANES), row),
                  pl.BlockSpec((1, 1, D), _mod_index(tpb, ctiles, 5)),
                  pl.BlockSpec((D, SHARED_DIM), const), pl.BlockSpec((D, SHARED_DIM), const),
                  pl.BlockSpec((SHARED_DIM, D), const),
                  pl.BlockSpec((1, D), const), pl.BlockSpec((1, D), const)],
        out_specs=pl.BlockSpec((TM, D), row),
        out_shape=jax.ShapeDtypeStruct((T, D), F32),
        compiler_params=_params("parallel"),
        name="moe_combine_norm",
    )(x1, ufp, yg, ew, mod_l, sg, su, sd, g, b)


def _moe_routed(ufp, eidx_t, rank_t, counts, wg, wu, wd, layer):
    T = ufp.shape[1]
    n_blocks = -(-(T * TOP_K + N_EXPERTS * (EXPERT_BLOCK - 1)) // EXPERT_BLOCK)
    n_rows = n_blocks * EXPERT_BLOCK
    padded = (counts + EXPERT_BLOCK - 1) // EXPERT_BLOCK * EXPERT_BLOCK
    pad_end = jnp.cumsum(padded)
    start_pad = pad_end - padded
    experts = jnp.arange(N_EXPERTS, dtype=jnp.int32)

    def lookup(table, idx):
        sel = idx[None] == experts.reshape((N_EXPERTS,) + (1,) * idx.ndim)
        return jnp.sum(jnp.where(sel, table.reshape((N_EXPERTS,) + (1,) * idx.ndim), 0), axis=0)

    dest_t = lookup(start_pad, eidx_t) + rank_t
    blk = jnp.arange(n_blocks, dtype=jnp.int32) * EXPERT_BLOCK
    block_e = jnp.minimum(jnp.sum((blk[:, None] >= pad_end[None, :]).astype(jnp.int32), axis=1), N_EXPERTS - 1)
    nvalid = jnp.clip(lookup(counts, block_e) - (blk - lookup(start_pad, block_e)), 0, EXPERT_BLOCK)
    row_tok = _sc_invert_rows(dest_t.reshape(-1), jnp.arange(n_rows, dtype=jnp.int32) % T, T)
    piece = jnp.arange(N_PIECES, dtype=jnp.int32)
    src = (piece[:, None] * T + row_tok[None, :]).reshape(1, N_PIECES * n_rows)
    xb = _sc_gather_rows(ufp.reshape(N_PIECES * T, PIECE), src).reshape(N_PIECES, n_rows, PIECE)
    yb = _experts_call(xb, block_e.astype(jnp.int32), nvalid.astype(jnp.int32), wg, wu, wd, layer)
    back = (piece[:, None, None] * n_rows + dest_t[None]).reshape(1, N_PIECES * TOP_K * T)
    yg = _sc_gather_rows(yb.reshape(N_PIECES * n_rows, PIECE), back)
    return yg.reshape(N_PIECES, TOP_K, T, PIECE)


def _rot_cols(w, half):
    return jnp.concatenate([-w[:, half:], w[:, :half]], axis=1)


def _prep_w_in(w):
    d = w.shape[0]
    offs = np.cumsum((0,) + IN_SIZES)
    cq, ckv, kr, dqkv, da, db, dz, gq, gk, gv, gates = (w[:, offs[t]:offs[t + 1]] for t in range(len(IN_SIZES)))
    z = lambda n: jnp.zeros((d, n), w.dtype)
    krg = jnp.concatenate([z(64), kr, z(32)], 1)
    krr = jnp.concatenate([z(64), _rot_cols(kr, MLA_ROPE // 2), z(32)], 1)
    ab = jnp.concatenate([da, db, z(LANES - 4 * DN_HEADS)], 1)
    hd = GQA_HEAD_DIM
    gq_rot = jnp.concatenate([_rot_cols(gq[:, h * hd:(h + 1) * hd], hd // 2) for h in range(GQA_HEADS)], 1)
    dup = lambda t: jnp.concatenate([t[:, 0:hd], t[:, 0:hd], t[:, hd:2 * hd], t[:, hd:2 * hd]], 1)
    gk_rot = jnp.concatenate([_rot_cols(gk[:, h * hd:(h + 1) * hd], hd // 2) for h in range(GQA_KV_HEADS)], 1)
    cat = jnp.concatenate([cq, ckv, krg, krr, dqkv, ab, dz, gq, gq_rot, dup(gk), dup(gk_rot), dup(gv), gates], 1)
    assert cat.shape[1] == NZ
    return cat.astype(MXU_DTYPE)


def _prep_w_uq(w):
    d = w.shape[0]
    hw = MLA_NOPE + MLA_ROPE
    a, b = [], []
    for h in range(MLA_HEADS):
        wh = w[:, h * hw:(h + 1) * hw]
        a += [wh, jnp.zeros((d, LANES - hw), w.dtype)]
        b += [jnp.zeros((d, MLA_NOPE), w.dtype), _rot_cols(wh[:, MLA_NOPE:], MLA_ROPE // 2),
              jnp.zeros((d, LANES - hw), w.dtype)]
    return jnp.concatenate(a + b, 1).astype(MXU_DTYPE)


def _prep_w_ukv(w):
    d = w.shape[0]
    hw = MLA_NOPE + MLA_V
    kpart, vpart = [], []
    for h in range(MLA_HEADS):
        wh = w[:, h * hw:(h + 1) * hw]
        kpart += [wh[:, :MLA_NOPE], jnp.zeros((d, LANES - MLA_NOPE), w.dtype)]
        vpart += [wh[:, MLA_NOPE:]]
    return jnp.concatenate(kpart + vpart, 1).astype(MXU_DTYPE)


def _rope_tables(n_rows, C):
    row = jnp.repeat(jnp.arange(n_rows, dtype=F32), GRID_W)
    col = jnp.tile(jnp.arange(GRID_W, dtype=F32), n_rows)

    def angles(dim):
        n = dim // 4
        inv = ROPE_BASE ** (-jnp.arange(n, dtype=F32) / n)
        return jnp.concatenate([row[:, None] * inv, col[:, None] * inv], axis=-1)

    def with_ctx(cos, sin):
        return (jnp.concatenate([jnp.ones((C, LANES), F32), cos], 0),
                jnp.concatenate([jnp.zeros((C, LANES), F32), sin], 0))

    L = n_rows * GRID_W
    am = angles(MLA_ROPE)
    one, zero = jnp.ones((L, MLA_NOPE), F32), jnp.zeros((L, MLA_NOPE), F32)
    cm = jnp.concatenate([one, jnp.cos(am), jnp.cos(am), one[:, :32]], 1)
    sm = jnp.concatenate([zero, jnp.sin(am), jnp.sin(am), zero[:, :32]], 1)
    ag = angles(GQA_HEAD_DIM)
    cg = jnp.tile(jnp.cos(ag), (1, 4))
    sg = jnp.tile(jnp.sin(ag), (1, 4))
    return with_ctx(cm, sm) + with_ctx(cg, sg)


def kernel(x, c, ctx, c_ctx, w_ada, b_ada, w_in, mla_q_norm, mla_kv_norm, w_uq, w_ukv, dn_conv, dn_a_log, dn_dt_bias, dn_norm, gqa_sink, w_o_mla, w_o_dn, w_o_gqa, w_out, ln1_g, ln1_b, w_router, router_bias, w_exp_gate, w_exp_up, w_exp_down, w_sh_gate, w_sh_up, w_sh_down, ln2_g, ln2_b):
    B, L, D = x.shape
    C = ctx.shape[1]
    S = C + L
    nl = w_in.shape[0]
    assert D == D_MODEL and nl == DEPTH and B <= CTX_MOD_ROW
    assert C % TM == 0 and L % TM == 0 and L % GRID_W == 0 and L >= 3 * WINDOW
    cast = lambda t: t.astype(MXU_DTYPE)

    cc = jnp.zeros((MOD_ROWS, D), F32).at[0:B].set(c).at[CTX_MOD_ROW].set(c_ctx)
    mods = _ada_call(cc, w_ada, b_ada)
    tabs = _rope_tables(L // GRID_W, C)
    dn_consts = _dn_constants()
    xs = jnp.concatenate([ctx, x], axis=1).reshape(B * S, D)

    for l in range(nl):
        mod_l = mods[l].reshape(MOD_ROWS * 6, 1, D)
        q, k, v, dqkv, ab, dz, gq, gkv, gates = _inproj_call(
            xs, mod_l, _prep_w_in(w_in[l]), _prep_w_uq(w_uq[l]), _prep_w_ukv(w_ukv[l]),
            mla_q_norm[l].reshape(1, -1), mla_kv_norm[l].reshape(1, -1), tabs, B, S, C)
        omla = _mla_call(q, k, v, B, S, C)
        conv8 = jnp.zeros((8, 3 * DN_WIDTH), F32).at[0:DN_CONV].set(dn_conv[l])
        gp = (jnp.zeros((8, LANES), F32).at[0, 0:2 * DN_HEADS].set(dn_a_log[l].reshape(-1))
              .at[1, 0:2 * DN_HEADS].set(dn_dt_bias[l].reshape(-1)))
        local = _dnlocal_call(dqkv, ab, conv8, gp, dn_consts, B, S, C)
        of, ob = _dnscan_call(local, B, S, C)
        ogqa = _gqa_call(gq, gkv, gqa_sink[l], B, S, C)
        wr = jnp.zeros((LANES, D), F32).at[0:N_EXPERTS].set(w_router[l].T)
        rb = jnp.zeros((LANES, 1), F32).at[0:N_EXPERTS, 0].set(router_bias[l])
        x1, ufp, eidx_t, rank_t, ew, cnt = _merge_call(
            xs, mod_l, omla, of, ob, dz, ogqa, gates,
            cast(w_o_mla[l]), cast(w_o_dn[l]), cast(w_o_gqa[l]), cast(w_out[l]),
            jnp.tile(dn_norm[l], DN_HEADS).reshape(1, DN_WIDTH), dn_consts[0],
            ln1_g[l].reshape(1, D), ln1_b[l].reshape(1, D), cast(wr), rb, B, S, C)
        yg = _moe_routed(ufp, eidx_t, rank_t, cnt[:, 0].astype(jnp.int32),
                         w_exp_gate, w_exp_up, w_exp_down, l)
        xs = _combine_call(x1, ufp, yg, ew, mod_l, cast(w_sh_gate[l]), cast(w_sh_up[l]), cast(w_sh_down[l]),
                           ln2_g[l].reshape(1, D), ln2_b[l].reshape(1, D), B, S, C)
    return xs.reshape(B, S, D)[:, C:, :]
```

```python
import functools

import numpy as np
import jax
import jax.numpy as jnp
from jax import lax
from jax.experimental import pallas as pl
from jax.experimental.pallas import tpu as pltpu
from jax.experimental.pallas import tpu_sc as plsc

F32 = jnp.float32
MXU_DTYPE = jnp.bfloat16

D_MODEL = 1024
DEPTH = 4
GRID_W = 64
NORM_EPS = 1e-6
ROPE_BASE = 10000.0
DEEPNORM_ALPHA = (2.0 * DEPTH) ** 0.25

MLA_HEADS = 8
MLA_Q_LORA = 256
MLA_KV_LORA = 128
MLA_NOPE = 64
MLA_ROPE = 32
MLA_V = 64
MLA_SCALE = (MLA_NOPE + MLA_ROPE) ** -0.5
LOG2E = float(np.log2(np.e))

DN_HEADS = 8
DN_HEAD_DIM = 64
DN_WIDTH = DN_HEADS * DN_HEAD_DIM
DN_CONV = 5
DN_CHUNK = 64

GQA_HEADS = 8
GQA_KV_HEADS = 2
GQA_HEAD_DIM = 64
GQA_SCALE = GQA_HEAD_DIM ** -0.5
WINDOW = 128

N_EXPERTS = 64
TOP_K = 8
N_GROUPS = 8
TOPK_GROUPS = 4
EXPERT_DIM = 256
SHARED_DIM = 256
ROUTED_SCALE = 2.5

IN_SIZES = (MLA_Q_LORA, MLA_KV_LORA, MLA_ROPE,
            3 * DN_WIDTH, 2 * DN_HEADS, 2 * DN_HEADS, DN_WIDTH,
            GQA_HEADS * GQA_HEAD_DIM, GQA_KV_HEADS * GQA_HEAD_DIM, GQA_KV_HEADS * GQA_HEAD_DIM,
            3 * D_MODEL)

LANES = 128
TM = 256
MOD_ROWS = 16
CTX_MOD_ROW = 8

OFF_A = 0
OFF_DQKV = 640
OFF_AB = OFF_DQKV + 3 * DN_WIDTH
OFF_DZ = OFF_AB + LANES
OFF_GQ = OFF_DZ + DN_WIDTH
OFF_GK = OFF_GQ + 1024
OFF_GATES = OFF_GK + 768
NZ = OFF_GATES + 3 * D_MODEL

VMEM_LIMIT = 56 * 1024 * 1024


def _mm(a, b):
    return jnp.dot(a.astype(MXU_DTYPE), b.astype(MXU_DTYPE), preferred_element_type=F32)


def _mm_nt(a, b):
    return lax.dot_general(a.astype(MXU_DTYPE), b.astype(MXU_DTYPE), (((1,), (1,)), ((), ())),
                           preferred_element_type=F32)


def _mm_tn(a, b):
    return lax.dot_general(a.astype(MXU_DTYPE), b.astype(MXU_DTYPE), (((0,), (0,)), ((), ())),
                           preferred_element_type=F32)


def _bmm(a, b):
    return jnp.einsum('cik,ckj->cij', a.astype(MXU_DTYPE), b.astype(MXU_DTYPE), preferred_element_type=F32)


def _bmm_nt(a, b):
    return jnp.einsum('cik,cjk->cij', a.astype(MXU_DTYPE), b.astype(MXU_DTYPE), preferred_element_type=F32)


def _split3(x):
    hi = x.astype(jnp.bfloat16).astype(F32)
    r = x - hi
    mid = r.astype(jnp.bfloat16).astype(F32)
    lo = (r - mid).astype(jnp.bfloat16).astype(F32)
    return hi, mid, lo


def _exact_mm(x, m01):
    hi, mid, lo = _split3(x)
    return _mm(hi, m01) + _mm(mid, m01) + _mm(lo, m01)


def _exact_mm_left(m01, x):
    hi, mid, lo = _split3(x)
    return _mm(m01, hi) + _mm(m01, mid) + _mm(m01, lo)


def _silu(x):
    return x * jax.nn.sigmoid(x)


def _layernorm(v, g, b):
    mu = jnp.mean(v, -1, keepdims=True)
    d = v - mu
    var = jnp.mean(d * d, -1, keepdims=True)
    return d * lax.rsqrt(var + NORM_EPS) * g + b


def _mod_index(tiles_per_b, ctx_tiles, k):
    def index(i):
        row = jnp.where((i % tiles_per_b) < ctx_tiles, CTX_MOD_ROW, i // tiles_per_b)
        return (row * 6 + k, 0, 0)
    return index


def _params(*sem):
    return pltpu.CompilerParams(dimension_semantics=sem, vmem_limit_bytes=VMEM_LIMIT)


def _ada_kernel(c_ref, w_ref, b_ref, o_ref):
    o_ref[0] = _mm(_silu(c_ref[...]), w_ref[0]) + b_ref[0]


def _ada_call(cc, w_ada, b_ada):
    nl, d, n6 = w_ada.shape
    tn = 1536
    return pl.pallas_call(
        _ada_kernel,
        grid=(nl, n6 // tn),
        in_specs=[pl.BlockSpec((MOD_ROWS, d), lambda l, j: (0, 0)),
                  pl.BlockSpec((1, d, tn), lambda l, j: (l, 0, j)),
                  pl.BlockSpec((1, 1, tn), lambda l, j: (l, 0, j))],
        out_specs=pl.BlockSpec((1, MOD_ROWS, tn), lambda l, j: (l, 0, j)),
        out_shape=jax.ShapeDtypeStruct((nl, MOD_ROWS, n6), F32),
        compiler_params=_params("parallel", "parallel"),
        name="ada_mod",
    )(cc, w_ada, b_ada.reshape(nl, 1, n6))


def _inproj_kernel(x_ref, sh_ref, sc_ref, w_ref, wuq_ref, wukv_ref, qn_ref, kvn_ref,
                   cm_ref, sm_ref, cg_ref, sg_ref,
                   q_out, k_out, v_out, dqkv_out, ab_out, dz_out, gq_out, gkv_out, gates_out):
    u = (x_ref[...] * (1.0 + sc_ref[0]) + sh_ref[0]).astype(MXU_DTYPE)

    def z(off, width):
        return jnp.dot(u, w_ref[:, off:off + width], preferred_element_type=F32)

    def rms(v, g):
        return v * lax.rsqrt(jnp.mean(v * v, -1, keepdims=True) + NORM_EPS) * g

    cm, sm, cg, sg = cm_ref[...], sm_ref[...], cg_ref[...], sg_ref[...]

    za = z(OFF_A, 640)
    qq = _mm(rms(za[:, 0:256], qn_ref[...]), wuq_ref[...])
    kvv = _mm(rms(za[:, 256:384], kvn_ref[...]), wukv_ref[...])
    k_rope = za[:, 384:512] * cm + za[:, 512:640] * sm
    for h in range(MLA_HEADS):
        sl = slice(h * LANES, (h + 1) * LANES)
        qa = qq[:, h * LANES:(h + 1) * LANES]
        qb = qq[:, 1024 + h * LANES:1024 + (h + 1) * LANES]
        q_out[:, sl] = ((qa * cm + qb * sm) * (MLA_SCALE * LOG2E)).astype(q_out.dtype)
        k_out[:, sl] = (kvv[:, sl] + k_rope).astype(k_out.dtype)
    v_out[...] = kvv[:, 1024:1536].astype(v_out.dtype)

    for t in range(3):
        dqkv_out[:, t * 512:(t + 1) * 512] = z(OFF_DQKV + t * 512, 512).astype(dqkv_out.dtype)
    ab_out[...] = z(OFF_AB, LANES)
    dz_out[...] = z(OFF_DZ, DN_WIDTH).astype(dz_out.dtype)

    zq = z(OFF_GQ, 1024)
    for p in range(4):
        sl = slice(p * LANES, (p + 1) * LANES)
        gq_out[:, sl] = ((zq[:, sl] * cg + zq[:, 512 + p * LANES:512 + (p + 1) * LANES] * sg)
                         * (GQA_SCALE * LOG2E)).astype(gq_out.dtype)
    zk = z(OFF_GK, 768)
    for j in range(2):
        sl = slice(j * LANES, (j + 1) * LANES)
        gkv_out[:, sl] = (zk[:, sl] * cg + zk[:, 256 + j * LANES:256 + (j + 1) * LANES] * sg).astype(gkv_out.dtype)
    gkv_out[:, 256:512] = zk[:, 512:768].astype(gkv_out.dtype)

    for t in range(3):
        gates_out[:, t * 1024:(t + 1) * 1024] = jax.nn.sigmoid(z(OFF_GATES + t * 1024, 1024)).astype(gates_out.dtype)


def _inproj_call(xs, mod_l, w_cat, wuq_cat, wukv_cat, qn, kvn, tabs, B, S, C):
    T, D = xs.shape
    tpb, ctiles = S // TM, C // TM
    act = MXU_DTYPE
    row = lambda i: (i, 0)
    const = lambda i: (0, 0)
    tab = lambda i: (i % tpb, 0)
    widths = (1024, 1024, 512, 1536, LANES, 512, 512, 512, 3072)
    dtypes = (act, act, act, act, F32, act, act, act, act)
    return pl.pallas_call(
        _inproj_kernel,
        grid=(T // TM,),
        in_specs=[pl.BlockSpec((TM, D), row),
                  pl.BlockSpec((1, 1, D), _mod_index(tpb, ctiles, 0)),
                  pl.BlockSpec((1, 1, D), _mod_index(tpb, ctiles, 1)),
                  pl.BlockSpec((D, NZ), const),
                  pl.BlockSpec((MLA_Q_LORA, 2048), const),
                  pl.BlockSpec((MLA_KV_LORA, 1536), const),
                  pl.BlockSpec((1, MLA_Q_LORA), const),
                  pl.BlockSpec((1, MLA_KV_LORA), const)]
                 + [pl.BlockSpec((TM, LANES), tab)] * 4,
        out_specs=[pl.BlockSpec((TM, w), row) for w in widths],
        out_shape=[jax.ShapeDtypeStruct((T, w), dt) for w, dt in zip(widths, dtypes)],
        compiler_params=_params("parallel"),
        name="in_proj",
    )(xs, mod_l, mod_l, w_cat, wuq_cat, wukv_cat, qn, kvn, *tabs)


def _mla_kernel(q_ref, k_ref, v_ref, o_ref, *, n_ctx, n_all):
    i = pl.program_id(2)
    tq = q_ref.shape[1]
    left = lax.broadcasted_iota(jnp.int32, (tq, LANES), 1) < MLA_V

    def attend(nk, ctx_rows):
        v = v_ref[0, 0:nk, :]
        v_left = lax.broadcasted_iota(jnp.int32, v.shape, 1) < MLA_V
        ones = jnp.ones_like(v)
        s = [_mm_nt(q_ref[0, :, hh * LANES:(hh + 1) * LANES], k_ref[0, 0:nk, hh * LANES:(hh + 1) * LANES])
             for hh in range(2)]
        if ctx_rows:
            row = lax.broadcasted_iota(jnp.int32, (tq, nk), 0)
            col = lax.broadcasted_iota(jnp.int32, (tq, nk), 1)
            ok = (row >= ctx_rows) | (col < ctx_rows)
            s = [jnp.where(ok, x, -jnp.inf) for x in s]
        p = [jnp.exp2((x - jnp.max(x, -1, keepdims=True)).astype(MXU_DTYPE)) for x in s]
        o = [_mm(p[0], jnp.where(v_left, v, ones)), _mm(p[1], jnp.where(v_left, ones, v))]
        outs = [x / pltpu.roll(x, MLA_V, axis=1) for x in o]
        o_ref[0] = jnp.where(left, outs[0], outs[1]).astype(o_ref.dtype)

    if tq > n_ctx:
        @pl.when(i == 0)
        def _():
            attend(n_all, n_ctx)

        @pl.when(i > 0)
        def _():
            attend(n_all, 0)
    else:
        ctx_tiles = n_ctx // tq

        @pl.when(i < ctx_tiles)
        def _():
            attend(n_ctx, 0)

        @pl.when(i >= ctx_tiles)
        def _():
            attend(n_all, 0)


MLA_TQ = (544, 384, 256)


def _mla_call(q, k, v, B, S, C):
    tq = next(t for t in MLA_TQ if S % t == 0)
    assert C % tq == 0 or tq > C
    q3, k3, v3 = q.reshape(B, S, 1024), k.reshape(B, S, 1024), v.reshape(B, S, 512)
    out = pl.pallas_call(
        functools.partial(_mla_kernel, n_ctx=C, n_all=S),
        grid=(B, MLA_HEADS // 2, S // tq),
        in_specs=[pl.BlockSpec((1, tq, 2 * LANES), lambda b, j, i: (b, i, j)),
                  pl.BlockSpec((1, S, 2 * LANES), lambda b, j, i: (b, 0, j)),
                  pl.BlockSpec((1, S, LANES), lambda b, j, i: (b, 0, j))],
        out_specs=pl.BlockSpec((1, tq, LANES), lambda b, j, i: (b, i, j)),
        out_shape=jax.ShapeDtypeStruct((B, S, 512), MXU_DTYPE),
        compiler_params=_params("parallel", "parallel", "arbitrary"),
        name="mla_attn",
    )(q3, k3, v3)
    return out.reshape(B * S, 512)


def _gqa_kernel(sink_ref, q_ref, kv_ref, o_ref, *, n_ctx, n_all):
    i = pl.program_id(1)
    qb_rows = q_ref.shape[1]
    span = qb_rows + 2 * WINDOW
    ctx_blocks = n_ctx // qb_rows
    group = GQA_HEADS // GQA_KV_HEADS
    left = lax.broadcasted_iota(jnp.int32, (qb_rows, LANES), 1) < GQA_HEAD_DIM

    def run(latent):
        nk = n_ctx + span if latent else n_ctx
        rows = group * qb_rows
        if latent:
            qb = i - ctx_blocks
            ws = pl.multiple_of(jnp.minimum(n_ctx + qb * qb_rows - WINDOW, n_all - span), WINDOW)
            col = lax.broadcasted_iota(jnp.int32, (rows, nk), 1)
            q_pos = qb * qb_rows + lax.broadcasted_iota(jnp.int32, (rows, nk), 0) % qb_rows
            k_pos = ws - 2 * n_ctx + col
            ok = (col < n_ctx) | ((jnp.abs(k_pos - q_pos) <= WINDOW) & (k_pos >= 0))
        head_of_row = lax.broadcasted_iota(jnp.int32, (rows, 1), 0) // qb_rows
        res = []
        for j in range(GQA_KV_HEADS):
            parts = []
            for g in range(group):
                qp = q_ref[0, :, (2 * j + g // 2) * LANES:(2 * j + g // 2 + 1) * LANES]
                parts.append(jnp.where(left if g % 2 == 0 else ~left, qp, jnp.zeros_like(qp)))
            q4 = jnp.concatenate(parts, axis=0)
            kc = kv_ref[0, 0:n_ctx, j * LANES:(j + 1) * LANES]
            vc = kv_ref[0, 0:n_ctx, 256 + j * LANES:256 + (j + 1) * LANES]
            if latent:
                kc = jnp.concatenate([kc, kv_ref[0, pl.ds(ws, span), j * LANES:(j + 1) * LANES]], axis=0)
                vc = jnp.concatenate([vc, kv_ref[0, pl.ds(ws, span), 256 + j * LANES:256 + (j + 1) * LANES]], axis=0)
            sink = jnp.zeros((rows, 1), F32)
            for g in range(group):
                sink = jnp.where(head_of_row == g, sink_ref[group * j + g] * LOG2E, sink)
            s = _mm_nt(q4, kc)
            if latent:
                s = jnp.where(ok, s, -jnp.inf)
            m = jnp.maximum(jnp.max(s, -1, keepdims=True), sink)
            p = jnp.exp2((s - m).astype(MXU_DTYPE))
            v_left = lax.broadcasted_iota(jnp.int32, vc.shape, 1) < GQA_HEAD_DIM
            o = _mm(p, jnp.where(v_left, vc, jnp.ones_like(vc)))
            res.append(o / (pltpu.roll(o, GQA_HEAD_DIM, axis=1) + jnp.exp2(sink - m)))
        for j in range(GQA_KV_HEADS):
            for pp in range(group // 2):
                r0 = res[j][(2 * pp) * qb_rows:(2 * pp + 1) * qb_rows]
                r1 = res[j][(2 * pp + 1) * qb_rows:(2 * pp + 2) * qb_rows]
                pair = 2 * j + pp
                o_ref[0, :, pair * LANES:(pair + 1) * LANES] = jnp.where(
                    left, r0, pltpu.roll(r1, GQA_HEAD_DIM, axis=1)).astype(o_ref.dtype)

    @pl.when(i < ctx_blocks)
    def _():
        run(False)

    @pl.when(i >= ctx_blocks)
    def _():
        run(True)


GQA_QB = 256


def _gqa_call(gq, gkv, sink, B, S, C):
    qb_rows = GQA_QB
    out = pl.pallas_call(
        functools.partial(_gqa_kernel, n_ctx=C, n_all=S),
        grid=(B, S // qb_rows),
        in_specs=[pl.BlockSpec(memory_space=pltpu.SMEM),
                  pl.BlockSpec((1, qb_rows, 512), lambda b, i: (b, i, 0)),
                  pl.BlockSpec((1, S, 512), lambda b, i: (b, 0, 0))],
        out_specs=pl.BlockSpec((1, qb_rows, 512), lambda b, i: (b, i, 0)),
        out_shape=jax.ShapeDtypeStruct((B, S, 512), MXU_DTYPE),
        compiler_params=_params("parallel", "arbitrary"),
        name="gqa_attn",
    )(sink, gq.reshape(B, S, 512), gkv.reshape(B, S, 512))
    return out.reshape(B * S, 512)


DN_TR = 256
DN_HALO = 16
DN_CPT = DN_TR // DN_CHUNK


def _stack(x, left):
    z = jnp.zeros_like(x)
    return jnp.concatenate([jnp.where(left, x, z), jnp.where(left, z, x)], axis=1)


def _dnlocal_kernel(main_ref, prev_ref, next_ref, ab_ref, conv_ref, gp_ref, bd_ref, trif_ref, trib_ref,
                    eg_ref, eb_ref,
                    u_out, w_out, qg_out, kg_out, qk_out, gl_out, pad_ref, *, ctx_tiles, n_tiles):
    i = pl.program_id(1)
    tr = DN_TR
    first = (i == 0) | (i == ctx_tiles)
    last = (i == ctx_tiles - 1) | (i == n_tiles - 1)
    xp = prev_ref[0].astype(F32)
    xn = next_ref[0].astype(F32)
    pad_ref[0:DN_HALO, :] = jnp.where(first, jnp.zeros_like(xp), xp)
    pad_ref[DN_HALO:DN_HALO + tr, :] = main_ref[0].astype(F32)
    pad_ref[DN_HALO + tr:, :] = jnp.where(last, jnp.zeros_like(xn), xn)
    y = jnp.zeros((tr, 3 * DN_WIDTH), F32)
    for t in range(DN_CONV):
        y = y + conv_ref[t:t + 1, :] * pad_ref[pl.ds(DN_HALO - DN_CONV // 2 + t, tr), :]
    y = _silu(y)
    q, k, v = y[:, 0:512], y[:, 512:1024], y[:, 1024:1536]
    bd = bd_ref[...]
    q = q * lax.rsqrt(_exact_mm(q * q, bd) + 1e-6) * (DN_HEAD_DIM ** -0.5)
    k = k * lax.rsqrt(_exact_mm(k * k, bd) + 1e-6)

    ab = ab_ref[0]
    g = -jnp.exp(gp_ref[0:1, :]) * jax.nn.softplus(ab + gp_ref[1:2, :])
    beta = jax.nn.sigmoid(ab)
    lane = lax.broadcasted_iota(jnp.int32, (tr, LANES), 1)
    gc = jnp.where(lane < DN_HEADS, _exact_mm_left(trif_ref[...], g), _exact_mm_left(trib_ref[...], g))
    gcx_all = _exact_mm(gc, eg_ref[...])
    bx_all = _exact_mm(beta, eb_ref[...])

    c = DN_CPT
    lane3 = lax.broadcasted_iota(jnp.int32, (1, 1, LANES), 2)
    left = (lane3 % LANES) < DN_HEAD_DIM
    tpos = lane3 % DN_HEAD_DIM
    lane6 = lax.broadcasted_iota(jnp.int32, (1, 1, 2 * LANES), 2)
    left6 = (lane6 % LANES) < DN_HEAD_DIM
    ri = lax.broadcasted_iota(jnp.int32, (1, DN_CHUNK, LANES), 1)
    cj = lax.broadcasted_iota(jnp.int32, (1, DN_CHUNK, LANES), 2) % DN_HEAD_DIM
    one = jnp.ones((), F32)
    zero = jnp.zeros((), F32)

    units = [(d, j) for d in range(2) for j in range(4)]
    st = {}
    for d, j in units:
        off = d * 512 + j * LANES
        gcx = gcx_all[:, off:off + LANES].reshape(c, DN_CHUNK, LANES)
        bx = bx_all[:, off:off + LANES].reshape(c, DN_CHUNK, LANES)
        qp = q[:, j * LANES:(j + 1) * LANES].reshape(c, DN_CHUNK, LANES)
        kp = k[:, j * LANES:(j + 1) * LANES].reshape(c, DN_CHUNK, LANES)
        vp = v[:, j * LANES:(j + 1) * LANES].reshape(c, DN_CHUNK, LANES)
        gl = gcx[:, DN_CHUNK - 1:DN_CHUNK, :] if d == 0 else gcx[:, 0:1, :]
        kb = kp * bx
        kq = _bmm_nt(jnp.concatenate([kb, qp], axis=1), _stack(kp, left))
        hi, mid, lo = _split3(gcx)
        a6 = jnp.where(tpos == 0, hi, jnp.where(tpos == 1, mid, jnp.where(tpos == 2, lo,
                       jnp.where(tpos < 6, one, zero))))
        b6 = jnp.where(tpos < 3, one, jnp.where(tpos == 3, -hi, jnp.where(tpos == 4, -mid,
                       jnp.where(tpos == 5, -lo, zero))))
        diff = _bmm_nt(a6, _stack(b6, left))
        st[d, j] = dict(gcx=gcx, bx=bx, qp=qp, kp=kp, vp=vp, gl=gl, kb=kb, kq=kq, diff=diff)
    for d, j in units:
        u = st[d, j]
        incl = (ri >= cj) if d == 0 else (ri <= cj)
        strict = (ri > cj) if d == 0 else (ri < cj)
        dm = jnp.exp(jnp.where(incl, u["diff"], -jnp.inf))
        u["qkm"] = u["kq"][:, DN_CHUNK:, :] * dm
        u["x"] = -jnp.where(strict, u["kq"][:, 0:DN_CHUNK, :] * dm, zero)
        u["r"] = u["x"]
    for d, j in units:
        u = st[d, j]
        u["x"] = _bmm(u["x"], _stack(u["x"], left))
    for level in range(5):
        for d, j in units:
            u = st[d, j]
            xs = _stack(u["x"], left)
            if level < 4:
                m = _bmm(jnp.concatenate([u["r"], u["x"]], axis=1), xs)
                u["r"] = u["r"] + u["x"] + m[:, 0:DN_CHUNK, :]
                u["x"] = m[:, DN_CHUNK:, :]
            else:
                u["r"] = u["r"] + u["x"] + _bmm(u["r"], xs)
    for d, j in units:
        u = st[d, j]
        eg = jnp.exp(u["gcx"])
        rhs = jnp.concatenate([u["vp"] * u["bx"], u["kb"] * eg], axis=-1)
        sol = rhs + _bmm(u["r"], _stack(rhs, left6))
        sl = slice(j * LANES, (j + 1) * LANES)
        u_out[0, d, :, sl] = sol[:, :, 0:LANES].reshape(tr, LANES)
        w_out[0, d, :, sl] = sol[:, :, LANES:].reshape(tr, LANES).astype(w_out.dtype)
        qg_out[0, d, :, sl] = (u["qp"] * eg).reshape(tr, LANES).astype(qg_out.dtype)
        kg_out[0, d, :, sl] = (u["kp"] * jnp.exp(u["gl"] - u["gcx"])).reshape(tr, LANES).astype(kg_out.dtype)
        qk_out[0, d, :, sl] = u["qkm"].reshape(tr, LANES).astype(qk_out.dtype)
        gl_out[0, d, :, :, sl] = jnp.exp(u["gl"])


def _dn_constants():
    idx = np.arange(DN_TR)
    same = (idx[:, None] // DN_CHUNK) == (idx[None, :] // DN_CHUNK)
    trif = (same & (idx[None, :] <= idx[:, None])).astype(np.float32)
    trib = (same & (idx[None, :] >= idx[:, None])).astype(np.float32)
    h = np.arange(512)
    bd = ((h[:, None] // DN_HEAD_DIM) == (h[None, :] // DN_HEAD_DIM)).astype(np.float32)
    col = np.arange(LANES)[:, None]
    out = np.arange(1024)[None, :]
    unit = (out // 512) * DN_HEADS + (out % 512) // DN_HEAD_DIM
    eg = (col == unit).astype(np.float32)
    eb = (col == unit + 2 * DN_HEADS).astype(np.float32)
    return tuple(jnp.asarray(a, MXU_DTYPE) for a in (bd, trif, trib, eg, eb))


def _dnlocal_call(dqkv, ab, conv_w, gp, consts, B, S, C):
    tr = DN_TR
    n_tiles = S // tr
    hb = tr // DN_HALO
    n_hblk = S // DN_HALO
    bd, trif, trib, eg, eb = consts
    const2 = lambda b, i: (0, 0)
    big = lambda b, i: (b, 0, i, 0)
    act = MXU_DTYPE
    shp = (B, 2, S, 512)
    return pl.pallas_call(
        functools.partial(_dnlocal_kernel, ctx_tiles=C // tr, n_tiles=n_tiles),
        grid=(B, n_tiles),
        in_specs=[pl.BlockSpec((1, tr, 1536), lambda b, i: (b, i, 0)),
                  pl.BlockSpec((1, DN_HALO, 1536), lambda b, i: (b, jnp.maximum(i * hb - 1, 0), 0)),
                  pl.BlockSpec((1, DN_HALO, 1536), lambda b, i: (b, jnp.minimum((i + 1) * hb, n_hblk - 1), 0)),
                  pl.BlockSpec((1, tr, LANES), lambda b, i: (b, i, 0)),
                  pl.BlockSpec((8, 1536), const2),
                  pl.BlockSpec((8, LANES), const2),
                  pl.BlockSpec((512, 512), const2),
                  pl.BlockSpec((tr, tr), const2),
                  pl.BlockSpec((tr, tr), const2),
                  pl.BlockSpec((LANES, 1024), const2),
                  pl.BlockSpec((LANES, 1024), const2)],
        out_specs=[pl.BlockSpec((1, 2, tr, 512), big)] * 5
                  + [pl.BlockSpec((1, 2, DN_CPT, 1, 512), lambda b, i: (b, 0, i, 0, 0))],
        out_shape=[jax.ShapeDtypeStruct(shp, F32)] + [jax.ShapeDtypeStruct(shp, act)] * 4
                  + [jax.ShapeDtypeStruct((B, 2, S // DN_CHUNK, 1, 512), F32)],
        scratch_shapes=[pltpu.VMEM((tr + 2 * DN_HALO, 1536), F32)],
        compiler_params=_params("parallel", "parallel"),
        name="dn_local",
    )(dqkv.reshape(B, S, 1536), dqkv.reshape(B, S, 1536), dqkv.reshape(B, S, 1536), ab.reshape(B, S, LANES),
      conv_w, gp, bd, trif, trib, eg, eb)


def _dnscan_kernel(uf, wf, qgf, kgf, qkf, glf, ub, wb, qgb, kgb, qkb, glb, of_out, ob_out, s_ref):
    n = pl.program_id(1)
    lane = lax.broadcasted_iota(jnp.int32, (1, LANES), 1)
    left = lane < DN_HEAD_DIM
    row = lax.broadcasted_iota(jnp.int32, (LANES, LANES), 0)
    col = lax.broadcasted_iota(jnp.int32, (LANES, LANES), 1)
    same_head = (row < DN_HEAD_DIM) == (col < DN_HEAD_DIM)
    dirs = ((uf, wf, qgf, kgf, qkf, glf, of_out), (ub, wb, qgb, kgb, qkb, glb, ob_out))
    units = [(bb, d, j) for bb in range(uf.shape[0]) for d in range(2) for j in range(4)]
    sidx = lambda bb, d, j: (bb * 2 + d) * 4 + j
    sl = lambda j: slice(j * LANES, (j + 1) * LANES)
    started = n > 0
    st = {t: jnp.where(started, s_ref[sidx(*t)], jnp.zeros((LANES, LANES), F32)) for t in units}
    pre = {(bb, d, j): _mm(dirs[d][1][bb, 0, :, sl(j)], st[bb, d, j]) for bb, d, j in units}
    o1 = {(bb, d, j): _mm(dirs[d][2][bb, 0, :, sl(j)], st[bb, d, j]) for bb, d, j in units}
    vn = {(bb, d, j): dirs[d][0][bb, 0, :, sl(j)] - pre[bb, d, j] for bb, d, j in units}
    for bb, d, j in units:
        v = vn[bb, d, j]
        z = jnp.zeros_like(v)
        vst = jnp.concatenate([jnp.where(left, v, z), jnp.where(left, z, v)], axis=0)
        dirs[d][6][bb, :, sl(j)] = o1[bb, d, j] + _mm(dirs[d][4][bb, 0, :, sl(j)], vst)
    for bb, d, j in units:
        upd = _mm_tn(dirs[d][3][bb, 0, :, sl(j)], vn[bb, d, j])
        s_ref[sidx(bb, d, j)] = (st[bb, d, j] * dirs[d][5][bb, 0, 0, :, sl(j)]
                                 + jnp.where(same_head, upd, jnp.zeros_like(upd)))


DN_SCAN_BATCH = 8


def _dnscan_call(local, B, S, C):
    u, w, qg, kg, qk, gl = local
    nch, nc = S // DN_CHUNK, C // DN_CHUNK
    bb = DN_SCAN_BATCH if B % DN_SCAN_BATCH == 0 else 1

    def bidx(n):
        return jnp.where(n < nc, nc - 1 - n, nch - 1 + nc - n)

    fspec = pl.BlockSpec((bb, 1, DN_CHUNK, 512), lambda b, n: (b, 0, n, 0))
    bspec = pl.BlockSpec((bb, 1, DN_CHUNK, 512), lambda b, n: (b, 1, bidx(n), 0))
    fgl = pl.BlockSpec((bb, 1, 1, 1, 512), lambda b, n: (b, 0, n, 0, 0))
    bgl = pl.BlockSpec((bb, 1, 1, 1, 512), lambda b, n: (b, 1, bidx(n), 0, 0))
    return pl.pallas_call(
        _dnscan_kernel,
        grid=(B // bb, nch),
        in_specs=[fspec] * 5 + [fgl] + [bspec] * 5 + [bgl],
        out_specs=[pl.BlockSpec((bb, DN_CHUNK, 512), lambda b, n: (b, n, 0)),
                   pl.BlockSpec((bb, DN_CHUNK, 512), lambda b, n: (b, bidx(n), 0))],
        out_shape=[jax.ShapeDtypeStruct((B, S, 512), F32)] * 2,
        scratch_shapes=[pltpu.VMEM((bb * 8, LANES, LANES), F32)],
        compiler_params=_params("parallel", "arbitrary"),
        name="dn_scan",
    )(u, w, qg, kg, qk, gl, u, w, qg, kg, qk, gl)


def _pack_pairs(v):
    w = v.shape[1] // 2
    bits = lax.bitcast_convert_type(v.astype(jnp.bfloat16).astype(F32), jnp.int32)
    return lax.shift_right_logical(bits[:, :w], 16) | bits[:, w:]


def _unpack_pairs(p):
    lo = lax.bitcast_convert_type(lax.shift_left(p, 16), F32)
    hi = lax.bitcast_convert_type(p & jnp.int32(-65536), F32)
    return lo, hi


def _merge_kernel(x_ref, ga_ref, shf_ref, scf_ref, omla_ref, of_ref, ob_ref, dz_ref, ogqa_ref, gates_ref,
                  wo1_ref, wo2_ref, wo3_ref, wout_ref, dnn_ref, bd_ref, lng_ref, lnb_ref, wr_ref, rb_ref,
                  x1_out, ufp_out, eidx_out, rank_out, ew_out, cnt_out, cnt_ref):
    @pl.when(pl.program_id(0) == 0)
    def _():
        cnt_ref[...] = jnp.zeros_like(cnt_ref)

    o = of_ref[...] + ob_ref[...]
    ms = _exact_mm(o * o, bd_ref[...]) * (1.0 / DN_HEAD_DIM)
    dn = o * lax.rsqrt(ms + NORM_EPS) * dnn_ref[...] * _silu(dz_ref[...].astype(F32))
    g1 = gates_ref[:, 0:1024].astype(F32)
    g2 = gates_ref[:, 1024:2048].astype(F32)
    g3 = gates_ref[:, 2048:3072].astype(F32)
    m = (g1 * _mm(omla_ref[...], wo1_ref[...]) + g2 * _mm(dn, wo2_ref[...])
         + g3 * _mm(ogqa_ref[...], wo3_ref[...]))
    y = _mm(m, wout_ref[...])
    x1 = _layernorm(DEEPNORM_ALPHA * x_ref[...] + ga_ref[0] * y, lng_ref[...], lnb_ref[...])
    x1_out[...] = x1
    uf = x1 * (1.0 + scf_ref[0]) + shf_ref[0]
    _split_pieces(_pack_pairs(uf), ufp_out)

    tm = uf.shape[0]
    scores = jax.nn.sigmoid(_mm_nt(wr_ref[...], uf))[0:N_EXPERTS]
    sel = scores + rb_ref[0:N_EXPERTS, :]
    gsz = N_EXPERTS // N_GROUPS
    neg = jnp.full((), -jnp.inf, F32)
    sel3 = sel.reshape(N_GROUPS, gsz, tm)
    mem = lax.broadcasted_iota(jnp.int32, (N_GROUPS, gsz, tm), 1)
    m1 = jnp.max(sel3, 1, keepdims=True)
    i1 = jnp.min(jnp.where(sel3 == m1, mem, gsz), 1, keepdims=True)
    m2 = jnp.max(jnp.where(mem == i1, neg, sel3), 1, keepdims=True)
    gs = (m1 + m2).reshape(N_GROUPS, tm)
    gi = lax.broadcasted_iota(jnp.int32, (N_GROUPS, tm), 0)
    grank = jnp.zeros((N_GROUPS, tm), jnp.int32)
    for gp in range(N_GROUPS):
        other = gs[gp:gp + 1, :]
        beats = (other > gs) | ((other == gs) & (gp < gi))
        grank = grank + beats.astype(jnp.int32)
    gsel = (grank < TOPK_GROUPS).reshape(N_GROUPS, 1, tm)
    cur = jnp.where(gsel, sel3, neg).reshape(N_EXPERTS, tm)
    ei = lax.broadcasted_iota(jnp.int32, (N_EXPERTS, tm), 0)
    zero = jnp.zeros((N_EXPERTS, tm), F32)
    one = jnp.ones((N_EXPERTS, tm), F32)
    chosen = zero
    picks = []
    for _ in range(TOP_K):
        mx = jnp.max(cur, 0, keepdims=True)
        ix = jnp.min(jnp.where(cur == mx, ei, N_EXPERTS), 0, keepdims=True)
        pick = ei == ix
        picks.append((ix, pick))
        chosen = chosen + jnp.where(pick, one, zero)
        cur = jnp.where(pick, neg, cur)

    r_i = lax.broadcasted_iota(jnp.int32, (tm, tm), 0)
    c_i = lax.broadcasted_iota(jnp.int32, (tm, tm), 1)
    before = jnp.where(r_i < c_i, 1.0, 0.0)
    pos = cnt_ref[:, 0:1] + _mm(chosen, before)
    cnt_new = cnt_ref[...] + jnp.sum(chosen, 1, keepdims=True)
    cnt_ref[...] = cnt_new
    cnt_out[...] = cnt_new

    w_rows = [jnp.sum(jnp.where(pick, scores, zero), 0, keepdims=True) for _, pick in picks]
    wsum = w_rows[0]
    for w_k in w_rows[1:]:
        wsum = wsum + w_k
    eidx_out[...] = jnp.concatenate([ix for ix, _ in picks], axis=0)
    rank_out[...] = jnp.concatenate(
        [jnp.sum(jnp.where(pick, pos, zero), 0, keepdims=True) for _, pick in picks], axis=0).astype(jnp.int32)
    w8 = jnp.concatenate([w_k / wsum * ROUTED_SCALE for w_k in w_rows], axis=0)
    ew_out[...] = jnp.concatenate([w8, jnp.zeros((LANES - TOP_K, tm), F32)], axis=0).T


def _merge_call(xs, mod_l, omla, of, ob, dz, ogqa, gates, wo1, wo2, wo3, wout, dnn, bd, lng, lnb, wr, rb, B, S, C):
    T, D = xs.shape
    tpb, ctiles = S // TM, C // TM
    row = lambda i: (i, 0)
    const = lambda i: (0, 0)
    modspec = lambda k: pl.BlockSpec((1, 1, D), _mod_index(tpb, ctiles, k))
    return pl.pallas_call(
        _merge_kernel,
        grid=(T // TM,),
        in_specs=[pl.BlockSpec((TM, D), row), modspec(2), modspec(3), modspec(4),
                  pl.BlockSpec((TM, 512), row), pl.BlockSpec((TM, 512), row), pl.BlockSpec((TM, 512), row),
                  pl.BlockSpec((TM, 512), row), pl.BlockSpec((TM, 512), row), pl.BlockSpec((TM, 3072), row),
                  pl.BlockSpec((512, D), const), pl.BlockSpec((512, D), const), pl.BlockSpec((512, D), const),
                  pl.BlockSpec((D, D), const), pl.BlockSpec((1, 512), const), pl.BlockSpec((512, 512), const),
                  pl.BlockSpec((1, D), const), pl.BlockSpec((1, D), const),
                  pl.BlockSpec((LANES, D), const), pl.BlockSpec((LANES, 1), const)],
        out_specs=[pl.BlockSpec((TM, D), row), pl.BlockSpec((N_PIECES, TM, PIECE), lambda i: (0, i, 0)),
                   pl.BlockSpec((TOP_K, TM), lambda i: (0, i)), pl.BlockSpec((TOP_K, TM), lambda i: (0, i)),
                   pl.BlockSpec((TM, LANES), row), pl.BlockSpec((N_EXPERTS, LANES), const)],
        out_shape=[jax.ShapeDtypeStruct((T, D), F32), jax.ShapeDtypeStruct((N_PIECES, T, PIECE), jnp.int32),
                   jax.ShapeDtypeStruct((TOP_K, T), jnp.int32), jax.ShapeDtypeStruct((TOP_K, T), jnp.int32),
                   jax.ShapeDtypeStruct((T, LANES), F32), jax.ShapeDtypeStruct((N_EXPERTS, LANES), F32)],
        scratch_shapes=[pltpu.VMEM((N_EXPERTS, LANES), F32)],
        compiler_params=_params("arbitrary"),
        name="merge_norm_route",
    )(xs, mod_l, mod_l, mod_l, omla, of.reshape(T, 512), ob.reshape(T, 512), dz, ogqa, gates,
      wo1, wo2, wo3, wout, dnn, bd, lng, lnb, wr, rb)


EXPERT_BLOCK = 512
SC_WINDOW = 128
N_PIECES = 2
PIECE = D_MODEL // 2 // N_PIECES


def _split_pieces(packed, out_ref):
    for h in range(N_PIECES):
        out_ref[h] = packed[:, h * PIECE:(h + 1) * PIECE]


def _mm_pieces(pieces, w):
    acc = None
    for h, (lo, hi) in enumerate(pieces):
        t = (_mm(lo, w[h * PIECE:(h + 1) * PIECE, :])
             + _mm(hi, w[D_MODEL // 2 + h * PIECE:D_MODEL // 2 + (h + 1) * PIECE, :]))
        acc = t if acc is None else acc + t
    return acc


def _sc_mesh():
    return plsc.VectorSubcoreMesh(core_axis_name="c", subcore_axis_name="s")


def _sc_gather_rows(y, idx):
    n = idx.shape[1]
    W = y.shape[1]

    @pl.kernel(out_type=jax.ShapeDtypeStruct((n, W), y.dtype), mesh=_sc_mesh(), scratch_types=[])
    def gather(y_hbm, i_hbm, o_hbm):
        def body(i_vmem, o_vmem):
            pltpu.sync_copy(y_hbm.at[i_vmem.at[0]], o_vmem)

        pltpu.emit_pipeline(
            body,
            grid=(n // SC_WINDOW,),
            in_specs=[pl.BlockSpec((1, SC_WINDOW), lambda i: (0, i))],
            out_specs=[pl.BlockSpec((SC_WINDOW, W), lambda i: (i, 0))],
            core_axis_name=("c", "s"),
            dimension_semantics=(pltpu.PARALLEL,),
        )(i_hbm, o_hbm)

    return gather(y, idx)


SC_LANES = 16
SC_WORKERS = 32
SC_CHUNK = 2176


def _sc_invert_rows(dest, default, n_tok):
    n_rows = default.shape[0]
    per_w = n_rows // SC_WORKERS
    n_k = dest.shape[0] // n_tok
    assert n_rows % (SC_WORKERS * SC_LANES) == 0 and n_tok % SC_CHUNK == 0

    @pl.kernel(out_type=jax.ShapeDtypeStruct((n_rows,), jnp.int32), mesh=_sc_mesh(),
               scratch_types=[pltpu.VMEM((per_w,), jnp.int32), pltpu.VMEM((SC_CHUNK,), jnp.int32)],
               compiler_params=pltpu.CompilerParams(needs_layout_passes=False))
    def invert(dest_hbm, dflt_hbm, out_hbm, rows_v, dest_v):
        wid = lax.axis_index("s") * 2 + lax.axis_index("c")
        base = wid * per_w
        pltpu.sync_copy(dflt_hbm.at[pl.ds(base, per_w)], rows_v)
        lanes = lax.iota(jnp.int32, SC_LANES)
        for k in range(n_k):
            @pl.loop(0, n_tok // SC_CHUNK)
            def _(c):
                pltpu.sync_copy(dest_hbm.at[pl.ds(k * n_tok + c * SC_CHUNK, SC_CHUNK)], dest_v)

                @pl.loop(0, SC_CHUNK, step=SC_LANES)
                def _(o):
                    local = dest_v[pl.ds(o, SC_LANES)] - base
                    mine = (local >= 0) & (local < per_w)
                    plsc.store_scatter(rows_v, [jnp.where(mine, local, 0)], c * SC_CHUNK + o + lanes, mask=mine)

        pltpu.sync_copy(rows_v, out_hbm.at[pl.ds(base, per_w)])

    return invert(dest, default)


def _experts_kernel(be_ref, nv_ref, xb_ref, wg_ref, wu_ref, wd_ref, y_out):
    b = pl.program_id(0)
    nv = nv_ref[b]

    @pl.when(nv > 0)
    def _():
        rows = lax.broadcasted_iota(jnp.int32, xb_ref.shape[1:], 0)
        pieces = []
        for h in range(N_PIECES):
            xh = xb_ref[h]
            pieces.append(_unpack_pairs(jnp.where(rows < nv, xh, jnp.zeros_like(xh))))
        hid = _silu(_mm_pieces(pieces, wg_ref.at[0, 0])) * _mm_pieces(pieces, wu_ref.at[0, 0])
        _split_pieces(_pack_pairs(_mm(hid, wd_ref[0, 0])), y_out)


def _experts_call(xb, block_e, nvalid, wg, wu, wd, layer):
    _, R, _ = xb.shape
    D = D_MODEL
    blk = (N_PIECES, EXPERT_BLOCK, PIECE)
    grid_spec = pltpu.PrefetchScalarGridSpec(
        num_scalar_prefetch=2,
        grid=(R // EXPERT_BLOCK,),
        in_specs=[pl.BlockSpec(blk, lambda b, be, nv: (0, b, 0)),
                  pl.BlockSpec((1, 1, D, EXPERT_DIM), lambda b, be, nv: (layer, be[b], 0, 0)),
                  pl.BlockSpec((1, 1, D, EXPERT_DIM), lambda b, be, nv: (layer, be[b], 0, 0)),
                  pl.BlockSpec((1, 1, EXPERT_DIM, D), lambda b, be, nv: (layer, be[b], 0, 0))],
        out_specs=pl.BlockSpec(blk, lambda b, be, nv: (0, b, 0)),
    )
    return pl.pallas_call(
        _experts_kernel,
        grid_spec=grid_spec,
        out_shape=jax.ShapeDtypeStruct((N_PIECES, R, PIECE), jnp.int32),
        compiler_params=_params("arbitrary"),
        name="moe_experts",
    )(block_e, nvalid, xb, wg, wu, wd)


def _shared_kernel(ufp_ref, sg_ref, su_ref, sd_ref, o_ref):
    pieces = [_unpack_pairs(ufp_ref[h]) for h in range(N_PIECES)]
    hs = _silu(_mm_pieces(pieces, sg_ref)) * _mm_pieces(pieces, su_ref)
    o_ref[...] = _mm(hs, sd_ref[...])


def _shared_call(ufp, sg, su, sd):
    _, T, _ = ufp.shape
    D = D_MODEL
    const = lambda i: (0, 0)
    return pl.pallas_call(
        _shared_kernel,
        grid=(T // TM,),
        in_specs=[pl.BlockSpec((N_PIECES, TM, PIECE), lambda i: (0, i, 0)),
                  pl.BlockSpec((D, SHARED_DIM), const), pl.BlockSpec((D, SHARED_DIM), const),
                  pl.BlockSpec((SHARED_DIM, D), const)],
        out_specs=pl.BlockSpec((TM, D), lambda i: (i, 0)),
        out_shape=jax.ShapeDtypeStruct((T, D), F32),
        compiler_params=_params("parallel"),
        name="moe_shared",
    )(ufp, sg, su, sd)


def _combine_kernel(x_ref, fs_ref, yg_ref, ew_ref, gf_ref, g_ref, b_ref, o_ref):
    ew = ew_ref[...]
    lane = lax.broadcasted_iota(jnp.int32, ew.shape, 1)
    acc = [[jnp.zeros((x_ref.shape[0], PIECE), F32) for _ in range(N_PIECES)] for _ in range(2)]
    for k in range(TOP_K):
        wk = jnp.sum(jnp.where(lane == k, ew, jnp.zeros_like(ew)), axis=1, keepdims=True)
        for h in range(N_PIECES):
            ylo, yhi = _unpack_pairs(yg_ref[h, k])
            acc[0][h] = acc[0][h] + wk * ylo
            acc[1][h] = acc[1][h] + wk * yhi
    f = fs_ref[...] + jnp.concatenate(acc[0] + acc[1], axis=1)
    o_ref[...] = _layernorm(DEEPNORM_ALPHA * x_ref[...] + gf_ref[0] * f, g_ref[...], b_ref[...])


def _combine_call(x1, fs, yg, ew, mod_l, g, b, B, S, C):
    T, D = x1.shape
    tpb, ctiles = S // TM, C // TM
    row = lambda i: (i, 0)
    const = lambda i: (0, 0)
    return pl.pallas_call(
        _combine_kernel,
        grid=(T // TM,),
        in_specs=[pl.BlockSpec((TM, D), row), pl.BlockSpec((TM, D), row),
                  pl.BlockSpec((N_PIECES, TOP_K, TM, PIECE), lambda i: (0, 0, i, 0)),
                  pl.BlockSpec((TM, LANES), row),
                  pl.BlockSpec((1, 1, D), _mod_index(tpb, ctiles, 5)),
                  pl.BlockSpec((1, D), const), pl.BlockSpec((1, D), const)],
        out_specs=pl.BlockSpec((TM, D), row),
        out_shape=jax.ShapeDtypeStruct((T, D), F32),
        compiler_params=_params("parallel"),
        name="moe_combine_norm",
    )(x1, fs, yg, ew, mod_l, g, b)


def _moe_routed(ufp, eidx_t, rank_t, counts, wg, wu, wd, layer):
    T = ufp.shape[1]
    n_blocks = -(-(T * TOP_K + N_EXPERTS * (EXPERT_BLOCK - 1)) // EXPERT_BLOCK)
    n_rows = n_blocks * EXPERT_BLOCK
    padded = (counts + EXPERT_BLOCK - 1) // EXPERT_BLOCK * EXPERT_BLOCK
    pad_end = jnp.cumsum(padded)
    start_pad = pad_end - padded
    experts = jnp.arange(N_EXPERTS, dtype=jnp.int32)

    def lookup(table, idx):
        sel = idx[None] == experts.reshape((N_EXPERTS,) + (1,) * idx.ndim)
        return jnp.sum(jnp.where(sel, table.reshape((N_EXPERTS,) + (1,) * idx.ndim), 0), axis=0)

    dest_t = lookup(start_pad, eidx_t) + rank_t
    blk = jnp.arange(n_blocks, dtype=jnp.int32) * EXPERT_BLOCK
    block_e = jnp.minimum(jnp.sum((blk[:, None] >= pad_end[None, :]).astype(jnp.int32), axis=1), N_EXPERTS - 1)
    nvalid = jnp.clip(lookup(counts, block_e) - (blk - lookup(start_pad, block_e)), 0, EXPERT_BLOCK)
    row_tok = _sc_invert_rows(dest_t.reshape(-1), jnp.arange(n_rows, dtype=jnp.int32) % T, T)
    piece = jnp.arange(N_PIECES, dtype=jnp.int32)
    src = (piece[:, None] * T + row_tok[None, :]).reshape(1, N_PIECES * n_rows)
    xb = _sc_gather_rows(ufp.reshape(N_PIECES * T, PIECE), src).reshape(N_PIECES, n_rows, PIECE)
    yb = _experts_call(xb, block_e.astype(jnp.int32), nvalid.astype(jnp.int32), wg, wu, wd, layer)
    back = (piece[:, None, None] * n_rows + dest_t[None]).reshape(1, N_PIECES * TOP_K * T)
    yg = _sc_gather_rows(yb.reshape(N_PIECES * n_rows, PIECE), back)
    return yg.reshape(N_PIECES, TOP_K, T, PIECE)


def _rot_cols(w, half):
    return jnp.concatenate([-w[:, half:], w[:, :half]], axis=1)


def _prep_w_in(w):
    d = w.shape[0]
    offs = np.cumsum((0,) + IN_SIZES)
    cq, ckv, kr, dqkv, da, db, dz, gq, gk, gv, gates = (w[:, offs[t]:offs[t + 1]] for t in range(len(IN_SIZES)))
    z = lambda n: jnp.zeros((d, n), w.dtype)
    krg = jnp.concatenate([z(64), kr, z(32)], 1)
    krr = jnp.concatenate([z(64), _rot_cols(kr, MLA_ROPE // 2), z(32)], 1)
    ab = jnp.concatenate([da, db, z(LANES - 4 * DN_HEADS)], 1)
    hd = GQA_HEAD_DIM
    gq_rot = jnp.concatenate([_rot_cols(gq[:, h * hd:(h + 1) * hd], hd // 2) for h in range(GQA_HEADS)], 1)
    dup = lambda t: jnp.concatenate([t[:, 0:hd], t[:, 0:hd], t[:, hd:2 * hd], t[:, hd:2 * hd]], 1)
    gk_rot = jnp.concatenate([_rot_cols(gk[:, h * hd:(h + 1) * hd], hd // 2) for h in range(GQA_KV_HEADS)], 1)
    cat = jnp.concatenate([cq, ckv, krg, krr, dqkv, ab, dz, gq, gq_rot, dup(gk), dup(gk_rot), dup(gv), gates], 1)
    assert cat.shape[1] == NZ
    return cat.astype(MXU_DTYPE)


def _prep_w_uq(w):
    d = w.shape[0]
    hw = MLA_NOPE + MLA_ROPE
    a, b = [], []
    for h in range(MLA_HEADS):
        wh = w[:, h * hw:(h + 1) * hw]
        a += [wh, jnp.zeros((d, LANES - hw), w.dtype)]
        b += [jnp.zeros((d, MLA_NOPE), w.dtype), _rot_cols(wh[:, MLA_NOPE:], MLA_ROPE // 2),
              jnp.zeros((d, LANES - hw), w.dtype)]
    return jnp.concatenate(a + b, 1).astype(MXU_DTYPE)


def _prep_w_ukv(w):
    d = w.shape[0]
    hw = MLA_NOPE + MLA_V
    kpart, vpart = [], []
    for h in range(MLA_HEADS):
        wh = w[:, h * hw:(h + 1) * hw]
        kpart += [wh[:, :MLA_NOPE], jnp.zeros((d, LANES - MLA_NOPE), w.dtype)]
        vpart += [wh[:, MLA_NOPE:]]
    return jnp.concatenate(kpart + vpart, 1).astype(MXU_DTYPE)


def _rope_tables(n_rows, C):
    row = jnp.repeat(jnp.arange(n_rows, dtype=F32), GRID_W)
    col = jnp.tile(jnp.arange(GRID_W, dtype=F32), n_rows)

    def angles(dim):
        n = dim // 4
        inv = ROPE_BASE ** (-jnp.arange(n, dtype=F32) / n)
        return jnp.concatenate([row[:, None] * inv, col[:, None] * inv], axis=-1)

    def with_ctx(cos, sin):
        return (jnp.concatenate([jnp.ones((C, LANES), F32), cos], 0),
                jnp.concatenate([jnp.zeros((C, LANES), F32), sin], 0))

    L = n_rows * GRID_W
    am = angles(MLA_ROPE)
    one, zero = jnp.ones((L, MLA_NOPE), F32), jnp.zeros((L, MLA_NOPE), F32)
    cm = jnp.concatenate([one, jnp.cos(am), jnp.cos(am), one[:, :32]], 1)
    sm = jnp.concatenate([zero, jnp.sin(am), jnp.sin(am), zero[:, :32]], 1)
    ag = angles(GQA_HEAD_DIM)
    cg = jnp.tile(jnp.cos(ag), (1, 4))
    sg = jnp.tile(jnp.sin(ag), (1, 4))
    return with_ctx(cm, sm) + with_ctx(cg, sg)


def kernel(x, c, ctx, c_ctx, w_ada, b_ada, w_in, mla_q_norm, mla_kv_norm, w_uq, w_ukv, dn_conv, dn_a_log, dn_dt_bias, dn_norm, gqa_sink, w_o_mla, w_o_dn, w_o_gqa, w_out, ln1_g, ln1_b, w_router, router_bias, w_exp_gate, w_exp_up, w_exp_down, w_sh_gate, w_sh_up, w_sh_down, ln2_g, ln2_b):
    B, L, D = x.shape
    C = ctx.shape[1]
    S = C + L
    nl = w_in.shape[0]
    assert D == D_MODEL and nl == DEPTH and B <= CTX_MOD_ROW
    assert C % TM == 0 and L % TM == 0 and L % GRID_W == 0 and L >= 3 * WINDOW
    cast = lambda t: t.astype(MXU_DTYPE)

    cc = jnp.zeros((MOD_ROWS, D), F32).at[0:B].set(c).at[CTX_MOD_ROW].set(c_ctx)
    mods = _ada_call(cc, w_ada, b_ada)
    tabs = _rope_tables(L // GRID_W, C)
    dn_consts = _dn_constants()
    xs = jnp.concatenate([ctx, x], axis=1).reshape(B * S, D)

    for l in range(nl):
        mod_l = mods[l].reshape(MOD_ROWS * 6, 1, D)
        q, k, v, dqkv, ab, dz, gq, gkv, gates = _inproj_call(
            xs, mod_l, _prep_w_in(w_in[l]), _prep_w_uq(w_uq[l]), _prep_w_ukv(w_ukv[l]),
            mla_q_norm[l].reshape(1, -1), mla_kv_norm[l].reshape(1, -1), tabs, B, S, C)
        omla = _mla_call(q, k, v, B, S, C)
        conv8 = jnp.zeros((8, 3 * DN_WIDTH), F32).at[0:DN_CONV].set(dn_conv[l])
        gp = (jnp.zeros((8, LANES), F32).at[0, 0:2 * DN_HEADS].set(dn_a_log[l].reshape(-1))
              .at[1, 0:2 * DN_HEADS].set(dn_dt_bias[l].reshape(-1)))
        local = _dnlocal_call(dqkv, ab, conv8, gp, dn_consts, B, S, C)
        of, ob = _dnscan_call(local, B, S, C)
        ogqa = _gqa_call(gq, gkv, gqa_sink[l], B, S, C)
        wr = jnp.zeros((LANES, D), F32).at[0:N_EXPERTS].set(w_router[l].T)
        rb = jnp.zeros((LANES, 1), F32).at[0:N_EXPERTS, 0].set(router_bias[l])
        x1, ufp, eidx_t, rank_t, ew, cnt = _merge_call(
            xs, mod_l, omla, of, ob, dz, ogqa, gates,
            cast(w_o_mla[l]), cast(w_o_dn[l]), cast(w_o_gqa[l]), cast(w_out[l]),
            jnp.tile(dn_norm[l], DN_HEADS).reshape(1, DN_WIDTH), dn_consts[0],
            ln1_g[l].reshape(1, D), ln1_b[l].reshape(1, D), cast(wr), rb, B, S, C)
        fs = _shared_call(ufp, cast(w_sh_gate[l]), cast(w_sh_up[l]), cast(w_sh_down[l]))
        yg = _moe_routed(ufp, eidx_t, rank_t, cnt[:, 0].astype(jnp.int32),
                         w_exp_gate, w_exp_up, w_exp_down, l)
        xs = _combine_call(x1, fs, yg, ew, mod_l, ln2_g[l].reshape(1, D), ln2_b[l].reshape(1, D), B, S, C)
    return xs.reshape(B, S, D)[:, C:, :]
```

```python
import functools

import numpy as np
import jax
import jax.numpy as jnp
from jax import lax
from jax.experimental import pallas as pl
from jax.experimental.pallas import tpu as pltpu
from jax.experimental.pallas import tpu_sc as plsc

F32 = jnp.float32
MXU_DTYPE = jnp.bfloat16

D_MODEL = 1024
DEPTH = 4
GRID_W = 64
NORM_EPS = 1e-6
ROPE_BASE = 10000.0
DEEPNORM_ALPHA = (2.0 * DEPTH) ** 0.25

MLA_HEADS = 8
MLA_Q_LORA = 256
MLA_KV_LORA = 128
MLA_NOPE = 64
MLA_ROPE = 32
MLA_V = 64
MLA_SCALE = (MLA_NOPE + MLA_ROPE) ** -0.5
LOG2E = float(np.log2(np.e))

DN_HEADS = 8
DN_HEAD_DIM = 64
DN_WIDTH = DN_HEADS * DN_HEAD_DIM
DN_CONV = 5
DN_CHUNK = 64

GQA_HEADS = 8
GQA_KV_HEADS = 2
GQA_HEAD_DIM = 64
GQA_SCALE = GQA_HEAD_DIM ** -0.5
WINDOW = 128

N_EXPERTS = 64
TOP_K = 8
N_GROUPS = 8
TOPK_GROUPS = 4
EXPERT_DIM = 256
SHARED_DIM = 256
ROUTED_SCALE = 2.5

IN_SIZES = (MLA_Q_LORA, MLA_KV_LORA, MLA_ROPE,
            3 * DN_WIDTH, 2 * DN_HEADS, 2 * DN_HEADS, DN_WIDTH,
            GQA_HEADS * GQA_HEAD_DIM, GQA_KV_HEADS * GQA_HEAD_DIM, GQA_KV_HEADS * GQA_HEAD_DIM,
            3 * D_MODEL)

LANES = 128
TM = 256
MOD_ROWS = 16
CTX_MOD_ROW = 8

OFF_A = 0
OFF_DQKV = 640
OFF_AB = OFF_DQKV + 3 * DN_WIDTH
OFF_DZ = OFF_AB + LANES
OFF_GQ = OFF_DZ + DN_WIDTH
OFF_GK = OFF_GQ + 1024
OFF_GATES = OFF_GK + 768
NZ = OFF_GATES + 3 * D_MODEL

VMEM_LIMIT = 56 * 1024 * 1024


def _mm(a, b):
    return jnp.dot(a.astype(MXU_DTYPE), b.astype(MXU_DTYPE), preferred_element_type=F32)


def _mm_nt(a, b):
    return lax.dot_general(a.astype(MXU_DTYPE), b.astype(MXU_DTYPE), (((1,), (1,)), ((), ())),
                           preferred_element_type=F32)


def _mm_tn(a, b):
    return lax.dot_general(a.astype(MXU_DTYPE), b.astype(MXU_DTYPE), (((0,), (0,)), ((), ())),
                           preferred_element_type=F32)


def _bmm(a, b):
    return jnp.einsum('cik,ckj->cij', a.astype(MXU_DTYPE), b.astype(MXU_DTYPE), preferred_element_type=F32)


def _bmm_nt(a, b):
    return jnp.einsum('cik,cjk->cij', a.astype(MXU_DTYPE), b.astype(MXU_DTYPE), preferred_element_type=F32)


def _split3(x):
    hi = x.astype(jnp.bfloat16).astype(F32)
    r = x - hi
    mid = r.astype(jnp.bfloat16).astype(F32)
    lo = (r - mid).astype(jnp.bfloat16).astype(F32)
    return hi, mid, lo


def _exact_mm(x, m01):
    hi, mid, lo = _split3(x)
    return _mm(hi, m01) + _mm(mid, m01) + _mm(lo, m01)


def _exact_mm_left(m01, x):
    hi, mid, lo = _split3(x)
    return _mm(m01, hi) + _mm(m01, mid) + _mm(m01, lo)


def _silu(x):
    return x * jax.nn.sigmoid(x)


def _layernorm(v, g, b):
    mu = jnp.mean(v, -1, keepdims=True)
    d = v - mu
    var = jnp.mean(d * d, -1, keepdims=True)
    return d * lax.rsqrt(var + NORM_EPS) * g + b


def _mod_index(tiles_per_b, ctx_tiles, k):
    def index(i):
        row = jnp.where((i % tiles_per_b) < ctx_tiles, CTX_MOD_ROW, i // tiles_per_b)
        return (row * 6 + k, 0, 0)
    return index


def _params(*sem):
    return pltpu.CompilerParams(dimension_semantics=sem, vmem_limit_bytes=VMEM_LIMIT)


def _ada_kernel(c_ref, w_ref, b_ref, o_ref):
    o_ref[0] = _mm(_silu(c_ref[...]), w_ref[0]) + b_ref[0]


def _ada_call(cc, w_ada, b_ada):
    nl, d, n6 = w_ada.shape
    tn = 1536
    return pl.pallas_call(
        _ada_kernel,
        grid=(nl, n6 // tn),
        in_specs=[pl.BlockSpec((MOD_ROWS, d), lambda l, j: (0, 0)),
                  pl.BlockSpec((1, d, tn), lambda l, j: (l, 0, j)),
                  pl.BlockSpec((1, 1, tn), lambda l, j: (l, 0, j))],
        out_specs=pl.BlockSpec((1, MOD_ROWS, tn), lambda l, j: (l, 0, j)),
        out_shape=jax.ShapeDtypeStruct((nl, MOD_ROWS, n6), F32),
        compiler_params=_params("parallel", "parallel"),
        name="ada_mod",
    )(cc, w_ada, b_ada.reshape(nl, 1, n6))


def _inproj_kernel(x_ref, sh_ref, sc_ref, w_ref, wuq_ref, wukv_ref, qn_ref, kvn_ref,
                   cm_ref, sm_ref, cg_ref, sg_ref,
                   q_out, k_out, v_out, dqkv_out, ab_out, dz_out, gq_out, gkv_out, gates_out):
    u = (x_ref[...] * (1.0 + sc_ref[0]) + sh_ref[0]).astype(MXU_DTYPE)

    def z(off, width):
        return jnp.dot(u, w_ref[:, off:off + width], preferred_element_type=F32)

    def rms(v, g):
        return v * lax.rsqrt(jnp.mean(v * v, -1, keepdims=True) + NORM_EPS) * g

    cm, sm, cg, sg = cm_ref[...], sm_ref[...], cg_ref[...], sg_ref[...]

    za = z(OFF_A, 640)
    qq = _mm(rms(za[:, 0:256], qn_ref[...]), wuq_ref[...])
    kvv = _mm(rms(za[:, 256:384], kvn_ref[...]), wukv_ref[...])
    k_rope = za[:, 384:512] * cm + za[:, 512:640] * sm
    for h in range(MLA_HEADS):
        sl = slice(h * LANES, (h + 1) * LANES)
        qa = qq[:, h * LANES:(h + 1) * LANES]
        qb = qq[:, 1024 + h * LANES:1024 + (h + 1) * LANES]
        q_out[:, sl] = ((qa * cm + qb * sm) * (MLA_SCALE * LOG2E)).astype(q_out.dtype)
        k_out[:, sl] = (kvv[:, sl] + k_rope).astype(k_out.dtype)
    v_out[...] = kvv[:, 1024:1536].astype(v_out.dtype)

    for t in range(3):
        dqkv_out[:, t * 512:(t + 1) * 512] = z(OFF_DQKV + t * 512, 512).astype(dqkv_out.dtype)
    ab_out[...] = z(OFF_AB, LANES)
    dz_out[...] = z(OFF_DZ, DN_WIDTH).astype(dz_out.dtype)

    zq = z(OFF_GQ, 1024)
    for p in range(4):
        sl = slice(p * LANES, (p + 1) * LANES)
        gq_out[:, sl] = ((zq[:, sl] * cg + zq[:, 512 + p * LANES:512 + (p + 1) * LANES] * sg)
                         * (GQA_SCALE * LOG2E)).astype(gq_out.dtype)
    zk = z(OFF_GK, 768)
    for j in range(2):
        sl = slice(j * LANES, (j + 1) * LANES)
        gkv_out[:, sl] = (zk[:, sl] * cg + zk[:, 256 + j * LANES:256 + (j + 1) * LANES] * sg).astype(gkv_out.dtype)
    gkv_out[:, 256:512] = zk[:, 512:768].astype(gkv_out.dtype)

    for t in range(3):
        gates_out[:, t * 1024:(t + 1) * 1024] = jax.nn.sigmoid(z(OFF_GATES + t * 1024, 1024)).astype(gates_out.dtype)


def _inproj_call(xs, mod_l, w_cat, wuq_cat, wukv_cat, qn, kvn, tabs, B, S, C):
    T, D = xs.shape
    tpb, ctiles = S // TM, C // TM
    act = MXU_DTYPE
    row = lambda i: (i, 0)
    const = lambda i: (0, 0)
    tab = lambda i: (i % tpb, 0)
    widths = (1024, 1024, 512, 1536, LANES, 512, 512, 512, 3072)
    dtypes = (act, act, act, act, F32, act, act, act, act)
    return pl.pallas_call(
        _inproj_kernel,
        grid=(T // TM,),
        in_specs=[pl.BlockSpec((TM, D), row),
                  pl.BlockSpec((1, 1, D), _mod_index(tpb, ctiles, 0)),
                  pl.BlockSpec((1, 1, D), _mod_index(tpb, ctiles, 1)),
                  pl.BlockSpec((D, NZ), const),
                  pl.BlockSpec((MLA_Q_LORA, 2048), const),
                  pl.BlockSpec((MLA_KV_LORA, 1536), const),
                  pl.BlockSpec((1, MLA_Q_LORA), const),
                  pl.BlockSpec((1, MLA_KV_LORA), const)]
                 + [pl.BlockSpec((TM, LANES), tab)] * 4,
        out_specs=[pl.BlockSpec((TM, w), row) for w in widths],
        out_shape=[jax.ShapeDtypeStruct((T, w), dt) for w, dt in zip(widths, dtypes)],
        compiler_params=_params("parallel"),
        name="in_proj",
    )(xs, mod_l, mod_l, w_cat, wuq_cat, wukv_cat, qn, kvn, *tabs)


def _mla_kernel(q_ref, k_ref, v_ref, o_ref, *, n_ctx, n_all):
    i = pl.program_id(2)
    tq = q_ref.shape[1]
    left = lax.broadcasted_iota(jnp.int32, (tq, LANES), 1) < MLA_V

    def attend(nk, ctx_rows):
        v = v_ref[0, 0:nk, :]
        v_left = lax.broadcasted_iota(jnp.int32, v.shape, 1) < MLA_V
        ones = jnp.ones_like(v)
        s = [_mm_nt(q_ref[0, :, hh * LANES:(hh + 1) * LANES], k_ref[0, 0:nk, hh * LANES:(hh + 1) * LANES])
             for hh in range(2)]
        if ctx_rows:
            row = lax.broadcasted_iota(jnp.int32, (tq, nk), 0)
            col = lax.broadcasted_iota(jnp.int32, (tq, nk), 1)
            ok = (row >= ctx_rows) | (col < ctx_rows)
            s = [jnp.where(ok, x, -jnp.inf) for x in s]
        p = [jnp.exp2((x - jnp.max(x, -1, keepdims=True)).astype(MXU_DTYPE)) for x in s]
        o = [_mm(p[0], jnp.where(v_left, v, ones)), _mm(p[1], jnp.where(v_left, ones, v))]
        outs = [x / pltpu.roll(x, MLA_V, axis=1) for x in o]
        o_ref[0] = jnp.where(left, outs[0], outs[1]).astype(o_ref.dtype)

    if tq > n_ctx:
        @pl.when(i == 0)
        def _():
            attend(n_all, n_ctx)

        @pl.when(i > 0)
        def _():
            attend(n_all, 0)
    else:
        ctx_tiles = n_ctx // tq

        @pl.when(i < ctx_tiles)
        def _():
            attend(n_ctx, 0)

        @pl.when(i >= ctx_tiles)
        def _():
            attend(n_all, 0)


MLA_TQ = (544, 384, 256)


def _mla_call(q, k, v, B, S, C):
    tq = next(t for t in MLA_TQ if S % t == 0)
    assert C % tq == 0 or tq > C
    q3, k3, v3 = q.reshape(B, S, 1024), k.reshape(B, S, 1024), v.reshape(B, S, 512)
    out = pl.pallas_call(
        functools.partial(_mla_kernel, n_ctx=C, n_all=S),
        grid=(B, MLA_HEADS // 2, S // tq),
        in_specs=[pl.BlockSpec((1, tq, 2 * LANES), lambda b, j, i: (b, i, j)),
                  pl.BlockSpec((1, S, 2 * LANES), lambda b, j, i: (b, 0, j)),
                  pl.BlockSpec((1, S, LANES), lambda b, j, i: (b, 0, j))],
        out_specs=pl.BlockSpec((1, tq, LANES), lambda b, j, i: (b, i, j)),
        out_shape=jax.ShapeDtypeStruct((B, S, 512), MXU_DTYPE),
        compiler_params=_params("parallel", "parallel", "arbitrary"),
        name="mla_attn",
    )(q3, k3, v3)
    return out.reshape(B * S, 512)


def _gqa_kernel(sink_ref, q_ref, kv_ref, o_ref, *, n_ctx, n_all):
    i = pl.program_id(1)
    qb_rows = q_ref.shape[1]
    span = qb_rows + 2 * WINDOW
    ctx_blocks = n_ctx // qb_rows
    group = GQA_HEADS // GQA_KV_HEADS
    left = lax.broadcasted_iota(jnp.int32, (qb_rows, LANES), 1) < GQA_HEAD_DIM

    def run(latent):
        nk = n_ctx + span if latent else n_ctx
        rows = group * qb_rows
        if latent:
            qb = i - ctx_blocks
            ws = pl.multiple_of(jnp.minimum(n_ctx + qb * qb_rows - WINDOW, n_all - span), WINDOW)
            col = lax.broadcasted_iota(jnp.int32, (rows, nk), 1)
            q_pos = qb * qb_rows + lax.broadcasted_iota(jnp.int32, (rows, nk), 0) % qb_rows
            k_pos = ws - 2 * n_ctx + col
            ok = (col < n_ctx) | ((jnp.abs(k_pos - q_pos) <= WINDOW) & (k_pos >= 0))
        head_of_row = lax.broadcasted_iota(jnp.int32, (rows, 1), 0) // qb_rows
        res = []
        for j in range(GQA_KV_HEADS):
            parts = []
            for g in range(group):
                qp = q_ref[0, :, (2 * j + g // 2) * LANES:(2 * j + g // 2 + 1) * LANES]
                parts.append(jnp.where(left if g % 2 == 0 else ~left, qp, jnp.zeros_like(qp)))
            q4 = jnp.concatenate(parts, axis=0)
            kc = kv_ref[0, 0:n_ctx, j * LANES:(j + 1) * LANES]
            vc = kv_ref[0, 0:n_ctx, 256 + j * LANES:256 + (j + 1) * LANES]
            if latent:
                kc = jnp.concatenate([kc, kv_ref[0, pl.ds(ws, span), j * LANES:(j + 1) * LANES]], axis=0)
                vc = jnp.concatenate([vc, kv_ref[0, pl.ds(ws, span), 256 + j * LANES:256 + (j + 1) * LANES]], axis=0)
            sink = jnp.zeros((rows, 1), F32)
            for g in range(group):
                sink = jnp.where(head_of_row == g, sink_ref[group * j + g] * LOG2E, sink)
            s = _mm_nt(q4, kc)
            if latent:
                s = jnp.where(ok, s, -jnp.inf)
            m = jnp.maximum(jnp.max(s, -1, keepdims=True), sink)
            p = jnp.exp2((s - m).astype(MXU_DTYPE))
            v_left = lax.broadcasted_iota(jnp.int32, vc.shape, 1) < GQA_HEAD_DIM
            o = _mm(p, jnp.where(v_left, vc, jnp.ones_like(vc)))
            res.append(o / (pltpu.roll(o, GQA_HEAD_DIM, axis=1) + jnp.exp2(sink - m)))
        for j in range(GQA_KV_HEADS):
            for pp in range(group // 2):
                r0 = res[j][(2 * pp) * qb_rows:(2 * pp + 1) * qb_rows]
                r1 = res[j][(2 * pp + 1) * qb_rows:(2 * pp + 2) * qb_rows]
                pair = 2 * j + pp
                o_ref[0, :, pair * LANES:(pair + 1) * LANES] = jnp.where(
                    left, r0, pltpu.roll(r1, GQA_HEAD_DIM, axis=1)).astype(o_ref.dtype)

    @pl.when(i < ctx_blocks)
    def _():
        run(False)

    @pl.when(i >= ctx_blocks)
    def _():
        run(True)


GQA_QB = 256


def _gqa_call(gq, gkv, sink, B, S, C):
    qb_rows = GQA_QB
    out = pl.pallas_call(
        functools.partial(_gqa_kernel, n_ctx=C, n_all=S),
        grid=(B, S // qb_rows),
        in_specs=[pl.BlockSpec(memory_space=pltpu.SMEM),
                  pl.BlockSpec((1, qb_rows, 512), lambda b, i: (b, i, 0)),
                  pl.BlockSpec((1, S, 512), lambda b, i: (b, 0, 0))],
        out_specs=pl.BlockSpec((1, qb_rows, 512), lambda b, i: (b, i, 0)),
        out_shape=jax.ShapeDtypeStruct((B, S, 512), MXU_DTYPE),
        compiler_params=_params("parallel", "arbitrary"),
        name="gqa_attn",
    )(sink, gq.reshape(B, S, 512), gkv.reshape(B, S, 512))
    return out.reshape(B * S, 512)


DN_TR = 256
DN_HALO = 16
DN_CPT = DN_TR // DN_CHUNK


def _stack(x, left):
    z = jnp.zeros_like(x)
    return jnp.concatenate([jnp.where(left, x, z), jnp.where(left, z, x)], axis=1)


def _dnlocal_kernel(main_ref, prev_ref, next_ref, ab_ref, conv_ref, gp_ref, bd_ref, trif_ref, trib_ref,
                    eg_ref, eb_ref,
                    u_out, w_out, qg_out, kg_out, qk_out, gl_out, pad_ref, *, ctx_tiles, n_tiles):
    i = pl.program_id(1)
    tr = DN_TR
    first = (i == 0) | (i == ctx_tiles)
    last = (i == ctx_tiles - 1) | (i == n_tiles - 1)
    xp = prev_ref[0].astype(F32)
    xn = next_ref[0].astype(F32)
    pad_ref[0:DN_HALO, :] = jnp.where(first, jnp.zeros_like(xp), xp)
    pad_ref[DN_HALO:DN_HALO + tr, :] = main_ref[0].astype(F32)
    pad_ref[DN_HALO + tr:, :] = jnp.where(last, jnp.zeros_like(xn), xn)
    y = jnp.zeros((tr, 3 * DN_WIDTH), F32)
    for t in range(DN_CONV):
        y = y + conv_ref[t:t + 1, :] * pad_ref[pl.ds(DN_HALO - DN_CONV // 2 + t, tr), :]
    y = _silu(y)
    q, k, v = y[:, 0:512], y[:, 512:1024], y[:, 1024:1536]
    bd = bd_ref[...]
    q = q * lax.rsqrt(_exact_mm(q * q, bd) + 1e-6) * (DN_HEAD_DIM ** -0.5)
    k = k * lax.rsqrt(_exact_mm(k * k, bd) + 1e-6)

    ab = ab_ref[0]
    g = -jnp.exp(gp_ref[0:1, :]) * jax.nn.softplus(ab + gp_ref[1:2, :])
    beta = jax.nn.sigmoid(ab)
    lane = lax.broadcasted_iota(jnp.int32, (tr, LANES), 1)
    gc = jnp.where(lane < DN_HEADS, _exact_mm_left(trif_ref[...], g), _exact_mm_left(trib_ref[...], g))
    gcx_all = _exact_mm(gc, eg_ref[...])
    bx_all = _exact_mm(beta, eb_ref[...])

    c = DN_CPT
    lane3 = lax.broadcasted_iota(jnp.int32, (1, 1, LANES), 2)
    left = (lane3 % LANES) < DN_HEAD_DIM
    tpos = lane3 % DN_HEAD_DIM
    lane6 = lax.broadcasted_iota(jnp.int32, (1, 1, 2 * LANES), 2)
    left6 = (lane6 % LANES) < DN_HEAD_DIM
    ri = lax.broadcasted_iota(jnp.int32, (1, DN_CHUNK, LANES), 1)
    cj = lax.broadcasted_iota(jnp.int32, (1, DN_CHUNK, LANES), 2) % DN_HEAD_DIM
    one = jnp.ones((), F32)
    zero = jnp.zeros((), F32)

    units = [(d, j) for d in range(2) for j in range(4)]
    st = {}
    for d, j in units:
        off = d * 512 + j * LANES
        gcx = gcx_all[:, off:off + LANES].reshape(c, DN_CHUNK, LANES)
        bx = bx_all[:, off:off + LANES].reshape(c, DN_CHUNK, LANES)
        qp = q[:, j * LANES:(j + 1) * LANES].reshape(c, DN_CHUNK, LANES)
        kp = k[:, j * LANES:(j + 1) * LANES].reshape(c, DN_CHUNK, LANES)
        vp = v[:, j * LANES:(j + 1) * LANES].reshape(c, DN_CHUNK, LANES)
        gl = gcx[:, DN_CHUNK - 1:DN_CHUNK, :] if d == 0 else gcx[:, 0:1, :]
        kb = kp * bx
        kq = _bmm_nt(jnp.concatenate([kb, qp], axis=1), _stack(kp, left))
        hi, mid, lo = _split3(gcx)
        a6 = jnp.where(tpos == 0, hi, jnp.where(tpos == 1, mid, jnp.where(tpos == 2, lo,
                       jnp.where(tpos < 6, one, zero))))
        b6 = jnp.where(tpos < 3, one, jnp.where(tpos == 3, -hi, jnp.where(tpos == 4, -mid,
                       jnp.where(tpos == 5, -lo, zero))))
        diff = _bmm_nt(a6, _stack(b6, left))
        st[d, j] = dict(gcx=gcx, bx=bx, qp=qp, kp=kp, vp=vp, gl=gl, kb=kb, kq=kq, diff=diff)
    for d, j in units:
        u = st[d, j]
        incl = (ri >= cj) if d == 0 else (ri <= cj)
        strict = (ri > cj) if d == 0 else (ri < cj)
        dm = jnp.exp(jnp.where(incl, u["diff"], -jnp.inf))
        u["qkm"] = u["kq"][:, DN_CHUNK:, :] * dm
        u["x"] = -jnp.where(strict, u["kq"][:, 0:DN_CHUNK, :] * dm, zero)
        u["r"] = u["x"]
    for d, j in units:
        u = st[d, j]
        u["x"] = _bmm(u["x"], _stack(u["x"], left))
    for level in range(5):
        for d, j in units:
            u = st[d, j]
            xs = _stack(u["x"], left)
            if level < 4:
                m = _bmm(jnp.concatenate([u["r"], u["x"]], axis=1), xs)
                u["r"] = u["r"] + u["x"] + m[:, 0:DN_CHUNK, :]
                u["x"] = m[:, DN_CHUNK:, :]
            else:
                u["r"] = u["r"] + u["x"] + _bmm(u["r"], xs)
    for d, j in units:
        u = st[d, j]
        eg = jnp.exp(u["gcx"])
        rhs = jnp.concatenate([u["vp"] * u["bx"], u["kb"] * eg], axis=-1)
        sol = rhs + _bmm(u["r"], _stack(rhs, left6))
        sl = slice(j * LANES, (j + 1) * LANES)
        u_out[0, d, :, sl] = sol[:, :, 0:LANES].reshape(tr, LANES)
        w_out[0, d, :, sl] = sol[:, :, LANES:].reshape(tr, LANES).astype(w_out.dtype)
        qg_out[0, d, :, sl] = (u["qp"] * eg).reshape(tr, LANES).astype(qg_out.dtype)
        kg_out[0, d, :, sl] = (u["kp"] * jnp.exp(u["gl"] - u["gcx"])).reshape(tr, LANES).astype(kg_out.dtype)
        qk_out[0, d, :, sl] = u["qkm"].reshape(tr, LANES).astype(qk_out.dtype)
        gl_out[0, d, :, :, sl] = jnp.exp(u["gl"])


def _dn_constants():
    idx = np.arange(DN_TR)
    same = (idx[:, None] // DN_CHUNK) == (idx[None, :] // DN_CHUNK)
    trif = (same & (idx[None, :] <= idx[:, None])).astype(np.float32)
    trib = (same & (idx[None, :] >= idx[:, None])).astype(np.float32)
    h = np.arange(512)
    bd = ((h[:, None] // DN_HEAD_DIM) == (h[None, :] // DN_HEAD_DIM)).astype(np.float32)
    col = np.arange(LANES)[:, None]
    out = np.arange(1024)[None, :]
    unit = (out // 512) * DN_HEADS + (out % 512) // DN_HEAD_DIM
    eg = (col == unit).astype(np.float32)
    eb = (col == unit + 2 * DN_HEADS).astype(np.float32)
    return tuple(jnp.asarray(a, MXU_DTYPE) for a in (bd, trif, trib, eg, eb))


def _dnlocal_call(dqkv, ab, conv_w, gp, consts, B, S, C):
    tr = DN_TR
    n_tiles = S // tr
    hb = tr // DN_HALO
    n_hblk = S // DN_HALO
    bd, trif, trib, eg, eb = consts
    const2 = lambda b, i: (0, 0)
    big = lambda b, i: (b, 0, i, 0)
    act = MXU_DTYPE
    shp = (B, 2, S, 512)
    return pl.pallas_call(
        functools.partial(_dnlocal_kernel, ctx_tiles=C // tr, n_tiles=n_tiles),
        grid=(B, n_tiles),
        in_specs=[pl.BlockSpec((1, tr, 1536), lambda b, i: (b, i, 0)),
                  pl.BlockSpec((1, DN_HALO, 1536), lambda b, i: (b, jnp.maximum(i * hb - 1, 0), 0)),
                  pl.BlockSpec((1, DN_HALO, 1536), lambda b, i: (b, jnp.minimum((i + 1) * hb, n_hblk - 1), 0)),
                  pl.BlockSpec((1, tr, LANES), lambda b, i: (b, i, 0)),
                  pl.BlockSpec((8, 1536), const2),
                  pl.BlockSpec((8, LANES), const2),
                  pl.BlockSpec((512, 512), const2),
                  pl.BlockSpec((tr, tr), const2),
                  pl.BlockSpec((tr, tr), const2),
                  pl.BlockSpec((LANES, 1024), const2),
                  pl.BlockSpec((LANES, 1024), const2)],
        out_specs=[pl.BlockSpec((1, 2, tr, 512), big)] * 5
                  + [pl.BlockSpec((1, 2, DN_CPT, 1, 512), lambda b, i: (b, 0, i, 0, 0))],
        out_shape=[jax.ShapeDtypeStruct(shp, F32)] + [jax.ShapeDtypeStruct(shp, act)] * 4
                  + [jax.ShapeDtypeStruct((B, 2, S // DN_CHUNK, 1, 512), F32)],
        scratch_shapes=[pltpu.VMEM((tr + 2 * DN_HALO, 1536), F32)],
        compiler_params=_params("parallel", "parallel"),
        name="dn_local",
    )(dqkv.reshape(B, S, 1536), dqkv.reshape(B, S, 1536), dqkv.reshape(B, S, 1536), ab.reshape(B, S, LANES),
      conv_w, gp, bd, trif, trib, eg, eb)


def _dnscan_kernel(uf, wf, qgf, kgf, qkf, glf, ub, wb, qgb, kgb, qkb, glb, of_out, ob_out, s_ref):
    n = pl.program_id(1)
    lane = lax.broadcasted_iota(jnp.int32, (1, LANES), 1)
    left = lane < DN_HEAD_DIM
    row = lax.broadcasted_iota(jnp.int32, (LANES, LANES), 0)
    col = lax.broadcasted_iota(jnp.int32, (LANES, LANES), 1)
    same_head = (row < DN_HEAD_DIM) == (col < DN_HEAD_DIM)
    dirs = ((uf, wf, qgf, kgf, qkf, glf, of_out), (ub, wb, qgb, kgb, qkb, glb, ob_out))
    units = [(bb, d, j) for bb in range(uf.shape[0]) for d in range(2) for j in range(4)]
    sidx = lambda bb, d, j: (bb * 2 + d) * 4 + j
    sl = lambda j: slice(j * LANES, (j + 1) * LANES)
    started = n > 0
    st = {t: jnp.where(started, s_ref[sidx(*t)], jnp.zeros((LANES, LANES), F32)) for t in units}
    pre = {(bb, d, j): _mm(dirs[d][1][bb, 0, :, sl(j)], st[bb, d, j]) for bb, d, j in units}
    o1 = {(bb, d, j): _mm(dirs[d][2][bb, 0, :, sl(j)], st[bb, d, j]) for bb, d, j in units}
    vn = {(bb, d, j): dirs[d][0][bb, 0, :, sl(j)] - pre[bb, d, j] for bb, d, j in units}
    for bb, d, j in units:
        v = vn[bb, d, j]
        z = jnp.zeros_like(v)
        vst = jnp.concatenate([jnp.where(left, v, z), jnp.where(left, z, v)], axis=0)
        dirs[d][6][bb, :, sl(j)] = o1[bb, d, j] + _mm(dirs[d][4][bb, 0, :, sl(j)], vst)
    for bb, d, j in units:
        upd = _mm_tn(dirs[d][3][bb, 0, :, sl(j)], vn[bb, d, j])
        s_ref[sidx(bb, d, j)] = (st[bb, d, j] * dirs[d][5][bb, 0, 0, :, sl(j)]
                                 + jnp.where(same_head, upd, jnp.zeros_like(upd)))


DN_SCAN_BATCH = 8


def _dnscan_call(local, B, S, C):
    u, w, qg, kg, qk, gl = local
    nch, nc = S // DN_CHUNK, C // DN_CHUNK
    bb = DN_SCAN_BATCH if B % DN_SCAN_BATCH == 0 else 1

    def bidx(n):
        return jnp.where(n < nc, nc - 1 - n, nch - 1 + nc - n)

    fspec = pl.BlockSpec((bb, 1, DN_CHUNK, 512), lambda b, n: (b, 0, n, 0))
    bspec = pl.BlockSpec((bb, 1, DN_CHUNK, 512), lambda b, n: (b, 1, bidx(n), 0))
    fgl = pl.BlockSpec((bb, 1, 1, 1, 512), lambda b, n: (b, 0, n, 0, 0))
    bgl = pl.BlockSpec((bb, 1, 1, 1, 512), lambda b, n: (b, 1, bidx(n), 0, 0))
    return pl.pallas_call(
        _dnscan_kernel,
        grid=(B // bb, nch),
        in_specs=[fspec] * 5 + [fgl] + [bspec] * 5 + [bgl],
        out_specs=[pl.BlockSpec((bb, DN_CHUNK, 512), lambda b, n: (b, n, 0)),
                   pl.BlockSpec((bb, DN_CHUNK, 512), lambda b, n: (b, bidx(n), 0))],
        out_shape=[jax.ShapeDtypeStruct((B, S, 512), F32)] * 2,
        scratch_shapes=[pltpu.VMEM((bb * 8, LANES, LANES), F32)],
        compiler_params=_params("parallel", "arbitrary"),
        name="dn_scan",
    )(u, w, qg, kg, qk, gl, u, w, qg, kg, qk, gl)


def _pack_pairs(v):
    w = v.shape[1] // 2
    bits = lax.bitcast_convert_type(v.astype(jnp.bfloat16).astype(F32), jnp.int32)
    return lax.shift_right_logical(bits[:, :w], 16) | bits[:, w:]


def _unpack_pairs(p):
    lo = lax.bitcast_convert_type(lax.shift_left(p, 16), F32)
    hi = lax.bitcast_convert_type(p & jnp.int32(-65536), F32)
    return lo, hi


def _merge_kernel(x_ref, ga_ref, shf_ref, scf_ref, omla_ref, of_ref, ob_ref, dz_ref, ogqa_ref, gates_ref,
                  wo1_ref, wo2_ref, wo3_ref, wout_ref, dnn_ref, bd_ref, lng_ref, lnb_ref, wr_ref, rb_ref,
                  x1_out, ufp_out, eidx_out, rank_out, ew_out, cnt_out, cnt_ref):
    @pl.when(pl.program_id(0) == 0)
    def _():
        cnt_ref[...] = jnp.zeros_like(cnt_ref)

    o = of_ref[...] + ob_ref[...]
    ms = _exact_mm(o * o, bd_ref[...]) * (1.0 / DN_HEAD_DIM)
    dn = o * lax.rsqrt(ms + NORM_EPS) * dnn_ref[...] * _silu(dz_ref[...].astype(F32))
    g1 = gates_ref[:, 0:1024].astype(F32)
    g2 = gates_ref[:, 1024:2048].astype(F32)
    g3 = gates_ref[:, 2048:3072].astype(F32)
    m = (g1 * _mm(omla_ref[...], wo1_ref[...]) + g2 * _mm(dn, wo2_ref[...])
         + g3 * _mm(ogqa_ref[...], wo3_ref[...]))
    y = _mm(m, wout_ref[...])
    x1 = _layernorm(DEEPNORM_ALPHA * x_ref[...] + ga_ref[0] * y, lng_ref[...], lnb_ref[...])
    x1_out[...] = x1
    uf = x1 * (1.0 + scf_ref[0]) + shf_ref[0]
    _split_pieces(_pack_pairs(uf), ufp_out)

    tm = uf.shape[0]
    scores = jax.nn.sigmoid(_mm_nt(wr_ref[...], uf))[0:N_EXPERTS]
    sel = scores + rb_ref[0:N_EXPERTS, :]
    gsz = N_EXPERTS // N_GROUPS
    neg = jnp.full((), -jnp.inf, F32)
    sel3 = sel.reshape(N_GROUPS, gsz, tm)
    mem = lax.broadcasted_iota(jnp.int32, (N_GROUPS, gsz, tm), 1)
    m1 = jnp.max(sel3, 1, keepdims=True)
    i1 = jnp.min(jnp.where(sel3 == m1, mem, gsz), 1, keepdims=True)
    m2 = jnp.max(jnp.where(mem == i1, neg, sel3), 1, keepdims=True)
    gs = (m1 + m2).reshape(N_GROUPS, tm)
    gi = lax.broadcasted_iota(jnp.int32, (N_GROUPS, tm), 0)
    grank = jnp.zeros((N_GROUPS, tm), jnp.int32)
    for gp in range(N_GROUPS):
        other = gs[gp:gp + 1, :]
        beats = (other > gs) | ((other == gs) & (gp < gi))
        grank = grank + beats.astype(jnp.int32)
    gsel = (grank < TOPK_GROUPS).reshape(N_GROUPS, 1, tm)
    cur = jnp.where(gsel, sel3, neg).reshape(N_EXPERTS, tm)
    ei = lax.broadcasted_iota(jnp.int32, (N_EXPERTS, tm), 0)
    zero = jnp.zeros((N_EXPERTS, tm), F32)
    one = jnp.ones((N_EXPERTS, tm), F32)
    chosen = zero
    picks = []
    for _ in range(TOP_K):
        mx = jnp.max(cur, 0, keepdims=True)
        ix = jnp.min(jnp.where(cur == mx, ei, N_EXPERTS), 0, keepdims=True)
        pick = ei == ix
        picks.append((ix, pick))
        chosen = chosen + jnp.where(pick, one, zero)
        cur = jnp.where(pick, neg, cur)

    r_i = lax.broadcasted_iota(jnp.int32, (tm, tm), 0)
    c_i = lax.broadcasted_iota(jnp.int32, (tm, tm), 1)
    before = jnp.where(r_i < c_i, 1.0, 0.0)
    pos = cnt_ref[:, 0:1] + _mm(chosen, before)
    cnt_new = cnt_ref[...] + jnp.sum(chosen, 1, keepdims=True)
    cnt_ref[...] = cnt_new
    cnt_out[...] = cnt_new

    w_rows = [jnp.sum(jnp.where(pick, scores, zero), 0, keepdims=True) for _, pick in picks]
    wsum = w_rows[0]
    for w_k in w_rows[1:]:
        wsum = wsum + w_k
    eidx_out[...] = jnp.concatenate([ix for ix, _ in picks], axis=0)
    rank_out[...] = jnp.concatenate(
        [jnp.sum(jnp.where(pick, pos, zero), 0, keepdims=True) for _, pick in picks], axis=0).astype(jnp.int32)
    w8 = jnp.concatenate([w_k / wsum * ROUTED_SCALE for w_k in w_rows], axis=0)
    ew_out[...] = jnp.concatenate([w8, jnp.zeros((LANES - TOP_K, tm), F32)], axis=0).T


def _merge_call(xs, mod_l, omla, of, ob, dz, ogqa, gates, wo1, wo2, wo3, wout, dnn, bd, lng, lnb, wr, rb, B, S, C):
    T, D = xs.shape
    tpb, ctiles = S // TM, C // TM
    row = lambda i: (i, 0)
    const = lambda i: (0, 0)
    modspec = lambda k: pl.BlockSpec((1, 1, D), _mod_index(tpb, ctiles, k))
    return pl.pallas_call(
        _merge_kernel,
        grid=(T // TM,),
        in_specs=[pl.BlockSpec((TM, D), row), modspec(2), modspec(3), modspec(4),
                  pl.BlockSpec((TM, 512), row), pl.BlockSpec((TM, 512), row), pl.BlockSpec((TM, 512), row),
                  pl.BlockSpec((TM, 512), row), pl.BlockSpec((TM, 512), row), pl.BlockSpec((TM, 3072), row),
                  pl.BlockSpec((512, D), const), pl.BlockSpec((512, D), const), pl.BlockSpec((512, D), const),
                  pl.BlockSpec((D, D), const), pl.BlockSpec((1, 512), const), pl.BlockSpec((512, 512), const),
                  pl.BlockSpec((1, D), const), pl.BlockSpec((1, D), const),
                  pl.BlockSpec((LANES, D), const), pl.BlockSpec((LANES, 1), const)],
        out_specs=[pl.BlockSpec((TM, D), row), pl.BlockSpec((N_PIECES, TM, PIECE), lambda i: (0, i, 0)),
                   pl.BlockSpec((TOP_K, TM), lambda i: (0, i)), pl.BlockSpec((TOP_K, TM), lambda i: (0, i)),
                   pl.BlockSpec((TM, LANES), row), pl.BlockSpec((N_EXPERTS, LANES), const)],
        out_shape=[jax.ShapeDtypeStruct((T, D), F32), jax.ShapeDtypeStruct((N_PIECES, T, PIECE), jnp.int32),
                   jax.ShapeDtypeStruct((TOP_K, T), jnp.int32), jax.ShapeDtypeStruct((TOP_K, T), jnp.int32),
                   jax.ShapeDtypeStruct((T, LANES), F32), jax.ShapeDtypeStruct((N_EXPERTS, LANES), F32)],
        scratch_shapes=[pltpu.VMEM((N_EXPERTS, LANES), F32)],
        compiler_params=_params("arbitrary"),
        name="merge_norm_route",
    )(xs, mod_l, mod_l, mod_l, omla, of.reshape(T, 512), ob.reshape(T, 512), dz, ogqa, gates,
      wo1, wo2, wo3, wout, dnn, bd, lng, lnb, wr, rb)


EXPERT_BLOCK = 1024
SC_WINDOW = 128
N_PIECES = 2
PIECE = D_MODEL // 2 // N_PIECES


def _split_pieces(packed, out_ref):
    for h in range(N_PIECES):
        out_ref[h] = packed[:, h * PIECE:(h + 1) * PIECE]


def _mm_pieces(pieces, w):
    acc = None
    for h, (lo, hi) in enumerate(pieces):
        t = (_mm(lo, w[h * PIECE:(h + 1) * PIECE, :])
             + _mm(hi, w[D_MODEL // 2 + h * PIECE:D_MODEL // 2 + (h + 1) * PIECE, :]))
        acc = t if acc is None else acc + t
    return acc


def _sc_mesh():
    return plsc.VectorSubcoreMesh(core_axis_name="c", subcore_axis_name="s")


def _sc_gather_rows(y, idx):
    n = idx.shape[1]
    W = y.shape[1]

    @pl.kernel(out_type=jax.ShapeDtypeStruct((n, W), y.dtype), mesh=_sc_mesh(), scratch_types=[])
    def gather(y_hbm, i_hbm, o_hbm):
        def body(i_vmem, o_vmem):
            pltpu.sync_copy(y_hbm.at[i_vmem.at[0]], o_vmem)

        pltpu.emit_pipeline(
            body,
            grid=(n // SC_WINDOW,),
            in_specs=[pl.BlockSpec((1, SC_WINDOW), lambda i: (0, i))],
            out_specs=[pl.BlockSpec((SC_WINDOW, W), lambda i: (i, 0))],
            core_axis_name=("c", "s"),
            dimension_semantics=(pltpu.PARALLEL,),
        )(i_hbm, o_hbm)

    return gather(y, idx)


SC_LANES = 16
SC_WORKERS = 32
SC_CHUNK = 2176


def _sc_invert_rows(dest, default, n_tok):
    n_rows = default.shape[0]
    per_w = n_rows // SC_WORKERS
    n_k = dest.shape[0] // n_tok
    assert n_rows % (SC_WORKERS * SC_LANES) == 0 and n_tok % SC_CHUNK == 0

    @pl.kernel(out_type=jax.ShapeDtypeStruct((n_rows,), jnp.int32), mesh=_sc_mesh(),
               scratch_types=[pltpu.VMEM((per_w,), jnp.int32), pltpu.VMEM((SC_CHUNK,), jnp.int32)],
               compiler_params=pltpu.CompilerParams(needs_layout_passes=False))
    def invert(dest_hbm, dflt_hbm, out_hbm, rows_v, dest_v):
        wid = lax.axis_index("s") * 2 + lax.axis_index("c")
        base = wid * per_w
        pltpu.sync_copy(dflt_hbm.at[pl.ds(base, per_w)], rows_v)
        lanes = lax.iota(jnp.int32, SC_LANES)
        for k in range(n_k):
            @pl.loop(0, n_tok // SC_CHUNK)
            def _(c):
                pltpu.sync_copy(dest_hbm.at[pl.ds(k * n_tok + c * SC_CHUNK, SC_CHUNK)], dest_v)

                @pl.loop(0, SC_CHUNK, step=SC_LANES)
                def _(o):
                    local = dest_v[pl.ds(o, SC_LANES)] - base
                    mine = (local >= 0) & (local < per_w)
                    plsc.store_scatter(rows_v, [jnp.where(mine, local, 0)], c * SC_CHUNK + o + lanes, mask=mine)

        pltpu.sync_copy(rows_v, out_hbm.at[pl.ds(base, per_w)])

    return invert(dest, default)


def _experts_kernel(be_ref, nv_ref, xb_ref, wg_ref, wu_ref, wd_ref, y_out):
    b = pl.program_id(0)
    nv = nv_ref[b]

    @pl.when(nv > 0)
    def _():
        rows = lax.broadcasted_iota(jnp.int32, xb_ref.shape[1:], 0)
        pieces = []
        for h in range(N_PIECES):
            xh = xb_ref[h]
            pieces.append(_unpack_pairs(jnp.where(rows < nv, xh, jnp.zeros_like(xh))))
        hid = _silu(_mm_pieces(pieces, wg_ref.at[0, 0])) * _mm_pieces(pieces, wu_ref.at[0, 0])
        _split_pieces(_pack_pairs(_mm(hid, wd_ref[0, 0])), y_out)


def _experts_call(xb, block_e, nvalid, wg, wu, wd, layer):
    _, R, _ = xb.shape
    D = D_MODEL
    blk = (N_PIECES, EXPERT_BLOCK, PIECE)
    grid_spec = pltpu.PrefetchScalarGridSpec(
        num_scalar_prefetch=2,
        grid=(R // EXPERT_BLOCK,),
        in_specs=[pl.BlockSpec(blk, lambda b, be, nv: (0, b, 0)),
                  pl.BlockSpec((1, 1, D, EXPERT_DIM), lambda b, be, nv: (layer, be[b], 0, 0)),
                  pl.BlockSpec((1, 1, D, EXPERT_DIM), lambda b, be, nv: (layer, be[b], 0, 0)),
                  pl.BlockSpec((1, 1, EXPERT_DIM, D), lambda b, be, nv: (layer, be[b], 0, 0))],
        out_specs=pl.BlockSpec(blk, lambda b, be, nv: (0, b, 0)),
    )
    return pl.pallas_call(
        _experts_kernel,
        grid_spec=grid_spec,
        out_shape=jax.ShapeDtypeStruct((N_PIECES, R, PIECE), jnp.int32),
        compiler_params=_params("arbitrary"),
        name="moe_experts",
    )(block_e, nvalid, xb, wg, wu, wd)


def _combine_kernel(x_ref, ufp_ref, yg_ref, ew_ref, gf_ref, sg_ref, su_ref, sd_ref, g_ref, b_ref, o_ref):
    pieces = [_unpack_pairs(ufp_ref[h]) for h in range(N_PIECES)]
    hs = _silu(_mm_pieces(pieces, sg_ref)) * _mm_pieces(pieces, su_ref)
    f = _mm(hs, sd_ref[...])
    ew = ew_ref[...]
    lane = lax.broadcasted_iota(jnp.int32, ew.shape, 1)
    acc = [[jnp.zeros((x_ref.shape[0], PIECE), F32) for _ in range(N_PIECES)] for _ in range(2)]
    for k in range(TOP_K):
        wk = jnp.sum(jnp.where(lane == k, ew, jnp.zeros_like(ew)), axis=1, keepdims=True)
        for h in range(N_PIECES):
            ylo, yhi = _unpack_pairs(yg_ref[h, k])
            acc[0][h] = acc[0][h] + wk * ylo
            acc[1][h] = acc[1][h] + wk * yhi
    f = f + jnp.concatenate(acc[0] + acc[1], axis=1)
    o_ref[...] = _layernorm(DEEPNORM_ALPHA * x_ref[...] + gf_ref[0] * f, g_ref[...], b_ref[...])


def _combine_call(x1, ufp, yg, ew, mod_l, sg, su, sd, g, b, B, S, C):
    T, D = x1.shape
    tpb, ctiles = S // TM, C // TM
    row = lambda i: (i, 0)
    const = lambda i: (0, 0)
    return pl.pallas_call(
        _combine_kernel,
        grid=(T // TM,),
        in_specs=[pl.BlockSpec((TM, D), row), pl.BlockSpec((N_PIECES, TM, PIECE), lambda i: (0, i, 0)),
                  pl.BlockSpec((N_PIECES, TOP_K, TM, PIECE), lambda i: (0, 0, i, 0)),
                  pl.BlockSpec((TM, LANES), row),
                  pl.BlockSpec((1, 1, D), _mod_index(tpb, ctiles, 5)),
                  pl.BlockSpec((D, SHARED_DIM), const), pl.BlockSpec((D, SHARED_DIM), const),
                  pl.BlockSpec((SHARED_DIM, D), const),
                  pl.BlockSpec((1, D), const), pl.BlockSpec((1, D), const)],
        out_specs=pl.BlockSpec((TM, D), row),
        out_shape=jax.ShapeDtypeStruct((T, D), F32),
        compiler_params=_params("parallel"),
        name="moe_combine_norm",
    )(x1, ufp, yg, ew, mod_l, sg, su, sd, g, b)


def _moe_routed(ufp, eidx_t, rank_t, counts, wg, wu, wd, layer):
    T = ufp.shape[1]
    n_blocks = -(-(T * TOP_K + N_EXPERTS * (EXPERT_BLOCK - 1)) // EXPERT_BLOCK)
    n_rows = n_blocks * EXPERT_BLOCK
    padded = (counts + EXPERT_BLOCK - 1) // EXPERT_BLOCK * EXPERT_BLOCK
    pad_end = jnp.cumsum(padded)
    start_pad = pad_end - padded
    experts = jnp.arange(N_EXPERTS, dtype=jnp.int32)

    def lookup(table, idx):
        sel = idx[None] == experts.reshape((N_EXPERTS,) + (1,) * idx.ndim)
        return jnp.sum(jnp.where(sel, table.reshape((N_EXPERTS,) + (1,) * idx.ndim), 0), axis=0)

    dest_t = lookup(start_pad, eidx_t) + rank_t
    blk = jnp.arange(n_blocks, dtype=jnp.int32) * EXPERT_BLOCK
    block_e = jnp.minimum(jnp.sum((blk[:, None] >= pad_end[None, :]).astype(jnp.int32), axis=1), N_EXPERTS - 1)
    nvalid = jnp.clip(lookup(counts, block_e) - (blk - lookup(start_pad, block_e)), 0, EXPERT_BLOCK)
    row_tok = _sc_invert_rows(dest_t.reshape(-1), jnp.arange(n_rows, dtype=jnp.int32) % T, T)
    piece = jnp.arange(N_PIECES, dtype=jnp.int32)
    src = (piece[:, None] * T + row_tok[None, :]).reshape(1, N_PIECES * n_rows)
    xb = _sc_gather_rows(ufp.reshape(N_PIECES * T, PIECE), src).reshape(N_PIECES, n_rows, PIECE)
    yb = _experts_call(xb, block_e.astype(jnp.int32), nvalid.astype(jnp.int32), wg, wu, wd, layer)
    back = (piece[:, None, None] * n_rows + dest_t[None]).reshape(1, N_PIECES * TOP_K * T)
    yg = _sc_gather_rows(yb.reshape(N_PIECES * n_rows, PIECE), back)
    return yg.reshape(N_PIECES, TOP_K, T, PIECE)


def _rot_cols(w, half):
    return jnp.concatenate([-w[:, half:], w[:, :half]], axis=1)


def _prep_w_in(w):
    d = w.shape[0]
    offs = np.cumsum((0,) + IN_SIZES)
    cq, ckv, kr, dqkv, da, db, dz, gq, gk, gv, gates = (w[:, offs[t]:offs[t + 1]] for t in range(len(IN_SIZES)))
    z = lambda n: jnp.zeros((d, n), w.dtype)
    krg = jnp.concatenate([z(64), kr, z(32)], 1)
    krr = jnp.concatenate([z(64), _rot_cols(kr, MLA_ROPE // 2), z(32)], 1)
    ab = jnp.concatenate([da, db, z(LANES - 4 * DN_HEADS)], 1)
    hd = GQA_HEAD_DIM
    gq_rot = jnp.concatenate([_rot_cols(gq[:, h * hd:(h + 1) * hd], hd // 2) for h in range(GQA_HEADS)], 1)
    dup = lambda t: jnp.concatenate([t[:, 0:hd], t[:, 0:hd], t[:, hd:2 * hd], t[:, hd:2 * hd]], 1)
    gk_rot = jnp.concatenate([_rot_cols(gk[:, h * hd:(h + 1) * hd], hd // 2) for h in range(GQA_KV_HEADS)], 1)
    cat = jnp.concatenate([cq, ckv, krg, krr, dqkv, ab, dz, gq, gq_rot, dup(gk), dup(gk_rot), dup(gv), gates], 1)
    assert cat.shape[1] == NZ
    return cat.astype(MXU_DTYPE)


def _prep_w_uq(w):
    d = w.shape[0]
    hw = MLA_NOPE + MLA_ROPE
    a, b = [], []
    for h in range(MLA_HEADS):
        wh = w[:, h * hw:(h + 1) * hw]
        a += [wh, jnp.zeros((d, LANES - hw), w.dtype)]
        b += [jnp.zeros((d, MLA_NOPE), w.dtype), _rot_cols(wh[:, MLA_NOPE:], MLA_ROPE // 2),
              jnp.zeros((d, LANES - hw), w.dtype)]
    return jnp.concatenate(a + b, 1).astype(MXU_DTYPE)


def _prep_w_ukv(w):
    d = w.shape[0]
    hw = MLA_NOPE + MLA_V
    kpart, vpart = [], []
    for h in range(MLA_HEADS):
        wh = w[:, h * hw:(h + 1) * hw]
        kpart += [wh[:, :MLA_NOPE], jnp.zeros((d, LANES - MLA_NOPE), w.dtype)]
        vpart += [wh[:, MLA_NOPE:]]
    return jnp.concatenate(kpart + vpart, 1).astype(MXU_DTYPE)


def _rope_tables(n_rows, C):
    row = jnp.repeat(jnp.arange(n_rows, dtype=F32), GRID_W)
    col = jnp.tile(jnp.arange(GRID_W, dtype=F32), n_rows)

    def angles(dim):
        n = dim // 4
        inv = ROPE_BASE ** (-jnp.arange(n, dtype=F32) / n)
        return jnp.concatenate([row[:, None] * inv, col[:, None] * inv], axis=-1)

    def with_ctx(cos, sin):
        return (jnp.concatenate([jnp.ones((C, LANES), F32), cos], 0),
                jnp.concatenate([jnp.zeros((C, LANES), F32), sin], 0))

    L = n_rows * GRID_W
    am = angles(MLA_ROPE)
    one, zero = jnp.ones((L, MLA_NOPE), F32), jnp.zeros((L, MLA_NOPE), F32)
    cm = jnp.concatenate([one, jnp.cos(am), jnp.cos(am), one[:, :32]], 1)
    sm = jnp.concatenate([zero, jnp.sin(am), jnp.sin(am), zero[:, :32]], 1)
    ag = angles(GQA_HEAD_DIM)
    cg = jnp.tile(jnp.cos(ag), (1, 4))
    sg = jnp.tile(jnp.sin(ag), (1, 4))
    return with_ctx(cm, sm) + with_ctx(cg, sg)


def kernel(x, c, ctx, c_ctx, w_ada, b_ada, w_in, mla_q_norm, mla_kv_norm, w_uq, w_ukv, dn_conv, dn_a_log, dn_dt_bias, dn_norm, gqa_sink, w_o_mla, w_o_dn, w_o_gqa, w_out, ln1_g, ln1_b, w_router, router_bias, w_exp_gate, w_exp_up, w_exp_down, w_sh_gate, w_sh_up, w_sh_down, ln2_g, ln2_b):
    B, L, D = x.shape
    C = ctx.shape[1]
    S = C + L
    nl = w_in.shape[0]
    assert D == D_MODEL and nl == DEPTH and B <= CTX_MOD_ROW
    assert C % TM == 0 and L % TM == 0 and L % GRID_W == 0 and L >= 3 * WINDOW
    cast = lambda t: t.astype(MXU_DTYPE)

    cc = jnp.zeros((MOD_ROWS, D), F32).at[0:B].set(c).at[CTX_MOD_ROW].set(c_ctx)
    mods = _ada_call(cc, w_ada, b_ada)
    tabs = _rope_tables(L // GRID_W, C)
    dn_consts = _dn_constants()
    xs = jnp.concatenate([ctx, x], axis=1).reshape(B * S, D)

    for l in range(nl):
        mod_l = mods[l].reshape(MOD_ROWS * 6, 1, D)
        q, k, v, dqkv, ab, dz, gq, gkv, gates = _inproj_call(
            xs, mod_l, _prep_w_in(w_in[l]), _prep_w_uq(w_uq[l]), _prep_w_ukv(w_ukv[l]),
            mla_q_norm[l].reshape(1, -1), mla_kv_norm[l].reshape(1, -1), tabs, B, S, C)
        omla = _mla_call(q, k, v, B, S, C)
        conv8 = jnp.zeros((8, 3 * DN_WIDTH), F32).at[0:DN_CONV].set(dn_conv[l])
        gp = (jnp.zeros((8, LANES), F32).at[0, 0:2 * DN_HEADS].set(dn_a_log[l].reshape(-1))
              .at[1, 0:2 * DN_HEADS].set(dn_dt_bias[l].reshape(-1)))
        local = _dnlocal_call(dqkv, ab, conv8, gp, dn_consts, B, S, C)
        of, ob = _dnscan_call(local, B, S, C)
        ogqa = _gqa_call(gq, gkv, gqa_sink[l], B, S, C)
        wr = jnp.zeros((LANES, D), F32).at[0:N_EXPERTS].set(w_router[l].T)
        rb = jnp.zeros((LANES, 1), F32).at[0:N_EXPERTS, 0].set(router_bias[l])
        x1, ufp, eidx_t, rank_t, ew, cnt = _merge_call(
            xs, mod_l, omla, of, ob, dz, ogqa, gates,
            cast(w_o_mla[l]), cast(w_o_dn[l]), cast(w_o_gqa[l]), cast(w_out[l]),
            jnp.tile(dn_norm[l], DN_HEADS).reshape(1, DN_WIDTH), dn_consts[0],
            ln1_g[l].reshape(1, D), ln1_b[l].reshape(1, D), cast(wr), rb, B, S, C)
        yg = _moe_routed(ufp, eidx_t, rank_t, cnt[:, 0].astype(jnp.int32),
                         w_exp_gate, w_exp_up, w_exp_down, l)
        xs = _combine_call(x1, ufp, yg, ew, mod_l, cast(w_sh_gate[l]), cast(w_sh_up[l]), cast(w_sh_down[l]),
                           ln2_g[l].reshape(1, D), ln2_b[l].reshape(1, D), B, S, C)
    return xs.reshape(B, S, D)[:, C:, :]
```
